```python
import math
import jax
import jax.numpy as jnp
from jax import lax
import numpy as np

D_MODEL = 1024
BATCH = 16
SEQ = 256
DEPTH = 2
DEC_BATCH = 2
DEC_SEQ = 4096
PAST_LEN = 512

GRID_W = 64
Q_BLOCK = 128
ROPE_BASE = 10000.0
EPS = 1e-6

A_HEADS = 8
A_KV_HEADS = 2
A_HEAD_DIM = 64
B_HEADS = 4
B_HEAD_DIM = 128
MLSTM_CHUNK = 128
C_HEADS = 4
C_HEAD_DIM = 64
C_V_DIM = 2 * C_HEAD_DIM
N_BRANCHES = 3
N_EXPERTS = 32
TOP_K = 4
D_EXPERT = 1024
SWIGLU_LIMIT = 7.0
SWIGLU_ALPHA = 1.702
N_MOD = 6
FORGET_BIAS = 3.0

DEEPNORM_ALPHA = (2 * DEPTH) ** 0.25
DEEPNORM_BETA = (8 * DEPTH) ** -0.25

W_A_Q = A_HEADS * A_HEAD_DIM
W_A_KV = A_KV_HEADS * A_HEAD_DIM
W_B = B_HEADS * B_HEAD_DIM
W_B_GATE = 2 * B_HEADS
W_C_QK = C_HEADS * 2 * C_HEAD_DIM
W_C_V = C_HEADS * C_V_DIM
W_GATES = N_BRANCHES * D_MODEL
IN_WIDTHS = (W_A_Q, W_A_KV, W_A_KV, W_B, W_B, W_B, W_B, W_B_GATE, W_B_GATE, W_C_QK, W_C_QK, W_C_V, W_GATES)
IN_SPLIT_POINTS = tuple(int(v) for v in np.cumsum(IN_WIDTHS)[:-1])
D_IN = sum(IN_WIDTHS)
FGATE_START = sum(IN_WIDTHS[:8])

kernel_name = 'hybrid_gqa_mlstm_diffattn_moe_dit_step'


def layer_norm(x):
    xf = x.astype(jnp.float32)
    mu = jnp.mean(xf, axis=-1, keepdims=True)
    xc = xf - mu
    var = jnp.mean(xc * xc, axis=-1, keepdims=True)
    return (xc * lax.rsqrt(var + EPS)).astype(x.dtype)


def rms_norm(x, g):
    xf = x.astype(jnp.float32)
    y = xf * lax.rsqrt(jnp.mean(xf * xf, axis=-1, keepdims=True) + EPS)
    return y.astype(x.dtype) * g


def rope_angles(pos, dim):
    freqs = ROPE_BASE ** (-jnp.arange(0, dim, 2, dtype=jnp.float32) / dim)
    return pos.astype(jnp.float32)[:, None] * freqs[None, :]


def rotate(x, ang):
    half = x.shape[-1] // 2
    cos = jnp.cos(ang).astype(x.dtype)
    sin = jnp.sin(ang).astype(x.dtype)
    x1, x2 = x[..., :half], x[..., half:]
    return jnp.concatenate([x1 * cos - x2 * sin, x2 * cos + x1 * sin], axis=-1)


def axial_rope(x, pos_r, pos_c):
    half = x.shape[-1] // 2
    return jnp.concatenate([rotate(x[..., :half], rope_angles(pos_r, half)),
                            rotate(x[..., half:], rope_angles(pos_c, half))], axis=-1)


def sweep_query_blocks(fn, qs):
    s = qs[0].shape[-2]
    nb = s // Q_BLOCK

    def to_blocks(q):
        q = q.reshape(q.shape[:-2] + (nb, Q_BLOCK, q.shape[-1]))
        return jnp.moveaxis(q, -3, 0)

    out = lax.map(fn, tuple(to_blocks(q) for q in qs))
    out = jnp.moveaxis(out, 0, -3)
    return out.reshape(out.shape[:-3] + (s, out.shape[-1]))


def gqa_attention(q, k, v):
    scale = q.shape[-1] ** -0.5

    def block(qs):
        (qb,) = qs
        s = jnp.einsum('bkgqd,bktd->bkgqt', qb, k).astype(jnp.float32) * scale
        p = jax.nn.softmax(s, axis=-1)
        return jnp.einsum('bkgqt,bktd->bkgqd', p.astype(v.dtype), v)

    return sweep_query_blocks(block, (q,))


def diff_attention(q1, q2, k1, k2, v, lam):
    scale = q1.shape[-1] ** -0.5

    def block(qs):
        q1b, q2b = qs
        p1 = jax.nn.softmax(jnp.einsum('bhqd,bhtd->bhqt', q1b, k1).astype(jnp.float32) * scale, axis=-1)
        p2 = jax.nn.softmax(jnp.einsum('bhqd,bhtd->bhqt', q2b, k2).astype(jnp.float32) * scale, axis=-1)
        a = p1 - lam * p2
        return jnp.einsum('bhqt,bhtd->bhqd', a.astype(v.dtype), v)

    return sweep_query_blocks(block, (q1, q2))


def mlstm_direction(q, k, v, log_i, log_f, state):
    bsz, nh, s, _ = q.shape
    nc = s // MLSTM_CHUNK

    def chunks(t):
        t = t.reshape(t.shape[:2] + (nc, MLSTM_CHUNK) + t.shape[3:])
        return jnp.moveaxis(t, 2, 0)

    causal = jnp.tril(jnp.ones((MLSTM_CHUNK, MLSTM_CHUNK), dtype=bool))

    def step(carry, xs):
        c_prev, n_prev, m_prev = carry
        qc, kc, vc, ic, fc = xs
        b = jnp.cumsum(fc, axis=-1)
        d = jnp.where(causal, b[..., :, None] - b[..., None, :] + ic[..., None, :], -jnp.inf)
        inter = b + m_prev[..., None]
        m_row = jnp.maximum(inter, jnp.max(d, axis=-1))
        w = jnp.exp(d - m_row[..., None])
        sc = jnp.einsum('bhjd,bhsd->bhjs', qc, kc) * w
        decay_row = jnp.exp(inter - m_row)
        num = decay_row[..., None] * jnp.einsum('bhjd,bhde->bhje', qc, c_prev) + jnp.einsum('bhjs,bhse->bhje', sc, vc)
        den = decay_row * jnp.einsum('bhjd,bhd->bhj', qc, n_prev) + jnp.sum(sc, axis=-1)
        h_out = num / jnp.maximum(jnp.abs(den), jnp.exp(-m_row))[..., None]
        b_last = b[..., -1]
        w_end_log = b_last[..., None] - b + ic
        m_new = jnp.maximum(b_last + m_prev, jnp.max(w_end_log, axis=-1))
        decay = jnp.exp(b_last + m_prev - m_new)
        w_end = jnp.exp(w_end_log - m_new[..., None])
        c_new = decay[..., None, None] * c_prev + jnp.einsum('bhs,bhsd,bhse->bhde', w_end, kc, vc)
        n_new = decay[..., None] * n_prev + jnp.einsum('bhs,bhsd->bhd', w_end, kc)
        return (c_new, n_new, m_new), h_out

    state, h = lax.scan(step, state, tuple(chunks(t) for t in (q, k, v, log_i, log_f)))
    h = jnp.moveaxis(h, 0, 2).reshape(bsz, nh, s, -1)
    return h, state


def token_mixer(u, p, lam_init, pos, ctx):
    f32 = jnp.float32
    bsz, s, _ = u.shape
    z = u @ p['w_in'] + p['b_in']
    (qa, ka, va, qb, kb, vb, ob, ib, fb, qc, kc, vc, gates) = jnp.split(z, IN_SPLIT_POINTS, axis=-1)

    def heads(t, n):
        return t.reshape(bsz, s, n, -1).transpose(0, 2, 1, 3)

    qa = rms_norm(heads(qa, A_HEADS), p['qn_a'])
    ka = rms_norm(heads(ka, A_KV_HEADS), p['kn_a'])
    va = heads(va, A_KV_HEADS)
    qc = heads(qc, C_HEADS)
    kc = heads(kc, C_HEADS)
    vc = heads(vc, C_HEADS)

    if ctx is None:
        keys_a, vals_a, keys_c, vals_c = ka, va, kc, vc
        zc = jnp.zeros((bsz, B_HEADS, B_HEAD_DIM, B_HEAD_DIM), f32)
        zn = jnp.zeros((bsz, B_HEADS, B_HEAD_DIM), f32)
        zm = jnp.zeros((bsz, B_HEADS), f32)
        init_f = (zc, zn, zm)
        init_b = (zc, zn, zm)
    else:
        pos_r, pos_c = pos
        qa = axial_rope(qa, pos_r, pos_c)
        ka = axial_rope(ka, pos_r, pos_c)
        qc = jnp.concatenate([axial_rope(qc[..., :C_HEAD_DIM], pos_r, pos_c),
                              axial_rope(qc[..., C_HEAD_DIM:], pos_r, pos_c)], axis=-1)
        kc = jnp.concatenate([axial_rope(kc[..., :C_HEAD_DIM], pos_r, pos_c),
                              axial_rope(kc[..., C_HEAD_DIM:], pos_r, pos_c)], axis=-1)
        ck_a, cv_a, ck_c, cv_c, st_c, st_n, st_m = ctx
        keys_a = jnp.concatenate([ck_a, ka], axis=-2)
        vals_a = jnp.concatenate([cv_a, va], axis=-2)
        keys_c = jnp.concatenate([ck_c, kc], axis=-2)
        vals_c = jnp.concatenate([cv_c, vc], axis=-2)
        st_c = st_c.astype(f32)
        st_n = st_n.astype(f32)
        st_m = st_m.astype(f32)
        init_f = (st_c[:, 0], st_n[:, 0], st_m[:, 0])
        init_b = (st_c[:, 1], st_n[:, 1], st_m[:, 1])

    g_a = A_HEADS // A_KV_HEADS
    out_a = gqa_attention(qa.reshape(bsz, A_KV_HEADS, g_a, s, A_HEAD_DIM), keys_a, vals_a)
    out_a = out_a.reshape(bsz, A_HEADS, s, A_HEAD_DIM).transpose(0, 2, 1, 3).reshape(bsz, s, W_A_Q)

    q_m = heads(qb, B_HEADS).astype(f32)
    k_m = heads(kb, B_HEADS).astype(f32) * (B_HEAD_DIM ** -0.5)
    v_m = heads(vb, B_HEADS).astype(f32)
    log_i = ib.astype(f32).reshape(bsz, s, 2, B_HEADS).transpose(2, 0, 3, 1)
    log_f = jax.nn.log_sigmoid(fb.astype(f32).reshape(bsz, s, 2, B_HEADS).transpose(2, 0, 3, 1))

    def flip(t):
        return jnp.flip(t, axis=2)

    h_f, s_f = mlstm_direction(q_m, k_m, v_m, log_i[0], log_f[0], init_f)
    h_b, s_b = mlstm_direction(flip(q_m), flip(k_m), flip(v_m), flip(log_i[1]), flip(log_f[1]), init_b)
    h_mem = (h_f + flip(h_b)).transpose(0, 2, 1, 3).astype(u.dtype)
    out_b = rms_norm(h_mem, p['hn_b']).reshape(bsz, s, W_B) * jax.nn.sigmoid(ob)

    lp = p['lam_c'].astype(f32)
    lam = jnp.exp(jnp.sum(lp[0] * lp[1])) - jnp.exp(jnp.sum(lp[2] * lp[3])) + lam_init
    out_c = diff_attention(qc[..., :C_HEAD_DIM], qc[..., C_HEAD_DIM:],
                           keys_c[..., :C_HEAD_DIM], keys_c[..., C_HEAD_DIM:], vals_c, lam)
    out_c = rms_norm(out_c, p['sn_c']) * (1.0 - lam_init)
    out_c = out_c.transpose(0, 2, 1, 3).reshape(bsz, s, W_C_V)

    g = jax.nn.sigmoid(gates).reshape(bsz, s, N_BRANCHES, D_MODEL)
    merged = (g[..., 0, :] * (out_a @ p['w_pa']) + g[..., 1, :] * (out_b @ p['w_pb'])
              + g[..., 2, :] * (out_c @ p['w_pc']))
    y = merged @ p['w_o'] + p['b_o']

    if ctx is None:
        new_c = jnp.stack([s_f[0], s_b[0]], axis=1)
        new_n = jnp.stack([s_f[1], s_b[1]], axis=1)
        new_m = jnp.stack([s_f[2], s_b[2]], axis=1)
        return y, (ka, va, kc, vc, new_c, new_n, new_m)
    return y, None


def moe_ffn(u, p):
    bsz, s, d = u.shape
    x = u.reshape(-1, d)
    logits = (x @ p['w_router'] + p['b_router']).astype(jnp.float32)
    top_v, top_i = lax.top_k(logits, TOP_K)
    top_w = jax.nn.softmax(top_v, axis=-1)
    gate = jnp.einsum('tke,tk->te', jax.nn.one_hot(top_i, N_EXPERTS, dtype=jnp.float32), top_w).astype(x.dtype)
    out = jnp.zeros_like(x)
    for e in range(N_EXPERTS):
        hcat = x @ p['w_up'][e] + p['b_up'][e]
        h_glu = jnp.minimum(hcat[:, :D_EXPERT], SWIGLU_LIMIT)
        h_lin = jnp.clip(hcat[:, D_EXPERT:], -SWIGLU_LIMIT, SWIGLU_LIMIT)
        act = h_glu * jax.nn.sigmoid(SWIGLU_ALPHA * h_glu) * (h_lin + 1.0)
        out = out + gate[:, e:e + 1] * (act @ p['w_down'][e] + p['b_down'][e])
    return out.reshape(bsz, s, d)


def trunk_layer(x, mod, p, lam_init, pos, ctx):
    shift1, scale1, gate1, shift2, scale2, gate2 = (mod[..., i, :] for i in range(N_MOD))
    u = layer_norm(x) * (1.0 + scale1) + shift1
    mix, new_ctx = token_mixer(u, p, lam_init, pos, ctx)
    x = layer_norm(DEEPNORM_ALPHA * x + gate1 * mix) * p['ln1_g'] + p['ln1_b']
    u = layer_norm(x) * (1.0 + scale2) + shift2
    x = layer_norm(DEEPNORM_ALPHA * x + gate2 * moe_ffn(u, p)) * p['ln2_g'] + p['ln2_b']
    return x, new_ctx


def setup_inputs(seed: int = 0) -> dict:
    key = jax.random.key(seed)
    keys = iter(jax.random.split(key, 48))

    def nrm(shape, scale=1.0):
        return jax.random.normal(next(keys), shape, jnp.float32) * scale

    def gain(shape):
        return 1.0 + nrm(shape, 0.02)

    d = D_MODEL
    b_in = nrm((DEPTH, D_IN), 0.02).at[:, FGATE_START:FGATE_START + W_B_GATE].add(FORGET_BIAS)
    return {
        'x_prompt': nrm((BATCH, SEQ, d)),
        'x_sample': nrm((DEC_BATCH, DEC_SEQ, d)),
        'cache_k_gqa': nrm((DEC_BATCH, DEPTH, A_KV_HEADS, PAST_LEN, A_HEAD_DIM)),
        'cache_v_gqa': nrm((DEC_BATCH, DEPTH, A_KV_HEADS, PAST_LEN, A_HEAD_DIM)),
        'cache_k_diff': nrm((DEC_BATCH, DEPTH, C_HEADS, PAST_LEN, 2 * C_HEAD_DIM)),
        'cache_v_diff': nrm((DEC_BATCH, DEPTH, C_HEADS, PAST_LEN, C_V_DIM)),
        'state_mlstm_c': nrm((DEC_BATCH, DEPTH, 2, B_HEADS, B_HEAD_DIM, B_HEAD_DIM), 0.1),
        'state_mlstm_n': nrm((DEC_BATCH, DEPTH, 2, B_HEADS, B_HEAD_DIM), 0.1),
        'state_mlstm_m': nrm((DEC_BATCH, DEPTH, 2, B_HEADS)),
        'c': nrm((DEC_BATCH, d)),
        'c_ctx': nrm((d,)),
        'w_in': nrm((DEPTH, d, D_IN), d ** -0.5),
        'b_in': b_in,
        'qn_a': gain((DEPTH, A_HEAD_DIM)),
        'kn_a': gain((DEPTH, A_HEAD_DIM)),
        'hn_b': gain((DEPTH, B_HEADS, B_HEAD_DIM)),
        'lam_c': nrm((DEPTH, 4, C_HEAD_DIM), 0.1),
        'sn_c': gain((DEPTH, C_V_DIM)),
        'w_pa': nrm((DEPTH, W_A_Q, d), W_A_Q ** -0.5),
        'w_pb': nrm((DEPTH, W_B, d), W_B ** -0.5),
        'w_pc': nrm((DEPTH, W_C_V, d), W_C_V ** -0.5),
        'w_o': nrm((DEPTH, d, d), d ** -0.5 * DEEPNORM_BETA),
        'b_o': nrm((DEPTH, d), 0.02),
        'w_mod': nrm((DEPTH, d, N_MOD * d), 0.5 * d ** -0.5),
        'b_mod': nrm((DEPTH, N_MOD * d), 0.02),
        'ln1_g': gain((DEPTH, d)),
        'ln1_b': nrm((DEPTH, d), 0.02),
        'ln2_g': gain((DEPTH, d)),
        'ln2_b': nrm((DEPTH, d), 0.02),
        'w_router': nrm((DEPTH, d, N_EXPERTS), d ** -0.5),
        'b_router': nrm((DEPTH, N_EXPERTS), 0.01),
        'w_up': nrm((DEPTH, N_EXPERTS, d, 2 * D_EXPERT), d ** -0.5),
        'b_up': nrm((DEPTH, N_EXPERTS, 2 * D_EXPERT), 0.02),
        'w_down': nrm((DEPTH, N_EXPERTS, D_EXPERT, d), D_EXPERT ** -0.5 * DEEPNORM_BETA),
        'b_down': nrm((DEPTH, N_EXPERTS, d), 0.02),
    }


def reference(x_prompt, x_sample, cache_k_gqa, cache_v_gqa, cache_k_diff, cache_v_diff,
              state_mlstm_c, state_mlstm_n, state_mlstm_m, c, c_ctx,
              w_in, b_in, qn_a, kn_a, hn_b, lam_c, sn_c, w_pa, w_pb, w_pc, w_o, b_o,
              w_mod, b_mod, ln1_g, ln1_b, ln2_g, ln2_b, w_router, b_router,
              w_up, b_up, w_down, b_down):
    n_lat = x_sample.shape[1]
    rows = n_lat // GRID_W
    pos_r = jnp.repeat(jnp.arange(rows, dtype=jnp.int32), GRID_W)
    pos_c = jnp.arange(rows * GRID_W, dtype=jnp.int32) % GRID_W

    h_ctx = x_prompt
    h_lat = x_sample
    ctx_outs = []
    for l in range(DEPTH):
        p = dict(w_in=w_in[l], b_in=b_in[l], qn_a=qn_a[l], kn_a=kn_a[l], hn_b=hn_b[l],
                 lam_c=lam_c[l], sn_c=sn_c[l], w_pa=w_pa[l], w_pb=w_pb[l], w_pc=w_pc[l],
                 w_o=w_o[l], b_o=b_o[l], ln1_g=ln1_g[l], ln1_b=ln1_b[l], ln2_g=ln2_g[l],
                 ln2_b=ln2_b[l], w_router=w_router[l], b_router=b_router[l],
                 w_up=w_up[l], b_up=b_up[l], w_down=w_down[l], b_down=b_down[l])
        lam_init = 0.8 - 0.6 * math.exp(-0.3 * l)
        mod_ctx = (jax.nn.silu(c_ctx) @ w_mod[l] + b_mod[l]).reshape(1, 1, N_MOD, D_MODEL)
        mod_lat = (jax.nn.silu(c) @ w_mod[l] + b_mod[l]).reshape(c.shape[0], 1, N_MOD, D_MODEL)
        h_ctx, ctx_l = trunk_layer(h_ctx, mod_ctx, p, lam_init, None, None)
        ctx_outs.append(ctx_l)
        cache_l = (cache_k_gqa[:, l], cache_v_gqa[:, l], cache_k_diff[:, l], cache_v_diff[:, l],
                   state_mlstm_c[:, l], state_mlstm_n[:, l], state_mlstm_m[:, l])
        h_lat, _ = trunk_layer(h_lat, mod_lat, p, lam_init, (pos_r, pos_c), cache_l)

    new_k_gqa = jnp.stack([t[0] for t in ctx_outs], axis=1)
    new_v_gqa = jnp.stack([t[1] for t in ctx_outs], axis=1)
    new_k_diff = jnp.stack([t[2] for t in ctx_outs], axis=1)
    new_v_diff = jnp.stack([t[3] for t in ctx_outs], axis=1)
    new_mlstm_c = jnp.stack([t[4] for t in ctx_outs], axis=1)
    new_mlstm_n = jnp.stack([t[5] for t in ctx_outs], axis=1)
    new_mlstm_m = jnp.stack([t[6] for t in ctx_outs], axis=1)
    return (h_ctx, h_lat, new_k_gqa, new_v_gqa, new_k_diff, new_v_diff, new_mlstm_c, new_mlstm_n, new_mlstm_m)
```

```python
import functools
import math

import numpy as np
import jax
import jax.numpy as jnp
from jax import lax
from jax.experimental import pallas as pl
from jax.experimental.pallas import tpu as pltpu
from jax.experimental.pallas import tpu_sc as plsc

F32 = jnp.float32
BF16 = jnp.bfloat16
I32 = jnp.int32

D_MODEL = 1024
BATCH = 16
SEQ = 256
DEPTH = 2
DEC_BATCH = 2
DEC_SEQ = 4096
PAST_LEN = 512
GRID_W = 64
ROPE_BASE = 10000.0
EPS = 1e-6
A_HEADS = 8
A_KV_HEADS = 2
A_HEAD_DIM = 64
B_HEADS = 4
B_HEAD_DIM = 128
MLSTM_CHUNK = 128
C_HEADS = 4
C_HEAD_DIM = 64
C_V_DIM = 128
N_EXPERTS = 32
TOP_K = 4
D_EXPERT = 1024
SWIGLU_LIMIT = 7.0
SWIGLU_ALPHA = 1.702
N_MOD = 6
DEEPNORM_ALPHA = (2 * DEPTH) ** 0.25

N_CTX = BATCH * SEQ
N_LAT = DEC_BATCH * DEC_SEQ
N_TOK = N_CTX + N_LAT
N_GROUPS = 1 + DEC_BATCH
GROUP_ROWS = 4096
assert N_CTX == GROUP_ROWS and DEC_SEQ == GROUP_ROWS

LANES = 128
TM = 256
TQ = 256
TME = 256
N_TILES = (N_TOK * TOP_K) // TME + N_EXPERTS
N_TILES_PAD = 256
P_ROWS = N_TILES * TME
ROUTE_CHUNK = 512
SC_WINDOW = 32
VMEM_LIMIT = 56 * 1024 * 1024

_W = dict(qa=(0, 512), ka=(512, 640), va=(640, 768), qb=(768, 1280), kb=(1280, 1792), vb=(1792, 2304),
          ob=(2304, 2816), ig=(2816, 2824), fg=(2824, 2832), qc=(2832, 3344), kc=(3344, 3856),
          vc=(3856, 4368), g=(4368, 7440))
QA_HEAD_ORDER = (0, 4, 1, 5, 2, 6, 3, 7)
_QA_COLS = np.concatenate([np.arange(h * 64, (h + 1) * 64) for h in QA_HEAD_ORDER])
_PERM = np.concatenate([_QA_COLS, np.arange(512, 2816), np.arange(2832, 7440), np.arange(2816, 2832)])
D_IN = 7440
D_IN_PAD = 7552
P_QA, P_KA, P_VA, P_QB, P_KB, P_VB, P_OB, P_QC, P_KC, P_VC, P_G, P_IF = (
    0, 512, 640, 768, 1280, 1792, 2304, 2816, 3328, 3840, 4352, 7424)


def _cparams(sem):
    return pltpu.CompilerParams(dimension_semantics=sem, vmem_limit_bytes=VMEM_LIMIT)


def _split3(a):
    a1 = a.astype(BF16)
    r = a - a1.astype(F32)
    a2 = r.astype(BF16)
    a3 = (r - a2.astype(F32)).astype(BF16)
    return a1, a2, a3


def _dot(a, b, dims=None):
    if dims is None:
        return jnp.dot(a, b, preferred_element_type=F32)
    return lax.dot_general(a, b, (dims, ((), ())), preferred_element_type=F32)


def _dot_exact_rhs01(a, m, dims=None):
    return sum(_dot(p, m, dims) for p in _split3(a))


def _dot_exact_lhs01(m, a, dims=None):
    return sum(_dot(m, p, dims) for p in _split3(a))


def _ln(x):
    mu = jnp.mean(x, axis=-1, keepdims=True)
    xc = x - mu
    var = jnp.mean(xc * xc, axis=-1, keepdims=True)
    return xc * lax.rsqrt(var + EPS)


def _log_sigmoid(x):
    return jnp.minimum(x, 0.0) - jnp.log(1.0 + jnp.exp(-jnp.abs(x)))


def _rope(x, cos, sin):
    lane = lax.broadcasted_iota(I32, (1, LANES), 1)
    first = (lane % 32) < 16
    outs = []
    for j in range(x.shape[1] // LANES):
        xj = x[:, j * LANES:(j + 1) * LANES]
        partner = jnp.where(first, pltpu.roll(xj, LANES - 16, 1), pltpu.roll(xj, 16, 1))
        outs.append(xj * cos + partner * sin)
    return outs[0] if len(outs) == 1 else jnp.concatenate(outs, axis=1)


def _mod_kernel(c_ref, w_ref, b_ref, o_ref):
    c = c_ref[...]
    s = c * jax.nn.sigmoid(c)
    o_ref[...] = _dot(s.astype(BF16), w_ref[...].astype(BF16)) + b_ref[...]


def _mod_call(cvecs, w_mod, b_mod):
    tn = 1536
    n = N_MOD * D_MODEL
    return pl.pallas_call(
        _mod_kernel,
        grid=(DEPTH, n // tn),
        in_specs=[pl.BlockSpec((8, D_MODEL), lambda l, j: (0, 0)),
                  pl.BlockSpec((None, D_MODEL, tn), lambda l, j: (l, 0, j)),
                  pl.BlockSpec((None, 1, tn), lambda l, j: (l, 0, j))],
        out_specs=pl.BlockSpec((None, 8, tn), lambda l, j: (l, 0, j)),
        out_shape=jax.ShapeDtypeStruct((DEPTH, 8, n), F32),
        compiler_params=_cparams(("arbitrary", "arbitrary")),
        name="mod",
    )(cvecs, w_mod, b_mod.reshape(DEPTH, 1, n))


def _inproj_kernel(x_ref, mod_ref, cos_ref, sin_ref, w_ref, b_ref, wgt_ref, bgt_ref, gq_ref, gk_ref, gmat_ref,
                   qa_o, kaf_o, kab_o, vaf_o, vab_o, qb_o, kb_o, vb_o, ob_o,
                   qc_o, kcf_o, kcb_o, vcf_o, vcb_o, g_o, ifg_o, ifgt_o):
    x = x_ref[...]
    shift = mod_ref[0, 0:1, :]
    scale = mod_ref[0, 1:2, :]
    ub = (_ln(x) * (1.0 + scale) + shift).astype(BF16)
    cos = cos_ref[...]
    sin = sin_ref[...]

    def proj(lo, n):
        return _dot(ub, w_ref[:, lo:lo + n]) + b_ref[:, lo:lo + n]

    def headnorm(z, gmat, gain):
        ss = _dot((z * z).astype(BF16), gmat)
        return z * lax.rsqrt(ss * (1.0 / A_HEAD_DIM) + EPS) * gain

    z = headnorm(proj(P_QA, 512), gmat_ref[...], gq_ref[...])
    qa_o[...] = (_rope(z, cos, sin) * 0.125).astype(BF16)
    z = _rope(headnorm(proj(P_KA, 128), gmat_ref[0:128, 0:128], gk_ref[...]), cos, sin)
    kaf_o[...] = z
    kab_o[...] = z.astype(BF16)
    z = proj(P_VA, 128)
    vaf_o[...] = z
    vab_o[...] = z.astype(BF16)
    qb_o[...] = proj(P_QB, 512).astype(BF16)
    kb_o[...] = (proj(P_KB, 512) * (B_HEAD_DIM ** -0.5)).astype(BF16)
    vb_o[...] = proj(P_VB, 512).astype(BF16)
    ob_o[...] = proj(P_OB, 512).astype(BF16)
    qc_o[...] = (_rope(proj(P_QC, 512), cos, sin) * 0.125).astype(BF16)
    z = _rope(proj(P_KC, 512), cos, sin)
    kcf_o[...] = z
    kcb_o[...] = z.astype(BF16)
    z = proj(P_VC, 512)
    vcf_o[...] = z
    vcb_o[...] = z.astype(BF16)
    for j in range(3):
        g_o[:, j * 1024:(j + 1) * 1024] = jax.nn.sigmoid(proj(P_G + j * 1024, 1024)).astype(BF16)
    z = proj(P_IF, LANES)
    lane = lax.broadcasted_iota(I32, (1, LANES), 1)
    ifg_o[...] = jnp.where(lane >= 8, _log_sigmoid(z), z)
    zt = _dot(wgt_ref[...], ub, ((1,), (1,))) + bgt_ref[...]
    row = lax.broadcasted_iota(I32, (16, 1), 0)
    ifgt_o[...] = jnp.where(row >= 8, _log_sigmoid(zt), zt)


def _inproj_call(x, mod, cos_t, sin_t, w_p, b_p, w_gt, b_gt, gq, gk, gmat):
    n_ctx_tiles = N_CTX // TM
    n_pos_tiles = DEC_SEQ // TM

    def tbl(i):
        return (jnp.where(i < n_ctx_tiles, 0, 1 + (i - n_ctx_tiles) % n_pos_tiles), 0)

    row = lambda n: pl.BlockSpec((TM, n), lambda i: (i, 0))
    const = lambda shape: pl.BlockSpec(shape, lambda i: (0,) * len(shape), pipeline_mode=pl.Buffered(1))
    sds = lambda n, dt: jax.ShapeDtypeStruct((N_TOK, n), dt)
    out_defs = [(512, BF16), (128, F32), (128, BF16), (128, F32), (128, BF16),
                (512, BF16), (512, BF16), (512, BF16), (512, BF16),
                (512, BF16), (512, F32), (512, BF16), (512, F32), (512, BF16),
                (3072, BF16), (LANES, F32)]
    return pl.pallas_call(
        _inproj_kernel,
        grid=(N_TOK // TM,),
        in_specs=[row(D_MODEL),
                  pl.BlockSpec((1, N_MOD, D_MODEL), lambda i: (i * TM // GROUP_ROWS, 0, 0)),
                  pl.BlockSpec((TM, LANES), tbl), pl.BlockSpec((TM, LANES), tbl),
                  const((D_MODEL, D_IN_PAD)), const((1, D_IN_PAD)),
                  const((16, D_MODEL)), const((16, 1)),
                  const((1, 512)), const((1, 128)), const((512, 512))],
        out_specs=[row(n) for n, _ in out_defs] + [pl.BlockSpec((16, TM), lambda i: (0, i))],
        out_shape=[sds(n, dt) for n, dt in out_defs] + [jax.ShapeDtypeStruct((16, N_TOK), F32)],
        compiler_params=_cparams(("arbitrary",)),
        name="inproj",
    )(x, mod, cos_t, sin_t, w_p, b_p, w_gt, b_gt, gq, gk, gmat)


def _attn_kernel(mode, nseg, lam_init, *refs):
    q_ref = refs[0]
    kv = refs[1:1 + 2 * nseg]
    rest = refs[1 + 2 * nseg:]
    o_ref = rest[-1]
    q = q_ref[...]
    tq = q.shape[0]
    lane = lax.broadcasted_iota(I32, (1, LANES), 1)
    zero = jnp.zeros_like(q)
    qq = jnp.concatenate([jnp.where(lane < 64, q, zero), jnp.where(lane >= 64, q, zero)], axis=0)
    ss = [_dot(qq, kv[2 * j][...], ((1,), (1,))) for j in range(nseg)]
    m = functools.reduce(jnp.maximum, [jnp.max(s, axis=-1, keepdims=True) for s in ss])
    es = [jnp.exp(s - m) for s in ss]
    l = sum(jnp.sum(e, axis=-1, keepdims=True) for e in es)
    inv = 1.0 / l
    if mode == "a":
        o = sum(_dot(es[j].astype(BF16), kv[2 * j + 1][...]) for j in range(nseg)) * inv
        out = jnp.where(lane < 64, o[:tq], o[tq:])
    else:
        lam_ref, sn_ref = rest[0], rest[1]
        lp = lam_ref[...]
        lam = (jnp.exp(jnp.sum(lp[0:1] * lp[1:2], axis=-1, keepdims=True))
               - jnp.exp(jnp.sum(lp[2:3] * lp[3:4], axis=-1, keepdims=True)) + lam_init)
        o = sum(_dot((es[j][:tq] * inv[:tq] - lam * (es[j][tq:] * inv[tq:])).astype(BF16), kv[2 * j + 1][...])
                for j in range(nseg))
        ms = jnp.mean(o * o, axis=-1, keepdims=True)
        out = o * lax.rsqrt(ms + EPS) * sn_ref[...] * (1.0 - lam_init)
    o_ref[...] = out.astype(o_ref.dtype)


def _attn_call(mode, lat, lam_init, q_all, k_all, v_all, k_cache, v_cache, lam_c, sn_c, name):
    wide = k_all.shape[1] == 512
    kcol = (lambda h: h) if wide else (lambda h: 0)
    if lat:
        nq = DEC_SEQ // TQ
        base_q = N_CTX // TQ
        grid = (DEC_BATCH, nq, 4)
        q_spec = pl.BlockSpec((TQ, LANES), lambda b, i, h: (base_q + b * nq + i, h))
        kn_spec = pl.BlockSpec((DEC_SEQ, LANES), lambda b, i, h: (1 + b, kcol(h)))
        if wide:
            kc_spec = pl.BlockSpec((None, None, PAST_LEN, LANES), lambda b, i, h: (b, h, 0, 0))
        else:
            kc_spec = pl.BlockSpec((None, PAST_LEN, LANES), lambda b, i, h: (b, 0, 0))
        in_specs = [q_spec, kc_spec, kc_spec, kn_spec, kn_spec]
        args = [q_all, k_cache, v_cache, k_all, v_all]
        nseg = 2
        n_out = N_LAT
        o_spec = pl.BlockSpec((TQ, LANES), lambda b, i, h: (b * nq + i, h))
        sem = ("arbitrary", "arbitrary", "arbitrary")
    else:
        grid = (BATCH, 4)
        q_spec = pl.BlockSpec((SEQ, LANES), lambda b, h: (b, h))
        k_spec = pl.BlockSpec((SEQ, LANES), lambda b, h: (b, kcol(h)))
        in_specs = [q_spec, k_spec, k_spec]
        args = [q_all, k_all, v_all]
        nseg = 1
        n_out = N_CTX
        o_spec = pl.BlockSpec((SEQ, LANES), lambda b, h: (b, h))
        sem = ("arbitrary", "arbitrary")
    if mode == "c":
        zero_map = (lambda b, i, h: (0, 0)) if lat else (lambda b, h: (0, 0))
        in_specs += [pl.BlockSpec((4, C_HEAD_DIM), zero_map), pl.BlockSpec((1, C_V_DIM), zero_map)]
        args += [lam_c, sn_c]
    return pl.pallas_call(
        functools.partial(_attn_kernel, mode, nseg, lam_init),
        grid=grid, in_specs=in_specs, out_specs=o_spec,
        out_shape=jax.ShapeDtypeStruct((n_out, 512), BF16),
        compiler_params=_cparams(sem), name=name,
    )(*args)


def _mlstm_kernel(has_init, want_state, *refs):
    q_ref, k_ref, v_ref, g_ref, gt_ref = refs[:5]
    pos = 5
    if has_init:
        cn0_ref, m0_ref = refs[pos:pos + 2]
        pos += 2
    h_ref = refs[pos]
    pos += 1
    if want_state:
        cn_o, m_o = refs[pos:pos + 2]
        pos += 2
    cn_s, m_s = refs[pos:pos + 2]

    d = pl.program_id(0)
    c = pl.program_id(2)
    last = pl.num_programs(2) - 1
    L = MLSTM_CHUNK

    @pl.when(c == 0)
    def _():
        if has_init:
            cn_s[...] = cn0_ref[...]
            m_s[...] = m0_ref[...]
        else:
            cn_s[...] = jnp.zeros_like(cn_s)
            m_s[...] = jnp.zeros_like(m_s)

    fwd = d == 0
    row = lax.broadcasted_iota(I32, (L, L), 0)
    col = lax.broadcasted_iota(I32, (L, L), 1)
    sgn = 1 - 2 * d
    keep = (row - col) * sgn >= 0
    m_c = keep.astype(BF16)
    m_r = ((col - row) * sgn >= 0).astype(BF16)
    g = g_ref[...]
    gt = gt_ref[...]
    bc = _dot_exact_lhs01(m_c, g)
    br = _dot_exact_rhs01(gt, m_r)
    lane = lax.broadcasted_iota(I32, (1, LANES), 1)
    ones_col = jnp.where(lane == 0, 1.0, 0.0).astype(BF16)

    def pick_col(a, off, h):
        return jnp.where(fwd, a[:, off + h:off + h + 1], a[:, off + 4 + h:off + 5 + h])

    def pick_row(a, off, h):
        return jnp.where(fwd, a[off + h:off + h + 1, :], a[off + 4 + h:off + 5 + h, :])

    for h in range(B_HEADS):
        sl = slice(h * L, (h + 1) * L)
        qh, kh, vh = q_ref[:, sl], k_ref[:, sl], v_ref[:, sl]
        i_col, f_col, b_col = pick_col(g, 0, h), pick_col(g, 8, h), pick_col(bc, 8, h)
        i_row, b_row = pick_row(gt, 0, h), pick_row(br, 8, h)
        m_prev = m_s[h][:, 0:1]
        total = jnp.sum(f_col, axis=0, keepdims=True)
        inter = b_col + m_prev
        dmat = jnp.where(keep, b_col + (i_row - b_row), -jnp.inf)
        m_row = jnp.maximum(inter, jnp.max(dmat, axis=-1, keepdims=True))
        w = jnp.exp(dmat - m_row)
        sc = (_dot(qh, kh, ((1,), (1,))) * w).astype(BF16)
        cn_prev = cn_s[h]
        v_aug = jnp.concatenate([vh, jnp.broadcast_to(ones_col, (L, LANES))], axis=1)
        nd = jnp.exp(inter - m_row) * _dot(qh, cn_prev.astype(BF16)) + _dot(sc, v_aug)
        den = nd[:, L:L + 1]
        h_ref[:, sl] = nd[:, :L] / jnp.maximum(jnp.abs(den), jnp.exp(-m_row))
        wel = total - b_col + i_col
        m_new = jnp.maximum(total + m_prev, jnp.max(wel, axis=0, keepdims=True))
        wv = (jnp.exp(wel - m_new) * v_aug.astype(F32)).astype(BF16)
        cn_s[h] = jnp.exp(total + m_prev - m_new) * cn_prev + _dot(kh, wv, ((0,), (0,)))
        m_s[h] = jnp.broadcast_to(m_new, (1, LANES))

    if want_state:
        @pl.when(c == last)
        def _():
            cn_o[...] = cn_s[...]
            m_o[...] = m_s[...]


def _mlstm_call(lat, q_all, k_all, v_all, g_all, gt_all, cn0, m0, name):
    L = MLSTM_CHUNK
    nb, nc = (DEC_BATCH, DEC_SEQ // L) if lat else (BATCH, SEQ // L)
    base = N_CTX // L if lat else 0
    n_rows = N_LAT if lat else N_CTX

    def chunk(d, b, c):
        return b * nc + c + d * (nc - 1 - 2 * c)

    tok = lambda n: pl.BlockSpec((L, n), lambda d, b, c: (base + chunk(d, b, c), 0))
    in_specs = [tok(512), tok(512), tok(512), tok(LANES),
                pl.BlockSpec((16, L), lambda d, b, c: (0, base + chunk(d, b, c)))]
    args = [q_all, k_all, v_all, g_all, gt_all]
    if lat:
        in_specs += [pl.BlockSpec((None, None, B_HEADS, L, 2 * L), lambda d, b, c: (b, d, 0, 0, 0)),
                     pl.BlockSpec((None, None, B_HEADS, 1, LANES), lambda d, b, c: (b, d, 0, 0, 0))]
        args += [cn0, m0]
    out_specs = [pl.BlockSpec((None, L, 512), lambda d, b, c: (d, chunk(d, b, c), 0))]
    out_shape = [jax.ShapeDtypeStruct((2, n_rows, 512), F32)]
    if not lat:
        out_specs += [pl.BlockSpec((None, None, B_HEADS, L, 2 * L), lambda d, b, c: (d, b, 0, 0, 0)),
                      pl.BlockSpec((None, None, B_HEADS, 1, LANES), lambda d, b, c: (d, b, 0, 0, 0))]
        out_shape += [jax.ShapeDtypeStruct((2, nb, B_HEADS, L, 2 * L), F32),
                      jax.ShapeDtypeStruct((2, nb, B_HEADS, 1, LANES), F32)]
    return pl.pallas_call(
        functools.partial(_mlstm_kernel, lat, not lat),
        grid=(2, nb, nc), in_specs=in_specs, out_specs=out_specs, out_shape=out_shape,
        scratch_shapes=[pltpu.VMEM((B_HEADS, L, 2 * L), F32), pltpu.VMEM((B_HEADS, 1, LANES), F32)],
        compiler_params=_cparams(("arbitrary", "arbitrary", "arbitrary")), name=name,
    )(*args)


def _merge_kernel(x_ref, mod_ref, oa_ref, h_ref, ob_ref, oc_ref, g_ref, hn_ref,
                  wpa_ref, wpb_ref, wpc_ref, wo_ref, bo_ref, ln_g_ref, ln_b_ref, wr_ref, br_ref,
                  x1_o, u2_o, ti_o, tw_o):
    hm = h_ref[0] + h_ref[1]
    parts = []
    for h in range(B_HEADS):
        hh = hm[:, h * LANES:(h + 1) * LANES]
        parts.append(hh * lax.rsqrt(jnp.mean(hh * hh, axis=-1, keepdims=True) + EPS))
    out_b = jnp.concatenate(parts, axis=1) * hn_ref[...] * jax.nn.sigmoid(ob_ref[...].astype(F32))
    merged = (g_ref[:, 0:1024].astype(F32) * _dot(oa_ref[...], wpa_ref[...])
              + g_ref[:, 1024:2048].astype(F32) * _dot(out_b.astype(BF16), wpb_ref[...])
              + g_ref[:, 2048:3072].astype(F32) * _dot(oc_ref[...], wpc_ref[...]))
    y = _dot(merged.astype(BF16), wo_ref[...]) + bo_ref[...]
    gate1 = mod_ref[0, 2:3, :]
    shift2 = mod_ref[0, 3:4, :]
    scale2 = mod_ref[0, 4:5, :]
    x1 = _ln(DEEPNORM_ALPHA * x_ref[...] + gate1 * y) * ln_g_ref[...] + ln_b_ref[...]
    x1_o[...] = x1
    u2 = _ln(x1) * (1.0 + scale2) + shift2
    u2_o[...] = u2
    wr = _split3(wr_ref[...])
    up = _split3(u2)
    nt = ((1,), (1,))
    logits = (_dot(wr[0], up[0], nt) + _dot(wr[0], up[1], nt) + _dot(wr[1], up[0], nt)
              + _dot(wr[0], up[2], nt) + _dot(wr[2], up[0], nt) + _dot(wr[1], up[1], nt)) + br_ref[...]
    eio = lax.broadcasted_iota(I32, (N_EXPERTS, 1), 0)
    cur = logits
    vals, idxs = [], []
    for _ in range(TOP_K):
        mx = jnp.max(cur, axis=0, keepdims=True)
        ix = jnp.min(jnp.where(cur == mx, eio, N_EXPERTS), axis=0, keepdims=True)
        vals.append(mx)
        idxs.append(ix)
        cur = jnp.where(eio == ix, -jnp.inf, cur)
    ex = [jnp.exp(v - vals[0]) for v in vals]
    inv = 1.0 / sum(ex)
    ti_o[...] = jnp.concatenate(idxs, axis=0)
    tw_o[...] = jnp.concatenate([e * inv for e in ex], axis=0)


def _merge_call(x, mod, out_a, h, ob, out_c, g, hn, wpa, wpb, wpc, wo, bo, ln_g, ln_b, wr_t, br):
    row = lambda n: pl.BlockSpec((TM, n), lambda i: (i, 0))
    const = lambda shape: pl.BlockSpec(shape, lambda i: (0,) * len(shape), pipeline_mode=pl.Buffered(1))
    return pl.pallas_call(
        _merge_kernel,
        grid=(N_TOK // TM,),
        in_specs=[row(D_MODEL),
                  pl.BlockSpec((1, N_MOD, D_MODEL), lambda i: (i * TM // GROUP_ROWS, 0, 0)),
                  row(512), pl.BlockSpec((2, TM, 512), lambda i: (0, i, 0)), row(512), row(512), row(3072),
                  const((1, 512)), const((512, D_MODEL)), const((512, D_MODEL)), const((512, D_MODEL)),
                  const((D_MODEL, D_MODEL)), const((1, D_MODEL)), const((1, D_MODEL)), const((1, D_MODEL)),
                  const((N_EXPERTS, D_MODEL)), const((N_EXPERTS, 1))],
        out_specs=[row(D_MODEL), row(D_MODEL),
                   pl.BlockSpec((TOP_K, TM), lambda i: (0, i)), pl.BlockSpec((TOP_K, TM), lambda i: (0, i))],
        out_shape=[jax.ShapeDtypeStruct((N_TOK, D_MODEL), F32), jax.ShapeDtypeStruct((N_TOK, D_MODEL), F32),
                   jax.ShapeDtypeStruct((TOP_K, N_TOK), I32), jax.ShapeDtypeStruct((TOP_K, N_TOK), F32)],
        compiler_params=_cparams(("arbitrary",)), name="merge",
    )(x, mod, out_a, h, ob, out_c, g, hn, wpa, wpb, wpc, wo, bo, ln_g, ln_b, wr_t, br)


def _route_kernel(ti_ref, pos_o, te_o, tv_o, nt_o, rank_s):
    ch = ROUTE_CHUNK
    eio = lax.broadcasted_iota(I32, (N_EXPERTS, 1), 0)
    tri = (lax.broadcasted_iota(I32, (ch, ch), 0) <= lax.broadcasted_iota(I32, (ch, ch), 1)).astype(BF16)

    def onehots(off):
        ti = ti_ref[:, pl.ds(off, ch)]
        return [(eio == ti[k:k + 1, :]).astype(F32) for k in range(TOP_K)]

    def count_body(c, carry):
        off = pl.multiple_of(c * ch, ch)
        ohs = onehots(off)
        oh = ohs[0] + ohs[1] + ohs[2] + ohs[3]
        incl = _dot(oh.astype(BF16), tri)
        excl = incl - oh + carry
        for k in range(TOP_K):
            rank_s[k:k + 1, pl.ds(off, ch)] = jnp.sum(ohs[k] * excl, axis=0, keepdims=True)
        return carry + incl[:, ch - 1:ch]

    counts = lax.fori_loop(0, N_TOK // ch, count_body, jnp.zeros((N_EXPERTS, 1), F32))
    ntile_e = jnp.floor((counts + (TME - 1)) * (1.0 / TME))
    lower = (lax.broadcasted_iota(I32, (N_EXPERTS, N_EXPERTS), 1)
             < lax.broadcasted_iota(I32, (N_EXPERTS, N_EXPERTS), 0)).astype(BF16)
    off_t = _dot(lower, jnp.broadcast_to(ntile_e, (N_EXPERTS, LANES)).astype(BF16))[:, 0:1]
    end_t = off_t + ntile_e
    total = jnp.sum(ntile_e, axis=0, keepdims=True)
    jt = lax.broadcasted_iota(I32, (1, N_TILES_PAD), 1).astype(F32)
    te = jnp.minimum(jnp.sum((end_t <= jt).astype(F32), axis=0, keepdims=True), N_EXPERTS - 1.0)
    sel = (eio.astype(F32) == te).astype(F32)
    cnt_j = jnp.sum(sel * counts, axis=0, keepdims=True)
    off_j = jnp.sum(sel * off_t, axis=0, keepdims=True)
    valid = jnp.clip(cnt_j - (jt - off_j) * TME, 0.0, float(TME))
    te_o[...] = te.astype(I32)
    tv_o[...] = jnp.where(jt < total, valid, 0.0).astype(I32)
    nt_o[...] = jnp.broadcast_to(total, (1, LANES)).astype(I32)
    off_rows = off_t * TME

    def pos_body(c, carry):
        off = pl.multiple_of(c * ch, ch)
        ohs = onehots(off)
        for k in range(TOP_K):
            base = jnp.sum(ohs[k] * off_rows, axis=0, keepdims=True)
            pos_o[k:k + 1, pl.ds(off, ch)] = (rank_s[k:k + 1, pl.ds(off, ch)] + base).astype(I32)
        return carry

    lax.fori_loop(0, N_TOK // ch, pos_body, 0)


def _route_call(top_i):
    return pl.pallas_call(
        _route_kernel,
        out_shape=[jax.ShapeDtypeStruct((TOP_K, N_TOK), I32), jax.ShapeDtypeStruct((1, N_TILES_PAD), I32),
                   jax.ShapeDtypeStruct((1, N_TILES_PAD), I32), jax.ShapeDtypeStruct((1, LANES), I32)],
        scratch_shapes=[pltpu.VMEM((TOP_K, N_TOK), F32)],
        compiler_params=pltpu.CompilerParams(vmem_limit_bytes=VMEM_LIMIT), name="route",
    )(top_i)


def _sc_mesh():
    return plsc.VectorSubcoreMesh(core_axis_name="c", subcore_axis_name="s")


def _sc_worker(n_rows):
    info = plsc.get_sparse_core_info()
    n_workers = info.num_cores * info.num_subcores
    per_w = n_rows // n_workers
    assert per_w * n_workers == n_rows and per_w % SC_WINDOW == 0
    wid = lax.axis_index("s") * info.num_cores + lax.axis_index("c")
    return wid * per_w, per_w


def _dispatch_call(u2, pos_flat):
    w = SC_WINDOW

    @functools.partial(pl.kernel, out_type=jax.ShapeDtypeStruct((P_ROWS, D_MODEL), u2.dtype), mesh=_sc_mesh(),
                       scratch_types=[pltpu.VMEM((w,), I32)] * TOP_K
                       + [pltpu.VMEM((w, D_MODEL), u2.dtype), pltpu.SemaphoreType.DMA])
    def k(x_hbm, p_hbm, o_hbm, i0, i1, i2, i3, rows_v, sem):
        base, per_w = _sc_worker(N_TOK)
        idx = (i0, i1, i2, i3)

        @pl.loop(0, per_w // w)
        def _(c):
            off = pl.multiple_of(base + c * w, 8)
            pltpu.sync_copy(x_hbm.at[pl.ds(off, w)], rows_v)
            for kk in range(TOP_K):
                pltpu.sync_copy(p_hbm.at[pl.ds(pl.multiple_of(kk * N_TOK + off, 8), w)], idx[kk])
            copies = [pltpu.async_copy(rows_v, o_hbm.at[idx[kk]], sem) for kk in range(TOP_K)]
            for cp in copies:
                cp.wait()

    return k(u2, pos_flat)


def _combine_call(ys, pos_flat):
    w = SC_WINDOW
    n = pos_flat.shape[0]

    @functools.partial(pl.kernel, out_type=jax.ShapeDtypeStruct((n, D_MODEL), ys.dtype), mesh=_sc_mesh(),
                       scratch_types=[pltpu.VMEM((w,), I32), pltpu.VMEM((w, D_MODEL), ys.dtype),
                                      pltpu.SemaphoreType.DMA])
    def k(y_hbm, p_hbm, o_hbm, idx_v, rows_v, sem):
        base, per_w = _sc_worker(n)

        @pl.loop(0, per_w // w)
        def _(c):
            off = pl.multiple_of(base + c * w, 8)
            pltpu.sync_copy(p_hbm.at[pl.ds(off, w)], idx_v)
            pltpu.async_copy(y_hbm.at[idx_v], rows_v, sem).wait()
            pltpu.sync_copy(rows_v, o_hbm.at[pl.ds(off, w)])

    return k(ys, pos_flat)


def _moe_kernel(te_ref, tv_ref, nt_ref, x_ref, wu_ref, bu_ref, wd_ref, bd_ref, y_ref, wu_s, wd_s):
    i = pl.program_id(0)

    @pl.when(i < nt_ref[0])
    def _():
        e = te_ref[i]
        prev = te_ref[jnp.maximum(i - 1, 0)]

        @pl.when((i == 0) | (e != prev))
        def _():
            wu_s[...] = wu_ref[...].astype(BF16)
            wd_s[...] = wd_ref[...].astype(BF16)

        rows = lax.broadcasted_iota(I32, (TME, 1), 0)
        x = jnp.where(rows < tv_ref[i], x_ref[...], 0.0).astype(BF16)
        h = _dot(x, wu_s[...]) + bu_ref[...]
        h_glu = jnp.minimum(h[:, :D_EXPERT], SWIGLU_LIMIT)
        h_lin = jnp.clip(h[:, D_EXPERT:], -SWIGLU_LIMIT, SWIGLU_LIMIT)
        act = h_glu * jax.nn.sigmoid(SWIGLU_ALPHA * h_glu) * (h_lin + 1.0)
        y_ref[...] = _dot(act.astype(BF16), wd_s[...]) + bd_ref[...]


def _moe_call(tile_e, tile_v, n_tiles, xs, w_up, b_up, w_down, b_down):
    def tile(i, te, tv, nt):
        return jnp.minimum(i, nt[0] - 1)

    grid_spec = pltpu.PrefetchScalarGridSpec(
        num_scalar_prefetch=3,
        grid=(N_TILES,),
        in_specs=[pl.BlockSpec((TME, D_MODEL), lambda i, te, tv, nt: (tile(i, te, tv, nt), 0)),
                  pl.BlockSpec((None, D_MODEL, 2 * D_EXPERT), lambda i, te, tv, nt: (te[tile(i, te, tv, nt)], 0, 0)),
                  pl.BlockSpec((None, 1, 2 * D_EXPERT), lambda i, te, tv, nt: (te[tile(i, te, tv, nt)], 0, 0)),
                  pl.BlockSpec((None, D_EXPERT, D_MODEL), lambda i, te, tv, nt: (te[tile(i, te, tv, nt)], 0, 0)),
                  pl.BlockSpec((None, 1, D_MODEL), lambda i, te, tv, nt: (te[tile(i, te, tv, nt)], 0, 0))],
        out_specs=pl.BlockSpec((TME, D_MODEL), lambda i, te, tv, nt: (tile(i, te, tv, nt), 0)),
        scratch_shapes=[pltpu.VMEM((D_MODEL, 2 * D_EXPERT), BF16), pltpu.VMEM((D_EXPERT, D_MODEL), BF16)],
    )
    return pl.pallas_call(
        _moe_kernel, grid_spec=grid_spec,
        out_shape=jax.ShapeDtypeStruct((P_ROWS, D_MODEL), F32),
        compiler_params=_cparams(("arbitrary",)), name="moe",
    )(tile_e, tile_v, n_tiles, xs, w_up, b_up.reshape(N_EXPERTS, 1, -1), w_down, b_down.reshape(N_EXPERTS, 1, -1))


def _final_kernel(yg_ref, tw_ref, x1_ref, mod_ref, ln_g_ref, ln_b_ref, o_ref):
    tw = tw_ref[...]
    moe = sum(tw[:, k:k + 1] * yg_ref[k] for k in range(TOP_K))
    gate2 = mod_ref[0, 5:6, :]
    o_ref[...] = _ln(DEEPNORM_ALPHA * x1_ref[...] + gate2 * moe) * ln_g_ref[...] + ln_b_ref[...]


def _final_call(yg, tw, x1, mod, ln_g, ln_b):
    row = lambda n: pl.BlockSpec((TM, n), lambda i: (i, 0))
    const = lambda shape: pl.BlockSpec(shape, lambda i: (0,) * len(shape))
    return pl.pallas_call(
        _final_kernel,
        grid=(N_TOK // TM,),
        in_specs=[pl.BlockSpec((TOP_K, TM, D_MODEL), lambda i: (0, i, 0)), row(TOP_K), row(D_MODEL),
                  pl.BlockSpec((1, N_MOD, D_MODEL), lambda i: (i * TM // GROUP_ROWS, 0, 0)),
                  const((1, D_MODEL)), const((1, D_MODEL))],
        out_specs=row(D_MODEL),
        out_shape=jax.ShapeDtypeStruct((N_TOK, D_MODEL), F32),
        compiler_params=_cparams(("arbitrary",)), name="final",
    )(yg, tw, x1, mod, ln_g, ln_b)


def _rope_tables():
    half = A_HEAD_DIM // 2
    freqs = ROPE_BASE ** (-jnp.arange(0, half, 2, dtype=F32) / half)
    t = jnp.arange(DEC_SEQ, dtype=jnp.int32)
    ang_r = (t // GRID_W).astype(F32)[:, None] * freqs[None, :]
    ang_c = (t % GRID_W).astype(F32)[:, None] * freqs[None, :]
    cos64 = jnp.concatenate([jnp.cos(ang_r)] * 2 + [jnp.cos(ang_c)] * 2, axis=1)
    sin64 = jnp.concatenate([-jnp.sin(ang_r), jnp.sin(ang_r), -jnp.sin(ang_c), jnp.sin(ang_c)], axis=1)
    cos_t = jnp.concatenate([jnp.ones((TM, LANES), F32), jnp.tile(cos64, (1, 2))], axis=0)
    sin_t = jnp.concatenate([jnp.zeros((TM, LANES), F32), jnp.tile(sin64, (1, 2))], axis=0)
    return cos_t, sin_t


def _layer(l, x, mod, tables, caches, p):
    cos_t, sin_t = tables
    ck_a, cv_a, ck_c, cv_c, cn0, m0 = caches
    lam_init = 0.8 - 0.6 * math.exp(-0.3 * l)
    w_in, b_in = p["w_in"], p["b_in"]
    w_p = jnp.pad(w_in[:, _PERM], ((0, 0), (0, D_IN_PAD - D_IN))).astype(BF16)
    b_p = jnp.pad(b_in[_PERM], (0, D_IN_PAD - D_IN)).reshape(1, D_IN_PAD)
    w_gt = w_in[:, 2816:2832].T.astype(BF16)
    b_gt = b_in[2816:2832].reshape(16, 1)
    gq = jnp.tile(p["qn_a"], A_HEADS).reshape(1, 512)
    gk = jnp.tile(p["kn_a"], A_KV_HEADS).reshape(1, 128)
    gmat = jnp.asarray(np.kron(np.eye(8), np.ones((64, 64))), BF16)

    (qa, kaf, kab, vaf, vab, qb, kb, vb, ob, qc, kcf, kcb, vcf, vcb, g, ifg, ifgt) = _inproj_call(
        x, mod, cos_t, sin_t, w_p, b_p, w_gt, b_gt, gq, gk, gmat)

    lam_c = p["lam_c"]
    sn_c = p["sn_c"].reshape(1, C_V_DIM)
    oa = jnp.concatenate([
        _attn_call("a", False, lam_init, qa, kab, vab, None, None, None, None, "attn_a_ctx"),
        _attn_call("a", True, lam_init, qa, kab, vab, ck_a, cv_a, None, None, "attn_a_lat")], axis=0)
    oc = jnp.concatenate([
        _attn_call("c", False, lam_init, qc, kcb, vcb, None, None, lam_c, sn_c, "attn_c_ctx"),
        _attn_call("c", True, lam_init, qc, kcb, vcb, ck_c, cv_c, lam_c, sn_c, "attn_c_lat")], axis=0)
    h_ctx, cn_fin, m_fin = _mlstm_call(False, qb, kb, vb, ifg, ifgt, None, None, "mlstm_ctx")
    (h_lat,) = _mlstm_call(True, qb, kb, vb, ifg, ifgt, cn0, m0, "mlstm_lat")
    h = jnp.concatenate([h_ctx, h_lat], axis=1)

    wpa = p["w_pa"].reshape(A_HEADS, A_HEAD_DIM, D_MODEL)[np.asarray(QA_HEAD_ORDER)].reshape(512, D_MODEL)
    x1, u2, top_i, top_w = _merge_call(
        x, mod, oa, h, ob, oc, g, p["hn_b"].reshape(1, 512),
        wpa.astype(BF16), p["w_pb"].astype(BF16), p["w_pc"].astype(BF16), p["w_o"].astype(BF16),
        p["b_o"].reshape(1, D_MODEL), p["ln1_g"].reshape(1, D_MODEL), p["ln1_b"].reshape(1, D_MODEL),
        p["w_router"].T, p["b_router"].reshape(N_EXPERTS, 1))

    pos, tile_e, tile_v, n_tiles = _route_call(top_i)
    pos_flat = pos.reshape(TOP_K * N_TOK)
    xs = _dispatch_call(u2, pos_flat)
    ys = _moe_call(tile_e[0, :N_TILES], tile_v[0, :N_TILES], n_tiles[0, :1], xs,
                   p["w_up"], p["b_up"], p["w_down"], p["b_down"])
    yg = _combine_call(ys, pos_flat).reshape(TOP_K, N_TOK, D_MODEL)
    x2 = _final_call(yg, top_w.T, x1, mod, p["ln2_g"].reshape(1, D_MODEL), p["ln2_b"].reshape(1, D_MODEL))

    def heads(t, n):
        return t[:N_CTX].reshape(BATCH, SEQ, n, -1).transpose(0, 2, 1, 3)

    ctx_out = (heads(kaf, A_KV_HEADS), heads(vaf, A_KV_HEADS), heads(kcf, C_HEADS), heads(vcf, C_HEADS),
               cn_fin[..., :B_HEAD_DIM].transpose(1, 0, 2, 3, 4),
               cn_fin[..., B_HEAD_DIM].transpose(1, 0, 2, 3),
               m_fin[:, :, :, 0, 0].transpose(1, 0, 2))
    return x2, ctx_out


def kernel(x_prompt, x_sample, cache_k_gqa, cache_v_gqa, cache_k_diff, cache_v_diff, state_mlstm_c, state_mlstm_n, state_mlstm_m, c, c_ctx, w_in, b_in, qn_a, kn_a, hn_b, lam_c, sn_c, w_pa, w_pb, w_pc, w_o, b_o, w_mod, b_mod, ln1_g, ln1_b, ln2_g, ln2_b, w_router, b_router, w_up, b_up, w_down, b_down):
    x = jnp.concatenate([x_prompt.reshape(N_CTX, D_MODEL), x_sample.reshape(N_LAT, D_MODEL)], axis=0)
    cvecs = jnp.zeros((8, D_MODEL), F32).at[0].set(c_ctx).at[1:1 + DEC_BATCH].set(c)
    mod_all = _mod_call(cvecs, w_mod, b_mod).reshape(DEPTH, 8, N_MOD, D_MODEL)[:, :N_GROUPS]
    tables = _rope_tables()
    params = dict(w_in=w_in, b_in=b_in, qn_a=qn_a, kn_a=kn_a, hn_b=hn_b, lam_c=lam_c, sn_c=sn_c, w_pa=w_pa,
                  w_pb=w_pb, w_pc=w_pc, w_o=w_o, b_o=b_o, ln1_g=ln1_g, ln1_b=ln1_b, ln2_g=ln2_g, ln2_b=ln2_b,
                  w_router=w_router, b_router=b_router, w_up=w_up, b_up=b_up, w_down=w_down, b_down=b_down)
    ctx_outs = []
    for l in range(DEPTH):
        p = {k: v[l] for k, v in params.items()}
        ck_a = cache_k_gqa[:, l].transpose(0, 2, 1, 3).reshape(DEC_BATCH, PAST_LEN, 128).astype(BF16)
        cv_a = cache_v_gqa[:, l].transpose(0, 2, 1, 3).reshape(DEC_BATCH, PAST_LEN, 128).astype(BF16)
        ck_c = cache_k_diff[:, l].astype(BF16)
        cv_c = cache_v_diff[:, l].astype(BF16)
        cn0 = jnp.concatenate([state_mlstm_c[:, l], state_mlstm_n[:, l][..., None],
                               jnp.zeros((DEC_BATCH, 2, B_HEADS, B_HEAD_DIM, B_HEAD_DIM - 1), F32)], axis=-1)
        m0 = jnp.broadcast_to(state_mlstm_m[:, l][..., None, None], (DEC_BATCH, 2, B_HEADS, 1, LANES))
        x, ctx_l = _layer(l, x, mod_all[l], tables, (ck_a, cv_a, ck_c, cv_c, cn0, m0), p)
        ctx_outs.append(ctx_l)
    stack = lambda j: jnp.stack([t[j] for t in ctx_outs], axis=1)
    return (x[:N_CTX].reshape(BATCH, SEQ, D_MODEL), x[N_CTX:].reshape(DEC_BATCH, DEC_SEQ, D_MODEL),
            stack(0), stack(1), stack(2), stack(3), stack(4), stack(5), stack(6))
```

```python
import functools
import math

import numpy as np
import jax
import jax.numpy as jnp
from jax import lax
from jax.experimental import pallas as pl
from jax.experimental.pallas import tpu as pltpu
from jax.experimental.pallas import tpu_sc as plsc

F32 = jnp.float32
BF16 = jnp.bfloat16
I32 = jnp.int32

D_MODEL = 1024
BATCH = 16
SEQ = 256
DEPTH = 2
DEC_BATCH = 2
DEC_SEQ = 4096
PAST_LEN = 512
GRID_W = 64
ROPE_BASE = 10000.0
EPS = 1e-6
A_HEADS = 8
A_KV_HEADS = 2
A_HEAD_DIM = 64
B_HEADS = 4
B_HEAD_DIM = 128
MLSTM_CHUNK = 128
C_HEADS = 4
C_HEAD_DIM = 64
C_V_DIM = 128
N_EXPERTS = 32
TOP_K = 4
D_EXPERT = 1024
SWIGLU_LIMIT = 7.0
SWIGLU_ALPHA = 1.702
N_MOD = 6
DEEPNORM_ALPHA = (2 * DEPTH) ** 0.25

N_CTX = BATCH * SEQ
N_LAT = DEC_BATCH * DEC_SEQ
N_TOK = N_CTX + N_LAT
N_GROUPS = 1 + DEC_BATCH
GROUP_ROWS = 4096
assert N_CTX == GROUP_ROWS and DEC_SEQ == GROUP_ROWS

LANES = 128
TM = 256
TQ = 256
KEY_BLOCK = 1024
TME = 256
N_TILES = (N_TOK * TOP_K) // TME + N_EXPERTS
N_TILES_PAD = 256
P_ROWS = N_TILES * TME
ROUTE_CHUNK = 512
SC_WINDOW = 32
VMEM_LIMIT = 56 * 1024 * 1024

_W = dict(qa=(0, 512), ka=(512, 640), va=(640, 768), qb=(768, 1280), kb=(1280, 1792), vb=(1792, 2304),
          ob=(2304, 2816), ig=(2816, 2824), fg=(2824, 2832), qc=(2832, 3344), kc=(3344, 3856),
          vc=(3856, 4368), g=(4368, 7440))
QA_HEAD_ORDER = (0, 4, 1, 5, 2, 6, 3, 7)
D_IN = 7440
D_IN_PAD = 7552
P_QA, P_KA, P_VA, P_QB, P_KB, P_VB, P_OB, P_QC, P_KC, P_VC, P_G, P_IF = (
    0, 512, 640, 768, 1280, 1792, 2304, 2816, 3328, 3840, 4352, 7424)


def _cparams(sem):
    return pltpu.CompilerParams(dimension_semantics=sem, vmem_limit_bytes=VMEM_LIMIT)


def _split3(a):
    a1 = a.astype(BF16)
    r = a - a1.astype(F32)
    a2 = r.astype(BF16)
    a3 = (r - a2.astype(F32)).astype(BF16)
    return a1, a2, a3


def _dot(a, b, dims=None):
    if dims is None:
        return jnp.dot(a, b, preferred_element_type=F32)
    return lax.dot_general(a, b, (dims, ((), ())), preferred_element_type=F32)


def _dot_exact_rhs01(a, m, dims=None):
    return sum(_dot(p, m, dims) for p in _split3(a))


def _dot_exact_lhs01(m, a, dims=None):
    return sum(_dot(m, p, dims) for p in _split3(a))


def _ln(x):
    mu = jnp.mean(x, axis=-1, keepdims=True)
    xc = x - mu
    var = jnp.mean(xc * xc, axis=-1, keepdims=True)
    return xc * lax.rsqrt(var + EPS)


def _log_sigmoid(x):
    return jnp.minimum(x, 0.0) - jnp.log(1.0 + jnp.exp(-jnp.abs(x)))


def _rope(x, cos, sin):
    lane = lax.broadcasted_iota(I32, (1, LANES), 1)
    first = (lane % 32) < 16
    outs = []
    for j in range(x.shape[1] // LANES):
        xj = x[:, j * LANES:(j + 1) * LANES]
        partner = jnp.where(first, pltpu.roll(xj, LANES - 16, 1), pltpu.roll(xj, 16, 1))
        outs.append(xj * cos + partner * sin)
    return outs[0] if len(outs) == 1 else jnp.concatenate(outs, axis=1)


def _mod_kernel(c_ref, w_ref, b_ref, o_ref):
    c = c_ref[...]
    s = c * jax.nn.sigmoid(c)
    o_ref[...] = _dot(s.astype(BF16), w_ref[...].astype(BF16)) + b_ref[...]


def _mod_call(cvecs, w_mod, b_mod):
    tn = 1536
    n = N_MOD * D_MODEL
    return pl.pallas_call(
        _mod_kernel,
        grid=(DEPTH, n // tn),
        in_specs=[pl.BlockSpec((8, D_MODEL), lambda l, j: (0, 0)),
                  pl.BlockSpec((None, D_MODEL, tn), lambda l, j: (l, 0, j)),
                  pl.BlockSpec((None, 1, tn), lambda l, j: (l, 0, j))],
        out_specs=pl.BlockSpec((None, 8, tn), lambda l, j: (l, 0, j)),
        out_shape=jax.ShapeDtypeStruct((DEPTH, 8, n), F32),
        compiler_params=_cparams(("arbitrary", "arbitrary")),
        name="mod",
    )(cvecs, w_mod, b_mod.reshape(DEPTH, 1, n))


def _inproj_kernel(n_alias, x_ref, mod_ref, cos_ref, sin_ref, w_ref, b_ref, wgt_ref, bgt_ref, gq_ref, gk_ref,
                   gmat_ref, *refs):
    (qa_o, kab_o, vab_o, qb_o, kb_o, vb_o, ob_o, qc_o, kcb_o, vcb_o, g_o, ifg_o, ifgt_o,
     nka_o, nva_o, nkc_o, nvc_o) = refs[n_alias:]
    is_ctx = pl.program_id(0) < N_CTX // TM
    x = x_ref[...]
    shift = mod_ref[0, 0:1, :]
    scale = mod_ref[0, 1:2, :]
    ub = (_ln(x) * (1.0 + scale) + shift).astype(BF16)
    cos = cos_ref[...]
    sin = sin_ref[...]

    def proj(lo, n):
        return _dot(ub, w_ref[:, lo:lo + n]) + b_ref[:, lo:lo + n]

    def headnorm(z, gmat, gain):
        ss = _dot((z * z).astype(BF16), gmat)
        return z * lax.rsqrt(ss * (1.0 / A_HEAD_DIM) + EPS) * gain

    z = headnorm(proj(P_QA, 512), gmat_ref[...], gq_ref[...])
    qa_o[...] = (_rope(z, cos, sin) * 0.125).astype(BF16)
    zk = _rope(headnorm(proj(P_KA, 128), gmat_ref[0:128, 0:128], gk_ref[...]), cos, sin)
    kab_o[...] = zk.astype(BF16)
    zv = proj(P_VA, 128)
    vab_o[...] = zv.astype(BF16)
    qb_o[...] = proj(P_QB, 512).astype(BF16)
    kb_o[...] = (proj(P_KB, 512) * (B_HEAD_DIM ** -0.5)).astype(BF16)
    vb_o[...] = proj(P_VB, 512).astype(BF16)
    ob_o[...] = proj(P_OB, 512).astype(BF16)
    qc_o[...] = (_rope(proj(P_QC, 512), cos, sin) * 0.125).astype(BF16)
    zkc = _rope(proj(P_KC, 512), cos, sin)
    kcb_o[...] = zkc.astype(BF16)
    zvc = proj(P_VC, 512)
    vcb_o[...] = zvc.astype(BF16)

    @pl.when(is_ctx)
    def _():
        for h in range(A_KV_HEADS):
            nka_o[h] = zk[:, h * A_HEAD_DIM:(h + 1) * A_HEAD_DIM]
            nva_o[h] = zv[:, h * A_HEAD_DIM:(h + 1) * A_HEAD_DIM]
        for h in range(C_HEADS):
            nkc_o[h] = zkc[:, h * LANES:(h + 1) * LANES]
            nvc_o[h] = zvc[:, h * LANES:(h + 1) * LANES]

    for j in range(3):
        g_o[:, j * 1024:(j + 1) * 1024] = jax.nn.sigmoid(proj(P_G + j * 1024, 1024)).astype(BF16)
    z = proj(P_IF, LANES)
    lane = lax.broadcasted_iota(I32, (1, LANES), 1)
    ifg_o[...] = jnp.where(lane >= 8, _log_sigmoid(z), z)
    zt = _dot(wgt_ref[...], ub, ((1,), (1,))) + bgt_ref[...]
    row = lax.broadcasted_iota(I32, (16, 1), 0)
    ifgt_o[...] = jnp.where(row >= 8, _log_sigmoid(zt), zt)


def _inproj_call(l, x, mod, cos_t, sin_t, w_p, b_p, w_gt, b_gt, gq, gk, gmat, cache_outs):
    assert TM == SEQ
    n_ctx_tiles = N_CTX // TM
    n_pos_tiles = DEC_SEQ // TM

    def tbl(i):
        return (jnp.where(i < n_ctx_tiles, 0, 1 + (i - n_ctx_tiles) % n_pos_tiles), 0)

    row = lambda n: pl.BlockSpec((TM, n), lambda i: (i, 0))
    const = lambda shape: pl.BlockSpec(shape, lambda i: (0,) * len(shape), pipeline_mode=pl.Buffered(1))
    sds = lambda n, dt: jax.ShapeDtypeStruct((N_TOK, n), dt)
    out_defs = [(512, BF16), (128, BF16), (128, BF16), (512, BF16), (512, BF16), (512, BF16), (512, BF16),
                (512, BF16), (512, BF16), (512, BF16), (3072, BF16), (LANES, F32)]
    cache_defs = [(A_KV_HEADS, A_HEAD_DIM), (A_KV_HEADS, A_HEAD_DIM), (C_HEADS, LANES), (C_HEADS, LANES)]
    cache_spec = lambda nh, d: pl.BlockSpec((None, None, nh, SEQ, d),
                                            lambda i: (jnp.minimum(i, n_ctx_tiles - 1), l, 0, 0, 0))
    alias_in = [] if cache_outs is None else list(cache_outs)
    n_in = 11
    n_plain_out = len(out_defs) + 1
    return pl.pallas_call(
        functools.partial(_inproj_kernel, len(alias_in)),
        grid=(N_TOK // TM,),
        in_specs=[row(D_MODEL),
                  pl.BlockSpec((1, N_MOD, D_MODEL), lambda i: (i * TM // GROUP_ROWS, 0, 0)),
                  pl.BlockSpec((TM, LANES), tbl), pl.BlockSpec((TM, LANES), tbl),
                  const((D_MODEL, D_IN_PAD)), const((1, D_IN_PAD)),
                  const((16, D_MODEL)), const((16, 1)),
                  const((1, 512)), const((1, 128)), const((512, 512))]
        + [pl.BlockSpec(memory_space=pl.ANY)] * len(alias_in),
        out_specs=[row(n) for n, _ in out_defs] + [pl.BlockSpec((16, TM), lambda i: (0, i))]
        + [cache_spec(nh, d) for nh, d in cache_defs],
        out_shape=[sds(n, dt) for n, dt in out_defs] + [jax.ShapeDtypeStruct((16, N_TOK), F32)]
        + [jax.ShapeDtypeStruct((BATCH, DEPTH, nh, SEQ, d), F32) for nh, d in cache_defs],
        input_output_aliases={n_in + j: n_plain_out + j for j in range(len(alias_in))},
        compiler_params=_cparams(("arbitrary",)),
        name="inproj",
    )(x, mod, cos_t, sin_t, w_p, b_p, w_gt, b_gt, gq, gk, gmat, *alias_in)


def _attn_kernel(mode, nseg, lam_init, *refs):
    q_ref = refs[0]
    kv = refs[1:1 + 2 * nseg]
    rest = refs[1 + 2 * nseg:]
    o_ref = rest[-1]
    q = q_ref[...]
    tq = q.shape[0]
    lane = lax.broadcasted_iota(I32, (1, LANES), 1)
    zero = jnp.zeros_like(q)
    qq = jnp.concatenate([jnp.where(lane < 64, q, zero), jnp.where(lane >= 64, q, zero)], axis=0)
    m = l = acc = None
    for j in range(nseg):
        k_ref, v_ref = kv[2 * j], kv[2 * j + 1]
        for off in range(0, k_ref.shape[0], KEY_BLOCK):
            n = min(KEY_BLOCK, k_ref.shape[0] - off)
            s = _dot(qq, k_ref[off:off + n, :], ((1,), (1,)))
            bm = jnp.max(s, axis=-1, keepdims=True)
            if m is None:
                m = bm
                e = jnp.exp(s - m)
                l = jnp.sum(e, axis=-1, keepdims=True)
                acc = _dot(e.astype(BF16), v_ref[off:off + n, :])
            else:
                m_new = jnp.maximum(m, bm)
                alpha = jnp.exp(m - m_new)
                e = jnp.exp(s - m_new)
                l = alpha * l + jnp.sum(e, axis=-1, keepdims=True)
                acc = alpha * acc + _dot(e.astype(BF16), v_ref[off:off + n, :])
                m = m_new
    o = acc * (1.0 / l)
    if mode == "a":
        out = jnp.where(lane < 64, o[:tq], o[tq:])
    else:
        lam_ref, sn_ref = rest[0], rest[1]
        lp = lam_ref[...]
        lam = (jnp.exp(jnp.sum(lp[0:1] * lp[1:2], axis=-1, keepdims=True))
               - jnp.exp(jnp.sum(lp[2:3] * lp[3:4], axis=-1, keepdims=True)) + lam_init)
        o = o[:tq] - lam * o[tq:]
        ms = jnp.mean(o * o, axis=-1, keepdims=True)
        out = o * lax.rsqrt(ms + EPS) * sn_ref[...] * (1.0 - lam_init)
    o_ref[...] = out.astype(o_ref.dtype)


def _attn_call(mode, lat, lam_init, q_all, k_all, v_all, k_cache, v_cache, lam_c, sn_c, out_buf, name):
    wide = k_all.shape[1] == 512
    kcol = (lambda h: h) if wide else (lambda h: 0)
    if lat:
        nq = DEC_SEQ // TQ
        base_q = N_CTX // TQ
        grid = (DEC_BATCH, nq, 4)
        q_spec = pl.BlockSpec((TQ, LANES), lambda b, i, h: (base_q + b * nq + i, h))
        kn_spec = pl.BlockSpec((DEC_SEQ, LANES), lambda b, i, h: (1 + b, kcol(h)))
        if wide:
            kc_spec = pl.BlockSpec((None, None, PAST_LEN, LANES), lambda b, i, h: (b, h, 0, 0))
        else:
            kc_spec = pl.BlockSpec((None, PAST_LEN, LANES), lambda b, i, h: (b, 0, 0))
        in_specs = [q_spec, kc_spec, kc_spec, kn_spec, kn_spec]
        args = [q_all, k_cache, v_cache, k_all, v_all]
        nseg = 2
        o_spec = pl.BlockSpec((TQ, LANES), lambda b, i, h: (base_q + b * nq + i, h))
        sem = ("arbitrary", "arbitrary", "arbitrary")
    else:
        grid = (BATCH, 4)
        q_spec = pl.BlockSpec((SEQ, LANES), lambda b, h: (b, h))
        k_spec = pl.BlockSpec((SEQ, LANES), lambda b, h: (b, kcol(h)))
        in_specs = [q_spec, k_spec, k_spec]
        args = [q_all, k_all, v_all]
        nseg = 1
        o_spec = pl.BlockSpec((SEQ, LANES), lambda b, h: (b, h))
        sem = ("arbitrary", "arbitrary")
    if mode == "c":
        zero_map = (lambda b, i, h: (0, 0)) if lat else (lambda b, h: (0, 0))
        in_specs += [pl.BlockSpec((4, C_HEAD_DIM), zero_map), pl.BlockSpec((1, C_V_DIM), zero_map)]
        args += [lam_c, sn_c]
    aliases = {}
    if out_buf is not None:
        aliases = {len(args): 0}
        in_specs += [pl.BlockSpec(memory_space=pl.ANY)]
        args += [out_buf]
    return pl.pallas_call(
        functools.partial(_attn_kernel, mode, nseg, lam_init),
        grid=grid, in_specs=in_specs, out_specs=o_spec,
        out_shape=jax.ShapeDtypeStruct((N_TOK, 512), BF16), input_output_aliases=aliases,
        compiler_params=_cparams(sem), name=name,
    )(*args)


def _mlstm_kernel(has_init, want_state, n_alias, *refs):
    q_ref, k_ref, v_ref, g_ref, gt_ref = refs[:5]
    pos = 5
    if has_init:
        cn0_ref, m0_ref = refs[pos:pos + 2]
        pos += 2
    pos += n_alias
    h_ref = refs[pos]
    pos += 1
    if want_state:
        c_o, n_o, m_o = refs[pos:pos + 3]
        pos += 3
    cn_s, m_s = refs[pos:pos + 2]

    d = pl.program_id(0)
    c = pl.program_id(2)
    last = pl.num_programs(2) - 1
    L = MLSTM_CHUNK

    @pl.when(c == 0)
    def _():
        if has_init:
            cn_s[...] = cn0_ref[...]
            m_s[...] = m0_ref[...]
        else:
            cn_s[...] = jnp.zeros_like(cn_s)
            m_s[...] = jnp.zeros_like(m_s)

    fwd = d == 0
    row = lax.broadcasted_iota(I32, (L, L), 0)
    col = lax.broadcasted_iota(I32, (L, L), 1)
    sgn = 1 - 2 * d
    keep = (row - col) * sgn >= 0
    m_c = keep.astype(BF16)
    m_r = ((col - row) * sgn >= 0).astype(BF16)
    g = g_ref[...]
    gt = gt_ref[...]
    bc = _dot_exact_lhs01(m_c, g)
    br = _dot_exact_rhs01(gt, m_r)
    lane = lax.broadcasted_iota(I32, (1, LANES), 1)
    ones_col = jnp.where(lane == 0, 1.0, 0.0).astype(BF16)

    def pick_col(a, off, h):
        return jnp.where(fwd, a[:, off + h:off + h + 1], a[:, off + 4 + h:off + 5 + h])

    def pick_row(a, off, h):
        return jnp.where(fwd, a[off + h:off + h + 1, :], a[off + 4 + h:off + 5 + h, :])

    for h in range(B_HEADS):
        sl = slice(h * L, (h + 1) * L)
        qh, kh, vh = q_ref[:, sl], k_ref[:, sl], v_ref[:, sl]
        i_col, f_col, b_col = pick_col(g, 0, h), pick_col(g, 8, h), pick_col(bc, 8, h)
        i_row, b_row = pick_row(gt, 0, h), pick_row(br, 8, h)
        m_prev = m_s[h][:, 0:1]
        total = jnp.sum(f_col, axis=0, keepdims=True)
        inter = b_col + m_prev
        dmat = jnp.where(keep, b_col + (i_row - b_row), -jnp.inf)
        m_row = jnp.maximum(inter, jnp.max(dmat, axis=-1, keepdims=True))
        w = jnp.exp(dmat - m_row)
        sc = (_dot(qh, kh, ((1,), (1,))) * w).astype(BF16)
        cn_prev = cn_s[h]
        v_aug = jnp.concatenate([vh, jnp.broadcast_to(ones_col, (L, LANES))], axis=1)
        nd = jnp.exp(inter - m_row) * _dot(qh, cn_prev.astype(BF16)) + _dot(sc, v_aug)
        den = nd[:, L:L + 1]
        h_ref[:, sl] = nd[:, :L] / jnp.maximum(jnp.abs(den), jnp.exp(-m_row))
        wel = total - b_col + i_col
        m_new = jnp.maximum(total + m_prev, jnp.max(wel, axis=0, keepdims=True))
        wv = (jnp.exp(wel - m_new) * v_aug.astype(F32)).astype(BF16)
        cn_s[h] = jnp.exp(total + m_prev - m_new) * cn_prev + _dot(kh, wv, ((0,), (0,)))
        m_s[h] = jnp.broadcast_to(m_new, (1, LANES))

    if want_state:
        @pl.when(c == last)
        def _():
            pick = ((lax.broadcasted_iota(I32, (8, 2 * L), 1) == L)
                    & (lax.broadcasted_iota(I32, (8, 2 * L), 0) == 0)).astype(BF16)
            for h in range(B_HEADS):
                cn = cn_s[h]
                c_o[h] = cn[:, :L]
                n_o[h:h + 1, :] = _dot_exact_lhs01(pick, cn, ((1,), (1,)))[0:1]
            m_o[...] = m_s[...]


def _mlstm_call(l, lat, q_all, k_all, v_all, g_all, gt_all, cn0, m0, h_buf, state_bufs, name):
    L = MLSTM_CHUNK
    nb, nc = (DEC_BATCH, DEC_SEQ // L) if lat else (BATCH, SEQ // L)
    base = N_CTX // L if lat else 0

    def chunk(d, b, c):
        return base + b * nc + c + d * (nc - 1 - 2 * c)

    tok = lambda n: pl.BlockSpec((L, n), lambda d, b, c: (chunk(d, b, c), 0))
    in_specs = [tok(512), tok(512), tok(512), tok(LANES),
                pl.BlockSpec((16, L), lambda d, b, c: (0, chunk(d, b, c)))]
    args = [q_all, k_all, v_all, g_all, gt_all]
    if lat:
        in_specs += [pl.BlockSpec((None, None, B_HEADS, L, 2 * L), lambda d, b, c: (b, d, 0, 0, 0)),
                     pl.BlockSpec((None, None, B_HEADS, 1, LANES), lambda d, b, c: (b, d, 0, 0, 0))]
        args += [cn0, m0]
    aliases = {}
    alias_in = ([h_buf] if h_buf is not None else []) + (list(state_bufs) if state_bufs is not None else [])
    if h_buf is not None:
        aliases[len(args)] = 0
    if state_bufs is not None:
        first = len(args) + (1 if h_buf is not None else 0)
        aliases[first] = 1
        aliases[first + 1] = 2
    in_specs += [pl.BlockSpec(memory_space=pl.ANY)] * len(alias_in)
    args += alias_in
    out_specs = [pl.BlockSpec((None, L, 512), lambda d, b, c: (d, chunk(d, b, c), 0))]
    out_shape = [jax.ShapeDtypeStruct((2, N_TOK, 512), F32)]
    if not lat:
        out_specs += [pl.BlockSpec((None, None, None, B_HEADS, L, L), lambda d, b, c: (b, l, d, 0, 0, 0)),
                      pl.BlockSpec((None, None, None, B_HEADS, L), lambda d, b, c: (b, l, d, 0, 0)),
                      pl.BlockSpec((None, None, B_HEADS, 1, LANES), lambda d, b, c: (d, b, 0, 0, 0))]
        out_shape += [jax.ShapeDtypeStruct((BATCH, DEPTH, 2, B_HEADS, L, L), F32),
                      jax.ShapeDtypeStruct((BATCH, DEPTH, 2, B_HEADS, L), F32),
                      jax.ShapeDtypeStruct((2, nb, B_HEADS, 1, LANES), F32)]
    return pl.pallas_call(
        functools.partial(_mlstm_kernel, lat, not lat, len(alias_in)),
        grid=(2, nb, nc), in_specs=in_specs, out_specs=out_specs, out_shape=out_shape,
        input_output_aliases=aliases,
        scratch_shapes=[pltpu.VMEM((B_HEADS, L, 2 * L), F32), pltpu.VMEM((B_HEADS, 1, LANES), F32)],
        compiler_params=_cparams(("arbitrary", "arbitrary", "arbitrary")), name=name,
    )(*args)


def _merge_kernel(x_ref, mod_ref, oa_ref, h_ref, ob_ref, oc_ref, g_ref, hn_ref,
                  wpa_ref, wpb_ref, wpc_ref, wo_ref, bo_ref, ln_g_ref, ln_b_ref, wr_ref, br_ref,
                  x1_o, u2_o, ti_o, tw_o):
    hm = h_ref[0] + h_ref[1]
    parts = []
    for h in range(B_HEADS):
        hh = hm[:, h * LANES:(h + 1) * LANES]
        parts.append(hh * lax.rsqrt(jnp.mean(hh * hh, axis=-1, keepdims=True) + EPS))
    out_b = jnp.concatenate(parts, axis=1) * hn_ref[...] * jax.nn.sigmoid(ob_ref[...].astype(F32))
    merged = (g_ref[:, 0:1024].astype(F32) * _dot(oa_ref[...], wpa_ref[...])
              + g_ref[:, 1024:2048].astype(F32) * _dot(out_b.astype(BF16), wpb_ref[...])
              + g_ref[:, 2048:3072].astype(F32) * _dot(oc_ref[...], wpc_ref[...]))
    y = _dot(merged.astype(BF16), wo_ref[...]) + bo_ref[...]
    gate1 = mod_ref[0, 2:3, :]
    shift2 = mod_ref[0, 3:4, :]
    scale2 = mod_ref[0, 4:5, :]
    x1 = _ln(DEEPNORM_ALPHA * x_ref[...] + gate1 * y) * ln_g_ref[...] + ln_b_ref[...]
    x1_o[...] = x1
    u2 = _ln(x1) * (1.0 + scale2) + shift2
    u2_o[...] = u2
    wr = _split3(wr_ref[...])
    up = _split3(u2)
    nt = ((1,), (1,))
    logits = (_dot(wr[0], up[0], nt) + _dot(wr[0], up[1], nt) + _dot(wr[1], up[0], nt)
              + _dot(wr[0], up[2], nt) + _dot(wr[2], up[0], nt) + _dot(wr[1], up[1], nt)) + br_ref[...]
    eio = lax.broadcasted_iota(I32, (N_EXPERTS, 1), 0)
    cur = logits
    vals, idxs = [], []
    for _ in range(TOP_K):
        mx = jnp.max(cur, axis=0, keepdims=True)
        ix = jnp.min(jnp.where(cur == mx, eio, N_EXPERTS), axis=0, keepdims=True)
        vals.append(mx)
        idxs.append(ix)
        cur = jnp.where(eio == ix, -jnp.inf, cur)
    ex = [jnp.exp(v - vals[0]) for v in vals]
    inv = 1.0 / sum(ex)
    ti_o[...] = jnp.concatenate(idxs, axis=0)
    tw_o[...] = jnp.concatenate([e * inv for e in ex], axis=0)


def _merge_call(x, mod, out_a, h, ob, out_c, g, hn, wpa, wpb, wpc, wo, bo, ln_g, ln_b, wr_t, br):
    row = lambda n: pl.BlockSpec((TM, n), lambda i: (i, 0))
    const = lambda shape: pl.BlockSpec(shape, lambda i: (0,) * len(shape), pipeline_mode=pl.Buffered(1))
    return pl.pallas_call(
        _merge_kernel,
        grid=(N_TOK // TM,),
        in_specs=[row(D_MODEL),
                  pl.BlockSpec((1, N_MOD, D_MODEL), lambda i: (i * TM // GROUP_ROWS, 0, 0)),
                  row(512), pl.BlockSpec((2, TM, 512), lambda i: (0, i, 0)), row(512), row(512), row(3072),
                  const((1, 512)), const((512, D_MODEL)), const((512, D_MODEL)), const((512, D_MODEL)),
                  const((D_MODEL, D_MODEL)), const((1, D_MODEL)), const((1, D_MODEL)), const((1, D_MODEL)),
                  const((N_EXPERTS, D_MODEL)), const((N_EXPERTS, 1))],
        out_specs=[row(D_MODEL), row(D_MODEL),
                   pl.BlockSpec((TOP_K, TM), lambda i: (0, i)), pl.BlockSpec((TOP_K, TM), lambda i: (0, i))],
        out_shape=[jax.ShapeDtypeStruct((N_TOK, D_MODEL), F32), jax.ShapeDtypeStruct((N_TOK, D_MODEL), F32),
                   jax.ShapeDtypeStruct((TOP_K, N_TOK), I32), jax.ShapeDtypeStruct((TOP_K, N_TOK), F32)],
        compiler_params=_cparams(("arbitrary",)), name="merge",
    )(x, mod, out_a, h, ob, out_c, g, hn, wpa, wpb, wpc, wo, bo, ln_g, ln_b, wr_t, br)


def _route_kernel(ti_ref, pos_o, te_o, tv_o, nt_o, rank_s):
    ch = ROUTE_CHUNK
    eio = lax.broadcasted_iota(I32, (N_EXPERTS, 1), 0)
    tri = (lax.broadcasted_iota(I32, (ch, ch), 0) <= lax.broadcasted_iota(I32, (ch, ch), 1)).astype(BF16)

    def onehots(off):
        ti = ti_ref[:, pl.ds(off, ch)]
        return [(eio == ti[k:k + 1, :]).astype(F32) for k in range(TOP_K)]

    def count_body(c, carry):
        off = pl.multiple_of(c * ch, ch)
        ohs = onehots(off)
        oh = ohs[0] + ohs[1] + ohs[2] + ohs[3]
        incl = _dot(oh.astype(BF16), tri)
        excl = incl - oh + carry
        for k in range(TOP_K):
            rank_s[k:k + 1, pl.ds(off, ch)] = jnp.sum(ohs[k] * excl, axis=0, keepdims=True)
        return carry + incl[:, ch - 1:ch]

    counts = lax.fori_loop(0, N_TOK // ch, count_body, jnp.zeros((N_EXPERTS, 1), F32))
    ntile_e = jnp.floor((counts + (TME - 1)) * (1.0 / TME))
    lower = (lax.broadcasted_iota(I32, (N_EXPERTS, N_EXPERTS), 1)
             < lax.broadcasted_iota(I32, (N_EXPERTS, N_EXPERTS), 0)).astype(BF16)
    off_t = _dot(lower, jnp.broadcast_to(ntile_e, (N_EXPERTS, LANES)).astype(BF16))[:, 0:1]
    end_t = off_t + ntile_e
    total = jnp.sum(ntile_e, axis=0, keepdims=True)
    jt = lax.broadcasted_iota(I32, (1, N_TILES_PAD), 1).astype(F32)
    te = jnp.minimum(jnp.sum((end_t <= jt).astype(F32), axis=0, keepdims=True), N_EXPERTS - 1.0)
    sel = (eio.astype(F32) == te).astype(F32)
    cnt_j = jnp.sum(sel * counts, axis=0, keepdims=True)
    off_j = jnp.sum(sel * off_t, axis=0, keepdims=True)
    valid = jnp.clip(cnt_j - (jt - off_j) * TME, 0.0, float(TME))
    te_o[...] = te.astype(I32)
    tv_o[...] = jnp.where(jt < total, valid, 0.0).astype(I32)
    nt_o[...] = jnp.broadcast_to(total, (1, LANES)).astype(I32)
    off_rows = off_t * TME

    def pos_body(c, carry):
        off = pl.multiple_of(c * ch, ch)
        ohs = onehots(off)
        for k in range(TOP_K):
            base = jnp.sum(ohs[k] * off_rows, axis=0, keepdims=True)
            pos_o[k:k + 1, pl.ds(off, ch)] = (rank_s[k:k + 1, pl.ds(off, ch)] + base).astype(I32)
        return carry

    lax.fori_loop(0, N_TOK // ch, pos_body, 0)


def _route_call(top_i):
    return pl.pallas_call(
        _route_kernel,
        out_shape=[jax.ShapeDtypeStruct((TOP_K, N_TOK), I32), jax.ShapeDtypeStruct((1, N_TILES_PAD), I32),
                   jax.ShapeDtypeStruct((1, N_TILES_PAD), I32), jax.ShapeDtypeStruct((1, LANES), I32)],
        scratch_shapes=[pltpu.VMEM((TOP_K, N_TOK), F32)],
        compiler_params=pltpu.CompilerParams(vmem_limit_bytes=VMEM_LIMIT), name="route",
    )(top_i)


def _sc_mesh():
    return plsc.VectorSubcoreMesh(core_axis_name="c", subcore_axis_name="s")


def _sc_worker(n_rows):
    info = plsc.get_sparse_core_info()
    n_workers = info.num_cores * info.num_subcores
    per_w = n_rows // n_workers
    assert per_w * n_workers == n_rows and per_w % SC_WINDOW == 0
    wid = lax.axis_index("s") * info.num_cores + lax.axis_index("c")
    return wid * per_w, per_w


def _dispatch_call(u2, pos_flat):
    w = SC_WINDOW

    @functools.partial(pl.kernel, out_type=jax.ShapeDtypeStruct((P_ROWS, D_MODEL), u2.dtype), mesh=_sc_mesh(),
                       scratch_types=[pltpu.VMEM((w,), I32)] * TOP_K
                       + [pltpu.VMEM((w, D_MODEL), u2.dtype), pltpu.SemaphoreType.DMA])
    def k(x_hbm, p_hbm, o_hbm, i0, i1, i2, i3, rows_v, sem):
        base, per_w = _sc_worker(N_TOK)
        idx = (i0, i1, i2, i3)

        @pl.loop(0, per_w // w)
        def _(c):
            off = pl.multiple_of(base + c * w, 8)
            pltpu.sync_copy(x_hbm.at[pl.ds(off, w)], rows_v)
            for kk in range(TOP_K):
                pltpu.sync_copy(p_hbm.at[pl.ds(pl.multiple_of(kk * N_TOK + off, 8), w)], idx[kk])
            copies = [pltpu.async_copy(rows_v, o_hbm.at[idx[kk]], sem) for kk in range(TOP_K)]
            for cp in copies:
                cp.wait()

    return k(u2, pos_flat)


def _combine_call(ys, pos_flat):
    w = SC_WINDOW
    n = pos_flat.shape[0]

    @functools.partial(pl.kernel, out_type=jax.ShapeDtypeStruct((n, D_MODEL), ys.dtype), mesh=_sc_mesh(),
                       scratch_types=[pltpu.VMEM((w,), I32), pltpu.VMEM((w, D_MODEL), ys.dtype),
                                      pltpu.SemaphoreType.DMA])
    def k(y_hbm, p_hbm, o_hbm, idx_v, rows_v, sem):
        base, per_w = _sc_worker(n)

        @pl.loop(0, per_w // w)
        def _(c):
            off = pl.multiple_of(base + c * w, 8)
            pltpu.sync_copy(p_hbm.at[pl.ds(off, w)], idx_v)
            pltpu.async_copy(y_hbm.at[idx_v], rows_v, sem).wait()
            pltpu.sync_copy(rows_v, o_hbm.at[pl.ds(off, w)])

    return k(ys, pos_flat)


def _moe_kernel(te_ref, tv_ref, nt_ref, x_ref, wu_ref, bu_ref, wd_ref, bd_ref, y_ref, wu_s, wd_s):
    i = pl.program_id(0)

    @pl.when(i < nt_ref[0])
    def _():
        e = te_ref[i]
        prev = te_ref[jnp.maximum(i - 1, 0)]

        @pl.when((i == 0) | (e != prev))
        def _():
            wu_s[...] = wu_ref[...].astype(BF16)
            wd_s[...] = wd_ref[...].astype(BF16)

        rows = lax.broadcasted_iota(I32, (TME, 1), 0)
        x = jnp.where(rows < tv_ref[i], x_ref[...], 0.0).astype(BF16)
        h = _dot(x, wu_s[...]) + bu_ref[...]
        h_glu = jnp.minimum(h[:, :D_EXPERT], SWIGLU_LIMIT)
        h_lin = jnp.clip(h[:, D_EXPERT:], -SWIGLU_LIMIT, SWIGLU_LIMIT)
        act = h_glu * jax.nn.sigmoid(SWIGLU_ALPHA * h_glu) * (h_lin + 1.0)
        y_ref[...] = _dot(act.astype(BF16), wd_s[...]) + bd_ref[...]


def _moe_call(l, tile_e, tile_v, n_tiles, xs, w_up, b_up, w_down, b_down):
    def tile(i, te, tv, nt):
        return jnp.minimum(i, nt[0] - 1)

    def wmap(i, te, tv, nt):
        return (l, te[tile(i, te, tv, nt)], 0, 0)

    grid_spec = pltpu.PrefetchScalarGridSpec(
        num_scalar_prefetch=3,
        grid=(N_TILES,),
        in_specs=[pl.BlockSpec((TME, D_MODEL), lambda i, te, tv, nt: (tile(i, te, tv, nt), 0)),
                  pl.BlockSpec((None, None, D_MODEL, 2 * D_EXPERT), wmap),
                  pl.BlockSpec((None, None, 1, 2 * D_EXPERT), wmap),
                  pl.BlockSpec((None, None, D_EXPERT, D_MODEL), wmap),
                  pl.BlockSpec((None, None, 1, D_MODEL), wmap)],
        out_specs=pl.BlockSpec((TME, D_MODEL), lambda i, te, tv, nt: (tile(i, te, tv, nt), 0)),
        scratch_shapes=[pltpu.VMEM((D_MODEL, 2 * D_EXPERT), BF16), pltpu.VMEM((D_EXPERT, D_MODEL), BF16)],
    )
    return pl.pallas_call(
        _moe_kernel, grid_spec=grid_spec,
        out_shape=jax.ShapeDtypeStruct((P_ROWS, D_MODEL), F32),
        compiler_params=_cparams(("arbitrary",)), name="moe",
    )(tile_e, tile_v, n_tiles, xs, w_up, b_up.reshape(DEPTH, N_EXPERTS, 1, -1),
      w_down, b_down.reshape(DEPTH, N_EXPERTS, 1, -1))


def _final_kernel(split, yg_ref, tw_ref, x1_ref, mod_ref, ln_g_ref, ln_b_ref, *o_refs):
    tw = tw_ref[...]
    moe = sum(tw[:, k:k + 1] * yg_ref[k] for k in range(TOP_K))
    gate2 = mod_ref[0, 5:6, :]
    y = _ln(DEEPNORM_ALPHA * x1_ref[...] + gate2 * moe) * ln_g_ref[...] + ln_b_ref[...]
    if not split:
        o_refs[0][...] = y
    else:
        is_ctx = pl.program_id(0) < N_CTX // TM

        @pl.when(is_ctx)
        def _():
            o_refs[0][...] = y

        @pl.when(jnp.logical_not(is_ctx))
        def _():
            o_refs[1][...] = y


def _final_call(split, yg, tw, x1, mod, ln_g, ln_b):
    row = lambda n: pl.BlockSpec((TM, n), lambda i: (i, 0))
    const = lambda shape: pl.BlockSpec(shape, lambda i: (0,) * len(shape))
    n_ctx_tiles = N_CTX // TM
    if split:
        out_specs = [pl.BlockSpec((TM, D_MODEL), lambda i: (jnp.minimum(i, n_ctx_tiles - 1), 0)),
                     pl.BlockSpec((TM, D_MODEL), lambda i: (jnp.maximum(i - n_ctx_tiles, 0), 0))]
        out_shape = [jax.ShapeDtypeStruct((N_CTX, D_MODEL), F32), jax.ShapeDtypeStruct((N_LAT, D_MODEL), F32)]
    else:
        out_specs = row(D_MODEL)
        out_shape = jax.ShapeDtypeStruct((N_TOK, D_MODEL), F32)
    return pl.pallas_call(
        functools.partial(_final_kernel, split),
        grid=(N_TOK // TM,),
        in_specs=[pl.BlockSpec((TOP_K, TM, D_MODEL), lambda i: (0, i, 0)), row(TOP_K), row(D_MODEL),
                  pl.BlockSpec((1, N_MOD, D_MODEL), lambda i: (i * TM // GROUP_ROWS, 0, 0)),
                  const((1, D_MODEL)), const((1, D_MODEL))],
        out_specs=out_specs, out_shape=out_shape,
        compiler_params=_cparams(("arbitrary",)), name="final",
    )(yg, tw, x1, mod, ln_g, ln_b)


def _rope_tables():
    half = A_HEAD_DIM // 2
    freqs = ROPE_BASE ** (-jnp.arange(0, half, 2, dtype=F32) / half)
    t = jnp.arange(DEC_SEQ, dtype=jnp.int32)
    ang_r = (t // GRID_W).astype(F32)[:, None] * freqs[None, :]
    ang_c = (t % GRID_W).astype(F32)[:, None] * freqs[None, :]
    cos64 = jnp.concatenate([jnp.cos(ang_r)] * 2 + [jnp.cos(ang_c)] * 2, axis=1)
    sin64 = jnp.concatenate([-jnp.sin(ang_r), jnp.sin(ang_r), -jnp.sin(ang_c), jnp.sin(ang_c)], axis=1)
    cos_t = jnp.concatenate([jnp.ones((TM, LANES), F32), jnp.tile(cos64, (1, 2))], axis=0)
    sin_t = jnp.concatenate([jnp.zeros((TM, LANES), F32), jnp.tile(sin64, (1, 2))], axis=0)
    return cos_t, sin_t


def _permute_in_cols(t):
    qa = [t[..., h * 64:(h + 1) * 64] for h in QA_HEAD_ORDER]
    pad = jnp.zeros(t.shape[:-1] + (D_IN_PAD - D_IN,), t.dtype)
    return jnp.concatenate(qa + [t[..., 512:2816], t[..., 2832:7440], t[..., 2816:2832], pad], axis=-1)


def _layer(l, x, mod, tables, caches, p, outs, weights):
    cos_t, sin_t = tables
    ck_a, cv_a, ck_c, cv_c, cn0, m0 = caches
    lam_init = 0.8 - 0.6 * math.exp(-0.3 * l)
    w_in, b_in = p["w_in"], p["b_in"]
    w_p = _permute_in_cols(w_in).astype(BF16)
    b_p = _permute_in_cols(b_in).reshape(1, D_IN_PAD)
    w_gt = w_in[:, 2816:2832].T.astype(BF16)
    b_gt = b_in[2816:2832].reshape(16, 1)
    gq = jnp.tile(p["qn_a"], A_HEADS).reshape(1, 512)
    gk = jnp.tile(p["kn_a"], A_KV_HEADS).reshape(1, 128)
    gmat = jnp.asarray(np.kron(np.eye(8), np.ones((64, 64))), BF16)

    (qa, kab, vab, qb, kb, vb, ob, qc, kcb, vcb, g, ifg, ifgt, nka, nva, nkc, nvc) = _inproj_call(
        l, x, mod, cos_t, sin_t, w_p, b_p, w_gt, b_gt, gq, gk, gmat, None if outs is None else outs[:4])

    lam_c = p["lam_c"]
    sn_c = p["sn_c"].reshape(1, C_V_DIM)
    oa = _attn_call("a", False, lam_init, qa, kab, vab, None, None, None, None, None, "attn_a_ctx")
    oa = _attn_call("a", True, lam_init, qa, kab, vab, ck_a, cv_a, None, None, oa, "attn_a_lat")
    oc = _attn_call("c", False, lam_init, qc, kcb, vcb, None, None, lam_c, sn_c, None, "attn_c_ctx")
    oc = _attn_call("c", True, lam_init, qc, kcb, vcb, ck_c, cv_c, lam_c, sn_c, oc, "attn_c_lat")
    h, c_fin, n_fin, m_fin = _mlstm_call(l, False, qb, kb, vb, ifg, ifgt, None, None, None,
                                         None if outs is None else outs[4:6], "mlstm_ctx")
    (h,) = _mlstm_call(l, True, qb, kb, vb, ifg, ifgt, cn0, m0, h, None, "mlstm_lat")

    wpa = p["w_pa"].reshape(A_HEADS, A_HEAD_DIM, D_MODEL)[np.asarray(QA_HEAD_ORDER)].reshape(512, D_MODEL)
    x1, u2, top_i, top_w = _merge_call(
        x, mod, oa, h, ob, oc, g, p["hn_b"].reshape(1, 512),
        wpa.astype(BF16), p["w_pb"].astype(BF16), p["w_pc"].astype(BF16), p["w_o"].astype(BF16),
        p["b_o"].reshape(1, D_MODEL), p["ln1_g"].reshape(1, D_MODEL), p["ln1_b"].reshape(1, D_MODEL),
        p["w_router"].T, p["b_router"].reshape(N_EXPERTS, 1))

    pos, tile_e, tile_v, n_tiles = _route_call(top_i)
    pos_flat = pos.reshape(TOP_K * N_TOK)
    xs = _dispatch_call(u2, pos_flat)
    ys = _moe_call(l, tile_e[0, :N_TILES], tile_v[0, :N_TILES], n_tiles[0, :1], xs,
                   weights["w_up"], weights["b_up"], weights["w_down"], weights["b_down"])
    yg = _combine_call(ys, pos_flat).reshape(TOP_K, N_TOK, D_MODEL)
    x2 = _final_call(l == DEPTH - 1, yg, top_w.T, x1, mod,
                     p["ln2_g"].reshape(1, D_MODEL), p["ln2_b"].reshape(1, D_MODEL))
    m_l = m_fin[:, :, :, 0, 0].transpose(1, 0, 2)
    m_all = m_l[:, None] if outs is None else jnp.concatenate([outs[6], m_l[:, None]], axis=1)
    return x2, (nka, nva, nkc, nvc, c_fin, n_fin, m_all)


def kernel(x_prompt, x_sample, cache_k_gqa, cache_v_gqa, cache_k_diff, cache_v_diff, state_mlstm_c, state_mlstm_n, state_mlstm_m, c, c_ctx, w_in, b_in, qn_a, kn_a, hn_b, lam_c, sn_c, w_pa, w_pb, w_pc, w_o, b_o, w_mod, b_mod, ln1_g, ln1_b, ln2_g, ln2_b, w_router, b_router, w_up, b_up, w_down, b_down):
    x = jnp.concatenate([x_prompt.reshape(N_CTX, D_MODEL), x_sample.reshape(N_LAT, D_MODEL)], axis=0)
    cvecs = jnp.zeros((8, D_MODEL), F32).at[0].set(c_ctx).at[1:1 + DEC_BATCH].set(c)
    mod_all = _mod_call(cvecs, w_mod, b_mod).reshape(DEPTH, 8, N_MOD, D_MODEL)[:, :N_GROUPS]
    tables = _rope_tables()
    params = dict(w_in=w_in, b_in=b_in, qn_a=qn_a, kn_a=kn_a, hn_b=hn_b, lam_c=lam_c, sn_c=sn_c, w_pa=w_pa,
                  w_pb=w_pb, w_pc=w_pc, w_o=w_o, b_o=b_o, ln1_g=ln1_g, ln1_b=ln1_b, ln2_g=ln2_g, ln2_b=ln2_b,
                  w_router=w_router, b_router=b_router)
    expert_weights = dict(w_up=w_up, b_up=b_up, w_down=w_down, b_down=b_down)
    outs = None
    for l in range(DEPTH):
        p = {k: v[l] for k, v in params.items()}
        ck_a = cache_k_gqa[:, l].transpose(0, 2, 1, 3).reshape(DEC_BATCH, PAST_LEN, 128).astype(BF16)
        cv_a = cache_v_gqa[:, l].transpose(0, 2, 1, 3).reshape(DEC_BATCH, PAST_LEN, 128).astype(BF16)
        ck_c = cache_k_diff[:, l].astype(BF16)
        cv_c = cache_v_diff[:, l].astype(BF16)
        cn0 = jnp.concatenate([state_mlstm_c[:, l], state_mlstm_n[:, l][..., None],
                               jnp.zeros((DEC_BATCH, 2, B_HEADS, B_HEAD_DIM, B_HEAD_DIM - 1), F32)], axis=-1)
        m0 = jnp.broadcast_to(state_mlstm_m[:, l][..., None, None], (DEC_BATCH, 2, B_HEADS, 1, LANES))
        x, outs = _layer(l, x, mod_all[l], tables, (ck_a, cv_a, ck_c, cv_c, cn0, m0), p, outs, expert_weights)
    y_ctx, y_lat = x
    return (y_ctx.reshape(BATCH, SEQ, D_MODEL), y_lat.reshape(DEC_BATCH, DEC_SEQ, D_MODEL)) + tuple(outs)
```

```python
import functools
import math

import numpy as np
import jax
import jax.numpy as jnp
from jax import lax
from jax.experimental import pallas as pl
from jax.experimental.pallas import tpu as pltpu
from jax.experimental.pallas import tpu_sc as plsc

F32 = jnp.float32
BF16 = jnp.bfloat16
I32 = jnp.int32

D_MODEL = 1024
BATCH = 16
SEQ = 256
DEPTH = 2
DEC_BATCH = 2
DEC_SEQ = 4096
PAST_LEN = 512
GRID_W = 64
ROPE_BASE = 10000.0
EPS = 1e-6
A_HEADS = 8
A_KV_HEADS = 2
A_HEAD_DIM = 64
B_HEADS = 4
B_HEAD_DIM = 128
MLSTM_CHUNK = 128
C_HEADS = 4
C_HEAD_DIM = 64
C_V_DIM = 128
N_EXPERTS = 32
TOP_K = 4
D_EXPERT = 1024
SWIGLU_LIMIT = 7.0
SWIGLU_ALPHA = 1.702
N_MOD = 6
DEEPNORM_ALPHA = (2 * DEPTH) ** 0.25

N_CTX = BATCH * SEQ
N_LAT = DEC_BATCH * DEC_SEQ
N_TOK = N_CTX + N_LAT
N_GROUPS = 1 + DEC_BATCH
GROUP_ROWS = 4096
assert N_CTX == GROUP_ROWS and DEC_SEQ == GROUP_ROWS

LANES = 128
TM = 256
TQ = 256
KEY_BLOCK = 1024
MERGE_ROWS = 128
TME = 512
N_TILES = (N_TOK * TOP_K) // TME + N_EXPERTS
N_TILES_PAD = 128
P_ROWS = N_TILES * TME
ROUTE_CHUNK = 512
D_PACK = D_MODEL // 2
SC_WINDOW = 64
VMEM_LIMIT = 56 * 1024 * 1024

_W = dict(qa=(0, 512), ka=(512, 640), va=(640, 768), qb=(768, 1280), kb=(1280, 1792), vb=(1792, 2304),
          ob=(2304, 2816), ig=(2816, 2824), fg=(2824, 2832), qc=(2832, 3344), kc=(3344, 3856),
          vc=(3856, 4368), g=(4368, 7440))
QA_HEAD_ORDER = (0, 4, 1, 5, 2, 6, 3, 7)
D_IN = 7440
D_IN_PAD = 7552
P_QA, P_KA, P_VA, P_QB, P_KB, P_VB, P_OB, P_QC, P_KC, P_VC, P_G, P_IF = (
    0, 512, 640, 768, 1280, 1792, 2304, 2816, 3328, 3840, 4352, 7424)


def _cparams(sem):
    return pltpu.CompilerParams(dimension_semantics=sem, vmem_limit_bytes=VMEM_LIMIT)


def _split3(a):
    a1 = a.astype(BF16)
    r = a - a1.astype(F32)
    a2 = r.astype(BF16)
    a3 = (r - a2.astype(F32)).astype(BF16)
    return a1, a2, a3


def _dot(a, b, dims=None):
    if dims is None:
        return jnp.dot(a, b, preferred_element_type=F32)
    return lax.dot_general(a, b, (dims, ((), ())), preferred_element_type=F32)


def _dot_exact_rhs01(a, m, dims=None):
    return sum(_dot(p, m, dims) for p in _split3(a))


def _dot_exact_lhs01(m, a, dims=None):
    return sum(_dot(m, p, dims) for p in _split3(a))


def _pack_rows(x):
    xb = x.astype(BF16).astype(F32)
    lo = pltpu.bitcast(xb[:, :D_PACK], I32)
    lo = lax.shift_right_logical(lo, jnp.full_like(lo, 16))
    hi = pltpu.bitcast(xb[:, D_PACK:], I32) & jnp.int32(-65536)
    return lo | hi


def _unpack_rows(w):
    return (pltpu.bitcast(lax.shift_left(w, jnp.full_like(w, 16)), F32),
            pltpu.bitcast(w & jnp.int32(-65536), F32))


def _ln(x):
    mu = jnp.mean(x, axis=-1, keepdims=True)
    xc = x - mu
    var = jnp.mean(xc * xc, axis=-1, keepdims=True)
    return xc * lax.rsqrt(var + EPS)


def _log_sigmoid(x):
    return jnp.minimum(x, 0.0) - jnp.log(1.0 + jnp.exp(-jnp.abs(x)))


def _rope(x, cos, sin):
    lane = lax.broadcasted_iota(I32, (1, LANES), 1)
    first = (lane % 32) < 16
    outs = []
    for j in range(x.shape[1] // LANES):
        xj = x[:, j * LANES:(j + 1) * LANES]
        partner = jnp.where(first, pltpu.roll(xj, LANES - 16, 1), pltpu.roll(xj, 16, 1))
        outs.append(xj * cos + partner * sin)
    return outs[0] if len(outs) == 1 else jnp.concatenate(outs, axis=1)


def _mod_kernel(c_ref, w_ref, b_ref, o_ref):
    c = c_ref[...]
    s = c * jax.nn.sigmoid(c)
    o_ref[...] = _dot(s.astype(BF16), w_ref[...].astype(BF16)) + b_ref[...]


def _mod_call(cvecs, w_mod, b_mod):
    tn = 1536
    n = N_MOD * D_MODEL
    return pl.pallas_call(
        _mod_kernel,
        grid=(DEPTH, n // tn),
        in_specs=[pl.BlockSpec((8, D_MODEL), lambda l, j: (0, 0)),
                  pl.BlockSpec((None, D_MODEL, tn), lambda l, j: (l, 0, j)),
                  pl.BlockSpec((None, 1, tn), lambda l, j: (l, 0, j))],
        out_specs=pl.BlockSpec((None, 8, tn), lambda l, j: (l, 0, j)),
        out_shape=jax.ShapeDtypeStruct((DEPTH, 8, n), F32),
        compiler_params=_cparams(("arbitrary", "arbitrary")),
        name="mod",
    )(cvecs, w_mod, b_mod.reshape(DEPTH, 1, n))


def _inproj_kernel(n_alias, x_ref, mod_ref, cos_ref, sin_ref, w_ref, b_ref, wgt_ref, bgt_ref, gq_ref, gk_ref,
                   gmat_ref, *refs):
    (qa_o, kab_o, vab_o, qb_o, kb_o, vb_o, ob_o, qc_o, kcb_o, vcb_o, g_o, ifg_o, ifgt_o,
     nka_o, nva_o, nkc_o, nvc_o) = refs[n_alias:]
    is_ctx = pl.program_id(0) < N_CTX // TM
    x = x_ref[...]
    shift = mod_ref[0, 0:1, :]
    scale = mod_ref[0, 1:2, :]
    ub = (_ln(x) * (1.0 + scale) + shift).astype(BF16)
    cos = cos_ref[...]
    sin = sin_ref[...]

    def proj(lo, n):
        return _dot(ub, w_ref[:, lo:lo + n]) + b_ref[:, lo:lo + n]

    def headnorm(z, gmat, gain):
        ss = _dot((z * z).astype(BF16), gmat)
        return z * lax.rsqrt(ss * (1.0 / A_HEAD_DIM) + EPS) * gain

    z = headnorm(proj(P_QA, 512), gmat_ref[...], gq_ref[...])
    qa_o[...] = (_rope(z, cos, sin) * 0.125).astype(BF16)
    zk = _rope(headnorm(proj(P_KA, 128), gmat_ref[0:128, 0:128], gk_ref[...]), cos, sin)
    kab_o[...] = zk.astype(BF16)
    zv = proj(P_VA, 128)
    vab_o[...] = zv.astype(BF16)
    qb_o[...] = proj(P_QB, 512).astype(BF16)
    kb_o[...] = (proj(P_KB, 512) * (B_HEAD_DIM ** -0.5)).astype(BF16)
    vb_o[...] = proj(P_VB, 512).astype(BF16)
    ob_o[...] = proj(P_OB, 512).astype(BF16)
    qc_o[...] = (_rope(proj(P_QC, 512), cos, sin) * 0.125).astype(BF16)
    zkc = _rope(proj(P_KC, 512), cos, sin)
    kcb_o[...] = zkc.astype(BF16)
    zvc = proj(P_VC, 512)
    vcb_o[...] = zvc.astype(BF16)

    @pl.when(is_ctx)
    def _():
        for h in range(A_KV_HEADS):
            nka_o[h] = zk[:, h * A_HEAD_DIM:(h + 1) * A_HEAD_DIM]
            nva_o[h] = zv[:, h * A_HEAD_DIM:(h + 1) * A_HEAD_DIM]
        for h in range(C_HEADS):
            nkc_o[h] = zkc[:, h * LANES:(h + 1) * LANES]
            nvc_o[h] = zvc[:, h * LANES:(h + 1) * LANES]

    for j in range(3):
        g_o[:, j * 1024:(j + 1) * 1024] = jax.nn.sigmoid(proj(P_G + j * 1024, 1024)).astype(BF16)
    z = proj(P_IF, LANES)
    lane = lax.broadcasted_iota(I32, (1, LANES), 1)
    ifg_o[...] = jnp.where(lane >= 8, _log_sigmoid(z), z)
    zt = _dot(wgt_ref[...], ub, ((1,), (1,))) + bgt_ref[...]
    row = lax.broadcasted_iota(I32, (16, 1), 0)
    ifgt_o[...] = jnp.where(row >= 8, _log_sigmoid(zt), zt)


def _inproj_call(l, x, mod, cos_t, sin_t, w_p, b_p, w_gt, b_gt, gq, gk, gmat, cache_outs):
    assert TM == SEQ
    n_ctx_tiles = N_CTX // TM
    n_pos_tiles = DEC_SEQ // TM

    def tbl(i):
        return (jnp.where(i < n_ctx_tiles, 0, 1 + (i - n_ctx_tiles) % n_pos_tiles), 0)

    row = lambda n: pl.BlockSpec((TM, n), lambda i: (i, 0))
    const = lambda shape: pl.BlockSpec(shape, lambda i: (0,) * len(shape), pipeline_mode=pl.Buffered(1))
    sds = lambda n, dt: jax.ShapeDtypeStruct((N_TOK, n), dt)
    out_defs = [(512, BF16), (128, BF16), (128, BF16), (512, BF16), (512, BF16), (512, BF16), (512, BF16),
                (512, BF16), (512, BF16), (512, BF16), (3072, BF16), (LANES, F32)]
    cache_defs = [(A_KV_HEADS, A_HEAD_DIM), (A_KV_HEADS, A_HEAD_DIM), (C_HEADS, LANES), (C_HEADS, LANES)]
    cache_spec = lambda nh, d: pl.BlockSpec((None, None, nh, SEQ, d),
                                            lambda i: (jnp.minimum(i, n_ctx_tiles - 1), l, 0, 0, 0))
    alias_in = [] if cache_outs is None else list(cache_outs)
    n_in = 11
    n_plain_out = len(out_defs) + 1
    return pl.pallas_call(
        functools.partial(_inproj_kernel, len(alias_in)),
        grid=(N_TOK // TM,),
        in_specs=[row(D_MODEL),
                  pl.BlockSpec((1, N_MOD, D_MODEL), lambda i: (i * TM // GROUP_ROWS, 0, 0)),
                  pl.BlockSpec((TM, LANES), tbl), pl.BlockSpec((TM, LANES), tbl),
                  const((D_MODEL, D_IN_PAD)), const((1, D_IN_PAD)),
                  const((16, D_MODEL)), const((16, 1)),
                  const((1, 512)), const((1, 128)), const((512, 512))]
        + [pl.BlockSpec(memory_space=pl.ANY)] * len(alias_in),
        out_specs=[row(n) for n, _ in out_defs] + [pl.BlockSpec((16, TM), lambda i: (0, i))]
        + [cache_spec(nh, d) for nh, d in cache_defs],
        out_shape=[sds(n, dt) for n, dt in out_defs] + [jax.ShapeDtypeStruct((16, N_TOK), F32)]
        + [jax.ShapeDtypeStruct((BATCH, DEPTH, nh, SEQ, d), F32) for nh, d in cache_defs],
        input_output_aliases={n_in + j: n_plain_out + j for j in range(len(alias_in))},
        compiler_params=_cparams(("arbitrary",)),
        name="inproj",
    )(x, mod, cos_t, sin_t, w_p, b_p, w_gt, b_gt, gq, gk, gmat, *alias_in)


def _attn_kernel(mode, nseg, lam_init, *refs):
    q_ref = refs[0]
    kv = refs[1:1 + 2 * nseg]
    rest = refs[1 + 2 * nseg:]
    o_ref = rest[-1]
    q = q_ref[...]
    tq = q.shape[0]
    lane = lax.broadcasted_iota(I32, (1, LANES), 1)
    zero = jnp.zeros_like(q)
    qq = jnp.concatenate([jnp.where(lane < 64, q, zero), jnp.where(lane >= 64, q, zero)], axis=0)
    m = l = acc = None
    for j in range(nseg):
        k_ref, v_ref = kv[2 * j], kv[2 * j + 1]
        for off in range(0, k_ref.shape[0], KEY_BLOCK):
            n = min(KEY_BLOCK, k_ref.shape[0] - off)
            s = _dot(qq, k_ref[off:off + n, :], ((1,), (1,)))
            bm = jnp.max(s, axis=-1, keepdims=True)
            if m is None:
                m = bm
                e = jnp.exp(s - m)
                l = jnp.sum(e, axis=-1, keepdims=True)
                acc = _dot(e.astype(BF16), v_ref[off:off + n, :])
            else:
                m_new = jnp.maximum(m, bm)
                alpha = jnp.exp(m - m_new)
                e = jnp.exp(s - m_new)
                l = alpha * l + jnp.sum(e, axis=-1, keepdims=True)
                acc = alpha * acc + _dot(e.astype(BF16), v_ref[off:off + n, :])
                m = m_new
    o = acc * (1.0 / l)
    if mode == "a":
        out = jnp.where(lane < 64, o[:tq], o[tq:])
    else:
        lam_ref, sn_ref = rest[0], rest[1]
        lp = lam_ref[...]
        lam = (jnp.exp(jnp.sum(lp[0:1] * lp[1:2], axis=-1, keepdims=True))
               - jnp.exp(jnp.sum(lp[2:3] * lp[3:4], axis=-1, keepdims=True)) + lam_init)
        o = o[:tq] - lam * o[tq:]
        ms = jnp.mean(o * o, axis=-1, keepdims=True)
        out = o * lax.rsqrt(ms + EPS) * sn_ref[...] * (1.0 - lam_init)
    o_ref[...] = out.astype(o_ref.dtype)


def _attn_call(mode, lat, lam_init, q_all, k_all, v_all, k_cache, v_cache, lam_c, sn_c, out_buf, name):
    wide = k_all.shape[1] == 512
    kcol = (lambda h: h) if wide else (lambda h: 0)
    if lat:
        nq = DEC_SEQ // TQ
        base_q = N_CTX // TQ
        grid = (DEC_BATCH, nq, 4)
        q_spec = pl.BlockSpec((TQ, LANES), lambda b, i, h: (base_q + b * nq + i, h))
        kn_spec = pl.BlockSpec((DEC_SEQ, LANES), lambda b, i, h: (1 + b, kcol(h)))
        if wide:
            kc_spec = pl.BlockSpec((None, None, PAST_LEN, LANES), lambda b, i, h: (b, h, 0, 0))
        else:
            kc_spec = pl.BlockSpec((None, PAST_LEN, LANES), lambda b, i, h: (b, 0, 0))
        in_specs = [q_spec, kc_spec, kc_spec, kn_spec, kn_spec]
        args = [q_all, k_cache, v_cache, k_all, v_all]
        nseg = 2
        o_spec = pl.BlockSpec((TQ, LANES), lambda b, i, h: (base_q + b * nq + i, h))
        sem = ("arbitrary", "arbitrary", "arbitrary")
    else:
        grid = (BATCH, 4)
        q_spec = pl.BlockSpec((SEQ, LANES), lambda b, h: (b, h))
        k_spec = pl.BlockSpec((SEQ, LANES), lambda b, h: (b, kcol(h)))
        in_specs = [q_spec, k_spec, k_spec]
        args = [q_all, k_all, v_all]
        nseg = 1
        o_spec = pl.BlockSpec((SEQ, LANES), lambda b, h: (b, h))
        sem = ("arbitrary", "arbitrary")
    if mode == "c":
        zero_map = (lambda b, i, h: (0, 0)) if lat else (lambda b, h: (0, 0))
        in_specs += [pl.BlockSpec((4, C_HEAD_DIM), zero_map), pl.BlockSpec((1, C_V_DIM), zero_map)]
        args += [lam_c, sn_c]
    aliases = {}
    if out_buf is not None:
        aliases = {len(args): 0}
        in_specs += [pl.BlockSpec(memory_space=pl.ANY)]
        args += [out_buf]
    return pl.pallas_call(
        functools.partial(_attn_kernel, mode, nseg, lam_init),
        grid=grid, in_specs=in_specs, out_specs=o_spec,
        out_shape=jax.ShapeDtypeStruct((N_TOK, 512), BF16), input_output_aliases=aliases,
        compiler_params=_cparams(sem), name=name,
    )(*args)


def _mlstm_kernel(has_init, want_state, n_alias, *refs):
    tok = (refs[0:5], refs[5:10])
    pos = 10
    if has_init:
        cn0_ref, m0_ref = refs[pos:pos + 2]
        pos += 2
    pos += n_alias
    h_refs = refs[pos:pos + 2]
    pos += 2
    if want_state:
        c_o, n_o, m_o = refs[pos:pos + 3]
        pos += 3
    cn_s, m_s = refs[pos:pos + 2]

    c = pl.program_id(1)
    last = pl.num_programs(1) - 1
    L = MLSTM_CHUNK

    @pl.when(c == 0)
    def _():
        if has_init:
            cn_s[...] = cn0_ref[...]
            m_s[...] = m0_ref[...]
        else:
            cn_s[...] = jnp.zeros_like(cn_s)
            m_s[...] = jnp.zeros_like(m_s)

    row = lax.broadcasted_iota(I32, (L, L), 0)
    col = lax.broadcasted_iota(I32, (L, L), 1)
    lane = lax.broadcasted_iota(I32, (1, LANES), 1)
    ones_col = jnp.where(lane == 0, 1.0, 0.0).astype(BF16)

    chains = []
    for d in range(2):
        q_ref, k_ref, v_ref, g_ref, gt_ref = tok[d]
        keep = (col <= row) if d == 0 else (col >= row)
        m_c = keep.astype(BF16)
        m_r = ((row <= col) if d == 0 else (row >= col)).astype(BF16)
        g = g_ref[...]
        gt = gt_ref[...]
        bc = _dot_exact_lhs01(m_c, g)
        br = _dot_exact_rhs01(gt, m_r)
        for h in range(B_HEADS):
            sl = slice(h * L, (h + 1) * L)
            gi, gf = 4 * d + h, 8 + 4 * d + h
            ch = dict(d=d, h=h, sl=sl, keep=keep, q=q_ref[:, sl], k=k_ref[:, sl], v=v_ref[:, sl],
                      i_col=g[:, gi:gi + 1], f_col=g[:, gf:gf + 1], b_col=bc[:, gf:gf + 1],
                      r_row=gt[gi:gi + 1, :] - br[gf:gf + 1, :],
                      m_prev=m_s[d, h][:, 0:1], cn_prev=cn_s[d, h])
            chains.append(ch)
    for ch in chains:
        ch["s"] = _dot(ch["q"], ch["k"], ((1,), (1,)))
        ch["qc"] = _dot(ch["q"], ch["cn_prev"].astype(BF16))
    for ch in chains:
        ch["total"] = jnp.sum(ch["f_col"], axis=0, keepdims=True)
        ch["inter"] = ch["b_col"] + ch["m_prev"]
        ch["v_aug"] = jnp.concatenate([ch["v"], jnp.broadcast_to(ones_col, (L, LANES))], axis=1)
    for ch in chains:
        ch["dmat"] = jnp.where(ch["keep"], ch["b_col"] + ch["r_row"], -jnp.inf)
    for ch in chains:
        ch["dmax"] = jnp.max(ch["dmat"], axis=-1, keepdims=True)
    for ch in chains:
        ch["m_row"] = jnp.maximum(ch["inter"], ch["dmax"])
    for ch in chains:
        ch["sc"] = (ch["s"] * jnp.exp(ch["dmat"] - ch["m_row"])).astype(BF16)
    for ch in chains:
        ch["a"] = _dot(ch["sc"], ch["v_aug"])
    for ch in chains:
        ch["nd"] = jnp.exp(ch["inter"] - ch["m_row"]) * ch["qc"] + ch["a"]
    for ch in chains:
        ch["scale"] = 1.0 / jnp.maximum(jnp.abs(ch["nd"][:, L:L + 1]), jnp.exp(-ch["m_row"]))
    for ch in chains:
        h_refs[ch["d"]][:, ch["sl"]] = ch["nd"][:, :L] * ch["scale"]
    for ch in chains:
        ch["wel"] = ch["total"] - ch["b_col"] + ch["i_col"]
        ch["m_new"] = jnp.maximum(ch["total"] + ch["m_prev"], jnp.max(ch["wel"], axis=0, keepdims=True))
    for ch in chains:
        ch["kw"] = (jnp.exp(ch["wel"] - ch["m_new"]) * ch["k"].astype(F32)).astype(BF16)
    for ch in chains:
        ch["u"] = _dot(ch["kw"], ch["v_aug"], ((0,), (0,)))
    for ch in chains:
        cn_s[ch["d"], ch["h"]] = jnp.exp(ch["total"] + ch["m_prev"] - ch["m_new"]) * ch["cn_prev"] + ch["u"]
        m_s[ch["d"], ch["h"]] = jnp.broadcast_to(ch["m_new"], (1, LANES))

    if want_state:
        @pl.when(c == last)
        def _():
            pick = ((lax.broadcasted_iota(I32, (8, 2 * L), 1) == L)
                    & (lax.broadcasted_iota(I32, (8, 2 * L), 0) == 0)).astype(BF16)
            for d in range(2):
                for h in range(B_HEADS):
                    cn = cn_s[d, h]
                    c_o[d, h] = cn[:, :L]
                    n_o[d, h:h + 1, :] = _dot_exact_lhs01(pick, cn, ((1,), (1,)))[0:1]
            m_o[...] = m_s[...]


def _mlstm_call(l, lat, q_all, k_all, v_all, g_all, gt_all, cn0, m0, h_bufs, state_bufs, name):
    L = MLSTM_CHUNK
    nb, nc = (DEC_BATCH, DEC_SEQ // L) if lat else (BATCH, SEQ // L)
    base = N_CTX // L if lat else 0
    chunk = (lambda b, c: base + b * nc + c, lambda b, c: base + b * nc + (nc - 1 - c))

    in_specs, args = [], []
    for d in range(2):
        tok = lambda n, d=d: pl.BlockSpec((L, n), lambda b, c: (chunk[d](b, c), 0))
        in_specs += [tok(512), tok(512), tok(512), tok(LANES),
                     pl.BlockSpec((16, L), lambda b, c, d=d: (0, chunk[d](b, c)))]
        args += [q_all, k_all, v_all, g_all, gt_all]
    if lat:
        in_specs += [pl.BlockSpec((None, 2, B_HEADS, L, 2 * L), lambda b, c: (b, 0, 0, 0, 0)),
                     pl.BlockSpec((None, 2, B_HEADS, 1, LANES), lambda b, c: (b, 0, 0, 0, 0))]
        args += [cn0, m0]
    aliases = {}
    alias_in = (list(h_bufs) if h_bufs is not None else []) + (list(state_bufs) if state_bufs is not None else [])
    if h_bufs is not None:
        aliases[len(args)] = 0
        aliases[len(args) + 1] = 1
    if state_bufs is not None:
        first = len(args) + (2 if h_bufs is not None else 0)
        aliases[first] = 2
        aliases[first + 1] = 3
    in_specs += [pl.BlockSpec(memory_space=pl.ANY)] * len(alias_in)
    args += alias_in
    out_specs = [pl.BlockSpec((L, 512), lambda b, c, d=d: (chunk[d](b, c), 0)) for d in range(2)]
    out_shape = [jax.ShapeDtypeStruct((N_TOK, 512), F32)] * 2
    if not lat:
        out_specs += [pl.BlockSpec((None, None, 2, B_HEADS, L, L), lambda b, c: (b, l, 0, 0, 0, 0)),
                      pl.BlockSpec((None, None, 2, B_HEADS, L), lambda b, c: (b, l, 0, 0, 0)),
                      pl.BlockSpec((None, 2, B_HEADS, 1, LANES), lambda b, c: (b, 0, 0, 0, 0))]
        out_shape += [jax.ShapeDtypeStruct((BATCH, DEPTH, 2, B_HEADS, L, L), F32),
                      jax.ShapeDtypeStruct((BATCH, DEPTH, 2, B_HEADS, L), F32),
                      jax.ShapeDtypeStruct((nb, 2, B_HEADS, 1, LANES), F32)]
    return pl.pallas_call(
        functools.partial(_mlstm_kernel, lat, not lat, len(alias_in)),
        grid=(nb, nc), in_specs=in_specs, out_specs=out_specs, out_shape=out_shape,
        input_output_aliases=aliases,
        scratch_shapes=[pltpu.VMEM((2, B_HEADS, L, 2 * L), F32), pltpu.VMEM((2, B_HEADS, 1, LANES), F32)],
        compiler_params=_cparams(("arbitrary", "arbitrary")), name=name,
    )(*args)


def _merge_kernel(x_ref, mod_ref, oa_ref, hf_ref, hb_ref, ob_ref, oc_ref, g_ref, hn_ref,
                  wpa_ref, wpb_ref, wpc_ref, wo_ref, bo_ref, ln_g_ref, ln_b_ref, wr_ref, br_ref,
                  x1_o, u2_o, ti_o, tw_o):
    gate1 = mod_ref[0, 2:3, :]
    shift2 = mod_ref[0, 3:4, :]
    scale2 = mod_ref[0, 4:5, :]
    wr = _split3(wr_ref[...])
    eio = lax.broadcasted_iota(I32, (N_EXPERTS, 1), 0)
    nt = ((1,), (1,))
    cs = [dict(rs=slice(r, r + MERGE_ROWS)) for r in range(0, TM, MERGE_ROWS)]
    for c in cs:
        c["pa"] = _dot(oa_ref[c["rs"], :], wpa_ref[...])
        c["pc"] = _dot(oc_ref[c["rs"], :], wpc_ref[...])
    for c in cs:
        hm = hf_ref[c["rs"], :] + hb_ref[c["rs"], :]
        parts = []
        for h in range(B_HEADS):
            hh = hm[:, h * LANES:(h + 1) * LANES]
            parts.append(hh * lax.rsqrt(jnp.mean(hh * hh, axis=-1, keepdims=True) + EPS))
        c["out_b"] = (jnp.concatenate(parts, axis=1) * hn_ref[...]
                      * jax.nn.sigmoid(ob_ref[c["rs"], :].astype(F32))).astype(BF16)
    for c in cs:
        c["pb"] = _dot(c["out_b"], wpb_ref[...])
    for c in cs:
        rs = c["rs"]
        c["merged"] = (g_ref[rs, 0:1024].astype(F32) * c["pa"] + g_ref[rs, 1024:2048].astype(F32) * c["pb"]
                       + g_ref[rs, 2048:3072].astype(F32) * c["pc"]).astype(BF16)
    for c in cs:
        c["y"] = _dot(c["merged"], wo_ref[...]) + bo_ref[...]
    for c in cs:
        c["x1"] = _ln(DEEPNORM_ALPHA * x_ref[c["rs"], :] + gate1 * c["y"]) * ln_g_ref[...] + ln_b_ref[...]
        x1_o[c["rs"], :] = c["x1"]
    for c in cs:
        c["u2"] = _ln(c["x1"]) * (1.0 + scale2) + shift2
        u2_o[c["rs"], :] = _pack_rows(c["u2"])
    for c in cs:
        up = _split3(c["u2"])
        c["cur"] = (_dot(wr[0], up[0], nt) + _dot(wr[0], up[1], nt) + _dot(wr[1], up[0], nt)
                    + _dot(wr[0], up[2], nt) + _dot(wr[2], up[0], nt) + _dot(wr[1], up[1], nt)) + br_ref[...]
        c["vals"], c["idxs"] = [], []
    for _ in range(TOP_K):
        for c in cs:
            mx = jnp.max(c["cur"], axis=0, keepdims=True)
            ix = jnp.min(jnp.where(c["cur"] == mx, eio, N_EXPERTS), axis=0, keepdims=True)
            c["vals"].append(mx)
            c["idxs"].append(ix)
            c["cur"] = jnp.where(eio == ix, -jnp.inf, c["cur"])
    for c in cs:
        ex = [jnp.exp(v - c["vals"][0]) for v in c["vals"]]
        inv = 1.0 / sum(ex)
        ti_o[:, c["rs"]] = jnp.concatenate(c["idxs"], axis=0)
        tw_o[:, c["rs"]] = jnp.concatenate([e * inv for e in ex], axis=0)


def _merge_call(x, mod, out_a, hf, hb, ob, out_c, g, hn, wpa, wpb, wpc, wo, bo, ln_g, ln_b, wr_t, br):
    row = lambda n: pl.BlockSpec((TM, n), lambda i: (i, 0))
    const = lambda shape: pl.BlockSpec(shape, lambda i: (0,) * len(shape), pipeline_mode=pl.Buffered(1))
    return pl.pallas_call(
        _merge_kernel,
        grid=(N_TOK // TM,),
        in_specs=[row(D_MODEL),
                  pl.BlockSpec((1, N_MOD, D_MODEL), lambda i: (i * TM // GROUP_ROWS, 0, 0)),
                  row(512), row(512), row(512), row(512), row(512), row(3072),
                  const((1, 512)), const((512, D_MODEL)), const((512, D_MODEL)), const((512, D_MODEL)),
                  const((D_MODEL, D_MODEL)), const((1, D_MODEL)), const((1, D_MODEL)), const((1, D_MODEL)),
                  const((N_EXPERTS, D_MODEL)), const((N_EXPERTS, 1))],
        out_specs=[row(D_MODEL), row(D_PACK),
                   pl.BlockSpec((TOP_K, TM), lambda i: (0, i)), pl.BlockSpec((TOP_K, TM), lambda i: (0, i))],
        out_shape=[jax.ShapeDtypeStruct((N_TOK, D_MODEL), F32), jax.ShapeDtypeStruct((N_TOK, D_PACK), I32),
                   jax.ShapeDtypeStruct((TOP_K, N_TOK), I32), jax.ShapeDtypeStruct((TOP_K, N_TOK), F32)],
        compiler_params=_cparams(("arbitrary",)), name="merge",
    )(x, mod, out_a, hf, hb, ob, out_c, g, hn, wpa, wpb, wpc, wo, bo, ln_g, ln_b, wr_t, br)


def _route_kernel(ti_ref, pos_o, te_o, tv_o, nt_o, rank_s):
    ch = ROUTE_CHUNK
    eio = lax.broadcasted_iota(I32, (N_EXPERTS, 1), 0)
    tri = (lax.broadcasted_iota(I32, (ch, ch), 0) <= lax.broadcasted_iota(I32, (ch, ch), 1)).astype(BF16)

    def onehots(off):
        ti = ti_ref[:, pl.ds(off, ch)]
        return [(eio == ti[k:k + 1, :]).astype(F32) for k in range(TOP_K)]

    def count_body(c, carry):
        off = pl.multiple_of(c * ch, ch)
        ohs = onehots(off)
        oh = ohs[0] + ohs[1] + ohs[2] + ohs[3]
        incl = _dot(oh.astype(BF16), tri)
        excl = incl - oh + carry
        for k in range(TOP_K):
            rank_s[k:k + 1, pl.ds(off, ch)] = jnp.sum(ohs[k] * excl, axis=0, keepdims=True)
        return carry + incl[:, ch - 1:ch]

    counts = lax.fori_loop(0, N_TOK // ch, count_body, jnp.zeros((N_EXPERTS, 1), F32))
    ntile_e = jnp.floor((counts + (TME - 1)) * (1.0 / TME))
    lower = (lax.broadcasted_iota(I32, (N_EXPERTS, N_EXPERTS), 1)
             < lax.broadcasted_iota(I32, (N_EXPERTS, N_EXPERTS), 0)).astype(BF16)
    off_t = _dot(lower, jnp.broadcast_to(ntile_e, (N_EXPERTS, LANES)).astype(BF16))[:, 0:1]
    end_t = off_t + ntile_e
    total = jnp.sum(ntile_e, axis=0, keepdims=True)
    jt = lax.broadcasted_iota(I32, (1, N_TILES_PAD), 1).astype(F32)
    te = jnp.minimum(jnp.sum((end_t <= jt).astype(F32), axis=0, keepdims=True), N_EXPERTS - 1.0)
    sel = (eio.astype(F32) == te).astype(F32)
    cnt_j = jnp.sum(sel * counts, axis=0, keepdims=True)
    off_j = jnp.sum(sel * off_t, axis=0, keepdims=True)
    valid = jnp.clip(cnt_j - (jt - off_j) * TME, 0.0, float(TME))
    te_o[...] = te.astype(I32)
    tv_o[...] = jnp.where(jt < total, valid, 0.0).astype(I32)
    nt_o[...] = jnp.broadcast_to(total, (1, LANES)).astype(I32)
    off_rows = off_t * TME

    def pos_body(c, carry):
        off = pl.multiple_of(c * ch, ch)
        ohs = onehots(off)
        for k in range(TOP_K):
            base = jnp.sum(ohs[k] * off_rows, axis=0, keepdims=True)
            pos_o[k:k + 1, pl.ds(off, ch)] = (rank_s[k:k + 1, pl.ds(off, ch)] + base).astype(I32)
        return carry

    lax.fori_loop(0, N_TOK // ch, pos_body, 0)


def _route_call(top_i):
    return pl.pallas_call(
        _route_kernel,
        out_shape=[jax.ShapeDtypeStruct((TOP_K, N_TOK), I32), jax.ShapeDtypeStruct((1, N_TILES_PAD), I32),
                   jax.ShapeDtypeStruct((1, N_TILES_PAD), I32), jax.ShapeDtypeStruct((1, LANES), I32)],
        scratch_shapes=[pltpu.VMEM((TOP_K, N_TOK), F32)],
        compiler_params=pltpu.CompilerParams(vmem_limit_bytes=VMEM_LIMIT), name="route",
    )(top_i)


def _sc_mesh():
    return plsc.VectorSubcoreMesh(core_axis_name="c", subcore_axis_name="s")


def _sc_worker(n_rows):
    info = plsc.get_sparse_core_info()
    n_workers = info.num_cores * info.num_subcores
    per_w = n_rows // n_workers
    assert per_w * n_workers == n_rows and per_w % SC_WINDOW == 0
    wid = lax.axis_index("s") * info.num_cores + lax.axis_index("c")
    return wid * per_w, per_w


def _dispatch_call(u2, pos_flat):
    w = SC_WINDOW

    @functools.partial(pl.kernel, out_type=jax.ShapeDtypeStruct((P_ROWS, u2.shape[1]), u2.dtype), mesh=_sc_mesh(),
                       scratch_types=[pltpu.VMEM((w,), I32)] * TOP_K
                       + [pltpu.VMEM((w, u2.shape[1]), u2.dtype), pltpu.SemaphoreType.DMA])
    def k(x_hbm, p_hbm, o_hbm, i0, i1, i2, i3, rows_v, sem):
        base, per_w = _sc_worker(N_TOK)
        idx = (i0, i1, i2, i3)

        @pl.loop(0, per_w // w)
        def _(c):
            off = pl.multiple_of(base + c * w, 8)
            pltpu.sync_copy(x_hbm.at[pl.ds(off, w)], rows_v)
            for kk in range(TOP_K):
                pltpu.sync_copy(p_hbm.at[pl.ds(pl.multiple_of(kk * N_TOK + off, 8), w)], idx[kk])
            copies = [pltpu.async_copy(rows_v, o_hbm.at[idx[kk]], sem) for kk in range(TOP_K)]
            for cp in copies:
                cp.wait()

    return k(u2, pos_flat)


def _combine_call(ys, pos_flat):
    w = SC_WINDOW
    n = pos_flat.shape[0]

    @functools.partial(pl.kernel, out_type=jax.ShapeDtypeStruct((n, ys.shape[1]), ys.dtype), mesh=_sc_mesh(),
                       scratch_types=[pltpu.VMEM((w,), I32), pltpu.VMEM((w, ys.shape[1]), ys.dtype),
                                      pltpu.SemaphoreType.DMA])
    def k(y_hbm, p_hbm, o_hbm, idx_v, rows_v, sem):
        base, per_w = _sc_worker(n)

        @pl.loop(0, per_w // w)
        def _(c):
            off = pl.multiple_of(base + c * w, 8)
            pltpu.sync_copy(p_hbm.at[pl.ds(off, w)], idx_v)
            pltpu.async_copy(y_hbm.at[idx_v], rows_v, sem).wait()
            pltpu.sync_copy(rows_v, o_hbm.at[pl.ds(off, w)])

    return k(ys, pos_flat)


def _moe_kernel(te_ref, tv_ref, nt_ref, x_ref, wu_ref, bu_ref, wd_ref, bd_ref, y_ref, wu_s, wd_s):
    i = pl.program_id(0)

    @pl.when(i < nt_ref[0])
    def _():
        e = te_ref[i]
        prev = te_ref[jnp.maximum(i - 1, 0)]

        @pl.when((i == 0) | (e != prev))
        def _():
            wu_s[...] = wu_ref[...].astype(BF16)
            wd_s[...] = wd_ref[...].astype(BF16)

        valid = tv_ref[i]
        half = TME // 2
        for j in range(2):
            @pl.when(valid > j * half)
            def _():
                sl = slice(j * half, (j + 1) * half)
                rows = lax.broadcasted_iota(I32, (half, 1), 0) + j * half
                lo, hi = _unpack_rows(jnp.where(rows < valid, x_ref[sl, :], 0))
                x = jnp.concatenate([lo.astype(BF16), hi.astype(BF16)], axis=1)
                h = _dot(x, wu_s[...]) + bu_ref[...]
                h_glu = jnp.minimum(h[:, :D_EXPERT], SWIGLU_LIMIT)
                h_lin = jnp.clip(h[:, D_EXPERT:], -SWIGLU_LIMIT, SWIGLU_LIMIT)
                act = h_glu * jax.nn.sigmoid(SWIGLU_ALPHA * h_glu) * (h_lin + 1.0)
                y_ref[sl, :] = _pack_rows(_dot(act.astype(BF16), wd_s[...]) + bd_ref[...])


def _moe_call(l, tile_e, tile_v, n_tiles, xs, w_up, b_up, w_down, b_down):
    def tile(i, te, tv, nt):
        return jnp.minimum(i, nt[0] - 1)

    def wmap(i, te, tv, nt):
        return (l, te[tile(i, te, tv, nt)], 0, 0)

    grid_spec = pltpu.PrefetchScalarGridSpec(
        num_scalar_prefetch=3,
        grid=(N_TILES,),
        in_specs=[pl.BlockSpec((TME, D_PACK), lambda i, te, tv, nt: (tile(i, te, tv, nt), 0)),
                  pl.BlockSpec((None, None, D_MODEL, 2 * D_EXPERT), wmap),
                  pl.BlockSpec((None, None, 1, 2 * D_EXPERT), wmap),
                  pl.BlockSpec((None, None, D_EXPERT, D_MODEL), wmap),
                  pl.BlockSpec((None, None, 1, D_MODEL), wmap)],
        out_specs=pl.BlockSpec((TME, D_PACK), lambda i, te, tv, nt: (tile(i, te, tv, nt), 0)),
        scratch_shapes=[pltpu.VMEM((D_MODEL, 2 * D_EXPERT), BF16), pltpu.VMEM((D_EXPERT, D_MODEL), BF16)],
    )
    return pl.pallas_call(
        _moe_kernel, grid_spec=grid_spec,
        out_shape=jax.ShapeDtypeStruct((P_ROWS, D_PACK), I32),
        compiler_params=_cparams(("arbitrary",)), name="moe",
    )(tile_e, tile_v, n_tiles, xs, w_up, b_up.reshape(DEPTH, N_EXPERTS, 1, -1),
      w_down, b_down.reshape(DEPTH, N_EXPERTS, 1, -1))


def _final_kernel(split, yg_ref, tw_ref, x1_ref, mod_ref, ln_g_ref, ln_b_ref, *o_refs):
    tw = tw_ref[...]
    halves = [_unpack_rows(yg_ref[k]) for k in range(TOP_K)]
    moe = jnp.concatenate([sum(tw[:, k:k + 1] * halves[k][j] for k in range(TOP_K)) for j in range(2)], axis=1)
    gate2 = mod_ref[0, 5:6, :]
    y = _ln(DEEPNORM_ALPHA * x1_ref[...] + gate2 * moe) * ln_g_ref[...] + ln_b_ref[...]
    if not split:
        o_refs[0][...] = y
    else:
        is_ctx = pl.program_id(0) < N_CTX // TM

        @pl.when(is_ctx)
        def _():
            o_refs[0][...] = y

        @pl.when(jnp.logical_not(is_ctx))
        def _():
            o_refs[1][...] = y


def _final_call(split, yg, tw, x1, mod, ln_g, ln_b):
    row = lambda n: pl.BlockSpec((TM, n), lambda i: (i, 0))
    const = lambda shape: pl.BlockSpec(shape, lambda i: (0,) * len(shape))
    n_ctx_tiles = N_CTX // TM
    if split:
        out_specs = [pl.BlockSpec((TM, D_MODEL), lambda i: (jnp.minimum(i, n_ctx_tiles - 1), 0)),
                     pl.BlockSpec((TM, D_MODEL), lambda i: (jnp.maximum(i - n_ctx_tiles, 0), 0))]
        out_shape = [jax.ShapeDtypeStruct((N_CTX, D_MODEL), F32), jax.ShapeDtypeStruct((N_LAT, D_MODEL), F32)]
    else:
        out_specs = row(D_MODEL)
        out_shape = jax.ShapeDtypeStruct((N_TOK, D_MODEL), F32)
    return pl.pallas_call(
        functools.partial(_final_kernel, split),
        grid=(N_TOK // TM,),
        in_specs=[pl.BlockSpec((TOP_K, TM, D_PACK), lambda i: (0, i, 0)), row(TOP_K), row(D_MODEL),
                  pl.BlockSpec((1, N_MOD, D_MODEL), lambda i: (i * TM // GROUP_ROWS, 0, 0)),
                  const((1, D_MODEL)), const((1, D_MODEL))],
        out_specs=out_specs, out_shape=out_shape,
        compiler_params=_cparams(("arbitrary",)), name="final",
    )(yg, tw, x1, mod, ln_g, ln_b)


def _rope_tables():
    half = A_HEAD_DIM // 2
    freqs = ROPE_BASE ** (-jnp.arange(0, half, 2, dtype=F32) / half)
    t = jnp.arange(DEC_SEQ, dtype=jnp.int32)
    ang_r = (t // GRID_W).astype(F32)[:, None] * freqs[None, :]
    ang_c = (t % GRID_W).astype(F32)[:, None] * freqs[None, :]
    cos64 = jnp.concatenate([jnp.cos(ang_r)] * 2 + [jnp.cos(ang_c)] * 2, axis=1)
    sin64 = jnp.concatenate([-jnp.sin(ang_r), jnp.sin(ang_r), -jnp.sin(ang_c), jnp.sin(ang_c)], axis=1)
    cos_t = jnp.concatenate([jnp.ones((TM, LANES), F32), jnp.tile(cos64, (1, 2))], axis=0)
    sin_t = jnp.concatenate([jnp.zeros((TM, LANES), F32), jnp.tile(sin64, (1, 2))], axis=0)
    return cos_t, sin_t


def _permute_in_cols(t):
    qa = [t[..., h * 64:(h + 1) * 64] for h in QA_HEAD_ORDER]
    pad = jnp.zeros(t.shape[:-1] + (D_IN_PAD - D_IN,), t.dtype)
    return jnp.concatenate(qa + [t[..., 512:2816], t[..., 2832:7440], t[..., 2816:2832], pad], axis=-1)


def _layer(l, x, mod, tables, caches, p, outs, weights):
    cos_t, sin_t = tables
    ck_a, cv_a, ck_c, cv_c, cn0, m0 = caches
    lam_init = 0.8 - 0.6 * math.exp(-0.3 * l)
    w_in, b_in = p["w_in"], p["b_in"]
    w_p = _permute_in_cols(w_in).astype(BF16)
    b_p = _permute_in_cols(b_in).reshape(1, D_IN_PAD)
    w_gt = w_in[:, 2816:2832].T.astype(BF16)
    b_gt = b_in[2816:2832].reshape(16, 1)
    gq = jnp.tile(p["qn_a"], A_HEADS).reshape(1, 512)
    gk = jnp.tile(p["kn_a"], A_KV_HEADS).reshape(1, 128)
    gmat = jnp.asarray(np.kron(np.eye(8), np.ones((64, 64))), BF16)

    (qa, kab, vab, qb, kb, vb, ob, qc, kcb, vcb, g, ifg, ifgt, nka, nva, nkc, nvc) = _inproj_call(
        l, x, mod, cos_t, sin_t, w_p, b_p, w_gt, b_gt, gq, gk, gmat, None if outs is None else outs[:4])

    lam_c = p["lam_c"]
    sn_c = p["sn_c"].reshape(1, C_V_DIM)
    oa = _attn_call("a", False, lam_init, qa, kab, vab, None, None, None, None, None, "attn_a_ctx")
    oa = _attn_call("a", True, lam_init, qa, kab, vab, ck_a, cv_a, None, None, oa, "attn_a_lat")
    oc = _attn_call("c", False, lam_init, qc, kcb, vcb, None, None, lam_c, sn_c, None, "attn_c_ctx")
    oc = _attn_call("c", True, lam_init, qc, kcb, vcb, ck_c, cv_c, lam_c, sn_c, oc, "attn_c_lat")
    hf, hb, c_fin, n_fin, m_fin = _mlstm_call(l, False, qb, kb, vb, ifg, ifgt, None, None, None,
                                              None if outs is None else outs[4:6], "mlstm_ctx")
    hf, hb = _mlstm_call(l, True, qb, kb, vb, ifg, ifgt, cn0, m0, (hf, hb), None, "mlstm_lat")

    wpa = p["w_pa"].reshape(A_HEADS, A_HEAD_DIM, D_MODEL)[np.asarray(QA_HEAD_ORDER)].reshape(512, D_MODEL)
    x1, u2, top_i, top_w = _merge_call(
        x, mod, oa, hf, hb, ob, oc, g, p["hn_b"].reshape(1, 512),
        wpa.astype(BF16), p["w_pb"].astype(BF16), p["w_pc"].astype(BF16), p["w_o"].astype(BF16),
        p["b_o"].reshape(1, D_MODEL), p["ln1_g"].reshape(1, D_MODEL), p["ln1_b"].reshape(1, D_MODEL),
        p["w_router"].T, p["b_router"].reshape(N_EXPERTS, 1))

    pos, tile_e, tile_v, n_tiles = _route_call(top_i)
    pos_flat = pos.reshape(TOP_K * N_TOK)
    xs = _dispatch_call(u2, pos_flat)
    ys = _moe_call(l, tile_e[0, :N_TILES], tile_v[0, :N_TILES], n_tiles[0, :1], xs,
                   weights["w_up"], weights["b_up"], weights["w_down"], weights["b_down"])
    yg = _combine_call(ys, pos_flat).reshape(TOP_K, N_TOK, D_PACK)
    x2 = _final_call(l == DEPTH - 1, yg, top_w.T, x1, mod,
                     p["ln2_g"].reshape(1, D_MODEL), p["ln2_b"].reshape(1, D_MODEL))
    m_l = m_fin[:, :, :, 0, 0]
    m_all = m_l[:, None] if outs is None else jnp.concatenate([outs[6], m_l[:, None]], axis=1)
    return x2, (nka, nva, nkc, nvc, c_fin, n_fin, m_all)


def kernel(x_prompt, x_sample, cache_k_gqa, cache_v_gqa, cache_k_diff, cache_v_diff, state_mlstm_c, state_mlstm_n, state_mlstm_m, c, c_ctx, w_in, b_in, qn_a, kn_a, hn_b, lam_c, sn_c, w_pa, w_pb, w_pc, w_o, b_o, w_mod, b_mod, ln1_g, ln1_b, ln2_g, ln2_b, w_router, b_router, w_up, b_up, w_down, b_down):
    x = jnp.concatenate([x_prompt.reshape(N_CTX, D_MODEL), x_sample.reshape(N_LAT, D_MODEL)], axis=0)
    cvecs = jnp.zeros((8, D_MODEL), F32).at[0].set(c_ctx).at[1:1 + DEC_BATCH].set(c)
    mod_all = _mod_call(cvecs, w_mod, b_mod).reshape(DEPTH, 8, N_MOD, D_MODEL)[:, :N_GROUPS]
    tables = _rope_tables()
    params = dict(w_in=w_in, b_in=b_in, qn_a=qn_a, kn_a=kn_a, hn_b=hn_b, lam_c=lam_c, sn_c=sn_c, w_pa=w_pa,
                  w_pb=w_pb, w_pc=w_pc, w_o=w_o, b_o=b_o, ln1_g=ln1_g, ln1_b=ln1_b, ln2_g=ln2_g, ln2_b=ln2_b,
                  w_router=w_router, b_router=b_router)
    expert_weights = dict(w_up=w_up, b_up=b_up, w_down=w_down, b_down=b_down)
    outs = None
    for l in range(DEPTH):
        p = {k: v[l] for k, v in params.items()}
        ck_a = cache_k_gqa[:, l].transpose(0, 2, 1, 3).reshape(DEC_BATCH, PAST_LEN, 128).astype(BF16)
        cv_a = cache_v_gqa[:, l].transpose(0, 2, 1, 3).reshape(DEC_BATCH, PAST_LEN, 128).astype(BF16)
        ck_c = cache_k_diff[:, l].astype(BF16)
        cv_c = cache_v_diff[:, l].astype(BF16)
        cn0 = jnp.concatenate([state_mlstm_c[:, l], state_mlstm_n[:, l][..., None],
                               jnp.zeros((DEC_BATCH, 2, B_HEADS, B_HEAD_DIM, B_HEAD_DIM - 1), F32)], axis=-1)
        m0 = jnp.broadcast_to(state_mlstm_m[:, l][..., None, None], (DEC_BATCH, 2, B_HEADS, 1, LANES))
        x, outs = _layer(l, x, mod_all[l], tables, (ck_a, cv_a, ck_c, cv_c, cn0, m0), p, outs, expert_weights)
    y_ctx, y_lat = x
    return (y_ctx.reshape(BATCH, SEQ, D_MODEL), y_lat.reshape(DEC_BATCH, DEC_SEQ, D_MODEL)) + tuple(outs)
```

```python
import functools
import math

import numpy as np
import jax
import jax.numpy as jnp
from jax import lax
from jax.experimental import pallas as pl
from jax.experimental.pallas import tpu as pltpu
from jax.experimental.pallas import tpu_sc as plsc

F32 = jnp.float32
BF16 = jnp.bfloat16
I32 = jnp.int32

D_MODEL = 1024
BATCH = 16
SEQ = 256
DEPTH = 2
DEC_BATCH = 2
DEC_SEQ = 4096
PAST_LEN = 512
GRID_W = 64
ROPE_BASE = 10000.0
EPS = 1e-6
A_HEADS = 8
A_KV_HEADS = 2
A_HEAD_DIM = 64
B_HEADS = 4
B_HEAD_DIM = 128
MLSTM_CHUNK = 128
C_HEADS = 4
C_HEAD_DIM = 64
C_V_DIM = 128
N_EXPERTS = 32
TOP_K = 4
D_EXPERT = 1024
SWIGLU_LIMIT = 7.0
SWIGLU_ALPHA = 1.702
N_MOD = 6
DEEPNORM_ALPHA = (2 * DEPTH) ** 0.25

N_CTX = BATCH * SEQ
N_LAT = DEC_BATCH * DEC_SEQ
N_TOK = N_CTX + N_LAT
N_GROUPS = 1 + DEC_BATCH
GROUP_ROWS = 4096
assert N_CTX == GROUP_ROWS and DEC_SEQ == GROUP_ROWS

LANES = 128
TM = 256
TQ = 512
KEY_BLOCK = 1024
MERGE_ROWS = 128
TME = 512
N_TILES = (N_TOK * TOP_K) // TME + N_EXPERTS
N_TILES_PAD = 128
P_ROWS = N_TILES * TME
ROUTE_CHUNK = 512
D_PACK = D_MODEL // 2
SC_WINDOW = 64
VMEM_LIMIT = 56 * 1024 * 1024

_W = dict(qa=(0, 512), ka=(512, 640), va=(640, 768), qb=(768, 1280), kb=(1280, 1792), vb=(1792, 2304),
          ob=(2304, 2816), ig=(2816, 2824), fg=(2824, 2832), qc=(2832, 3344), kc=(3344, 3856),
          vc=(3856, 4368), g=(4368, 7440))
QA_HEAD_ORDER = (0, 4, 1, 5, 2, 6, 3, 7)
D_IN = 7440
D_IN_PAD = 7552
P_QA, P_KA, P_VA, P_QB, P_KB, P_VB, P_OB, P_QC, P_KC, P_VC, P_G, P_IF = (
    0, 512, 640, 768, 1280, 1792, 2304, 2816, 3328, 3840, 4352, 7424)


def _cparams(sem):
    return pltpu.CompilerParams(dimension_semantics=sem, vmem_limit_bytes=VMEM_LIMIT)


def _split3(a):
    a1 = a.astype(BF16)
    r = a - a1.astype(F32)
    a2 = r.astype(BF16)
    a3 = (r - a2.astype(F32)).astype(BF16)
    return a1, a2, a3


def _dot(a, b, dims=None):
    if dims is None:
        return jnp.dot(a, b, preferred_element_type=F32)
    return lax.dot_general(a, b, (dims, ((), ())), preferred_element_type=F32)


def _dot_exact_rhs01(a, m, dims=None):
    return sum(_dot(p, m, dims) for p in _split3(a))


def _dot_exact_lhs01(m, a, dims=None):
    return sum(_dot(m, p, dims) for p in _split3(a))


def _pack_rows(x):
    xb = x.astype(BF16).astype(F32)
    lo = pltpu.bitcast(xb[:, :D_PACK], I32)
    lo = lax.shift_right_logical(lo, jnp.full_like(lo, 16))
    hi = pltpu.bitcast(xb[:, D_PACK:], I32) & jnp.int32(-65536)
    return lo | hi


def _unpack_rows(w):
    return (pltpu.bitcast(lax.shift_left(w, jnp.full_like(w, 16)), F32),
            pltpu.bitcast(w & jnp.int32(-65536), F32))


def _ln(x):
    mu = jnp.mean(x, axis=-1, keepdims=True)
    xc = x - mu
    var = jnp.mean(xc * xc, axis=-1, keepdims=True)
    return xc * lax.rsqrt(var + EPS)


def _log_sigmoid(x):
    return jnp.minimum(x, 0.0) - jnp.log(1.0 + jnp.exp(-jnp.abs(x)))


def _rope(x, cos, sin):
    lane = lax.broadcasted_iota(I32, (1, LANES), 1)
    first = (lane % 32) < 16
    outs = []
    for j in range(x.shape[1] // LANES):
        xj = x[:, j * LANES:(j + 1) * LANES]
        partner = jnp.where(first, pltpu.roll(xj, LANES - 16, 1), pltpu.roll(xj, 16, 1))
        outs.append(xj * cos + partner * sin)
    return outs[0] if len(outs) == 1 else jnp.concatenate(outs, axis=1)


def _mod_kernel(c_ref, w_ref, b_ref, o_ref):
    c = c_ref[...]
    s = c * jax.nn.sigmoid(c)
    o_ref[...] = _dot(s.astype(BF16), w_ref[...].astype(BF16)) + b_ref[...]


def _mod_call(cvecs, w_mod, b_mod):
    tn = 1536
    n = N_MOD * D_MODEL
    return pl.pallas_call(
        _mod_kernel,
        grid=(DEPTH, n // tn),
        in_specs=[pl.BlockSpec((8, D_MODEL), lambda l, j: (0, 0)),
                  pl.BlockSpec((None, D_MODEL, tn), lambda l, j: (l, 0, j)),
                  pl.BlockSpec((None, 1, tn), lambda l, j: (l, 0, j))],
        out_specs=pl.BlockSpec((None, 8, tn), lambda l, j: (l, 0, j)),
        out_shape=jax.ShapeDtypeStruct((DEPTH, 8, n), F32),
        compiler_params=_cparams(("arbitrary", "arbitrary")),
        name="mod",
    )(cvecs, w_mod, b_mod.reshape(DEPTH, 1, n))


def _inproj_kernel(n_alias, x_ref, mod_ref, cos_ref, sin_ref, w_ref, b_ref, wgt_ref, bgt_ref, gq_ref, gk_ref,
                   gmat_ref, *refs):
    (qa_o, kab_o, vab_o, qb_o, kb_o, vb_o, ob_o, qc_o, kcb_o, vcb_o, g_o, ifg_o, ifgt_o,
     nka_o, nva_o, nkc_o, nvc_o) = refs[n_alias:]
    is_ctx = pl.program_id(0) < N_CTX // TM
    x = x_ref[...]
    shift = mod_ref[0, 0:1, :]
    scale = mod_ref[0, 1:2, :]
    ub = (_ln(x) * (1.0 + scale) + shift).astype(BF16)
    cos = cos_ref[...]
    sin = sin_ref[...]

    def proj(lo, n):
        return _dot(ub, w_ref[:, lo:lo + n]) + b_ref[:, lo:lo + n]

    def headnorm(z, gmat, gain):
        ss = _dot((z * z).astype(BF16), gmat)
        return z * lax.rsqrt(ss * (1.0 / A_HEAD_DIM) + EPS) * gain

    segs = [(P_QA, 512), (P_KA, 128), (P_VA, 128), (P_QB, 512), (P_KB, 512), (P_VB, 512), (P_OB, 512),
            (P_QC, 512), (P_KC, 512), (P_VC, 512), (P_G, 1024), (P_G + 1024, 1024), (P_G + 2048, 1024),
            (P_IF, LANES)]
    keep = {}

    def epilogue(k, z):
        if k == 0:
            z = headnorm(z, gmat_ref[...], gq_ref[...])
            qa_o[...] = (_rope(z, cos, sin) * 0.125).astype(BF16)
        elif k == 1:
            keep["k"] = _rope(headnorm(z, gmat_ref[0:128, 0:128], gk_ref[...]), cos, sin)
            kab_o[...] = keep["k"].astype(BF16)
        elif k == 2:
            keep["v"] = z
            vab_o[...] = z.astype(BF16)
        elif k == 3:
            qb_o[...] = z.astype(BF16)
        elif k == 4:
            kb_o[...] = (z * (B_HEAD_DIM ** -0.5)).astype(BF16)
        elif k == 5:
            vb_o[...] = z.astype(BF16)
        elif k == 6:
            ob_o[...] = z.astype(BF16)
        elif k == 7:
            qc_o[...] = (_rope(z, cos, sin) * 0.125).astype(BF16)
        elif k == 8:
            keep["kc"] = _rope(z, cos, sin)
            kcb_o[...] = keep["kc"].astype(BF16)
        elif k == 9:
            keep["vc"] = z
            vcb_o[...] = z.astype(BF16)
        elif k in (10, 11, 12):
            g_o[:, (k - 10) * 1024:(k - 9) * 1024] = jax.nn.sigmoid(z).astype(BF16)
        else:
            lane = lax.broadcasted_iota(I32, (1, LANES), 1)
            ifg_o[...] = jnp.where(lane >= 8, _log_sigmoid(z), z)

    z_prev = proj(*segs[0])
    for k in range(1, len(segs)):
        z_next = proj(*segs[k])
        epilogue(k - 1, z_prev)
        z_prev = z_next
    zt = _dot(wgt_ref[...], ub, ((1,), (1,))) + bgt_ref[...]
    epilogue(len(segs) - 1, z_prev)
    row = lax.broadcasted_iota(I32, (16, 1), 0)
    ifgt_o[...] = jnp.where(row >= 8, _log_sigmoid(zt), zt)

    @pl.when(is_ctx)
    def _():
        for h in range(A_KV_HEADS):
            nka_o[h] = keep["k"][:, h * A_HEAD_DIM:(h + 1) * A_HEAD_DIM]
            nva_o[h] = keep["v"][:, h * A_HEAD_DIM:(h + 1) * A_HEAD_DIM]
        for h in range(C_HEADS):
            nkc_o[h] = keep["kc"][:, h * LANES:(h + 1) * LANES]
            nvc_o[h] = keep["vc"][:, h * LANES:(h + 1) * LANES]


def _inproj_call(l, x, mod, cos_t, sin_t, w_p, b_p, w_gt, b_gt, gq, gk, gmat, cache_outs):
    assert TM == SEQ
    n_ctx_tiles = N_CTX // TM
    n_pos_tiles = DEC_SEQ // TM

    def tbl(i):
        return (jnp.where(i < n_ctx_tiles, 0, 1 + (i - n_ctx_tiles) % n_pos_tiles), 0)

    row = lambda n: pl.BlockSpec((TM, n), lambda i: (i, 0))
    const = lambda shape: pl.BlockSpec(shape, lambda i: (0,) * len(shape), pipeline_mode=pl.Buffered(1))
    sds = lambda n, dt: jax.ShapeDtypeStruct((N_TOK, n), dt)
    out_defs = [(512, BF16), (128, BF16), (128, BF16), (512, BF16), (512, BF16), (512, BF16), (512, BF16),
                (512, BF16), (512, BF16), (512, BF16), (3072, BF16), (LANES, F32)]
    cache_defs = [(A_KV_HEADS, A_HEAD_DIM), (A_KV_HEADS, A_HEAD_DIM), (C_HEADS, LANES), (C_HEADS, LANES)]
    cache_spec = lambda nh, d: pl.BlockSpec((None, None, nh, SEQ, d),
                                            lambda i: (jnp.minimum(i, n_ctx_tiles - 1), l, 0, 0, 0))
    alias_in = [] if cache_outs is None else list(cache_outs)
    n_in = 11
    n_plain_out = len(out_defs) + 1
    return pl.pallas_call(
        functools.partial(_inproj_kernel, len(alias_in)),
        grid=(N_TOK // TM,),
        in_specs=[row(D_MODEL),
                  pl.BlockSpec((1, N_MOD, D_MODEL), lambda i: (i * TM // GROUP_ROWS, 0, 0)),
                  pl.BlockSpec((TM, LANES), tbl), pl.BlockSpec((TM, LANES), tbl),
                  const((D_MODEL, D_IN_PAD)), const((1, D_IN_PAD)),
                  const((16, D_MODEL)), const((16, 1)),
                  const((1, 512)), const((1, 128)), const((512, 512))]
        + [pl.BlockSpec(memory_space=pl.ANY)] * len(alias_in),
        out_specs=[row(n) for n, _ in out_defs] + [pl.BlockSpec((16, TM), lambda i: (0, i))]
        + [cache_spec(nh, d) for nh, d in cache_defs],
        out_shape=[sds(n, dt) for n, dt in out_defs] + [jax.ShapeDtypeStruct((16, N_TOK), F32)]
        + [jax.ShapeDtypeStruct((BATCH, DEPTH, nh, SEQ, d), F32) for nh, d in cache_defs],
        input_output_aliases={n_in + j: n_plain_out + j for j in range(len(alias_in))},
        compiler_params=_cparams(("arbitrary",)),
        name="inproj",
    )(x, mod, cos_t, sin_t, w_p, b_p, w_gt, b_gt, gq, gk, gmat, *alias_in)


def _attn_kernel(mode, nseg, lam_init, *refs):
    q_ref = refs[0]
    kv = refs[1:1 + 2 * nseg]
    rest = refs[1 + 2 * nseg:]
    o_ref = rest[-1]
    tq = q_ref.shape[0]
    lane = lax.broadcasted_iota(I32, (1, LANES), 1)
    if mode == "c":
        lam_ref, sn_ref = rest[0], rest[1]
        lp = lam_ref[...]
        lam = (jnp.exp(jnp.sum(lp[0:1] * lp[1:2], axis=-1, keepdims=True))
               - jnp.exp(jnp.sum(lp[2:3] * lp[3:4], axis=-1, keepdims=True)) + lam_init)
    for blk in range(q_ref.shape[1] // LANES):
        qs = slice(blk * LANES, (blk + 1) * LANES)
        ks = qs if kv[0].shape[1] > LANES else slice(0, LANES)
        q = q_ref[:, qs]
        zero = jnp.zeros_like(q)
        qq = jnp.concatenate([jnp.where(lane < 64, q, zero), jnp.where(lane >= 64, q, zero)], axis=0)
        m = l = acc = None
        for j in range(nseg):
            k_ref, v_ref = kv[2 * j], kv[2 * j + 1]
            for off in range(0, k_ref.shape[0], KEY_BLOCK):
                n = min(KEY_BLOCK, k_ref.shape[0] - off)
                s = _dot(qq, k_ref[off:off + n, ks], ((1,), (1,)))
                bm = jnp.max(s, axis=-1, keepdims=True)
                if m is None:
                    m = bm
                    e = jnp.exp(s - m)
                    l = jnp.sum(e, axis=-1, keepdims=True)
                    acc = _dot(e.astype(BF16), v_ref[off:off + n, ks])
                else:
                    m_new = jnp.maximum(m, bm)
                    alpha = jnp.exp(m - m_new)
                    e = jnp.exp(s - m_new)
                    l = alpha * l + jnp.sum(e, axis=-1, keepdims=True)
                    acc = alpha * acc + _dot(e.astype(BF16), v_ref[off:off + n, ks])
                    m = m_new
        o = acc * (1.0 / l)
        if mode == "a":
            out = jnp.where(lane < 64, o[:tq], o[tq:])
        else:
            o = o[:tq] - lam * o[tq:]
            ms = jnp.mean(o * o, axis=-1, keepdims=True)
            out = o * lax.rsqrt(ms + EPS) * sn_ref[...] * (1.0 - lam_init)
        o_ref[:, qs] = out.astype(o_ref.dtype)


def _attn_call(mode, lat, lam_init, q_all, k_all, v_all, k_cache, v_cache, lam_c, sn_c, out_buf, name):
    wide = k_all.shape[1] == 512
    kcol = (lambda h: h) if wide else (lambda h: 0)
    if lat:
        nq = DEC_SEQ // TQ
        base_q = N_CTX // TQ
        grid = (DEC_BATCH, nq, 4)
        q_spec = pl.BlockSpec((TQ, LANES), lambda b, i, h: (base_q + b * nq + i, h))
        kn_spec = pl.BlockSpec((DEC_SEQ, LANES), lambda b, i, h: (1 + b, kcol(h)))
        if wide:
            kc_spec = pl.BlockSpec((None, None, PAST_LEN, LANES), lambda b, i, h: (b, h, 0, 0))
        else:
            kc_spec = pl.BlockSpec((None, PAST_LEN, LANES), lambda b, i, h: (b, 0, 0))
        in_specs = [q_spec, kc_spec, kc_spec, kn_spec, kn_spec]
        args = [q_all, k_cache, v_cache, k_all, v_all]
        nseg = 2
        o_spec = pl.BlockSpec((TQ, LANES), lambda b, i, h: (base_q + b * nq + i, h))
        sem = ("arbitrary", "arbitrary", "arbitrary")
    else:
        grid = (BATCH,)
        q_spec = pl.BlockSpec((SEQ, 512), lambda b: (b, 0))
        k_spec = pl.BlockSpec((SEQ, k_all.shape[1]), lambda b: (b, 0))
        in_specs = [q_spec, k_spec, k_spec]
        args = [q_all, k_all, v_all]
        nseg = 1
        o_spec = pl.BlockSpec((SEQ, 512), lambda b: (b, 0))
        sem = ("arbitrary",)
    if mode == "c":
        zero_map = (lambda b, i, h: (0, 0)) if lat else (lambda b: (0, 0))
        in_specs += [pl.BlockSpec((4, C_HEAD_DIM), zero_map), pl.BlockSpec((1, C_V_DIM), zero_map)]
        args += [lam_c, sn_c]
    aliases = {}
    if out_buf is not None:
        aliases = {len(args): 0}
        in_specs += [pl.BlockSpec(memory_space=pl.ANY)]
        args += [out_buf]
    return pl.pallas_call(
        functools.partial(_attn_kernel, mode, nseg, lam_init),
        grid=grid, in_specs=in_specs, out_specs=o_spec,
        out_shape=jax.ShapeDtypeStruct((N_TOK, 512), BF16), input_output_aliases=aliases,
        compiler_params=_cparams(sem), name=name,
    )(*args)


def _mlstm_kernel(has_init, want_state, n_alias, *refs):
    tok = (refs[0:5], refs[5:10])
    pos = 10
    if has_init:
        cn0_ref, m0_ref = refs[pos:pos + 2]
        pos += 2
    pos += n_alias
    h_refs = refs[pos:pos + 2]
    pos += 2
    if want_state:
        c_o, n_o, m_o = refs[pos:pos + 3]
        pos += 3
    cn_s, m_s = refs[pos:pos + 2]

    c = pl.program_id(1)
    last = pl.num_programs(1) - 1
    L = MLSTM_CHUNK

    @pl.when(c == 0)
    def _():
        if has_init:
            cn_s[...] = cn0_ref[...]
            m_s[...] = m0_ref[...]
        else:
            cn_s[...] = jnp.zeros_like(cn_s)
            m_s[...] = jnp.zeros_like(m_s)

    row = lax.broadcasted_iota(I32, (L, L), 0)
    col = lax.broadcasted_iota(I32, (L, L), 1)
    lane = lax.broadcasted_iota(I32, (1, LANES), 1)
    ones_col = jnp.where(lane == 0, 1.0, 0.0).astype(BF16)

    chains = []
    for d in range(2):
        q_ref, k_ref, v_ref, g_ref, gt_ref = tok[d]
        keep = (col <= row) if d == 0 else (col >= row)
        m_c = keep.astype(BF16)
        m_r = ((row <= col) if d == 0 else (row >= col)).astype(BF16)
        g = g_ref[...]
        gt = gt_ref[...]
        bc = _dot_exact_lhs01(m_c, g)
        br = _dot_exact_rhs01(gt, m_r)
        for h in range(B_HEADS):
            sl = slice(h * L, (h + 1) * L)
            gi, gf = 4 * d + h, 8 + 4 * d + h
            ch = dict(d=d, h=h, sl=sl, keep=keep, q=q_ref[:, sl], k=k_ref[:, sl], v=v_ref[:, sl],
                      i_col=g[:, gi:gi + 1], f_col=g[:, gf:gf + 1], b_col=bc[:, gf:gf + 1],
                      r_row=gt[gi:gi + 1, :] - br[gf:gf + 1, :],
                      m_prev=m_s[d, h][:, 0:1], cn_prev=cn_s[d, h])
            chains.append(ch)
    for ch in chains:
        ch["s"] = _dot(ch["q"], ch["k"], ((1,), (1,)))
        ch["qc"] = _dot(ch["q"], ch["cn_prev"].astype(BF16))
    for ch in chains:
        ch["total"] = jnp.sum(ch["f_col"], axis=0, keepdims=True)
        ch["inter"] = ch["b_col"] + ch["m_prev"]
        ch["v_aug"] = jnp.concatenate([ch["v"], jnp.broadcast_to(ones_col, (L, LANES))], axis=1)
    for ch in chains:
        ch["dmat"] = jnp.where(ch["keep"], ch["b_col"] + ch["r_row"], -jnp.inf)
    for ch in chains:
        ch["dmax"] = jnp.max(ch["dmat"], axis=-1, keepdims=True)
    for ch in chains:
        ch["m_row"] = jnp.maximum(ch["inter"], ch["dmax"])
    for ch in chains:
        ch["sc"] = (ch["s"] * jnp.exp(ch["dmat"] - ch["m_row"])).astype(BF16)
    for ch in chains:
        ch["a"] = _dot(ch["sc"], ch["v_aug"])
    for ch in chains:
        ch["nd"] = jnp.exp(ch["inter"] - ch["m_row"]) * ch["qc"] + ch["a"]
    for ch in chains:
        ch["scale"] = 1.0 / jnp.maximum(jnp.abs(ch["nd"][:, L:L + 1]), jnp.exp(-ch["m_row"]))
    for ch in chains:
        h_refs[ch["d"]][:, ch["sl"]] = ch["nd"][:, :L] * ch["scale"]
    for ch in chains:
        ch["wel"] = ch["total"] - ch["b_col"] + ch["i_col"]
        ch["m_new"] = jnp.maximum(ch["total"] + ch["m_prev"], jnp.max(ch["wel"], axis=0, keepdims=True))
    for ch in chains:
        ch["kw"] = (jnp.exp(ch["wel"] - ch["m_new"]) * ch["k"].astype(F32)).astype(BF16)
    for ch in chains:
        ch["u"] = _dot(ch["kw"], ch["v_aug"], ((0,), (0,)))
    for ch in chains:
        cn_s[ch["d"], ch["h"]] = jnp.exp(ch["total"] + ch["m_prev"] - ch["m_new"]) * ch["cn_prev"] + ch["u"]
        m_s[ch["d"], ch["h"]] = jnp.broadcast_to(ch["m_new"], (1, LANES))

    if want_state:
        @pl.when(c == last)
        def _():
            pick = ((lax.broadcasted_iota(I32, (8, 2 * L), 1) == L)
                    & (lax.broadcasted_iota(I32, (8, 2 * L), 0) == 0)).astype(BF16)
            for d in range(2):
                for h in range(B_HEADS):
                    cn = cn_s[d, h]
                    c_o[d, h] = cn[:, :L]
                    n_o[d, h:h + 1, :] = _dot_exact_lhs01(pick, cn, ((1,), (1,)))[0:1]
            m_o[...] = m_s[...]


def _mlstm_call(l, lat, q_all, k_all, v_all, g_all, gt_all, cn0, m0, h_bufs, state_bufs, name):
    L = MLSTM_CHUNK
    nb, nc = (DEC_BATCH, DEC_SEQ // L) if lat else (BATCH, SEQ // L)
    base = N_CTX // L if lat else 0
    chunk = (lambda b, c: base + b * nc + c, lambda b, c: base + b * nc + (nc - 1 - c))

    in_specs, args = [], []
    for d in range(2):
        tok = lambda n, d=d: pl.BlockSpec((L, n), lambda b, c: (chunk[d](b, c), 0))
        in_specs += [tok(512), tok(512), tok(512), tok(LANES),
                     pl.BlockSpec((16, L), lambda b, c, d=d: (0, chunk[d](b, c)))]
        args += [q_all, k_all, v_all, g_all, gt_all]
    if lat:
        in_specs += [pl.BlockSpec((None, 2, B_HEADS, L, 2 * L), lambda b, c: (b, 0, 0, 0, 0)),
                     pl.BlockSpec((None, 2, B_HEADS, 1, LANES), lambda b, c: (b, 0, 0, 0, 0))]
        args += [cn0, m0]
    aliases = {}
    alias_in = (list(h_bufs) if h_bufs is not None else []) + (list(state_bufs) if state_bufs is not None else [])
    if h_bufs is not None:
        aliases[len(args)] = 0
        aliases[len(args) + 1] = 1
    if state_bufs is not None:
        first = len(args) + (2 if h_bufs is not None else 0)
        aliases[first] = 2
        aliases[first + 1] = 3
    in_specs += [pl.BlockSpec(memory_space=pl.ANY)] * len(alias_in)
    args += alias_in
    out_specs = [pl.BlockSpec((L, 512), lambda b, c, d=d: (chunk[d](b, c), 0)) for d in range(2)]
    out_shape = [jax.ShapeDtypeStruct((N_TOK, 512), F32)] * 2
    if not lat:
        out_specs += [pl.BlockSpec((None, None, 2, B_HEADS, L, L), lambda b, c: (b, l, 0, 0, 0, 0)),
                      pl.BlockSpec((None, None, 2, B_HEADS, L), lambda b, c: (b, l, 0, 0, 0)),
                      pl.BlockSpec((None, 2, B_HEADS, 1, LANES), lambda b, c: (b, 0, 0, 0, 0))]
        out_shape += [jax.ShapeDtypeStruct((BATCH, DEPTH, 2, B_HEADS, L, L), F32),
                      jax.ShapeDtypeStruct((BATCH, DEPTH, 2, B_HEADS, L), F32),
                      jax.ShapeDtypeStruct((nb, 2, B_HEADS, 1, LANES), F32)]
    return pl.pallas_call(
        functools.partial(_mlstm_kernel, lat, not lat, len(alias_in)),
        grid=(nb, nc), in_specs=in_specs, out_specs=out_specs, out_shape=out_shape,
        input_output_aliases=aliases,
        scratch_shapes=[pltpu.VMEM((2, B_HEADS, L, 2 * L), F32), pltpu.VMEM((2, B_HEADS, 1, LANES), F32)],
        compiler_params=_cparams(("arbitrary", "arbitrary")), name=name,
    )(*args)


def _merge_kernel(x_ref, mod_ref, oa_ref, hf_ref, hb_ref, ob_ref, oc_ref, g_ref, hn_ref,
                  wpa_ref, wpb_ref, wpc_ref, wo_ref, bo_ref, ln_g_ref, ln_b_ref, wr_ref, br_ref,
                  x1_o, u2_o, ti_o, tw_o):
    gate1 = mod_ref[0, 2:3, :]
    shift2 = mod_ref[0, 3:4, :]
    scale2 = mod_ref[0, 4:5, :]
    wr = _split3(wr_ref[...])
    eio = lax.broadcasted_iota(I32, (N_EXPERTS, 1), 0)
    nt = ((1,), (1,))
    cs = [dict(rs=slice(r, r + MERGE_ROWS)) for r in range(0, TM, MERGE_ROWS)]
    for c in cs:
        c["pa"] = _dot(oa_ref[c["rs"], :], wpa_ref[...])
        c["pc"] = _dot(oc_ref[c["rs"], :], wpc_ref[...])
    for c in cs:
        hm = hf_ref[c["rs"], :] + hb_ref[c["rs"], :]
        parts = []
        for h in range(B_HEADS):
            hh = hm[:, h * LANES:(h + 1) * LANES]
            parts.append(hh * lax.rsqrt(jnp.mean(hh * hh, axis=-1, keepdims=True) + EPS))
        c["out_b"] = (jnp.concatenate(parts, axis=1) * hn_ref[...]
                      * jax.nn.sigmoid(ob_ref[c["rs"], :].astype(F32))).astype(BF16)
    for c in cs:
        c["pb"] = _dot(c["out_b"], wpb_ref[...])
    for c in cs:
        rs = c["rs"]
        c["merged"] = (g_ref[rs, 0:1024].astype(F32) * c["pa"] + g_ref[rs, 1024:2048].astype(F32) * c["pb"]
                       + g_ref[rs, 2048:3072].astype(F32) * c["pc"]).astype(BF16)
    for c in cs:
        c["y"] = _dot(c["merged"], wo_ref[...]) + bo_ref[...]
    for c in cs:
        c["x1"] = _ln(DEEPNORM_ALPHA * x_ref[c["rs"], :] + gate1 * c["y"]) * ln_g_ref[...] + ln_b_ref[...]
        x1_o[c["rs"], :] = c["x1"]
    for c in cs:
        c["u2"] = _ln(c["x1"]) * (1.0 + scale2) + shift2
        u2_o[c["rs"], :] = _pack_rows(c["u2"])
    for c in cs:
        up = _split3(c["u2"])
        c["cur"] = (_dot(wr[0], up[0], nt) + _dot(wr[0], up[1], nt) + _dot(wr[1], up[0], nt)
                    + _dot(wr[0], up[2], nt) + _dot(wr[2], up[0], nt) + _dot(wr[1], up[1], nt)) + br_ref[...]
        c["vals"], c["idxs"] = [], []
    for _ in range(TOP_K):
        for c in cs:
            mx = jnp.max(c["cur"], axis=0, keepdims=True)
            ix = jnp.min(jnp.where(c["cur"] == mx, eio, N_EXPERTS), axis=0, keepdims=True)
            c["vals"].append(mx)
            c["idxs"].append(ix)
            c["cur"] = jnp.where(eio == ix, -jnp.inf, c["cur"])
    for c in cs:
        ex = [jnp.exp(v - c["vals"][0]) for v in c["vals"]]
        inv = 1.0 / sum(ex)
        ti_o[:, c["rs"]] = jnp.concatenate(c["idxs"], axis=0)
        tw_o[:, c["rs"]] = jnp.concatenate([e * inv for e in ex], axis=0)


def _merge_call(x, mod, out_a, hf, hb, ob, out_c, g, hn, wpa, wpb, wpc, wo, bo, ln_g, ln_b, wr_t, br):
    row = lambda n: pl.BlockSpec((TM, n), lambda i: (i, 0))
    const = lambda shape: pl.BlockSpec(shape, lambda i: (0,) * len(shape), pipeline_mode=pl.Buffered(1))
    return pl.pallas_call(
        _merge_kernel,
        grid=(N_TOK // TM,),
        in_specs=[row(D_MODEL),
                  pl.BlockSpec((1, N_MOD, D_MODEL), lambda i: (i * TM // GROUP_ROWS, 0, 0)),
                  row(512), row(512), row(512), row(512), row(512), row(3072),
                  const((1, 512)), const((512, D_MODEL)), const((512, D_MODEL)), const((512, D_MODEL)),
                  const((D_MODEL, D_MODEL)), const((1, D_MODEL)), const((1, D_MODEL)), const((1, D_MODEL)),
                  const((N_EXPERTS, D_MODEL)), const((N_EXPERTS, 1))],
        out_specs=[row(D_MODEL), row(D_PACK),
                   pl.BlockSpec((TOP_K, TM), lambda i: (0, i)), pl.BlockSpec((TOP_K, TM), lambda i: (0, i))],
        out_shape=[jax.ShapeDtypeStruct((N_TOK, D_MODEL), F32), jax.ShapeDtypeStruct((N_TOK, D_PACK), I32),
                   jax.ShapeDtypeStruct((TOP_K, N_TOK), I32), jax.ShapeDtypeStruct((TOP_K, N_TOK), F32)],
        compiler_params=_cparams(("arbitrary",)), name="merge",
    )(x, mod, out_a, hf, hb, ob, out_c, g, hn, wpa, wpb, wpc, wo, bo, ln_g, ln_b, wr_t, br)


def _route_kernel(ti_ref, pos_o, te_o, tv_o, nt_o, rank_s):
    ch = ROUTE_CHUNK
    eio = lax.broadcasted_iota(I32, (N_EXPERTS, 1), 0)
    tri = (lax.broadcasted_iota(I32, (ch, ch), 0) <= lax.broadcasted_iota(I32, (ch, ch), 1)).astype(BF16)

    def onehots(off):
        ti = ti_ref[:, pl.ds(off, ch)]
        return [(eio == ti[k:k + 1, :]).astype(F32) for k in range(TOP_K)]

    def count_body(c, carry):
        off = pl.multiple_of(c * ch, ch)
        ohs = onehots(off)
        oh = ohs[0] + ohs[1] + ohs[2] + ohs[3]
        incl = _dot(oh.astype(BF16), tri)
        excl = incl - oh + carry
        for k in range(TOP_K):
            rank_s[k:k + 1, pl.ds(off, ch)] = jnp.sum(ohs[k] * excl, axis=0, keepdims=True)
        return carry + incl[:, ch - 1:ch]

    counts = lax.fori_loop(0, N_TOK // ch, count_body, jnp.zeros((N_EXPERTS, 1), F32))
    ntile_e = jnp.floor((counts + (TME - 1)) * (1.0 / TME))
    lower = (lax.broadcasted_iota(I32, (N_EXPERTS, N_EXPERTS), 1)
             < lax.broadcasted_iota(I32, (N_EXPERTS, N_EXPERTS), 0)).astype(BF16)
    off_t = _dot(lower, jnp.broadcast_to(ntile_e, (N_EXPERTS, LANES)).astype(BF16))[:, 0:1]
    end_t = off_t + ntile_e
    total = jnp.sum(ntile_e, axis=0, keepdims=True)
    jt = lax.broadcasted_iota(I32, (1, N_TILES_PAD), 1).astype(F32)
    te = jnp.minimum(jnp.sum((end_t <= jt).astype(F32), axis=0, keepdims=True), N_EXPERTS - 1.0)
    sel = (eio.astype(F32) == te).astype(F32)
    cnt_j = jnp.sum(sel * counts, axis=0, keepdims=True)
    off_j = jnp.sum(sel * off_t, axis=0, keepdims=True)
    valid = jnp.clip(cnt_j - (jt - off_j) * TME, 0.0, float(TME))
    te_o[...] = te.astype(I32)
    tv_o[...] = jnp.where(jt < total, valid, 0.0).astype(I32)
    nt_o[...] = jnp.broadcast_to(total, (1, LANES)).astype(I32)
    off_rows = off_t * TME

    def pos_body(c, carry):
        off = pl.multiple_of(c * ch, ch)
        ohs = onehots(off)
        for k in range(TOP_K):
            base = jnp.sum(ohs[k] * off_rows, axis=0, keepdims=True)
            pos_o[k:k + 1, pl.ds(off, ch)] = (rank_s[k:k + 1, pl.ds(off, ch)] + base).astype(I32)
        return carry

    lax.fori_loop(0, N_TOK // ch, pos_body, 0)


def _route_call(top_i):
    return pl.pallas_call(
        _route_kernel,
        out_shape=[jax.ShapeDtypeStruct((TOP_K, N_TOK), I32), jax.ShapeDtypeStruct((1, N_TILES_PAD), I32),
                   jax.ShapeDtypeStruct((1, N_TILES_PAD), I32), jax.ShapeDtypeStruct((1, LANES), I32)],
        scratch_shapes=[pltpu.VMEM((TOP_K, N_TOK), F32)],
        compiler_params=pltpu.CompilerParams(vmem_limit_bytes=VMEM_LIMIT), name="route",
    )(top_i)


def _sc_mesh():
    return plsc.VectorSubcoreMesh(core_axis_name="c", subcore_axis_name="s")


def _sc_worker(n_rows):
    info = plsc.get_sparse_core_info()
    n_workers = info.num_cores * info.num_subcores
    per_w = n_rows // n_workers
    assert per_w * n_workers == n_rows and per_w % SC_WINDOW == 0
    wid = lax.axis_index("s") * info.num_cores + lax.axis_index("c")
    return wid * per_w, per_w


def _dispatch_call(u2, pos_flat):
    w = SC_WINDOW

    @functools.partial(pl.kernel, out_type=jax.ShapeDtypeStruct((P_ROWS, u2.shape[1]), u2.dtype), mesh=_sc_mesh(),
                       scratch_types=[pltpu.VMEM((w,), I32)] * TOP_K
                       + [pltpu.VMEM((w, u2.shape[1]), u2.dtype), pltpu.SemaphoreType.DMA])
    def k(x_hbm, p_hbm, o_hbm, i0, i1, i2, i3, rows_v, sem):
        base, per_w = _sc_worker(N_TOK)
        idx = (i0, i1, i2, i3)

        @pl.loop(0, per_w // w)
        def _(c):
            off = pl.multiple_of(base + c * w, 8)
            pltpu.sync_copy(x_hbm.at[pl.ds(off, w)], rows_v)
            for kk in range(TOP_K):
                pltpu.sync_copy(p_hbm.at[pl.ds(pl.multiple_of(kk * N_TOK + off, 8), w)], idx[kk])
            copies = [pltpu.async_copy(rows_v, o_hbm.at[idx[kk]], sem) for kk in range(TOP_K)]
            for cp in copies:
                cp.wait()

    return k(u2, pos_flat)


def _combine_call(ys, pos_flat):
    w = SC_WINDOW
    n = pos_flat.shape[0]

    @functools.partial(pl.kernel, out_type=jax.ShapeDtypeStruct((n, ys.shape[1]), ys.dtype), mesh=_sc_mesh(),
                       scratch_types=[pltpu.VMEM((w,), I32), pltpu.VMEM((w, ys.shape[1]), ys.dtype),
                                      pltpu.SemaphoreType.DMA])
    def k(y_hbm, p_hbm, o_hbm, idx_v, rows_v, sem):
        base, per_w = _sc_worker(n)

        @pl.loop(0, per_w // w)
        def _(c):
            off = pl.multiple_of(base + c * w, 8)
            pltpu.sync_copy(p_hbm.at[pl.ds(off, w)], idx_v)
            pltpu.async_copy(y_hbm.at[idx_v], rows_v, sem).wait()
            pltpu.sync_copy(rows_v, o_hbm.at[pl.ds(off, w)])

    return k(ys, pos_flat)


def _moe_kernel(l, te_ref, tv_ref, nt_ref, x_ref, wu_hbm, bu_ref, wd_hbm, bd_ref, y_ref,
                wu_f, wd_f, wu_s, wd_s, sem, group_s):
    i = pl.program_id(0)
    nt = nt_ref[0]

    def fetch(e, slot):
        return (pltpu.make_async_copy(wu_hbm.at[l, e], wu_f.at[slot], sem.at[0, slot]),
                pltpu.make_async_copy(wd_hbm.at[l, e], wd_f.at[slot], sem.at[1, slot]))

    @pl.when(i == 0)
    def _():
        group_s[0] = 0
        for cp in fetch(te_ref[0], 0):
            cp.start()

    @pl.when(i < nt)
    def _():
        e = te_ref[i]
        prev = te_ref[jnp.maximum(i - 1, 0)]

        @pl.when((i == 0) | (e != prev))
        def _():
            group = group_s[0]
            slot = group % 2
            j = lax.while_loop(lambda j: (j < nt) & (te_ref[jnp.minimum(j, N_TILES - 1)] == e),
                               lambda j: j + 1, i + 1)

            @pl.when(j < nt)
            def _():
                for cp in fetch(te_ref[jnp.minimum(j, N_TILES - 1)], 1 - slot):
                    cp.start()

            for cp in fetch(e, slot):
                cp.wait()
            wu_s[...] = wu_f[slot].astype(BF16)
            wd_s[...] = wd_f[slot].astype(BF16)
            group_s[0] = group + 1

        valid = tv_ref[i]
        half = TME // 2
        for j in range(2):
            @pl.when(valid > j * half)
            def _():
                sl = slice(j * half, (j + 1) * half)
                rows = lax.broadcasted_iota(I32, (half, 1), 0) + j * half
                lo, hi = _unpack_rows(jnp.where(rows < valid, x_ref[sl, :], 0))
                x = jnp.concatenate([lo.astype(BF16), hi.astype(BF16)], axis=1)
                h = _dot(x, wu_s[...]) + bu_ref[...]
                h_glu = jnp.minimum(h[:, :D_EXPERT], SWIGLU_LIMIT)
                h_lin = jnp.clip(h[:, D_EXPERT:], -SWIGLU_LIMIT, SWIGLU_LIMIT)
                act = h_glu * jax.nn.sigmoid(SWIGLU_ALPHA * h_glu) * (h_lin + 1.0)
                y_ref[sl, :] = _pack_rows(_dot(act.astype(BF16), wd_s[...]) + bd_ref[...])


def _moe_call(l, tile_e, tile_v, n_tiles, xs, w_up, b_up, w_down, b_down):
    def tile(i, te, tv, nt):
        return jnp.minimum(i, nt[0] - 1)

    def wmap(i, te, tv, nt):
        return (l, te[tile(i, te, tv, nt)], 0, 0)

    grid_spec = pltpu.PrefetchScalarGridSpec(
        num_scalar_prefetch=3,
        grid=(N_TILES,),
        in_specs=[pl.BlockSpec((TME, D_PACK), lambda i, te, tv, nt: (tile(i, te, tv, nt), 0)),
                  pl.BlockSpec(memory_space=pl.ANY),
                  pl.BlockSpec((None, None, 1, 2 * D_EXPERT), wmap),
                  pl.BlockSpec(memory_space=pl.ANY),
                  pl.BlockSpec((None, None, 1, D_MODEL), wmap)],
        out_specs=pl.BlockSpec((TME, D_PACK), lambda i, te, tv, nt: (tile(i, te, tv, nt), 0)),
        scratch_shapes=[pltpu.VMEM((2, D_MODEL, 2 * D_EXPERT), F32), pltpu.VMEM((2, D_EXPERT, D_MODEL), F32),
                        pltpu.VMEM((D_MODEL, 2 * D_EXPERT), BF16), pltpu.VMEM((D_EXPERT, D_MODEL), BF16),
                        pltpu.SemaphoreType.DMA((2, 2)), pltpu.SMEM((1,), I32)],
    )
    return pl.pallas_call(
        functools.partial(_moe_kernel, l), grid_spec=grid_spec,
        out_shape=jax.ShapeDtypeStruct((P_ROWS, D_PACK), I32),
        compiler_params=_cparams(("arbitrary",)), name="moe",
    )(tile_e, tile_v, n_tiles, xs, w_up, b_up.reshape(DEPTH, N_EXPERTS, 1, -1),
      w_down, b_down.reshape(DEPTH, N_EXPERTS, 1, -1))


def _final_kernel(split, yg_ref, tw_ref, x1_ref, mod_ref, ln_g_ref, ln_b_ref, *o_refs):
    tw = tw_ref[...]
    halves = [_unpack_rows(yg_ref[k]) for k in range(TOP_K)]
    moe = jnp.concatenate([sum(tw[:, k:k + 1] * halves[k][j] for k in range(TOP_K)) for j in range(2)], axis=1)
    gate2 = mod_ref[0, 5:6, :]
    y = _ln(DEEPNORM_ALPHA * x1_ref[...] + gate2 * moe) * ln_g_ref[...] + ln_b_ref[...]
    if not split:
        o_refs[0][...] = y
    else:
        is_ctx = pl.program_id(0) < N_CTX // TM

        @pl.when(is_ctx)
        def _():
            o_refs[0][...] = y

        @pl.when(jnp.logical_not(is_ctx))
        def _():
            o_refs[1][...] = y


def _final_call(split, yg, tw, x1, mod, ln_g, ln_b):
    row = lambda n: pl.BlockSpec((TM, n), lambda i: (i, 0))
    const = lambda shape: pl.BlockSpec(shape, lambda i: (0,) * len(shape))
    n_ctx_tiles = N_CTX // TM
    if split:
        out_specs = [pl.BlockSpec((TM, D_MODEL), lambda i: (jnp.minimum(i, n_ctx_tiles - 1), 0)),
                     pl.BlockSpec((TM, D_MODEL), lambda i: (jnp.maximum(i - n_ctx_tiles, 0), 0))]
        out_shape = [jax.ShapeDtypeStruct((N_CTX, D_MODEL), F32), jax.ShapeDtypeStruct((N_LAT, D_MODEL), F32)]
    else:
        out_specs = row(D_MODEL)
        out_shape = jax.ShapeDtypeStruct((N_TOK, D_MODEL), F32)
    return pl.pallas_call(
        functools.partial(_final_kernel, split),
        grid=(N_TOK // TM,),
        in_specs=[pl.BlockSpec((TOP_K, TM, D_PACK), lambda i: (0, i, 0)), row(TOP_K), row(D_MODEL),
                  pl.BlockSpec((1, N_MOD, D_MODEL), lambda i: (i * TM // GROUP_ROWS, 0, 0)),
                  const((1, D_MODEL)), const((1, D_MODEL))],
        out_specs=out_specs, out_shape=out_shape,
        compiler_params=_cparams(("arbitrary",)), name="final",
    )(yg, tw, x1, mod, ln_g, ln_b)


def _rope_tables():
    half = A_HEAD_DIM // 2
    freqs = ROPE_BASE ** (-jnp.arange(0, half, 2, dtype=F32) / half)
    t = jnp.arange(DEC_SEQ, dtype=jnp.int32)
    ang_r = (t // GRID_W).astype(F32)[:, None] * freqs[None, :]
    ang_c = (t % GRID_W).astype(F32)[:, None] * freqs[None, :]
    cos64 = jnp.concatenate([jnp.cos(ang_r)] * 2 + [jnp.cos(ang_c)] * 2, axis=1)
    sin64 = jnp.concatenate([-jnp.sin(ang_r), jnp.sin(ang_r), -jnp.sin(ang_c), jnp.sin(ang_c)], axis=1)
    cos_t = jnp.concatenate([jnp.ones((TM, LANES), F32), jnp.tile(cos64, (1, 2))], axis=0)
    sin_t = jnp.concatenate([jnp.zeros((TM, LANES), F32), jnp.tile(sin64, (1, 2))], axis=0)
    return cos_t, sin_t


def _permute_in_cols(t):
    qa = [t[..., h * 64:(h + 1) * 64] for h in QA_HEAD_ORDER]
    pad = jnp.zeros(t.shape[:-1] + (D_IN_PAD - D_IN,), t.dtype)
    return jnp.concatenate(qa + [t[..., 512:2816], t[..., 2832:7440], t[..., 2816:2832], pad], axis=-1)


def _layer(l, x, mod, tables, caches, p, outs, weights):
    cos_t, sin_t = tables
    ck_a, cv_a, ck_c, cv_c, cn0, m0 = caches
    lam_init = 0.8 - 0.6 * math.exp(-0.3 * l)
    w_in, b_in = p["w_in"], p["b_in"]
    w_p = _permute_in_cols(w_in).astype(BF16)
    b_p = _permute_in_cols(b_in).reshape(1, D_IN_PAD)
    w_gt = w_in[:, 2816:2832].T.astype(BF16)
    b_gt = b_in[2816:2832].reshape(16, 1)
    gq = jnp.tile(p["qn_a"], A_HEADS).reshape(1, 512)
    gk = jnp.tile(p["kn_a"], A_KV_HEADS).reshape(1, 128)
    gmat = jnp.asarray(np.kron(np.eye(8), np.ones((64, 64))), BF16)

    (qa, kab, vab, qb, kb, vb, ob, qc, kcb, vcb, g, ifg, ifgt, nka, nva, nkc, nvc) = _inproj_call(
        l, x, mod, cos_t, sin_t, w_p, b_p, w_gt, b_gt, gq, gk, gmat, None if outs is None else outs[:4])

    lam_c = p["lam_c"]
    sn_c = p["sn_c"].reshape(1, C_V_DIM)
    oa = _attn_call("a", False, lam_init, qa, kab, vab, None, None, None, None, None, "attn_a_ctx")
    oa = _attn_call("a", True, lam_init, qa, kab, vab, ck_a, cv_a, None, None, oa, "attn_a_lat")
    oc = _attn_call("c", False, lam_init, qc, kcb, vcb, None, None, lam_c, sn_c, None, "attn_c_ctx")
    oc = _attn_call("c", True, lam_init, qc, kcb, vcb, ck_c, cv_c, lam_c, sn_c, oc, "attn_c_lat")
    hf, hb, c_fin, n_fin, m_fin = _mlstm_call(l, False, qb, kb, vb, ifg, ifgt, None, None, None,
                                              None if outs is None else outs[4:6], "mlstm_ctx")
    hf, hb = _mlstm_call(l, True, qb, kb, vb, ifg, ifgt, cn0, m0, (hf, hb), None, "mlstm_lat")

    wpa = p["w_pa"].reshape(A_HEADS, A_HEAD_DIM, D_MODEL)[np.asarray(QA_HEAD_ORDER)].reshape(512, D_MODEL)
    x1, u2, top_i, top_w = _merge_call(
        x, mod, oa, hf, hb, ob, oc, g, p["hn_b"].reshape(1, 512),
        wpa.astype(BF16), p["w_pb"].astype(BF16), p["w_pc"].astype(BF16), p["w_o"].astype(BF16),
        p["b_o"].reshape(1, D_MODEL), p["ln1_g"].reshape(1, D_MODEL), p["ln1_b"].reshape(1, D_MODEL),
        p["w_router"].T, p["b_router"].reshape(N_EXPERTS, 1))

    pos, tile_e, tile_v, n_tiles = _route_call(top_i)
    pos_flat = pos.reshape(TOP_K * N_TOK)
    xs = _dispatch_call(u2, pos_flat)
    ys = _moe_call(l, tile_e[0, :N_TILES], tile_v[0, :N_TILES], n_tiles[0, :1], xs,
                   weights["w_up"], weights["b_up"], weights["w_down"], weights["b_down"])
    yg = _combine_call(ys, pos_flat).reshape(TOP_K, N_TOK, D_PACK)
    x2 = _final_call(l == DEPTH - 1, yg, top_w.T, x1, mod,
                     p["ln2_g"].reshape(1, D_MODEL), p["ln2_b"].reshape(1, D_MODEL))
    m_l = m_fin[:, :, :, 0, 0]
    m_all = m_l[:, None] if outs is None else jnp.concatenate([outs[6], m_l[:, None]], axis=1)
    return x2, (nka, nva, nkc, nvc, c_fin, n_fin, m_all)


def kernel(x_prompt, x_sample, cache_k_gqa, cache_v_gqa, cache_k_diff, cache_v_diff, state_mlstm_c, state_mlstm_n, state_mlstm_m, c, c_ctx, w_in, b_in, qn_a, kn_a, hn_b, lam_c, sn_c, w_pa, w_pb, w_pc, w_o, b_o, w_mod, b_mod, ln1_g, ln1_b, ln2_g, ln2_b, w_router, b_router, w_up, b_up, w_down, b_down):
    x = jnp.concatenate([x_prompt.reshape(N_CTX, D_MODEL), x_sample.reshape(N_LAT, D_MODEL)], axis=0)
    cvecs = jnp.zeros((8, D_MODEL), F32).at[0].set(c_ctx).at[1:1 + DEC_BATCH].set(c)
    mod_all = _mod_call(cvecs, w_mod, b_mod).reshape(DEPTH, 8, N_MOD, D_MODEL)[:, :N_GROUPS]
    tables = _rope_tables()
    params = dict(w_in=w_in, b_in=b_in, qn_a=qn_a, kn_a=kn_a, hn_b=hn_b, lam_c=lam_c, sn_c=sn_c, w_pa=w_pa,
                  w_pb=w_pb, w_pc=w_pc, w_o=w_o, b_o=b_o, ln1_g=ln1_g, ln1_b=ln1_b, ln2_g=ln2_g, ln2_b=ln2_b,
                  w_router=w_router, b_router=b_router)
    expert_weights = dict(w_up=w_up, b_up=b_up, w_down=w_down, b_down=b_down)
    outs = None
    for l in range(DEPTH):
        p = {k: v[l] for k, v in params.items()}
        ck_a = cache_k_gqa[:, l].transpose(0, 2, 1, 3).reshape(DEC_BATCH, PAST_LEN, 128).astype(BF16)
        cv_a = cache_v_gqa[:, l].transpose(0, 2, 1, 3).reshape(DEC_BATCH, PAST_LEN, 128).astype(BF16)
        ck_c = cache_k_diff[:, l].astype(BF16)
        cv_c = cache_v_diff[:, l].astype(BF16)
        cn0 = jnp.concatenate([state_mlstm_c[:, l], state_mlstm_n[:, l][..., None],
                               jnp.zeros((DEC_BATCH, 2, B_HEADS, B_HEAD_DIM, B_HEAD_DIM - 1), F32)], axis=-1)
        m0 = jnp.broadcast_to(state_mlstm_m[:, l][..., None, None], (DEC_BATCH, 2, B_HEADS, 1, LANES))
        x, outs = _layer(l, x, mod_all[l], tables, (ck_a, cv_a, ck_c, cv_c, cn0, m0), p, outs, expert_weights)
    y_ctx, y_lat = x
    return (y_ctx.reshape(BATCH, SEQ, D_MODEL), y_lat.reshape(DEC_BATCH, DEC_SEQ, D_MODEL)) + tuple(outs)
```

```python
import functools
import math

import numpy as np
import jax
import jax.numpy as jnp
from jax import lax
from jax.experimental import pallas as pl
from jax.experimental.pallas import tpu as pltpu
from jax.experimental.pallas import tpu_sc as plsc

F32 = jnp.float32
BF16 = jnp.bfloat16
I32 = jnp.int32

D_MODEL = 1024
BATCH = 16
SEQ = 256
DEPTH = 2
DEC_BATCH = 2
DEC_SEQ = 4096
PAST_LEN = 512
GRID_W = 64
ROPE_BASE = 10000.0
EPS = 1e-6
A_HEADS = 8
A_KV_HEADS = 2
A_HEAD_DIM = 64
B_HEADS = 4
B_HEAD_DIM = 128
MLSTM_CHUNK = 128
C_HEADS = 4
C_HEAD_DIM = 64
C_V_DIM = 128
N_EXPERTS = 32
TOP_K = 4
D_EXPERT = 1024
SWIGLU_LIMIT = 7.0
SWIGLU_ALPHA = 1.702
N_MOD = 6
DEEPNORM_ALPHA = (2 * DEPTH) ** 0.25

N_CTX = BATCH * SEQ
N_LAT = DEC_BATCH * DEC_SEQ
N_GROUPS = 1 + DEC_BATCH

LANES = 128
TM = 256
TQ = 512
KEY_BLOCK = 1024
MERGE_ROWS = 128
TME = 512
TILE_LANES = 128
ROUTE_CHUNK = 512
D_PACK = D_MODEL // 2
SC_WINDOW = 64
VMEM_LIMIT = 56 * 1024 * 1024

QA_HEAD_ORDER = (0, 4, 1, 5, 2, 6, 3, 7)
D_IN = 7440
D_IN_PAD = 7552
P_QA, P_KA, P_VA, P_QB, P_KB, P_VB, P_OB, P_QC, P_KC, P_VC, P_G, P_IF = (
    0, 512, 640, 768, 1280, 1792, 2304, 2816, 3328, 3840, 4352, 7424)


def _max_tiles(n_tok):
    return (n_tok * TOP_K) // TME + N_EXPERTS


def _cparams(sem):
    return pltpu.CompilerParams(dimension_semantics=sem, vmem_limit_bytes=VMEM_LIMIT)


def _split3(a):
    a1 = a.astype(BF16)
    r = a - a1.astype(F32)
    a2 = r.astype(BF16)
    a3 = (r - a2.astype(F32)).astype(BF16)
    return a1, a2, a3


def _dot(a, b, dims=None):
    if dims is None:
        return jnp.dot(a, b, preferred_element_type=F32)
    return lax.dot_general(a, b, (dims, ((), ())), preferred_element_type=F32)


def _dot_exact_rhs01(a, m, dims=None):
    return sum(_dot(p, m, dims) for p in _split3(a))


def _dot_exact_lhs01(m, a, dims=None):
    return sum(_dot(m, p, dims) for p in _split3(a))


def _pack_rows(x):
    xb = x.astype(BF16).astype(F32)
    lo = pltpu.bitcast(xb[:, :D_PACK], I32)
    lo = lax.shift_right_logical(lo, jnp.full_like(lo, 16))
    hi = pltpu.bitcast(xb[:, D_PACK:], I32) & jnp.int32(-65536)
    return lo | hi


def _unpack_rows(w):
    return (pltpu.bitcast(lax.shift_left(w, jnp.full_like(w, 16)), F32),
            pltpu.bitcast(w & jnp.int32(-65536), F32))


def _ln(x):
    mu = jnp.mean(x, axis=-1, keepdims=True)
    xc = x - mu
    var = jnp.mean(xc * xc, axis=-1, keepdims=True)
    return xc * lax.rsqrt(var + EPS)


def _log_sigmoid(x):
    return jnp.minimum(x, 0.0) - jnp.log(1.0 + jnp.exp(-jnp.abs(x)))


def _rope(x, cos, sin):
    lane = lax.broadcasted_iota(I32, (1, LANES), 1)
    first = (lane % 32) < 16
    outs = []
    for j in range(x.shape[1] // LANES):
        xj = x[:, j * LANES:(j + 1) * LANES]
        partner = jnp.where(first, pltpu.roll(xj, LANES - 16, 1), pltpu.roll(xj, 16, 1))
        outs.append(xj * cos + partner * sin)
    return outs[0] if len(outs) == 1 else jnp.concatenate(outs, axis=1)


def _mod_spec(lat):
    if lat:
        return pl.BlockSpec((1, N_MOD, D_MODEL), lambda i: (1 + i * TM // DEC_SEQ, 0, 0))
    return pl.BlockSpec((1, N_MOD, D_MODEL), lambda i: (0, 0, 0))


def _mod_kernel(c_ref, w_ref, b_ref, o_ref):
    c = c_ref[...]
    s = c * jax.nn.sigmoid(c)
    o_ref[...] = _dot(s.astype(BF16), w_ref[...].astype(BF16)) + b_ref[...]


def _mod_call(cvecs, w_mod, b_mod):
    tn = 1536
    n = N_MOD * D_MODEL
    return pl.pallas_call(
        _mod_kernel,
        grid=(DEPTH, n // tn),
        in_specs=[pl.BlockSpec((8, D_MODEL), lambda l, j: (0, 0)),
                  pl.BlockSpec((None, D_MODEL, tn), lambda l, j: (l, 0, j)),
                  pl.BlockSpec((None, 1, tn), lambda l, j: (l, 0, j))],
        out_specs=pl.BlockSpec((None, 8, tn), lambda l, j: (l, 0, j)),
        out_shape=jax.ShapeDtypeStruct((DEPTH, 8, n), F32),
        compiler_params=_cparams(("arbitrary", "arbitrary")),
        name="mod",
    )(cvecs, w_mod, b_mod.reshape(DEPTH, 1, n))


def _inproj_kernel(with_cache, n_alias, x_ref, mod_ref, cos_ref, sin_ref, w_ref, b_ref, wgt_ref, bgt_ref,
                   gq_ref, gk_ref, gmat_ref, *refs):
    outs = refs[n_alias:]
    qa_o, kab_o, vab_o, qb_o, kb_o, vb_o, ob_o, qc_o, kcb_o, vcb_o, g_o, ifg_o, ifgt_o = outs[:13]
    x = x_ref[...]
    shift = mod_ref[0, 0:1, :]
    scale = mod_ref[0, 1:2, :]
    ub = (_ln(x) * (1.0 + scale) + shift).astype(BF16)
    cos = cos_ref[...]
    sin = sin_ref[...]

    def proj(lo, n):
        return _dot(ub, w_ref[:, lo:lo + n]) + b_ref[:, lo:lo + n]

    def headnorm(z, gmat, gain):
        ss = _dot((z * z).astype(BF16), gmat)
        return z * lax.rsqrt(ss * (1.0 / A_HEAD_DIM) + EPS) * gain

    segs = [(P_QA, 512), (P_KA, 128), (P_VA, 128), (P_QB, 512), (P_KB, 512), (P_VB, 512), (P_OB, 512),
            (P_QC, 512), (P_KC, 512), (P_VC, 512), (P_G, 1024), (P_G + 1024, 1024), (P_G + 2048, 1024),
            (P_IF, LANES)]
    keep = {}

    def epilogue(k, z):
        if k == 0:
            z = headnorm(z, gmat_ref[...], gq_ref[...])
            qa_o[...] = (_rope(z, cos, sin) * 0.125).astype(BF16)
        elif k == 1:
            keep["k"] = _rope(headnorm(z, gmat_ref[0:128, 0:128], gk_ref[...]), cos, sin)
            kab_o[...] = keep["k"].astype(BF16)
        elif k == 2:
            keep["v"] = z
            vab_o[...] = z.astype(BF16)
        elif k == 3:
            qb_o[...] = z.astype(BF16)
        elif k == 4:
            kb_o[...] = (z * (B_HEAD_DIM ** -0.5)).astype(BF16)
        elif k == 5:
            vb_o[...] = z.astype(BF16)
        elif k == 6:
            ob_o[...] = z.astype(BF16)
        elif k == 7:
            qc_o[...] = (_rope(z, cos, sin) * 0.125).astype(BF16)
        elif k == 8:
            keep["kc"] = _rope(z, cos, sin)
            kcb_o[...] = keep["kc"].astype(BF16)
        elif k == 9:
            keep["vc"] = z
            vcb_o[...] = z.astype(BF16)
        elif k in (10, 11, 12):
            g_o[:, (k - 10) * 1024:(k - 9) * 1024] = jax.nn.sigmoid(z).astype(BF16)
        else:
            lane = lax.broadcasted_iota(I32, (1, LANES), 1)
            ifg_o[...] = jnp.where(lane >= 8, _log_sigmoid(z), z)

    z_prev = proj(*segs[0])
    for k in range(1, len(segs)):
        z_next = proj(*segs[k])
        epilogue(k - 1, z_prev)
        z_prev = z_next
    zt = _dot(wgt_ref[...], ub, ((1,), (1,))) + bgt_ref[...]
    epilogue(len(segs) - 1, z_prev)
    row = lax.broadcasted_iota(I32, (16, 1), 0)
    ifgt_o[...] = jnp.where(row >= 8, _log_sigmoid(zt), zt)

    if with_cache:
        nka_o, nva_o, nkc_o, nvc_o = outs[13:]
        for h in range(A_KV_HEADS):
            nka_o[h] = keep["k"][:, h * A_HEAD_DIM:(h + 1) * A_HEAD_DIM]
            nva_o[h] = keep["v"][:, h * A_HEAD_DIM:(h + 1) * A_HEAD_DIM]
        for h in range(C_HEADS):
            nkc_o[h] = keep["kc"][:, h * LANES:(h + 1) * LANES]
            nvc_o[h] = keep["vc"][:, h * LANES:(h + 1) * LANES]


def _inproj_call(l, lat, x, mod, cos_t, sin_t, w_p, b_p, w_gt, b_gt, gq, gk, gmat, cache_outs):
    n = x.shape[0]
    row = lambda w: pl.BlockSpec((TM, w), lambda i: (i, 0))
    const = lambda shape: pl.BlockSpec(shape, lambda i: (0,) * len(shape), pipeline_mode=pl.Buffered(1))
    sds = lambda w, dt: jax.ShapeDtypeStruct((n, w), dt)
    out_defs = [(512, BF16), (128, BF16), (128, BF16), (512, BF16), (512, BF16), (512, BF16), (512, BF16),
                (512, BF16), (512, BF16), (512, BF16), (3072, BF16), (LANES, F32)]
    out_specs = [row(w) for w, _ in out_defs] + [pl.BlockSpec((16, TM), lambda i: (0, i))]
    out_shape = [sds(w, dt) for w, dt in out_defs] + [jax.ShapeDtypeStruct((16, n), F32)]
    alias_in, aliases = [], {}
    if lat:
        tbl = pl.BlockSpec((TM, LANES), lambda i: (i % (DEC_SEQ // TM), 0))
    else:
        assert TM == SEQ
        tbl = pl.BlockSpec((TM, LANES), lambda i: (0, 0))
        cache_defs = [(A_KV_HEADS, A_HEAD_DIM), (A_KV_HEADS, A_HEAD_DIM), (C_HEADS, LANES), (C_HEADS, LANES)]
        out_specs += [pl.BlockSpec((None, None, nh, SEQ, d), lambda i: (i, l, 0, 0, 0)) for nh, d in cache_defs]
        out_shape += [jax.ShapeDtypeStruct((BATCH, DEPTH, nh, SEQ, d), F32) for nh, d in cache_defs]
        if cache_outs is not None:
            alias_in = list(cache_outs)
            aliases = {11 + j: len(out_defs) + 1 + j for j in range(4)}
    return pl.pallas_call(
        functools.partial(_inproj_kernel, not lat, len(alias_in)),
        grid=(n // TM,),
        in_specs=[row(D_MODEL), _mod_spec(lat), tbl, tbl,
                  const((D_MODEL, D_IN_PAD)), const((1, D_IN_PAD)),
                  const((16, D_MODEL)), const((16, 1)),
                  const((1, 512)), const((1, 128)), const((512, 512))]
        + [pl.BlockSpec(memory_space=pl.ANY)] * len(alias_in),
        out_specs=out_specs, out_shape=out_shape, input_output_aliases=aliases,
        compiler_params=_cparams(("arbitrary",)),
        name="inproj_lat" if lat else "inproj_ctx",
    )(x, mod, cos_t, sin_t, w_p, b_p, w_gt, b_gt, gq, gk, gmat, *alias_in)


def _attn_kernel(mode, nseg, lam_init, *refs):
    q_ref = refs[0]
    kv = refs[1:1 + 2 * nseg]
    rest = refs[1 + 2 * nseg:]
    o_ref = rest[-1]
    tq = q_ref.shape[0]
    lane = lax.broadcasted_iota(I32, (1, LANES), 1)
    if mode == "c":
        lam_ref, sn_ref = rest[0], rest[1]
        lp = lam_ref[...]
        lam = (jnp.exp(jnp.sum(lp[0:1] * lp[1:2], axis=-1, keepdims=True))
               - jnp.exp(jnp.sum(lp[2:3] * lp[3:4], axis=-1, keepdims=True)) + lam_init)
    for blk in range(q_ref.shape[1] // LANES):
        qs = slice(blk * LANES, (blk + 1) * LANES)
        ks = qs if kv[0].shape[1] > LANES else slice(0, LANES)
        q = q_ref[:, qs]
        zero = jnp.zeros_like(q)
        qq = jnp.concatenate([jnp.where(lane < 64, q, zero), jnp.where(lane >= 64, q, zero)], axis=0)
        m = l = acc = None
        for j in range(nseg):
            k_ref, v_ref = kv[2 * j], kv[2 * j + 1]
            for off in range(0, k_ref.shape[0], KEY_BLOCK):
                n = min(KEY_BLOCK, k_ref.shape[0] - off)
                s = _dot(qq, k_ref[off:off + n, ks], ((1,), (1,)))
                bm = jnp.max(s, axis=-1, keepdims=True)
                if m is None:
                    m = bm
                    e = jnp.exp(s - m)
                    l = jnp.sum(e, axis=-1, keepdims=True)
                    acc = _dot(e.astype(BF16), v_ref[off:off + n, ks])
                else:
                    m_new = jnp.maximum(m, bm)
                    alpha = jnp.exp(m - m_new)
                    e = jnp.exp(s - m_new)
                    l = alpha * l + jnp.sum(e, axis=-1, keepdims=True)
                    acc = alpha * acc + _dot(e.astype(BF16), v_ref[off:off + n, ks])
                    m = m_new
        o = acc * (1.0 / l)
        if mode == "a":
            out = jnp.where(lane < 64, o[:tq], o[tq:])
        else:
            o = o[:tq] - lam * o[tq:]
            ms = jnp.mean(o * o, axis=-1, keepdims=True)
            out = o * lax.rsqrt(ms + EPS) * sn_ref[...] * (1.0 - lam_init)
        o_ref[:, qs] = out.astype(o_ref.dtype)


def _attn_call(mode, lat, lam_init, q_all, k_all, v_all, k_cache, v_cache, lam_c, sn_c):
    wide = k_all.shape[1] == 512
    kcol = (lambda h: h) if wide else (lambda h: 0)
    if lat:
        nq = DEC_SEQ // TQ
        grid = (DEC_BATCH, nq, 4)
        q_spec = pl.BlockSpec((TQ, LANES), lambda b, i, h: (b * nq + i, h))
        kn_spec = pl.BlockSpec((DEC_SEQ, LANES), lambda b, i, h: (b, kcol(h)))
        if wide:
            kc_spec = pl.BlockSpec((None, None, PAST_LEN, LANES), lambda b, i, h: (b, h, 0, 0))
        else:
            kc_spec = pl.BlockSpec((None, PAST_LEN, LANES), lambda b, i, h: (b, 0, 0))
        in_specs = [q_spec, kc_spec, kc_spec, kn_spec, kn_spec]
        args = [q_all, k_cache, v_cache, k_all, v_all]
        nseg = 2
        o_spec = q_spec
        sem = ("arbitrary", "arbitrary", "arbitrary")
        zero_map = lambda b, i, h: (0, 0)
    else:
        grid = (BATCH,)
        q_spec = pl.BlockSpec((SEQ, 512), lambda b: (b, 0))
        k_spec = pl.BlockSpec((SEQ, k_all.shape[1]), lambda b: (b, 0))
        in_specs = [q_spec, k_spec, k_spec]
        args = [q_all, k_all, v_all]
        nseg = 1
        o_spec = q_spec
        sem = ("arbitrary",)
        zero_map = lambda b: (0, 0)
    if mode == "c":
        in_specs += [pl.BlockSpec((4, C_HEAD_DIM), zero_map), pl.BlockSpec((1, C_V_DIM), zero_map)]
        args += [lam_c, sn_c]
    return pl.pallas_call(
        functools.partial(_attn_kernel, mode, nseg, lam_init),
        grid=grid, in_specs=in_specs, out_specs=o_spec,
        out_shape=jax.ShapeDtypeStruct((q_all.shape[0], 512), BF16),
        compiler_params=_cparams(sem), name=f"attn_{mode}_{'lat' if lat else 'ctx'}",
    )(*args)


def _mlstm_kernel(has_init, want_state, n_alias, *refs):
    tok = (refs[0:5], refs[5:10])
    pos = 10
    if has_init:
        cn0_ref, m0_ref = refs[pos:pos + 2]
        pos += 2
    pos += n_alias
    h_refs = refs[pos:pos + 2]
    pos += 2
    if want_state:
        c_o, n_o, m_o = refs[pos:pos + 3]
        pos += 3
    cn_s, m_s = refs[pos:pos + 2]

    c = pl.program_id(1)
    last = pl.num_programs(1) - 1
    L = MLSTM_CHUNK

    @pl.when(c == 0)
    def _():
        if has_init:
            cn_s[...] = cn0_ref[...]
            m_s[...] = m0_ref[...]
        else:
            cn_s[...] = jnp.zeros_like(cn_s)
            m_s[...] = jnp.zeros_like(m_s)

    row = lax.broadcasted_iota(I32, (L, L), 0)
    col = lax.broadcasted_iota(I32, (L, L), 1)
    lane = lax.broadcasted_iota(I32, (1, LANES), 1)
    ones_col = jnp.where(lane == 0, 1.0, 0.0).astype(BF16)

    chains = []
    for d in range(2):
        q_ref, k_ref, v_ref, g_ref, gt_ref = tok[d]
        keep = (col <= row) if d == 0 else (col >= row)
        m_c = keep.astype(BF16)
        m_r = ((row <= col) if d == 0 else (row >= col)).astype(BF16)
        g = g_ref[...]
        gt = gt_ref[...]
        bc = _dot_exact_lhs01(m_c, g)
        br = _dot_exact_rhs01(gt, m_r)
        for h in range(B_HEADS):
            sl = slice(h * L, (h + 1) * L)
            gi, gf = 4 * d + h, 8 + 4 * d + h
            ch = dict(d=d, h=h, sl=sl, keep=keep, q=q_ref[:, sl], k=k_ref[:, sl], v=v_ref[:, sl],
                      i_col=g[:, gi:gi + 1], f_col=g[:, gf:gf + 1], b_col=bc[:, gf:gf + 1],
                      r_row=gt[gi:gi + 1, :] - br[gf:gf + 1, :],
                      m_prev=m_s[d, h][:, 0:1], cn_prev=cn_s[d, h])
            chains.append(ch)
    for ch in chains:
        ch["s"] = _dot(ch["q"], ch["k"], ((1,), (1,)))
        ch["qc"] = _dot(ch["q"], ch["cn_prev"].astype(BF16))
    for ch in chains:
        ch["total"] = jnp.sum(ch["f_col"], axis=0, keepdims=True)
        ch["inter"] = ch["b_col"] + ch["m_prev"]
        ch["v_aug"] = jnp.concatenate([ch["v"], jnp.broadcast_to(ones_col, (L, LANES))], axis=1)
    for ch in chains:
        ch["dmat"] = jnp.where(ch["keep"], ch["b_col"] + ch["r_row"], -jnp.inf)
    for ch in chains:
        ch["dmax"] = jnp.max(ch["dmat"], axis=-1, keepdims=True)
    for ch in chains:
        ch["m_row"] = jnp.maximum(ch["inter"], ch["dmax"])
    for ch in chains:
        ch["sc"] = (ch["s"] * jnp.exp(ch["dmat"] - ch["m_row"])).astype(BF16)
    for ch in chains:
        ch["a"] = _dot(ch["sc"], ch["v_aug"])
    for ch in chains:
        ch["nd"] = jnp.exp(ch["inter"] - ch["m_row"]) * ch["qc"] + ch["a"]
    for ch in chains:
        ch["scale"] = 1.0 / jnp.maximum(jnp.abs(ch["nd"][:, L:L + 1]), jnp.exp(-ch["m_row"]))
    for ch in chains:
        h_refs[ch["d"]][:, ch["sl"]] = ch["nd"][:, :L] * ch["scale"]
    for ch in chains:
        ch["wel"] = ch["total"] - ch["b_col"] + ch["i_col"]
        ch["m_new"] = jnp.maximum(ch["total"] + ch["m_prev"], jnp.max(ch["wel"], axis=0, keepdims=True))
    for ch in chains:
        ch["kw"] = (jnp.exp(ch["wel"] - ch["m_new"]) * ch["k"].astype(F32)).astype(BF16)
    for ch in chains:
        ch["u"] = _dot(ch["kw"], ch["v_aug"], ((0,), (0,)))
    for ch in chains:
        cn_s[ch["d"], ch["h"]] = jnp.exp(ch["total"] + ch["m_prev"] - ch["m_new"]) * ch["cn_prev"] + ch["u"]
        m_s[ch["d"], ch["h"]] = jnp.broadcast_to(ch["m_new"], (1, LANES))

    if want_state:
        @pl.when(c == last)
        def _():
            pick = ((lax.broadcasted_iota(I32, (8, 2 * L), 1) == L)
                    & (lax.broadcasted_iota(I32, (8, 2 * L), 0) == 0)).astype(BF16)
            for d in range(2):
                for h in range(B_HEADS):
                    cn = cn_s[d, h]
                    c_o[d, h] = cn[:, :L]
                    n_o[d, h:h + 1, :] = _dot_exact_lhs01(pick, cn, ((1,), (1,)))[0:1]
            m_o[...] = m_s[...]


def _mlstm_call(l, lat, q_all, k_all, v_all, g_all, gt_all, cn0, m0, state_bufs):
    L = MLSTM_CHUNK
    n = q_all.shape[0]
    nb, nc = (DEC_BATCH, DEC_SEQ // L) if lat else (BATCH, SEQ // L)
    chunk = (lambda b, c: b * nc + c, lambda b, c: b * nc + (nc - 1 - c))

    in_specs, args = [], []
    for d in range(2):
        tok = lambda w, d=d: pl.BlockSpec((L, w), lambda b, c: (chunk[d](b, c), 0))
        in_specs += [tok(512), tok(512), tok(512), tok(LANES),
                     pl.BlockSpec((16, L), lambda b, c, d=d: (0, chunk[d](b, c)))]
        args += [q_all, k_all, v_all, g_all, gt_all]
    if lat:
        in_specs += [pl.BlockSpec((None, 2, B_HEADS, L, 2 * L), lambda b, c: (b, 0, 0, 0, 0)),
                     pl.BlockSpec((None, 2, B_HEADS, 1, LANES), lambda b, c: (b, 0, 0, 0, 0))]
        args += [cn0, m0]
    aliases = {}
    alias_in = list(state_bufs) if state_bufs is not None else []
    if alias_in:
        aliases = {len(args): 2, len(args) + 1: 3}
    in_specs += [pl.BlockSpec(memory_space=pl.ANY)] * len(alias_in)
    args += alias_in
    out_specs = [pl.BlockSpec((L, 512), lambda b, c, d=d: (chunk[d](b, c), 0)) for d in range(2)]
    out_shape = [jax.ShapeDtypeStruct((n, 512), F32)] * 2
    if not lat:
        out_specs += [pl.BlockSpec((None, None, 2, B_HEADS, L, L), lambda b, c: (b, l, 0, 0, 0, 0)),
                      pl.BlockSpec((None, None, 2, B_HEADS, L), lambda b, c: (b, l, 0, 0, 0)),
                      pl.BlockSpec((None, 2, B_HEADS, 1, LANES), lambda b, c: (b, 0, 0, 0, 0))]
        out_shape += [jax.ShapeDtypeStruct((BATCH, DEPTH, 2, B_HEADS, L, L), F32),
                      jax.ShapeDtypeStruct((BATCH, DEPTH, 2, B_HEADS, L), F32),
                      jax.ShapeDtypeStruct((nb, 2, B_HEADS, 1, LANES), F32)]
    return pl.pallas_call(
        functools.partial(_mlstm_kernel, lat, not lat, len(alias_in)),
        grid=(nb, nc), in_specs=in_specs, out_specs=out_specs, out_shape=out_shape,
        input_output_aliases=aliases,
        scratch_shapes=[pltpu.VMEM((2, B_HEADS, L, 2 * L), F32), pltpu.VMEM((2, B_HEADS, 1, LANES), F32)],
        compiler_params=_cparams(("arbitrary", "arbitrary")), name="mlstm_lat" if lat else "mlstm_ctx",
    )(*args)


def _merge_kernel(x_ref, mod_ref, oa_ref, hf_ref, hb_ref, ob_ref, oc_ref, g_ref, hn_ref,
                  wpa_ref, wpb_ref, wpc_ref, wo_ref, bo_ref, ln_g_ref, ln_b_ref, wr_ref, br_ref,
                  x1_o, u2_o, ti_o, tw_o):
    gate1 = mod_ref[0, 2:3, :]
    shift2 = mod_ref[0, 3:4, :]
    scale2 = mod_ref[0, 4:5, :]
    wr = _split3(wr_ref[...])
    eio = lax.broadcasted_iota(I32, (N_EXPERTS, 1), 0)
    nt = ((1,), (1,))
    cs = [dict(rs=slice(r, r + MERGE_ROWS)) for r in range(0, TM, MERGE_ROWS)]
    for c in cs:
        c["pa"] = _dot(oa_ref[c["rs"], :], wpa_ref[...])
        c["pc"] = _dot(oc_ref[c["rs"], :], wpc_ref[...])
    for c in cs:
        hm = hf_ref[c["rs"], :] + hb_ref[c["rs"], :]
        parts = []
        for h in range(B_HEADS):
            hh = hm[:, h * LANES:(h + 1) * LANES]
            parts.append(hh * lax.rsqrt(jnp.mean(hh * hh, axis=-1, keepdims=True) + EPS))
        c["out_b"] = (jnp.concatenate(parts, axis=1) * hn_ref[...]
                      * jax.nn.sigmoid(ob_ref[c["rs"], :].astype(F32))).astype(BF16)
    for c in cs:
        c["pb"] = _dot(c["out_b"], wpb_ref[...])
    for c in cs:
        rs = c["rs"]
        c["merged"] = (g_ref[rs, 0:1024].astype(F32) * c["pa"] + g_ref[rs, 1024:2048].astype(F32) * c["pb"]
                       + g_ref[rs, 2048:3072].astype(F32) * c["pc"]).astype(BF16)
    for c in cs:
        c["y"] = _dot(c["merged"], wo_ref[...]) + bo_ref[...]
    for c in cs:
        c["x1"] = _ln(DEEPNORM_ALPHA * x_ref[c["rs"], :] + gate1 * c["y"]) * ln_g_ref[...] + ln_b_ref[...]
        x1_o[c["rs"], :] = c["x1"]
    for c in cs:
        c["u2"] = _ln(c["x1"]) * (1.0 + scale2) + shift2
        u2_o[c["rs"], :] = _pack_rows(c["u2"])
    for c in cs:
        up = _split3(c["u2"])
        c["cur"] = (_dot(wr[0], up[0], nt) + _dot(wr[0], up[1], nt) + _dot(wr[1], up[0], nt)
                    + _dot(wr[0], up[2], nt) + _dot(wr[2], up[0], nt) + _dot(wr[1], up[1], nt)) + br_ref[...]
        c["vals"], c["idxs"] = [], []
    for _ in range(TOP_K):
        for c in cs:
            mx = jnp.max(c["cur"], axis=0, keepdims=True)
            ix = jnp.min(jnp.where(c["cur"] == mx, eio, N_EXPERTS), axis=0, keepdims=True)
            c["vals"].append(mx)
            c["idxs"].append(ix)
            c["cur"] = jnp.where(eio == ix, -jnp.inf, c["cur"])
    for c in cs:
        ex = [jnp.exp(v - c["vals"][0]) for v in c["vals"]]
        inv = 1.0 / sum(ex)
        ti_o[:, c["rs"]] = jnp.concatenate(c["idxs"], axis=0)
        tw_o[:, c["rs"]] = jnp.concatenate([e * inv for e in ex], axis=0)


def _merge_call(lat, x, mod, out_a, hf, hb, ob, out_c, g, hn, wpa, wpb, wpc, wo, bo, ln_g, ln_b, wr_t, br):
    n = x.shape[0]
    row = lambda w: pl.BlockSpec((TM, w), lambda i: (i, 0))
    const = lambda shape: pl.BlockSpec(shape, lambda i: (0,) * len(shape), pipeline_mode=pl.Buffered(1))
    return pl.pallas_call(
        _merge_kernel,
        grid=(n // TM,),
        in_specs=[row(D_MODEL), _mod_spec(lat),
                  row(512), row(512), row(512), row(512), row(512), row(3072),
                  const((1, 512)), const((512, D_MODEL)), const((512, D_MODEL)), const((512, D_MODEL)),
                  const((D_MODEL, D_MODEL)), const((1, D_MODEL)), const((1, D_MODEL)), const((1, D_MODEL)),
                  const((N_EXPERTS, D_MODEL)), const((N_EXPERTS, 1))],
        out_specs=[row(D_MODEL), row(D_PACK),
                   pl.BlockSpec((TOP_K, TM), lambda i: (0, i)), pl.BlockSpec((TOP_K, TM), lambda i: (0, i))],
        out_shape=[jax.ShapeDtypeStruct((n, D_MODEL), F32), jax.ShapeDtypeStruct((n, D_PACK), I32),
                   jax.ShapeDtypeStruct((TOP_K, n), I32), jax.ShapeDtypeStruct((TOP_K, n), F32)],
        compiler_params=_cparams(("arbitrary",)), name="merge_lat" if lat else "merge_ctx",
    )(x, mod, out_a, hf, hb, ob, out_c, g, hn, wpa, wpb, wpc, wo, bo, ln_g, ln_b, wr_t, br)


def _route_kernel(ti_ref, pos_o, te_o, tv_o, nt_o, rank_s):
    ch = ROUTE_CHUNK
    n_tok = ti_ref.shape[1]
    eio = lax.broadcasted_iota(I32, (N_EXPERTS, 1), 0)
    tri = (lax.broadcasted_iota(I32, (ch, ch), 0) <= lax.broadcasted_iota(I32, (ch, ch), 1)).astype(BF16)

    def onehots(off):
        ti = ti_ref[:, pl.ds(off, ch)]
        return [(eio == ti[k:k + 1, :]).astype(F32) for k in range(TOP_K)]

    def count_body(c, carry):
        off = pl.multiple_of(c * ch, ch)
        ohs = onehots(off)
        oh = ohs[0] + ohs[1] + ohs[2] + ohs[3]
        incl = _dot(oh.astype(BF16), tri)
        excl = incl - oh + carry
        for k in range(TOP_K):
            rank_s[k:k + 1, pl.ds(off, ch)] = jnp.sum(ohs[k] * excl, axis=0, keepdims=True)
        return carry + incl[:, ch - 1:ch]

    counts = lax.fori_loop(0, n_tok // ch, count_body, jnp.zeros((N_EXPERTS, 1), F32))
    ntile_e = jnp.floor((counts + (TME - 1)) * (1.0 / TME))
    lower = (lax.broadcasted_iota(I32, (N_EXPERTS, N_EXPERTS), 1)
             < lax.broadcasted_iota(I32, (N_EXPERTS, N_EXPERTS), 0)).astype(BF16)
    off_t = _dot(lower, jnp.broadcast_to(ntile_e, (N_EXPERTS, LANES)).astype(BF16))[:, 0:1]
    end_t = off_t + ntile_e
    total = jnp.sum(ntile_e, axis=0, keepdims=True)
    jt = lax.broadcasted_iota(I32, (1, TILE_LANES), 1).astype(F32)
    te = jnp.minimum(jnp.sum((end_t <= jt).astype(F32), axis=0, keepdims=True), N_EXPERTS - 1.0)
    sel = (eio.astype(F32) == te).astype(F32)
    cnt_j = jnp.sum(sel * counts, axis=0, keepdims=True)
    off_j = jnp.sum(sel * off_t, axis=0, keepdims=True)
    valid = jnp.clip(cnt_j - (jt - off_j) * TME, 0.0, float(TME))
    te_o[...] = te.astype(I32)
    tv_o[...] = jnp.where(jt < total, valid, 0.0).astype(I32)
    nt_o[...] = jnp.broadcast_to(total, (1, LANES)).astype(I32)
    off_rows = off_t * TME

    def pos_body(c, carry):
        off = pl.multiple_of(c * ch, ch)
        ohs = onehots(off)
        for k in range(TOP_K):
            base = jnp.sum(ohs[k] * off_rows, axis=0, keepdims=True)
            pos_o[k:k + 1, pl.ds(off, ch)] = (rank_s[k:k + 1, pl.ds(off, ch)] + base).astype(I32)
        return carry

    lax.fori_loop(0, n_tok // ch, pos_body, 0)


def _route_call(top_i, name):
    n = top_i.shape[1]
    assert _max_tiles(n) <= TILE_LANES
    return pl.pallas_call(
        _route_kernel,
        out_shape=[jax.ShapeDtypeStruct((TOP_K, n), I32), jax.ShapeDtypeStruct((1, TILE_LANES), I32),
                   jax.ShapeDtypeStruct((1, TILE_LANES), I32), jax.ShapeDtypeStruct((1, LANES), I32)],
        scratch_shapes=[pltpu.VMEM((TOP_K, n), F32)],
        compiler_params=pltpu.CompilerParams(vmem_limit_bytes=VMEM_LIMIT), name=name,
    )(top_i)


def _sc_mesh():
    return plsc.VectorSubcoreMesh(core_axis_name="c", subcore_axis_name="s")


def _sc_worker(n_rows):
    info = plsc.get_sparse_core_info()
    n_workers = info.num_cores * info.num_subcores
    per_w = n_rows // n_workers
    assert per_w * n_workers == n_rows and per_w % SC_WINDOW == 0
    wid = lax.axis_index("s") * info.num_cores + lax.axis_index("c")
    return wid * per_w, per_w


def _dispatch_call(u2, pos_flat, p_rows):
    w = SC_WINDOW
    n = u2.shape[0]

    @functools.partial(pl.kernel, out_type=jax.ShapeDtypeStruct((p_rows, u2.shape[1]), u2.dtype), mesh=_sc_mesh(),
                       scratch_types=[pltpu.VMEM((w,), I32)] * TOP_K
                       + [pltpu.VMEM((w, u2.shape[1]), u2.dtype), pltpu.SemaphoreType.DMA])
    def k(x_hbm, p_hbm, o_hbm, i0, i1, i2, i3, rows_v, sem):
        base, per_w = _sc_worker(n)
        idx = (i0, i1, i2, i3)

        @pl.loop(0, per_w // w)
        def _(c):
            off = pl.multiple_of(base + c * w, 8)
            pltpu.sync_copy(x_hbm.at[pl.ds(off, w)], rows_v)
            for kk in range(TOP_K):
                pltpu.sync_copy(p_hbm.at[pl.ds(pl.multiple_of(kk * n + off, 8), w)], idx[kk])
            copies = [pltpu.async_copy(rows_v, o_hbm.at[idx[kk]], sem) for kk in range(TOP_K)]
            for cp in copies:
                cp.wait()

    return k(u2, pos_flat)


def _combine_call(ys, pos_flat):
    w = SC_WINDOW
    n = pos_flat.shape[0]

    @functools.partial(pl.kernel, out_type=jax.ShapeDtypeStruct((n, ys.shape[1]), ys.dtype), mesh=_sc_mesh(),
                       scratch_types=[pltpu.VMEM((w,), I32), pltpu.VMEM((w, ys.shape[1]), ys.dtype),
                                      pltpu.SemaphoreType.DMA])
    def k(y_hbm, p_hbm, o_hbm, idx_v, rows_v, sem):
        base, per_w = _sc_worker(n)

        @pl.loop(0, per_w // w)
        def _(c):
            off = pl.multiple_of(base + c * w, 8)
            pltpu.sync_copy(p_hbm.at[pl.ds(off, w)], idx_v)
            pltpu.async_copy(y_hbm.at[idx_v], rows_v, sem).wait()
            pltpu.sync_copy(rows_v, o_hbm.at[pl.ds(off, w)])

    return k(ys, pos_flat)


def _moe_kernel(l, n_tiles_max, te_ref, tv_ref, nt_ref, x_ref, wu_hbm, bu_ref, wd_hbm, bd_ref, y_ref,
                wu_f, wd_f, wu_s, wd_s, sem, group_s):
    i = pl.program_id(0)
    nt = nt_ref[0]

    def fetch(e, slot):
        return (pltpu.make_async_copy(wu_hbm.at[l, e], wu_f.at[slot], sem.at[0, slot]),
                pltpu.make_async_copy(wd_hbm.at[l, e], wd_f.at[slot], sem.at[1, slot]))

    @pl.when(i == 0)
    def _():
        group_s[0] = 0
        for cp in fetch(te_ref[0], 0):
            cp.start()

    @pl.when(i < nt)
    def _():
        e = te_ref[i]
        prev = te_ref[jnp.maximum(i - 1, 0)]

        @pl.when((i == 0) | (e != prev))
        def _():
            group = group_s[0]
            slot = group % 2
            j = lax.while_loop(lambda j: (j < nt) & (te_ref[jnp.minimum(j, n_tiles_max - 1)] == e),
                               lambda j: j + 1, i + 1)

            @pl.when(j < nt)
            def _():
                for cp in fetch(te_ref[jnp.minimum(j, n_tiles_max - 1)], 1 - slot):
                    cp.start()

            for cp in fetch(e, slot):
                cp.wait()
            wu_s[...] = wu_f[slot].astype(BF16)
            wd_s[...] = wd_f[slot].astype(BF16)
            group_s[0] = group + 1

        valid = tv_ref[i]
        half = TME // 2
        for j in range(2):
            @pl.when(valid > j * half)
            def _():
                sl = slice(j * half, (j + 1) * half)
                rows = lax.broadcasted_iota(I32, (half, 1), 0) + j * half
                lo, hi = _unpack_rows(jnp.where(rows < valid, x_ref[sl, :], 0))
                x = jnp.concatenate([lo.astype(BF16), hi.astype(BF16)], axis=1)
                h = _dot(x, wu_s[...]) + bu_ref[...]
                h_glu = jnp.minimum(h[:, :D_EXPERT], SWIGLU_LIMIT)
                h_lin = jnp.clip(h[:, D_EXPERT:], -SWIGLU_LIMIT, SWIGLU_LIMIT)
                act = h_glu * jax.nn.sigmoid(SWIGLU_ALPHA * h_glu) * (h_lin + 1.0)
                y_ref[sl, :] = _pack_rows(_dot(act.astype(BF16), wd_s[...]) + bd_ref[...])


def _moe_call(l, tile_e, tile_v, n_tiles, xs, w_up, b_up, w_down, b_down, name):
    n_tiles_max = xs.shape[0] // TME

    def tile(i, te, tv, nt):
        return jnp.minimum(i, nt[0] - 1)

    def wmap(i, te, tv, nt):
        return (l, te[tile(i, te, tv, nt)], 0, 0)

    grid_spec = pltpu.PrefetchScalarGridSpec(
        num_scalar_prefetch=3,
        grid=(n_tiles_max,),
        in_specs=[pl.BlockSpec((TME, D_PACK), lambda i, te, tv, nt: (tile(i, te, tv, nt), 0)),
                  pl.BlockSpec(memory_space=pl.ANY),
                  pl.BlockSpec((None, None, 1, 2 * D_EXPERT), wmap),
                  pl.BlockSpec(memory_space=pl.ANY),
                  pl.BlockSpec((None, None, 1, D_MODEL), wmap)],
        out_specs=pl.BlockSpec((TME, D_PACK), lambda i, te, tv, nt: (tile(i, te, tv, nt), 0)),
        scratch_shapes=[pltpu.VMEM((2, D_MODEL, 2 * D_EXPERT), F32), pltpu.VMEM((2, D_EXPERT, D_MODEL), F32),
                        pltpu.VMEM((D_MODEL, 2 * D_EXPERT), BF16), pltpu.VMEM((D_EXPERT, D_MODEL), BF16),
                        pltpu.SemaphoreType.DMA((2, 2)), pltpu.SMEM((1,), I32)],
    )
    return pl.pallas_call(
        functools.partial(_moe_kernel, l, n_tiles_max), grid_spec=grid_spec,
        out_shape=jax.ShapeDtypeStruct((xs.shape[0], D_PACK), I32),
        compiler_params=_cparams(("arbitrary",)), name=name,
    )(tile_e, tile_v, n_tiles, xs, w_up, b_up.reshape(DEPTH, N_EXPERTS, 1, -1),
      w_down, b_down.reshape(DEPTH, N_EXPERTS, 1, -1))


def _final_kernel(yg_ref, tw_ref, x1_ref, mod_ref, ln_g_ref, ln_b_ref, o_ref):
    tw = tw_ref[...]
    halves = [_unpack_rows(yg_ref[k]) for k in range(TOP_K)]
    moe = jnp.concatenate([sum(tw[:, k:k + 1] * halves[k][j] for k in range(TOP_K)) for j in range(2)], axis=1)
    gate2 = mod_ref[0, 5:6, :]
    o_ref[...] = _ln(DEEPNORM_ALPHA * x1_ref[...] + gate2 * moe) * ln_g_ref[...] + ln_b_ref[...]


def _final_call(lat, yg, tw, x1, mod, ln_g, ln_b):
    n = x1.shape[0]
    row = lambda w: pl.BlockSpec((TM, w), lambda i: (i, 0))
    const = lambda shape: pl.BlockSpec(shape, lambda i: (0,) * len(shape))
    return pl.pallas_call(
        _final_kernel,
        grid=(n // TM,),
        in_specs=[pl.BlockSpec((TOP_K, TM, D_PACK), lambda i: (0, i, 0)), row(TOP_K), row(D_MODEL),
                  _mod_spec(lat), const((1, D_MODEL)), const((1, D_MODEL))],
        out_specs=row(D_MODEL), out_shape=jax.ShapeDtypeStruct((n, D_MODEL), F32),
        compiler_params=_cparams(("arbitrary",)), name="final_lat" if lat else "final_ctx",
    )(yg, tw, x1, mod, ln_g, ln_b)


def _rope_tables():
    half = A_HEAD_DIM // 2
    freqs = ROPE_BASE ** (-jnp.arange(0, half, 2, dtype=F32) / half)
    t = jnp.arange(DEC_SEQ, dtype=jnp.int32)
    ang_r = (t // GRID_W).astype(F32)[:, None] * freqs[None, :]
    ang_c = (t % GRID_W).astype(F32)[:, None] * freqs[None, :]
    cos64 = jnp.concatenate([jnp.cos(ang_r)] * 2 + [jnp.cos(ang_c)] * 2, axis=1)
    sin64 = jnp.concatenate([-jnp.sin(ang_r), jnp.sin(ang_r), -jnp.sin(ang_c), jnp.sin(ang_c)], axis=1)
    lat = (jnp.tile(cos64, (1, 2)), jnp.tile(sin64, (1, 2)))
    ctx = (jnp.ones((TM, LANES), F32), jnp.zeros((TM, LANES), F32))
    return ctx, lat


def _permute_in_cols(t):
    qa = [t[..., h * 64:(h + 1) * 64] for h in QA_HEAD_ORDER]
    pad = jnp.zeros(t.shape[:-1] + (D_IN_PAD - D_IN,), t.dtype)
    return jnp.concatenate(qa + [t[..., 512:2816], t[..., 2832:7440], t[..., 2816:2832], pad], axis=-1)


def _layer_weights(p):
    w_in, b_in = p["w_in"], p["b_in"]
    wpa = p["w_pa"].reshape(A_HEADS, A_HEAD_DIM, D_MODEL)[np.asarray(QA_HEAD_ORDER)].reshape(512, D_MODEL)
    row = lambda t: t.reshape(1, -1)
    return dict(
        inproj=(_permute_in_cols(w_in).astype(BF16), _permute_in_cols(b_in).reshape(1, D_IN_PAD),
                w_in[:, 2816:2832].T.astype(BF16), b_in[2816:2832].reshape(16, 1),
                row(jnp.tile(p["qn_a"], A_HEADS)), row(jnp.tile(p["kn_a"], A_KV_HEADS)),
                jnp.asarray(np.kron(np.eye(8), np.ones((64, 64))), BF16)),
        merge=(row(p["hn_b"]), wpa.astype(BF16), p["w_pb"].astype(BF16), p["w_pc"].astype(BF16),
               p["w_o"].astype(BF16), row(p["b_o"]), row(p["ln1_g"]), row(p["ln1_b"]),
               p["w_router"].T, p["b_router"].reshape(N_EXPERTS, 1)),
        lam_c=p["lam_c"], sn_c=row(p["sn_c"]), ln2=(row(p["ln2_g"]), row(p["ln2_b"])))


def _chain_layer(l, lat, x, mod, tables, caches, w, experts, outs):
    cos_t, sin_t = tables
    lam_init = 0.8 - 0.6 * math.exp(-0.3 * l)
    tag = "lat" if lat else "ctx"
    res = _inproj_call(l, lat, x, mod, cos_t, sin_t, *w["inproj"], None if outs is None else outs[:4])
    qa, kab, vab, qb, kb, vb, ob, qc, kcb, vcb, g, ifg, ifgt = res[:13]
    if lat:
        ck_a, cv_a, ck_c, cv_c, cn0, m0 = caches
    else:
        ck_a = cv_a = ck_c = cv_c = cn0 = m0 = None
    oa = _attn_call("a", lat, lam_init, qa, kab, vab, ck_a, cv_a, None, None)
    oc = _attn_call("c", lat, lam_init, qc, kcb, vcb, ck_c, cv_c, w["lam_c"], w["sn_c"])
    ml = _mlstm_call(l, lat, qb, kb, vb, ifg, ifgt, cn0, m0, None if outs is None else outs[4:6])
    hf, hb = ml[:2]
    x1, u2, top_i, top_w = _merge_call(lat, x, mod, oa, hf, hb, ob, oc, g, *w["merge"])

    n = x.shape[0]
    pos, tile_e, tile_v, n_tiles = _route_call(top_i, "route_" + tag)
    pos_flat = pos.reshape(TOP_K * n)
    nt_max = _max_tiles(n)
    xs = _dispatch_call(u2, pos_flat, nt_max * TME)
    ys = _moe_call(l, tile_e[0, :nt_max], tile_v[0, :nt_max], n_tiles[0, :1], xs,
                   experts["w_up"], experts["b_up"], experts["w_down"], experts["b_down"], "moe_" + tag)
    yg = _combine_call(ys, pos_flat).reshape(TOP_K, n, D_PACK)
    x2 = _final_call(lat, yg, top_w.T, x1, mod, *w["ln2"])
    if lat:
        return x2, None
    m_l = ml[4][:, :, :, 0, 0]
    m_all = m_l[:, None] if outs is None else jnp.concatenate([outs[6], m_l[:, None]], axis=1)
    return x2, tuple(res[13:]) + (ml[2], ml[3], m_all)


def kernel(x_prompt, x_sample, cache_k_gqa, cache_v_gqa, cache_k_diff, cache_v_diff, state_mlstm_c, state_mlstm_n, state_mlstm_m, c, c_ctx, w_in, b_in, qn_a, kn_a, hn_b, lam_c, sn_c, w_pa, w_pb, w_pc, w_o, b_o, w_mod, b_mod, ln1_g, ln1_b, ln2_g, ln2_b, w_router, b_router, w_up, b_up, w_down, b_down):
    x_ctx = x_prompt.reshape(N_CTX, D_MODEL)
    x_lat = x_sample.reshape(N_LAT, D_MODEL)
    cvecs = jnp.zeros((8, D_MODEL), F32).at[0].set(c_ctx).at[1:1 + DEC_BATCH].set(c)
    mod_all = _mod_call(cvecs, w_mod, b_mod).reshape(DEPTH, 8, N_MOD, D_MODEL)[:, :N_GROUPS]
    tables_ctx, tables_lat = _rope_tables()
    params = dict(w_in=w_in, b_in=b_in, qn_a=qn_a, kn_a=kn_a, hn_b=hn_b, lam_c=lam_c, sn_c=sn_c, w_pa=w_pa,
                  w_pb=w_pb, w_pc=w_pc, w_o=w_o, b_o=b_o, ln1_g=ln1_g, ln1_b=ln1_b, ln2_g=ln2_g, ln2_b=ln2_b,
                  w_router=w_router, b_router=b_router)
    experts = dict(w_up=w_up, b_up=b_up, w_down=w_down, b_down=b_down)
    outs = None
    for l in range(DEPTH):
        w = _layer_weights({k: v[l] for k, v in params.items()})
        ck_a = cache_k_gqa[:, l].transpose(0, 2, 1, 3).reshape(DEC_BATCH, PAST_LEN, 128).astype(BF16)
        cv_a = cache_v_gqa[:, l].transpose(0, 2, 1, 3).reshape(DEC_BATCH, PAST_LEN, 128).astype(BF16)
        ck_c = cache_k_diff[:, l].astype(BF16)
        cv_c = cache_v_diff[:, l].astype(BF16)
        cn0 = jnp.concatenate([state_mlstm_c[:, l], state_mlstm_n[:, l][..., None],
                               jnp.zeros((DEC_BATCH, 2, B_HEADS, B_HEAD_DIM, B_HEAD_DIM - 1), F32)], axis=-1)
        m0 = jnp.broadcast_to(state_mlstm_m[:, l][..., None, None], (DEC_BATCH, 2, B_HEADS, 1, LANES))
        x_ctx, outs = _chain_layer(l, False, x_ctx, mod_all[l], tables_ctx, None, w, experts, outs)
        x_lat, _ = _chain_layer(l, True, x_lat, mod_all[l], tables_lat,
                                (ck_a, cv_a, ck_c, cv_c, cn0, m0), w, experts, None)
    return (x_ctx.reshape(BATCH, SEQ, D_MODEL), x_lat.reshape(DEC_BATCH, DEC_SEQ, D_MODEL)) + tuple(outs)
```

```python
import functools
import math

import numpy as np
import jax
import jax.numpy as jnp
from jax import lax
from jax.experimental import pallas as pl
from jax.experimental.pallas import tpu as pltpu
from jax.experimental.pallas import tpu_sc as plsc

F32 = jnp.float32
BF16 = jnp.bfloat16
I32 = jnp.int32

D_MODEL = 1024
BATCH = 16
SEQ = 256
DEPTH = 2
DEC_BATCH = 2
DEC_SEQ = 4096
PAST_LEN = 512
GRID_W = 64
ROPE_BASE = 10000.0
EPS = 1e-6
A_HEADS = 8
A_KV_HEADS = 2
A_HEAD_DIM = 64
B_HEADS = 4
B_HEAD_DIM = 128
MLSTM_CHUNK = 128
C_HEADS = 4
C_HEAD_DIM = 64
C_V_DIM = 128
N_EXPERTS = 32
TOP_K = 4
D_EXPERT = 1024
SWIGLU_LIMIT = 7.0
SWIGLU_ALPHA = 1.702
N_MOD = 6
DEEPNORM_ALPHA = (2 * DEPTH) ** 0.25

N_CTX = BATCH * SEQ
N_LAT = DEC_BATCH * DEC_SEQ
N_GROUPS = 1 + DEC_BATCH

LANES = 128
TM = 256
TQ = 512
KEY_BLOCK = 1024
MERGE_ROWS = 128
TME = 512
TILE_LANES = 128
ROUTE_CHUNK = 512
D_PACK = D_MODEL // 2
SC_WINDOW = 64
VMEM_LIMIT = 56 * 1024 * 1024

QA_HEAD_ORDER = (0, 4, 1, 5, 2, 6, 3, 7)
D_IN = 7440
D_IN_PAD = 7552
P_QA, P_KA, P_VA, P_QB, P_KB, P_VB, P_OB, P_QC, P_KC, P_VC, P_G, P_IF = (
    0, 512, 640, 768, 1280, 1792, 2304, 2816, 3328, 3840, 4352, 7424)


def _max_tiles(n_tok):
    return (n_tok * TOP_K) // TME + N_EXPERTS


def _cparams(sem):
    return pltpu.CompilerParams(dimension_semantics=sem, vmem_limit_bytes=VMEM_LIMIT)


def _split3(a):
    a1 = a.astype(BF16)
    r = a - a1.astype(F32)
    a2 = r.astype(BF16)
    a3 = (r - a2.astype(F32)).astype(BF16)
    return a1, a2, a3


def _dot(a, b, dims=None):
    if dims is None:
        return jnp.dot(a, b, preferred_element_type=F32)
    return lax.dot_general(a, b, (dims, ((), ())), preferred_element_type=F32)


def _dot_exact_rhs01(a, m, dims=None):
    return sum(_dot(p, m, dims) for p in _split3(a))


def _dot_exact_lhs01(m, a, dims=None):
    return sum(_dot(m, p, dims) for p in _split3(a))


def _pack_rows(x):
    xb = x.astype(BF16).astype(F32)
    lo = pltpu.bitcast(xb[:, :D_PACK], I32)
    lo = lax.shift_right_logical(lo, jnp.full_like(lo, 16))
    hi = pltpu.bitcast(xb[:, D_PACK:], I32) & jnp.int32(-65536)
    return lo | hi


def _unpack_rows(w):
    return (pltpu.bitcast(lax.shift_left(w, jnp.full_like(w, 16)), F32),
            pltpu.bitcast(w & jnp.int32(-65536), F32))


def _ln(x):
    mu = jnp.mean(x, axis=-1, keepdims=True)
    xc = x - mu
    var = jnp.mean(xc * xc, axis=-1, keepdims=True)
    return xc * lax.rsqrt(var + EPS)


def _log_sigmoid(x):
    return jnp.minimum(x, 0.0) - jnp.log(1.0 + jnp.exp(-jnp.abs(x)))


def _rope(x, cos, sin):
    lane = lax.broadcasted_iota(I32, (1, LANES), 1)
    first = (lane % 32) < 16
    outs = []
    for j in range(x.shape[1] // LANES):
        xj = x[:, j * LANES:(j + 1) * LANES]
        partner = jnp.where(first, pltpu.roll(xj, LANES - 16, 1), pltpu.roll(xj, 16, 1))
        outs.append(xj * cos + partner * sin)
    return outs[0] if len(outs) == 1 else jnp.concatenate(outs, axis=1)


def _mod_spec(lat):
    if lat:
        return pl.BlockSpec((1, N_MOD, D_MODEL), lambda i: (1 + i * TM // DEC_SEQ, 0, 0))
    return pl.BlockSpec((1, N_MOD, D_MODEL), lambda i: (0, 0, 0))


def _mod_kernel(c_ref, w_ref, b_ref, o_ref):
    c = c_ref[...]
    s = c * jax.nn.sigmoid(c)
    o_ref[...] = _dot(s.astype(BF16), w_ref[...].astype(BF16)) + b_ref[...]


def _mod_call(cvecs, w_mod, b_mod):
    tn = 1536
    n = N_MOD * D_MODEL
    return pl.pallas_call(
        _mod_kernel,
        grid=(DEPTH, n // tn),
        in_specs=[pl.BlockSpec((8, D_MODEL), lambda l, j: (0, 0)),
                  pl.BlockSpec((None, D_MODEL, tn), lambda l, j: (l, 0, j)),
                  pl.BlockSpec((None, 1, tn), lambda l, j: (l, 0, j))],
        out_specs=pl.BlockSpec((None, 8, tn), lambda l, j: (l, 0, j)),
        out_shape=jax.ShapeDtypeStruct((DEPTH, 8, n), F32),
        compiler_params=_cparams(("arbitrary", "arbitrary")),
        name="mod",
    )(cvecs, w_mod, b_mod.reshape(DEPTH, 1, n))


def _inproj_kernel(with_cache, n_alias, x_ref, mod_ref, cos_ref, sin_ref, w_ref, b_ref, wgt_ref, bgt_ref,
                   gq_ref, gk_ref, gmat_ref, *refs):
    outs = refs[n_alias:]
    qa_o, kab_o, vab_o, qb_o, kb_o, vb_o, ob_o, qc_o, kcb_o, vcb_o, g_o, ifg_o, ifgt_o = outs[:13]
    x = x_ref[...]
    shift = mod_ref[0, 0:1, :]
    scale = mod_ref[0, 1:2, :]
    ub = (_ln(x) * (1.0 + scale) + shift).astype(BF16)
    cos = cos_ref[...]
    sin = sin_ref[...]

    def proj(lo, n):
        return _dot(ub, w_ref[:, lo:lo + n]) + b_ref[:, lo:lo + n]

    def headnorm(z, gmat, gain):
        ss = _dot((z * z).astype(BF16), gmat)
        return z * lax.rsqrt(ss * (1.0 / A_HEAD_DIM) + EPS) * gain

    segs = [(P_QA, 512), (P_KA, 128), (P_VA, 128), (P_QB, 512), (P_KB, 512), (P_VB, 512), (P_OB, 512),
            (P_QC, 512), (P_KC, 512), (P_VC, 512), (P_G, 1024), (P_G + 1024, 1024), (P_G + 2048, 1024),
            (P_IF, LANES)]
    keep = {}

    def epilogue(k, z):
        if k == 0:
            z = headnorm(z, gmat_ref[...], gq_ref[...])
            qa_o[...] = (_rope(z, cos, sin) * 0.125).astype(BF16)
        elif k == 1:
            keep["k"] = _rope(headnorm(z, gmat_ref[0:128, 0:128], gk_ref[...]), cos, sin)
            kab_o[...] = keep["k"].astype(BF16)
        elif k == 2:
            keep["v"] = z
            vab_o[...] = z.astype(BF16)
        elif k == 3:
            qb_o[...] = z.astype(BF16)
        elif k == 4:
            kb_o[...] = (z * (B_HEAD_DIM ** -0.5)).astype(BF16)
        elif k == 5:
            vb_o[...] = z.astype(BF16)
        elif k == 6:
            ob_o[...] = z.astype(BF16)
        elif k == 7:
            qc_o[...] = (_rope(z, cos, sin) * 0.125).astype(BF16)
        elif k == 8:
            keep["kc"] = _rope(z, cos, sin)
            kcb_o[...] = keep["kc"].astype(BF16)
        elif k == 9:
            keep["vc"] = z
            vcb_o[...] = z.astype(BF16)
        elif k in (10, 11, 12):
            g_o[:, (k - 10) * 1024:(k - 9) * 1024] = jax.nn.sigmoid(z).astype(BF16)
        else:
            lane = lax.broadcasted_iota(I32, (1, LANES), 1)
            ifg_o[...] = jnp.where(lane >= 8, _log_sigmoid(z), z)

    z_prev = proj(*segs[0])
    for k in range(1, len(segs)):
        z_next = proj(*segs[k])
        epilogue(k - 1, z_prev)
        z_prev = z_next
    zt = _dot(wgt_ref[...], ub, ((1,), (1,))) + bgt_ref[...]
    epilogue(len(segs) - 1, z_prev)
    row = lax.broadcasted_iota(I32, (16, 1), 0)
    ifgt_o[...] = jnp.where(row >= 8, _log_sigmoid(zt), zt)

    if with_cache:
        nka_o, nva_o, nkc_o, nvc_o = outs[13:]
        for h in range(A_KV_HEADS):
            nka_o[h] = keep["k"][:, h * A_HEAD_DIM:(h + 1) * A_HEAD_DIM]
            nva_o[h] = keep["v"][:, h * A_HEAD_DIM:(h + 1) * A_HEAD_DIM]
        for h in range(C_HEADS):
            nkc_o[h] = keep["kc"][:, h * LANES:(h + 1) * LANES]
            nvc_o[h] = keep["vc"][:, h * LANES:(h + 1) * LANES]


def _inproj_call(l, lat, x, mod, cos_t, sin_t, w_p, b_p, w_gt, b_gt, gq, gk, gmat, cache_outs):
    n = x.shape[0]
    row = lambda w: pl.BlockSpec((TM, w), lambda i: (i, 0))
    const = lambda shape: pl.BlockSpec(shape, lambda i: (0,) * len(shape), pipeline_mode=pl.Buffered(1))
    sds = lambda w, dt: jax.ShapeDtypeStruct((n, w), dt)
    out_defs = [(512, BF16), (128, BF16), (128, BF16), (512, BF16), (512, BF16), (512, BF16), (512, BF16),
                (512, BF16), (512, BF16), (512, BF16), (3072, BF16), (LANES, F32)]
    out_specs = [row(w) for w, _ in out_defs] + [pl.BlockSpec((16, TM), lambda i: (0, i))]
    out_shape = [sds(w, dt) for w, dt in out_defs] + [jax.ShapeDtypeStruct((16, n), F32)]
    alias_in, aliases = [], {}
    if lat:
        tbl = pl.BlockSpec((TM, LANES), lambda i: (i % (DEC_SEQ // TM), 0))
    else:
        assert TM == SEQ
        tbl = pl.BlockSpec((TM, LANES), lambda i: (0, 0))
        cache_defs = [(A_KV_HEADS, A_HEAD_DIM), (A_KV_HEADS, A_HEAD_DIM), (C_HEADS, LANES), (C_HEADS, LANES)]
        out_specs += [pl.BlockSpec((None, None, nh, SEQ, d), lambda i: (i, l, 0, 0, 0)) for nh, d in cache_defs]
        out_shape += [jax.ShapeDtypeStruct((BATCH, DEPTH, nh, SEQ, d), F32) for nh, d in cache_defs]
        if cache_outs is not None:
            alias_in = list(cache_outs)
            aliases = {11 + j: len(out_defs) + 1 + j for j in range(4)}
    return pl.pallas_call(
        functools.partial(_inproj_kernel, not lat, len(alias_in)),
        grid=(n // TM,),
        in_specs=[row(D_MODEL), _mod_spec(lat), tbl, tbl,
                  const((D_MODEL, D_IN_PAD)), const((1, D_IN_PAD)),
                  const((16, D_MODEL)), const((16, 1)),
                  const((1, 512)), const((1, 128)), const((512, 512))]
        + [pl.BlockSpec(memory_space=pl.ANY)] * len(alias_in),
        out_specs=out_specs, out_shape=out_shape, input_output_aliases=aliases,
        compiler_params=_cparams(("arbitrary",)),
        name="inproj_lat" if lat else "inproj_ctx",
    )(x, mod, cos_t, sin_t, w_p, b_p, w_gt, b_gt, gq, gk, gmat, *alias_in)


def _attn_kernel(mode, nseg, lam_init, *refs):
    q_ref = refs[0]
    kv = refs[1:1 + 2 * nseg]
    rest = refs[1 + 2 * nseg:]
    o_ref = rest[-1]
    tq = q_ref.shape[0]
    lane = lax.broadcasted_iota(I32, (1, LANES), 1)
    if mode == "c":
        lam_ref, sn_ref = rest[0], rest[1]
        lp = lam_ref[...]
        lam = (jnp.exp(jnp.sum(lp[0:1] * lp[1:2], axis=-1, keepdims=True))
               - jnp.exp(jnp.sum(lp[2:3] * lp[3:4], axis=-1, keepdims=True)) + lam_init)
    for blk in range(q_ref.shape[1] // LANES):
        qs = slice(blk * LANES, (blk + 1) * LANES)
        ks = qs if kv[0].shape[1] > LANES else slice(0, LANES)
        q = q_ref[:, qs]
        zero = jnp.zeros_like(q)
        qq = jnp.concatenate([jnp.where(lane < 64, q, zero), jnp.where(lane >= 64, q, zero)], axis=0)
        m = l = acc = None
        for j in range(nseg):
            k_ref, v_ref = kv[2 * j], kv[2 * j + 1]
            for off in range(0, k_ref.shape[0], KEY_BLOCK):
                n = min(KEY_BLOCK, k_ref.shape[0] - off)
                s = _dot(qq, k_ref[off:off + n, ks], ((1,), (1,)))
                bm = jnp.max(s, axis=-1, keepdims=True)
                if m is None:
                    m = bm
                    e = jnp.exp(s - m)
                    l = jnp.sum(e, axis=-1, keepdims=True)
                    acc = _dot(e.astype(BF16), v_ref[off:off + n, ks])
                else:
                    m_new = jnp.maximum(m, bm)
                    alpha = jnp.exp(m - m_new)
                    e = jnp.exp(s - m_new)
                    l = alpha * l + jnp.sum(e, axis=-1, keepdims=True)
                    acc = alpha * acc + _dot(e.astype(BF16), v_ref[off:off + n, ks])
                    m = m_new
        o = acc * (1.0 / l)
        if mode == "a":
            out = jnp.where(lane < 64, o[:tq], o[tq:])
        else:
            o = o[:tq] - lam * o[tq:]
            ms = jnp.mean(o * o, axis=-1, keepdims=True)
            out = o * lax.rsqrt(ms + EPS) * sn_ref[...] * (1.0 - lam_init)
        o_ref[:, qs] = out.astype(o_ref.dtype)


def _attn_call(mode, lat, lam_init, q_all, k_all, v_all, k_cache, v_cache, lam_c, sn_c):
    wide = k_all.shape[1] == 512
    kcol = (lambda h: h) if wide else (lambda h: 0)
    if lat:
        nq = DEC_SEQ // TQ
        grid = (DEC_BATCH, nq, 4)
        q_spec = pl.BlockSpec((TQ, LANES), lambda b, i, h: (b * nq + i, h))
        kn_spec = pl.BlockSpec((DEC_SEQ, LANES), lambda b, i, h: (b, kcol(h)))
        if wide:
            kc_spec = pl.BlockSpec((None, None, PAST_LEN, LANES), lambda b, i, h: (b, h, 0, 0))
        else:
            kc_spec = pl.BlockSpec((None, PAST_LEN, LANES), lambda b, i, h: (b, 0, 0))
        in_specs = [q_spec, kc_spec, kc_spec, kn_spec, kn_spec]
        args = [q_all, k_cache, v_cache, k_all, v_all]
        nseg = 2
        o_spec = q_spec
        sem = ("arbitrary", "arbitrary", "arbitrary")
        zero_map = lambda b, i, h: (0, 0)
    else:
        grid = (BATCH,)
        q_spec = pl.BlockSpec((SEQ, 512), lambda b: (b, 0))
        k_spec = pl.BlockSpec((SEQ, k_all.shape[1]), lambda b: (b, 0))
        in_specs = [q_spec, k_spec, k_spec]
        args = [q_all, k_all, v_all]
        nseg = 1
        o_spec = q_spec
        sem = ("arbitrary",)
        zero_map = lambda b: (0, 0)
    if mode == "c":
        in_specs += [pl.BlockSpec((4, C_HEAD_DIM), zero_map), pl.BlockSpec((1, C_V_DIM), zero_map)]
        args += [lam_c, sn_c]
    return pl.pallas_call(
        functools.partial(_attn_kernel, mode, nseg, lam_init),
        grid=grid, in_specs=in_specs, out_specs=o_spec,
        out_shape=jax.ShapeDtypeStruct((q_all.shape[0], 512), BF16),
        compiler_params=_cparams(sem), name=f"attn_{mode}_{'lat' if lat else 'ctx'}",
    )(*args)


def _mlstm_kernel(has_init, want_state, n_alias, *refs):
    tok = (refs[0:5], refs[5:10])
    pos = 10
    if has_init:
        cn0_ref, m0_ref = refs[pos:pos + 2]
        pos += 2
    pos += n_alias
    h_refs = refs[pos:pos + 2]
    pos += 2
    if want_state:
        c_o, n_o, m_o = refs[pos:pos + 3]
        pos += 3
    cn_s, m_s = refs[pos:pos + 2]

    c = pl.program_id(1)
    last = pl.num_programs(1) - 1
    L = MLSTM_CHUNK

    @pl.when(c == 0)
    def _():
        if has_init:
            cn_s[...] = cn0_ref[...]
            m_s[...] = m0_ref[...]
        else:
            cn_s[...] = jnp.zeros_like(cn_s)
            m_s[...] = jnp.zeros_like(m_s)

    row = lax.broadcasted_iota(I32, (L, L), 0)
    col = lax.broadcasted_iota(I32, (L, L), 1)
    lane = lax.broadcasted_iota(I32, (1, LANES), 1)
    ones_col = jnp.where(lane == 0, 1.0, 0.0).astype(BF16)

    chains = []
    for d in range(2):
        q_ref, k_ref, v_ref, g_ref, gt_ref = tok[d]
        keep = (col <= row) if d == 0 else (col >= row)
        m_c = keep.astype(BF16)
        m_r = ((row <= col) if d == 0 else (row >= col)).astype(BF16)
        g = g_ref[...]
        gt = gt_ref[...]
        bc = _dot_exact_lhs01(m_c, g)
        br = _dot_exact_rhs01(gt, m_r)
        for h in range(B_HEADS):
            sl = slice(h * L, (h + 1) * L)
            gi, gf = 4 * d + h, 8 + 4 * d + h
            ch = dict(d=d, h=h, sl=sl, keep=keep, q=q_ref[:, sl], k=k_ref[:, sl], v=v_ref[:, sl],
                      i_col=g[:, gi:gi + 1], f_col=g[:, gf:gf + 1], b_col=bc[:, gf:gf + 1],
                      r_row=gt[gi:gi + 1, :] - br[gf:gf + 1, :],
                      m_prev=m_s[d, h][:, 0:1], cn_prev=cn_s[d, h])
            chains.append(ch)
    for ch in chains:
        ch["s"] = _dot(ch["q"], ch["k"], ((1,), (1,)))
        ch["qc"] = _dot(ch["q"], ch["cn_prev"].astype(BF16))
    for ch in chains:
        ch["total"] = jnp.sum(ch["f_col"], axis=0, keepdims=True)
        ch["inter"] = ch["b_col"] + ch["m_prev"]
        ch["v_aug"] = jnp.concatenate([ch["v"], jnp.broadcast_to(ones_col, (L, LANES))], axis=1)
    for ch in chains:
        ch["dmat"] = jnp.where(ch["keep"], ch["b_col"] + ch["r_row"], -jnp.inf)
    for ch in chains:
        ch["dmax"] = jnp.max(ch["dmat"], axis=-1, keepdims=True)
    for ch in chains:
        ch["m_row"] = jnp.maximum(ch["inter"], ch["dmax"])
    for ch in chains:
        ch["sc"] = (ch["s"] * jnp.exp(ch["dmat"] - ch["m_row"])).astype(BF16)
    for ch in chains:
        ch["a"] = _dot(ch["sc"], ch["v_aug"])
    for ch in chains:
        ch["nd"] = jnp.exp(ch["inter"] - ch["m_row"]) * ch["qc"] + ch["a"]
    for ch in chains:
        ch["scale"] = 1.0 / jnp.maximum(jnp.abs(ch["nd"][:, L:L + 1]), jnp.exp(-ch["m_row"]))
    for ch in chains:
        h_refs[ch["d"]][:, ch["sl"]] = ch["nd"][:, :L] * ch["scale"]
    for ch in chains:
        ch["wel"] = ch["total"] - ch["b_col"] + ch["i_col"]
        ch["m_new"] = jnp.maximum(ch["total"] + ch["m_prev"], jnp.max(ch["wel"], axis=0, keepdims=True))
    for ch in chains:
        ch["kw"] = (jnp.exp(ch["wel"] - ch["m_new"]) * ch["k"].astype(F32)).astype(BF16)
    for ch in chains:
        ch["u"] = _dot(ch["kw"], ch["v_aug"], ((0,), (0,)))
    for ch in chains:
        cn_s[ch["d"], ch["h"]] = jnp.exp(ch["total"] + ch["m_prev"] - ch["m_new"]) * ch["cn_prev"] + ch["u"]
        m_s[ch["d"], ch["h"]] = jnp.broadcast_to(ch["m_new"], (1, LANES))

    if want_state:
        @pl.when(c == last)
        def _():
            pick = ((lax.broadcasted_iota(I32, (8, 2 * L), 1) == L)
                    & (lax.broadcasted_iota(I32, (8, 2 * L), 0) == 0)).astype(BF16)
            for d in range(2):
                for h in range(B_HEADS):
                    cn = cn_s[d, h]
                    c_o[d, h] = cn[:, :L]
                    n_o[d, h:h + 1, :] = _dot_exact_lhs01(pick, cn, ((1,), (1,)))[0:1]
            m_o[...] = m_s[...]


def _mlstm_call(l, lat, q_all, k_all, v_all, g_all, gt_all, cn0, m0, state_bufs):
    L = MLSTM_CHUNK
    n = q_all.shape[0]
    nb, nc = (DEC_BATCH, DEC_SEQ // L) if lat else (BATCH, SEQ // L)
    chunk = (lambda b, c: b * nc + c, lambda b, c: b * nc + (nc - 1 - c))

    in_specs, args = [], []
    for d in range(2):
        tok = lambda w, d=d: pl.BlockSpec((L, w), lambda b, c: (chunk[d](b, c), 0))
        in_specs += [tok(512), tok(512), tok(512), tok(LANES),
                     pl.BlockSpec((16, L), lambda b, c, d=d: (0, chunk[d](b, c)))]
        args += [q_all, k_all, v_all, g_all, gt_all]
    if lat:
        in_specs += [pl.BlockSpec((None, 2, B_HEADS, L, 2 * L), lambda b, c: (b, 0, 0, 0, 0)),
                     pl.BlockSpec((None, 2, B_HEADS, 1, LANES), lambda b, c: (b, 0, 0, 0, 0))]
        args += [cn0, m0]
    aliases = {}
    alias_in = list(state_bufs) if state_bufs is not None else []
    if alias_in:
        aliases = {len(args): 2, len(args) + 1: 3}
    in_specs += [pl.BlockSpec(memory_space=pl.ANY)] * len(alias_in)
    args += alias_in
    out_specs = [pl.BlockSpec((L, 512), lambda b, c, d=d: (chunk[d](b, c), 0)) for d in range(2)]
    out_shape = [jax.ShapeDtypeStruct((n, 512), F32)] * 2
    if not lat:
        out_specs += [pl.BlockSpec((None, None, 2, B_HEADS, L, L), lambda b, c: (b, l, 0, 0, 0, 0)),
                      pl.BlockSpec((None, None, 2, B_HEADS, L), lambda b, c: (b, l, 0, 0, 0)),
                      pl.BlockSpec((None, 2, B_HEADS, 1, LANES), lambda b, c: (b, 0, 0, 0, 0))]
        out_shape += [jax.ShapeDtypeStruct((BATCH, DEPTH, 2, B_HEADS, L, L), F32),
                      jax.ShapeDtypeStruct((BATCH, DEPTH, 2, B_HEADS, L), F32),
                      jax.ShapeDtypeStruct((nb, 2, B_HEADS, 1, LANES), F32)]
    return pl.pallas_call(
        functools.partial(_mlstm_kernel, lat, not lat, len(alias_in)),
        grid=(nb, nc), in_specs=in_specs, out_specs=out_specs, out_shape=out_shape,
        input_output_aliases=aliases,
        scratch_shapes=[pltpu.VMEM((2, B_HEADS, L, 2 * L), F32), pltpu.VMEM((2, B_HEADS, 1, LANES), F32)],
        compiler_params=_cparams(("arbitrary", "arbitrary")), name="mlstm_lat" if lat else "mlstm_ctx",
    )(*args)


def _merge_kernel(x_ref, mod_ref, oa_ref, hf_ref, hb_ref, ob_ref, oc_ref, g_ref, hn_ref,
                  wpa_ref, wpb_ref, wpc_ref, wo_ref, bo_ref, ln_g_ref, ln_b_ref, wr_ref, br_ref,
                  x1_o, u2_o, ti_o, tw_o):
    gate1 = mod_ref[0, 2:3, :]
    shift2 = mod_ref[0, 3:4, :]
    scale2 = mod_ref[0, 4:5, :]
    wr = _split3(wr_ref[...])
    eio = lax.broadcasted_iota(I32, (N_EXPERTS, 1), 0)
    nt = ((1,), (1,))
    cs = [dict(rs=slice(r, r + MERGE_ROWS)) for r in range(0, TM, MERGE_ROWS)]
    for c in cs:
        c["pa"] = _dot(oa_ref[c["rs"], :], wpa_ref[...])
        c["pc"] = _dot(oc_ref[c["rs"], :], wpc_ref[...])
    for c in cs:
        hm = hf_ref[c["rs"], :] + hb_ref[c["rs"], :]
        parts = []
        for h in range(B_HEADS):
            hh = hm[:, h * LANES:(h + 1) * LANES]
            parts.append(hh * lax.rsqrt(jnp.mean(hh * hh, axis=-1, keepdims=True) + EPS))
        c["out_b"] = (jnp.concatenate(parts, axis=1) * hn_ref[...]
                      * jax.nn.sigmoid(ob_ref[c["rs"], :].astype(F32))).astype(BF16)
    for c in cs:
        c["pb"] = _dot(c["out_b"], wpb_ref[...])
    for c in cs:
        rs = c["rs"]
        c["merged"] = (g_ref[rs, 0:1024].astype(F32) * c["pa"] + g_ref[rs, 1024:2048].astype(F32) * c["pb"]
                       + g_ref[rs, 2048:3072].astype(F32) * c["pc"]).astype(BF16)
    for c in cs:
        c["y"] = _dot(c["merged"], wo_ref[...]) + bo_ref[...]
    for c in cs:
        c["x1"] = _ln(DEEPNORM_ALPHA * x_ref[c["rs"], :] + gate1 * c["y"]) * ln_g_ref[...] + ln_b_ref[...]
        x1_o[c["rs"], :] = c["x1"]
    for c in cs:
        c["u2"] = _ln(c["x1"]) * (1.0 + scale2) + shift2
        u2_o[c["rs"], :] = _pack_rows(c["u2"])
    for c in cs:
        up = _split3(c["u2"])
        c["cur"] = (_dot(wr[0], up[0], nt) + _dot(wr[0], up[1], nt) + _dot(wr[1], up[0], nt)
                    + _dot(wr[0], up[2], nt) + _dot(wr[2], up[0], nt) + _dot(wr[1], up[1], nt)) + br_ref[...]
        c["vals"], c["idxs"] = [], []
    for _ in range(TOP_K):
        for c in cs:
            mx = jnp.max(c["cur"], axis=0, keepdims=True)
            ix = jnp.min(jnp.where(c["cur"] == mx, eio, N_EXPERTS), axis=0, keepdims=True)
            c["vals"].append(mx)
            c["idxs"].append(ix)
            c["cur"] = jnp.where(eio == ix, -jnp.inf, c["cur"])
    for c in cs:
        ex = [jnp.exp(v - c["vals"][0]) for v in c["vals"]]
        inv = 1.0 / sum(ex)
        ti_o[:, c["rs"]] = jnp.concatenate(c["idxs"], axis=0)
        tw_o[:, c["rs"]] = jnp.concatenate([e * inv for e in ex], axis=0)


def _merge_call(lat, x, mod, out_a, hf, hb, ob, out_c, g, hn, wpa, wpb, wpc, wo, bo, ln_g, ln_b, wr_t, br):
    n = x.shape[0]
    row = lambda w: pl.BlockSpec((TM, w), lambda i: (i, 0))
    const = lambda shape: pl.BlockSpec(shape, lambda i: (0,) * len(shape), pipeline_mode=pl.Buffered(1))
    return pl.pallas_call(
        _merge_kernel,
        grid=(n // TM,),
        in_specs=[row(D_MODEL), _mod_spec(lat),
                  row(512), row(512), row(512), row(512), row(512), row(3072),
                  const((1, 512)), const((512, D_MODEL)), const((512, D_MODEL)), const((512, D_MODEL)),
                  const((D_MODEL, D_MODEL)), const((1, D_MODEL)), const((1, D_MODEL)), const((1, D_MODEL)),
                  const((N_EXPERTS, D_MODEL)), const((N_EXPERTS, 1))],
        out_specs=[row(D_MODEL), row(D_PACK),
                   pl.BlockSpec((TOP_K, TM), lambda i: (0, i)), pl.BlockSpec((TOP_K, TM), lambda i: (0, i))],
        out_shape=[jax.ShapeDtypeStruct((n, D_MODEL), F32), jax.ShapeDtypeStruct((n, D_PACK), I32),
                   jax.ShapeDtypeStruct((TOP_K, n), I32), jax.ShapeDtypeStruct((TOP_K, n), F32)],
        compiler_params=_cparams(("arbitrary",)), name="merge_lat" if lat else "merge_ctx",
    )(x, mod, out_a, hf, hb, ob, out_c, g, hn, wpa, wpb, wpc, wo, bo, ln_g, ln_b, wr_t, br)


def _route_kernel(n_first, ti_ref, pos_o, te_o, tv_o, ts_o, nt_o, rank_s):
    ch = ROUTE_CHUNK
    n_tok = ti_ref.shape[1]
    eio = lax.broadcasted_iota(I32, (N_EXPERTS, 1), 0)
    tri = (lax.broadcasted_iota(I32, (ch, ch), 0) <= lax.broadcasted_iota(I32, (ch, ch), 1)).astype(BF16)

    def onehots(off):
        ti = ti_ref[:, pl.ds(off, ch)]
        return [(eio == ti[k:k + 1, :]).astype(F32) for k in range(TOP_K)]

    def count_body(c, carry):
        off = pl.multiple_of(c * ch, ch)
        ohs = onehots(off)
        oh = ohs[0] + ohs[1] + ohs[2] + ohs[3]
        incl = _dot(oh.astype(BF16), tri)
        excl = incl - oh + carry
        for k in range(TOP_K):
            rank_s[k:k + 1, pl.ds(off, ch)] = jnp.sum(ohs[k] * excl, axis=0, keepdims=True)
        return carry + incl[:, ch - 1:ch]

    counts_first = lax.fori_loop(0, n_first // ch, count_body, jnp.zeros((N_EXPERTS, 1), F32))
    counts = lax.fori_loop(n_first // ch, n_tok // ch, count_body, counts_first)
    ntile_e = jnp.floor((counts + (TME - 1)) * (1.0 / TME))
    lower = (lax.broadcasted_iota(I32, (N_EXPERTS, N_EXPERTS), 1)
             < lax.broadcasted_iota(I32, (N_EXPERTS, N_EXPERTS), 0)).astype(BF16)
    off_t = _dot(lower, jnp.broadcast_to(ntile_e, (N_EXPERTS, LANES)).astype(BF16))[:, 0:1]
    end_t = off_t + ntile_e
    total = jnp.sum(ntile_e, axis=0, keepdims=True)
    jt = lax.broadcasted_iota(I32, (1, TILE_LANES), 1).astype(F32)
    te = jnp.minimum(jnp.sum((end_t <= jt).astype(F32), axis=0, keepdims=True), N_EXPERTS - 1.0)
    sel = (eio.astype(F32) == te).astype(F32)
    cnt_j = jnp.sum(sel * counts, axis=0, keepdims=True)
    off_j = jnp.sum(sel * off_t, axis=0, keepdims=True)
    valid = jnp.clip(cnt_j - (jt - off_j) * TME, 0.0, float(TME))
    first_j = jnp.sum(sel * counts_first, axis=0, keepdims=True)
    te_o[...] = te.astype(I32)
    tv_o[...] = jnp.where(jt < total, valid, 0.0).astype(I32)
    ts_o[...] = jnp.clip(first_j - (jt - off_j) * TME, 0.0, float(TME)).astype(I32)
    nt_o[...] = jnp.broadcast_to(total, (1, LANES)).astype(I32)
    off_rows = off_t * TME

    def pos_body(c, carry):
        off = pl.multiple_of(c * ch, ch)
        ohs = onehots(off)
        for k in range(TOP_K):
            base = jnp.sum(ohs[k] * off_rows, axis=0, keepdims=True)
            pos_o[k:k + 1, pl.ds(off, ch)] = (rank_s[k:k + 1, pl.ds(off, ch)] + base).astype(I32)
        return carry

    lax.fori_loop(0, n_tok // ch, pos_body, 0)


def _route_call(top_i, n_first):
    n = top_i.shape[1]
    assert _max_tiles(n) <= TILE_LANES and n_first % ROUTE_CHUNK == 0
    tiles = jax.ShapeDtypeStruct((1, TILE_LANES), I32)
    return pl.pallas_call(
        functools.partial(_route_kernel, n_first),
        out_shape=[jax.ShapeDtypeStruct((TOP_K, n), I32), tiles, tiles, tiles,
                   jax.ShapeDtypeStruct((1, LANES), I32)],
        scratch_shapes=[pltpu.VMEM((TOP_K, n), F32)],
        compiler_params=pltpu.CompilerParams(vmem_limit_bytes=VMEM_LIMIT), name="route",
    )(top_i)


def _sc_mesh():
    return plsc.VectorSubcoreMesh(core_axis_name="c", subcore_axis_name="s")


def _sc_worker(n_rows):
    info = plsc.get_sparse_core_info()
    n_workers = info.num_cores * info.num_subcores
    per_w = n_rows // n_workers
    assert per_w * n_workers == n_rows and per_w % SC_WINDOW == 0
    wid = lax.axis_index("s") * info.num_cores + lax.axis_index("c")
    return wid * per_w, per_w


def _dispatch_call(u2, pos_flat, p_rows):
    w = SC_WINDOW
    n = u2.shape[0]

    @functools.partial(pl.kernel, out_type=jax.ShapeDtypeStruct((p_rows, u2.shape[1]), u2.dtype), mesh=_sc_mesh(),
                       scratch_types=[pltpu.VMEM((w,), I32)] * TOP_K
                       + [pltpu.VMEM((w, u2.shape[1]), u2.dtype), pltpu.SemaphoreType.DMA])
    def k(x_hbm, p_hbm, o_hbm, i0, i1, i2, i3, rows_v, sem):
        base, per_w = _sc_worker(n)
        idx = (i0, i1, i2, i3)

        @pl.loop(0, per_w // w)
        def _(c):
            off = pl.multiple_of(base + c * w, 8)
            pltpu.sync_copy(x_hbm.at[pl.ds(off, w)], rows_v)
            for kk in range(TOP_K):
                pltpu.sync_copy(p_hbm.at[pl.ds(pl.multiple_of(kk * n + off, 8), w)], idx[kk])
            copies = [pltpu.async_copy(rows_v, o_hbm.at[idx[kk]], sem) for kk in range(TOP_K)]
            for cp in copies:
                cp.wait()

    return k(u2, pos_flat)


def _combine_call(ys, pos_flat):
    w = SC_WINDOW
    n = pos_flat.shape[0]

    @functools.partial(pl.kernel, out_type=jax.ShapeDtypeStruct((n, ys.shape[1]), ys.dtype), mesh=_sc_mesh(),
                       scratch_types=[pltpu.VMEM((w,), I32), pltpu.VMEM((w, ys.shape[1]), ys.dtype),
                                      pltpu.SemaphoreType.DMA])
    def k(y_hbm, p_hbm, o_hbm, idx_v, rows_v, sem):
        base, per_w = _sc_worker(n)

        @pl.loop(0, per_w // w)
        def _(c):
            off = pl.multiple_of(base + c * w, 8)
            pltpu.sync_copy(p_hbm.at[pl.ds(off, w)], idx_v)
            pltpu.async_copy(y_hbm.at[idx_v], rows_v, sem).wait()
            pltpu.sync_copy(rows_v, o_hbm.at[pl.ds(off, w)])

    return k(ys, pos_flat)


def _moe_kernel(l, n_tiles_max, te_ref, tv_ref, ts_ref, nt_ref, xc_ref, xl_ref, wu_hbm, bu_ref, wd_hbm, bd_ref,
                y_ref, wu_f, wd_f, wu_s, wd_s, sem, group_s):
    i = pl.program_id(0)
    nt = nt_ref[0]

    def fetch(e, slot):
        return (pltpu.make_async_copy(wu_hbm.at[l, e], wu_f.at[slot], sem.at[0, slot]),
                pltpu.make_async_copy(wd_hbm.at[l, e], wd_f.at[slot], sem.at[1, slot]))

    @pl.when(i == 0)
    def _():
        group_s[0] = 0
        for cp in fetch(te_ref[0], 0):
            cp.start()

    @pl.when(i < nt)
    def _():
        e = te_ref[i]
        prev = te_ref[jnp.maximum(i - 1, 0)]

        @pl.when((i == 0) | (e != prev))
        def _():
            group = group_s[0]
            slot = group % 2
            j = lax.while_loop(lambda j: (j < nt) & (te_ref[jnp.minimum(j, n_tiles_max - 1)] == e),
                               lambda j: j + 1, i + 1)

            @pl.when(j < nt)
            def _():
                for cp in fetch(te_ref[jnp.minimum(j, n_tiles_max - 1)], 1 - slot):
                    cp.start()

            for cp in fetch(e, slot):
                cp.wait()
            wu_s[...] = wu_f[slot].astype(BF16)
            wd_s[...] = wd_f[slot].astype(BF16)
            group_s[0] = group + 1

        valid = tv_ref[i]
        split = ts_ref[i]
        half = TME // 2
        for j in range(2):
            @pl.when(valid > j * half)
            def _():
                sl = slice(j * half, (j + 1) * half)
                rows = lax.broadcasted_iota(I32, (half, 1), 0) + j * half
                words = jnp.where(rows < split, xc_ref[sl, :], xl_ref[sl, :])
                lo, hi = _unpack_rows(jnp.where(rows < valid, words, 0))
                x = jnp.concatenate([lo.astype(BF16), hi.astype(BF16)], axis=1)
                h = _dot(x, wu_s[...]) + bu_ref[...]
                h_glu = jnp.minimum(h[:, :D_EXPERT], SWIGLU_LIMIT)
                h_lin = jnp.clip(h[:, D_EXPERT:], -SWIGLU_LIMIT, SWIGLU_LIMIT)
                act = h_glu * jax.nn.sigmoid(SWIGLU_ALPHA * h_glu) * (h_lin + 1.0)
                y_ref[sl, :] = _pack_rows(_dot(act.astype(BF16), wd_s[...]) + bd_ref[...])


def _moe_call(l, tile_e, tile_v, tile_s, n_tiles, xs_ctx, xs_lat, w_up, b_up, w_down, b_down):
    n_tiles_max = xs_ctx.shape[0] // TME

    def tile(i, te, tv, ts, nt):
        return jnp.minimum(i, nt[0] - 1)

    def wmap(i, te, tv, ts, nt):
        return (l, te[tile(i, te, tv, ts, nt)], 0, 0)

    rows = pl.BlockSpec((TME, D_PACK), lambda i, te, tv, ts, nt: (tile(i, te, tv, ts, nt), 0))
    grid_spec = pltpu.PrefetchScalarGridSpec(
        num_scalar_prefetch=4,
        grid=(n_tiles_max,),
        in_specs=[rows, rows,
                  pl.BlockSpec(memory_space=pl.ANY),
                  pl.BlockSpec((None, None, 1, 2 * D_EXPERT), wmap),
                  pl.BlockSpec(memory_space=pl.ANY),
                  pl.BlockSpec((None, None, 1, D_MODEL), wmap)],
        out_specs=rows,
        scratch_shapes=[pltpu.VMEM((2, D_MODEL, 2 * D_EXPERT), F32), pltpu.VMEM((2, D_EXPERT, D_MODEL), F32),
                        pltpu.VMEM((D_MODEL, 2 * D_EXPERT), BF16), pltpu.VMEM((D_EXPERT, D_MODEL), BF16),
                        pltpu.SemaphoreType.DMA((2, 2)), pltpu.SMEM((1,), I32)],
    )
    return pl.pallas_call(
        functools.partial(_moe_kernel, l, n_tiles_max), grid_spec=grid_spec,
        out_shape=jax.ShapeDtypeStruct((xs_ctx.shape[0], D_PACK), I32),
        compiler_params=_cparams(("arbitrary",)), name="moe",
    )(tile_e, tile_v, tile_s, n_tiles, xs_ctx, xs_lat, w_up, b_up.reshape(DEPTH, N_EXPERTS, 1, -1),
      w_down, b_down.reshape(DEPTH, N_EXPERTS, 1, -1))


def _final_kernel(yg_ref, tw_ref, x1_ref, mod_ref, ln_g_ref, ln_b_ref, *refs):
    o_ref = refs[-1]
    tw = tw_ref[...]
    halves = [_unpack_rows(yg_ref[k]) for k in range(TOP_K)]
    moe = jnp.concatenate([sum(tw[:, k:k + 1] * halves[k][j] for k in range(TOP_K)) for j in range(2)], axis=1)
    gate2 = mod_ref[0, 5:6, :]
    o_ref[...] = _ln(DEEPNORM_ALPHA * x1_ref[...] + gate2 * moe) * ln_g_ref[...] + ln_b_ref[...]


def _final_call(lat, yg, tw, x1, mod, ln_g, ln_b, after=None):
    n = x1.shape[0]
    row = lambda w: pl.BlockSpec((TM, w), lambda i: (i, 0))
    const = lambda shape: pl.BlockSpec(shape, lambda i: (0,) * len(shape))
    extra = [] if after is None else [after]
    return pl.pallas_call(
        _final_kernel,
        grid=(n // TM,),
        in_specs=[pl.BlockSpec((TOP_K, TM, D_PACK), lambda i: (0, i, 0)), row(TOP_K), row(D_MODEL),
                  _mod_spec(lat), const((1, D_MODEL)), const((1, D_MODEL))]
        + [pl.BlockSpec(memory_space=pl.ANY)] * len(extra),
        out_specs=row(D_MODEL), out_shape=jax.ShapeDtypeStruct((n, D_MODEL), F32),
        compiler_params=_cparams(("arbitrary",)), name="final_lat" if lat else "final_ctx",
    )(yg, tw, x1, mod, ln_g, ln_b, *extra)


def _rope_tables():
    half = A_HEAD_DIM // 2
    freqs = ROPE_BASE ** (-jnp.arange(0, half, 2, dtype=F32) / half)
    t = jnp.arange(DEC_SEQ, dtype=jnp.int32)
    ang_r = (t // GRID_W).astype(F32)[:, None] * freqs[None, :]
    ang_c = (t % GRID_W).astype(F32)[:, None] * freqs[None, :]
    cos64 = jnp.concatenate([jnp.cos(ang_r)] * 2 + [jnp.cos(ang_c)] * 2, axis=1)
    sin64 = jnp.concatenate([-jnp.sin(ang_r), jnp.sin(ang_r), -jnp.sin(ang_c), jnp.sin(ang_c)], axis=1)
    lat = (jnp.tile(cos64, (1, 2)), jnp.tile(sin64, (1, 2)))
    ctx = (jnp.ones((TM, LANES), F32), jnp.zeros((TM, LANES), F32))
    return ctx, lat


def _permute_in_cols(t):
    qa = [t[..., h * 64:(h + 1) * 64] for h in QA_HEAD_ORDER]
    pad = jnp.zeros(t.shape[:-1] + (D_IN_PAD - D_IN,), t.dtype)
    return jnp.concatenate(qa + [t[..., 512:2816], t[..., 2832:7440], t[..., 2816:2832], pad], axis=-1)


def _layer_weights(p):
    w_in, b_in = p["w_in"], p["b_in"]
    wpa = p["w_pa"].reshape(A_HEADS, A_HEAD_DIM, D_MODEL)[np.asarray(QA_HEAD_ORDER)].reshape(512, D_MODEL)
    row = lambda t: t.reshape(1, -1)
    return dict(
        inproj=(_permute_in_cols(w_in).astype(BF16), _permute_in_cols(b_in).reshape(1, D_IN_PAD),
                w_in[:, 2816:2832].T.astype(BF16), b_in[2816:2832].reshape(16, 1),
                row(jnp.tile(p["qn_a"], A_HEADS)), row(jnp.tile(p["kn_a"], A_KV_HEADS)),
                jnp.asarray(np.kron(np.eye(8), np.ones((64, 64))), BF16)),
        merge=(row(p["hn_b"]), wpa.astype(BF16), p["w_pb"].astype(BF16), p["w_pc"].astype(BF16),
               p["w_o"].astype(BF16), row(p["b_o"]), row(p["ln1_g"]), row(p["ln1_b"]),
               p["w_router"].T, p["b_router"].reshape(N_EXPERTS, 1)),
        lam_c=p["lam_c"], sn_c=row(p["sn_c"]), ln2=(row(p["ln2_g"]), row(p["ln2_b"])))


def _chain_mixer(l, lat, x, mod, tables, caches, w, outs):
    cos_t, sin_t = tables
    lam_init = 0.8 - 0.6 * math.exp(-0.3 * l)
    res = _inproj_call(l, lat, x, mod, cos_t, sin_t, *w["inproj"], None if outs is None else outs[:4])
    qa, kab, vab, qb, kb, vb, ob, qc, kcb, vcb, g, ifg, ifgt = res[:13]
    if lat:
        ck_a, cv_a, ck_c, cv_c, cn0, m0 = caches
    else:
        ck_a = cv_a = ck_c = cv_c = cn0 = m0 = None
    oa = _attn_call("a", lat, lam_init, qa, kab, vab, ck_a, cv_a, None, None)
    oc = _attn_call("c", lat, lam_init, qc, kcb, vcb, ck_c, cv_c, w["lam_c"], w["sn_c"])
    ml = _mlstm_call(l, lat, qb, kb, vb, ifg, ifgt, cn0, m0, None if outs is None else outs[4:6])
    hf, hb = ml[:2]
    merged = _merge_call(lat, x, mod, oa, hf, hb, ob, oc, g, *w["merge"])
    if lat:
        return merged, None
    m_l = ml[4][:, :, :, 0, 0]
    m_all = m_l[:, None] if outs is None else jnp.concatenate([outs[6], m_l[:, None]], axis=1)
    return merged, tuple(res[13:]) + (ml[2], ml[3], m_all)


def _moe_layer(l, mix_ctx, mix_lat, mod, w, experts):
    top_i = jnp.concatenate([mix_ctx[2], mix_lat[2]], axis=1)
    pos, tile_e, tile_v, tile_s, n_tiles = _route_call(top_i, N_CTX)
    p_rows = _max_tiles(N_CTX + N_LAT) * TME
    pos_flat = (pos[:, :N_CTX].reshape(TOP_K * N_CTX), pos[:, N_CTX:].reshape(TOP_K * N_LAT))
    xs = [_dispatch_call(mix[1], pf, p_rows) for mix, pf in zip((mix_ctx, mix_lat), pos_flat)]
    ys = _moe_call(l, tile_e[0], tile_v[0], tile_s[0], n_tiles[0, :1], xs[0], xs[1],
                   experts["w_up"], experts["b_up"], experts["w_down"], experts["b_down"])
    yg_ctx = _combine_call(ys, pos_flat[0]).reshape(TOP_K, N_CTX, D_PACK)
    yg_lat = _combine_call(ys, pos_flat[1]).reshape(TOP_K, N_LAT, D_PACK)
    x_ctx = _final_call(False, yg_ctx, mix_ctx[3].T, mix_ctx[0], mod, *w["ln2"])
    finish_lat = functools.partial(_final_call, True, yg_lat, mix_lat[3].T, mix_lat[0], mod, *w["ln2"])
    return x_ctx, finish_lat


def kernel(x_prompt, x_sample, cache_k_gqa, cache_v_gqa, cache_k_diff, cache_v_diff, state_mlstm_c, state_mlstm_n, state_mlstm_m, c, c_ctx, w_in, b_in, qn_a, kn_a, hn_b, lam_c, sn_c, w_pa, w_pb, w_pc, w_o, b_o, w_mod, b_mod, ln1_g, ln1_b, ln2_g, ln2_b, w_router, b_router, w_up, b_up, w_down, b_down):
    x_ctx = x_prompt.reshape(N_CTX, D_MODEL)
    x_lat = x_sample.reshape(N_LAT, D_MODEL)
    cvecs = jnp.zeros((8, D_MODEL), F32).at[0].set(c_ctx).at[1:1 + DEC_BATCH].set(c)
    mod_all = _mod_call(cvecs, w_mod, b_mod).reshape(DEPTH, 8, N_MOD, D_MODEL)[:, :N_GROUPS]
    tables_ctx, tables_lat = _rope_tables()
    params = dict(w_in=w_in, b_in=b_in, qn_a=qn_a, kn_a=kn_a, hn_b=hn_b, lam_c=lam_c, sn_c=sn_c, w_pa=w_pa,
                  w_pb=w_pb, w_pc=w_pc, w_o=w_o, b_o=b_o, ln1_g=ln1_g, ln1_b=ln1_b, ln2_g=ln2_g, ln2_b=ln2_b,
                  w_router=w_router, b_router=b_router)
    experts = dict(w_up=w_up, b_up=b_up, w_down=w_down, b_down=b_down)
    outs = None
    finish_lat = None
    for l in range(DEPTH):
        w = _layer_weights({k: v[l] for k, v in params.items()})
        ck_a = cache_k_gqa[:, l].transpose(0, 2, 1, 3).reshape(DEC_BATCH, PAST_LEN, 128).astype(BF16)
        cv_a = cache_v_gqa[:, l].transpose(0, 2, 1, 3).reshape(DEC_BATCH, PAST_LEN, 128).astype(BF16)
        ck_c = cache_k_diff[:, l].astype(BF16)
        cv_c = cache_v_diff[:, l].astype(BF16)
        cn0 = jnp.concatenate([state_mlstm_c[:, l], state_mlstm_n[:, l][..., None],
                               jnp.zeros((DEC_BATCH, 2, B_HEADS, B_HEAD_DIM, B_HEAD_DIM - 1), F32)], axis=-1)
        m0 = jnp.broadcast_to(state_mlstm_m[:, l][..., None, None], (DEC_BATCH, 2, B_HEADS, 1, LANES))
        mix_ctx, outs = _chain_mixer(l, False, x_ctx, mod_all[l], tables_ctx, None, w, outs)
        if finish_lat is not None:
            x_lat = finish_lat(after=mix_ctx[3])
        mix_lat, _ = _chain_mixer(l, True, x_lat, mod_all[l], tables_lat, (ck_a, cv_a, ck_c, cv_c, cn0, m0), w, None)
        x_ctx, finish_lat = _moe_layer(l, mix_ctx, mix_lat, mod_all[l], w, experts)
    x_lat = finish_lat()
    return (x_ctx.reshape(BATCH, SEQ, D_MODEL), x_lat.reshape(DEC_BATCH, DEC_SEQ, D_MODEL)) + tuple(outs)
```

```python
import functools
import math

import numpy as np
import jax
import jax.numpy as jnp
from jax import lax
from jax.experimental import pallas as pl
from jax.experimental.pallas import tpu as pltpu
from jax.experimental.pallas import tpu_sc as plsc

F32 = jnp.float32
BF16 = jnp.bfloat16
I32 = jnp.int32

D_MODEL = 1024
BATCH = 16
SEQ = 256
DEPTH = 2
DEC_BATCH = 2
DEC_SEQ = 4096
PAST_LEN = 512
GRID_W = 64
ROPE_BASE = 10000.0
EPS = 1e-6
A_HEADS = 8
A_KV_HEADS = 2
A_HEAD_DIM = 64
B_HEADS = 4
B_HEAD_DIM = 128
MLSTM_CHUNK = 128
C_HEADS = 4
C_HEAD_DIM = 64
C_V_DIM = 128
N_EXPERTS = 32
TOP_K = 4
D_EXPERT = 1024
SWIGLU_LIMIT = 7.0
SWIGLU_ALPHA = 1.702
N_MOD = 6
DEEPNORM_ALPHA = (2 * DEPTH) ** 0.25

N_CTX = BATCH * SEQ
N_LAT = DEC_BATCH * DEC_SEQ
N_GROUPS = 1 + DEC_BATCH

LANES = 128
TM = 256
TQ = 512
KEY_BLOCK = 1024
MERGE_ROWS = 128
TME = 512
TILE_LANES = 128
ROUTE_CHUNK = 512
D_PACK = D_MODEL // 2
SC_WINDOW = 64
VMEM_LIMIT = 56 * 1024 * 1024

QA_HEAD_ORDER = (0, 4, 1, 5, 2, 6, 3, 7)
D_IN = 7440
D_IN_PAD = 7552
P_QA, P_KA, P_VA, P_QB, P_KB, P_VB, P_OB, P_QC, P_KC, P_VC, P_G, P_IF = (
    0, 512, 640, 768, 1280, 1792, 2304, 2816, 3328, 3840, 4352, 7424)


def _max_tiles(n_tok):
    return (n_tok * TOP_K) // TME + N_EXPERTS


def _cparams(sem):
    return pltpu.CompilerParams(dimension_semantics=sem, vmem_limit_bytes=VMEM_LIMIT)


def _split3(a):
    a1 = a.astype(BF16)
    r = a - a1.astype(F32)
    a2 = r.astype(BF16)
    a3 = (r - a2.astype(F32)).astype(BF16)
    return a1, a2, a3


def _dot(a, b, dims=None):
    if dims is None:
        return jnp.dot(a, b, preferred_element_type=F32)
    return lax.dot_general(a, b, (dims, ((), ())), preferred_element_type=F32)


def _dot_exact_rhs01(a, m, dims=None):
    return sum(_dot(p, m, dims) for p in _split3(a))


def _dot_exact_lhs01(m, a, dims=None):
    return sum(_dot(m, p, dims) for p in _split3(a))


def _pack_rows(x):
    xb = x.astype(BF16).astype(F32)
    lo = pltpu.bitcast(xb[:, :D_PACK], I32)
    lo = lax.shift_right_logical(lo, jnp.full_like(lo, 16))
    hi = pltpu.bitcast(xb[:, D_PACK:], I32) & jnp.int32(-65536)
    return lo | hi


def _unpack_rows(w):
    return (pltpu.bitcast(lax.shift_left(w, jnp.full_like(w, 16)), F32),
            pltpu.bitcast(w & jnp.int32(-65536), F32))


def _ln(x):
    mu = jnp.mean(x, axis=-1, keepdims=True)
    xc = x - mu
    var = jnp.mean(xc * xc, axis=-1, keepdims=True)
    return xc * lax.rsqrt(var + EPS)


def _log_sigmoid(x):
    return jnp.minimum(x, 0.0) - jnp.log(1.0 + jnp.exp(-jnp.abs(x)))


def _rope(x, cos, sin):
    lane = lax.broadcasted_iota(I32, (1, LANES), 1)
    first = (lane % 32) < 16
    outs = []
    for j in range(x.shape[1] // LANES):
        xj = x[:, j * LANES:(j + 1) * LANES]
        partner = jnp.where(first, pltpu.roll(xj, LANES - 16, 1), pltpu.roll(xj, 16, 1))
        outs.append(xj * cos + partner * sin)
    return outs[0] if len(outs) == 1 else jnp.concatenate(outs, axis=1)


def _mod_spec(lat):
    if lat:
        return pl.BlockSpec((1, N_MOD, D_MODEL), lambda i: (1 + i * TM // DEC_SEQ, 0, 0))
    return pl.BlockSpec((1, N_MOD, D_MODEL), lambda i: (0, 0, 0))


def _mod_kernel(c_ref, w_ref, b_ref, o_ref):
    c = c_ref[...]
    s = c * jax.nn.sigmoid(c)
    o_ref[...] = _dot(s.astype(BF16), w_ref[...].astype(BF16)) + b_ref[...]


def _mod_call(cvecs, w_mod, b_mod):
    tn = 1536
    n = N_MOD * D_MODEL
    return pl.pallas_call(
        _mod_kernel,
        grid=(DEPTH, n // tn),
        in_specs=[pl.BlockSpec((8, D_MODEL), lambda l, j: (0, 0)),
                  pl.BlockSpec((None, D_MODEL, tn), lambda l, j: (l, 0, j)),
                  pl.BlockSpec((None, 1, tn), lambda l, j: (l, 0, j))],
        out_specs=pl.BlockSpec((None, 8, tn), lambda l, j: (l, 0, j)),
        out_shape=jax.ShapeDtypeStruct((DEPTH, 8, n), F32),
        compiler_params=_cparams(("arbitrary", "arbitrary")),
        name="mod",
    )(cvecs, w_mod, b_mod.reshape(DEPTH, 1, n))


def _inproj_kernel(with_cache, n_alias, x_ref, mod_ref, cos_ref, sin_ref, w_ref, b_ref, wgt_ref, bgt_ref,
                   gq_ref, gk_ref, gmat_ref, *refs):
    outs = refs[n_alias:]
    qa_o, kab_o, vab_o, qb_o, kb_o, vb_o, ob_o, qc_o, kcb_o, vcb_o, g_o, ifg_o, ifgt_o = outs[:13]
    x = x_ref[...]
    shift = mod_ref[0, 0:1, :]
    scale = mod_ref[0, 1:2, :]
    ub = (_ln(x) * (1.0 + scale) + shift).astype(BF16)
    cos = cos_ref[...]
    sin = sin_ref[...]

    def proj(lo, n):
        return _dot(ub, w_ref[lo:lo + n, :], ((1,), (1,))) + b_ref[:, lo:lo + n]

    def headnorm(z, gmat, gain):
        ss = _dot((z * z).astype(BF16), gmat)
        return z * lax.rsqrt(ss * (1.0 / A_HEAD_DIM) + EPS) * gain

    segs = [(P_QA, 512), (P_KA, 128), (P_VA, 128), (P_QB, 512), (P_KB, 512), (P_VB, 512), (P_OB, 512),
            (P_QC, 512), (P_KC, 512), (P_VC, 512), (P_G, 1024), (P_G + 1024, 1024), (P_G + 2048, 1024),
            (P_IF, LANES)]
    keep = {}

    def epilogue(k, z):
        if k == 0:
            z = headnorm(z, gmat_ref[...], gq_ref[...])
            qa_o[...] = (_rope(z, cos, sin) * 0.125).astype(BF16)
        elif k == 1:
            keep["k"] = _rope(headnorm(z, gmat_ref[0:128, 0:128], gk_ref[...]), cos, sin)
            kab_o[...] = keep["k"].astype(BF16)
        elif k == 2:
            keep["v"] = z
            vab_o[...] = z.astype(BF16)
        elif k == 3:
            qb_o[...] = z.astype(BF16)
        elif k == 4:
            kb_o[...] = (z * (B_HEAD_DIM ** -0.5)).astype(BF16)
        elif k == 5:
            vb_o[...] = z.astype(BF16)
        elif k == 6:
            ob_o[...] = z.astype(BF16)
        elif k == 7:
            qc_o[...] = (_rope(z, cos, sin) * 0.125).astype(BF16)
        elif k == 8:
            keep["kc"] = _rope(z, cos, sin)
            kcb_o[...] = keep["kc"].astype(BF16)
        elif k == 9:
            keep["vc"] = z
            vcb_o[...] = z.astype(BF16)
        elif k in (10, 11, 12):
            g_o[:, (k - 10) * 1024:(k - 9) * 1024] = jax.nn.sigmoid(z).astype(BF16)
        else:
            lane = lax.broadcasted_iota(I32, (1, LANES), 1)
            ifg_o[...] = jnp.where(lane >= 8, _log_sigmoid(z), z)

    z_prev = proj(*segs[0])
    for k in range(1, len(segs)):
        z_next = proj(*segs[k])
        epilogue(k - 1, z_prev)
        z_prev = z_next
    zt = _dot(wgt_ref[...], ub, ((1,), (1,))) + bgt_ref[...]
    epilogue(len(segs) - 1, z_prev)
    row = lax.broadcasted_iota(I32, (16, 1), 0)
    ifgt_o[...] = jnp.where(row >= 8, _log_sigmoid(zt), zt)

    if with_cache:
        nka_o, nva_o, nkc_o, nvc_o = outs[13:]
        for h in range(A_KV_HEADS):
            nka_o[h] = keep["k"][:, h * A_HEAD_DIM:(h + 1) * A_HEAD_DIM]
            nva_o[h] = keep["v"][:, h * A_HEAD_DIM:(h + 1) * A_HEAD_DIM]
        for h in range(C_HEADS):
            nkc_o[h] = keep["kc"][:, h * LANES:(h + 1) * LANES]
            nvc_o[h] = keep["vc"][:, h * LANES:(h + 1) * LANES]


def _inproj_call(l, lat, x, mod, cos_t, sin_t, w_p, b_p, w_gt, b_gt, gq, gk, gmat, cache_outs):
    n = x.shape[0]
    row = lambda w: pl.BlockSpec((TM, w), lambda i: (i, 0))
    const = lambda shape: pl.BlockSpec(shape, lambda i: (0,) * len(shape), pipeline_mode=pl.Buffered(1))
    sds = lambda w, dt: jax.ShapeDtypeStruct((n, w), dt)
    out_defs = [(512, BF16), (128, BF16), (128, BF16), (512, BF16), (512, BF16), (512, BF16), (512, BF16),
                (512, BF16), (512, BF16), (512, BF16), (3072, BF16), (LANES, F32)]
    out_specs = [row(w) for w, _ in out_defs] + [pl.BlockSpec((16, TM), lambda i: (0, i))]
    out_shape = [sds(w, dt) for w, dt in out_defs] + [jax.ShapeDtypeStruct((16, n), F32)]
    alias_in, aliases = [], {}
    if lat:
        tbl = pl.BlockSpec((TM, LANES), lambda i: (i % (DEC_SEQ // TM), 0))
    else:
        assert TM == SEQ
        tbl = pl.BlockSpec((TM, LANES), lambda i: (0, 0))
        cache_defs = [(A_KV_HEADS, A_HEAD_DIM), (A_KV_HEADS, A_HEAD_DIM), (C_HEADS, LANES), (C_HEADS, LANES)]
        out_specs += [pl.BlockSpec((None, None, nh, SEQ, d), lambda i: (i, l, 0, 0, 0)) for nh, d in cache_defs]
        out_shape += [jax.ShapeDtypeStruct((BATCH, DEPTH, nh, SEQ, d), F32) for nh, d in cache_defs]
        if cache_outs is not None:
            alias_in = list(cache_outs)
            aliases = {11 + j: len(out_defs) + 1 + j for j in range(4)}
    return pl.pallas_call(
        functools.partial(_inproj_kernel, not lat, len(alias_in)),
        grid=(n // TM,),
        in_specs=[row(D_MODEL), _mod_spec(lat), tbl, tbl,
                  const((D_IN_PAD, D_MODEL)), const((1, D_IN_PAD)),
                  const((16, D_MODEL)), const((16, 1)),
                  const((1, 512)), const((1, 128)), const((512, 512))]
        + [pl.BlockSpec(memory_space=pl.ANY)] * len(alias_in),
        out_specs=out_specs, out_shape=out_shape, input_output_aliases=aliases,
        compiler_params=_cparams(("arbitrary",)),
        name="inproj_lat" if lat else "inproj_ctx",
    )(x, mod, cos_t, sin_t, w_p, b_p, w_gt, b_gt, gq, gk, gmat, *alias_in)


def _attn_kernel(mode, nseg, lam_init, *refs):
    q_ref = refs[0]
    kv = refs[1:1 + 2 * nseg]
    rest = refs[1 + 2 * nseg:]
    o_ref = rest[-1]
    tq = q_ref.shape[0]
    lane = lax.broadcasted_iota(I32, (1, LANES), 1)
    if mode == "c":
        lam_ref, sn_ref = rest[0], rest[1]
        lp = lam_ref[...]
        lam = (jnp.exp(jnp.sum(lp[0:1] * lp[1:2], axis=-1, keepdims=True))
               - jnp.exp(jnp.sum(lp[2:3] * lp[3:4], axis=-1, keepdims=True)) + lam_init)
    for blk in range(q_ref.shape[1] // LANES):
        qs = slice(blk * LANES, (blk + 1) * LANES)
        ks = qs if kv[0].shape[1] > LANES else slice(0, LANES)
        q = q_ref[:, qs]
        zero = jnp.zeros_like(q)
        qq = jnp.concatenate([jnp.where(lane < 64, q, zero), jnp.where(lane >= 64, q, zero)], axis=0)
        m = l = acc = None
        for j in range(nseg):
            k_ref, v_ref = kv[2 * j], kv[2 * j + 1]
            for off in range(0, k_ref.shape[0], KEY_BLOCK):
                n = min(KEY_BLOCK, k_ref.shape[0] - off)
                s = _dot(qq, k_ref[off:off + n, ks], ((1,), (1,)))
                bm = jnp.max(s, axis=-1, keepdims=True)
                if m is None:
                    m = bm
                    e = jnp.exp(s - m)
                    l = jnp.sum(e, axis=-1, keepdims=True)
                    acc = _dot(e.astype(BF16), v_ref[off:off + n, ks])
                else:
                    m_new = jnp.maximum(m, bm)
                    alpha = jnp.exp(m - m_new)
                    e = jnp.exp(s - m_new)
                    l = alpha * l + jnp.sum(e, axis=-1, keepdims=True)
                    acc = alpha * acc + _dot(e.astype(BF16), v_ref[off:off + n, ks])
                    m = m_new
        o = acc * (1.0 / l)
        if mode == "a":
            out = jnp.where(lane < 64, o[:tq], o[tq:])
        else:
            o = o[:tq] - lam * o[tq:]
            ms = jnp.mean(o * o, axis=-1, keepdims=True)
            out = o * lax.rsqrt(ms + EPS) * sn_ref[...] * (1.0 - lam_init)
        o_ref[:, qs] = out.astype(o_ref.dtype)


def _attn_call(mode, lat, lam_init, q_all, k_all, v_all, k_cache, v_cache, lam_c, sn_c):
    wide = k_all.shape[1] == 512
    kcol = (lambda h: h) if wide else (lambda h: 0)
    if lat:
        nq = DEC_SEQ // TQ
        grid = (DEC_BATCH, nq, 4)
        q_spec = pl.BlockSpec((TQ, LANES), lambda b, i, h: (b * nq + i, h))
        kn_spec = pl.BlockSpec((DEC_SEQ, LANES), lambda b, i, h: (b, kcol(h)))
        if wide:
            kc_spec = pl.BlockSpec((None, None, PAST_LEN, LANES), lambda b, i, h: (b, h, 0, 0))
        else:
            kc_spec = pl.BlockSpec((None, PAST_LEN, LANES), lambda b, i, h: (b, 0, 0))
        in_specs = [q_spec, kc_spec, kc_spec, kn_spec, kn_spec]
        args = [q_all, k_cache, v_cache, k_all, v_all]
        nseg = 2
        o_spec = q_spec
        sem = ("arbitrary", "arbitrary", "arbitrary")
        zero_map = lambda b, i, h: (0, 0)
    else:
        grid = (BATCH,)
        q_spec = pl.BlockSpec((SEQ, 512), lambda b: (b, 0))
        k_spec = pl.BlockSpec((SEQ, k_all.shape[1]), lambda b: (b, 0))
        in_specs = [q_spec, k_spec, k_spec]
        args = [q_all, k_all, v_all]
        nseg = 1
        o_spec = q_spec
        sem = ("arbitrary",)
        zero_map = lambda b: (0, 0)
    if mode == "c":
        in_specs += [pl.BlockSpec((4, C_HEAD_DIM), zero_map), pl.BlockSpec((1, C_V_DIM), zero_map)]
        args += [lam_c, sn_c]
    return pl.pallas_call(
        functools.partial(_attn_kernel, mode, nseg, lam_init),
        grid=grid, in_specs=in_specs, out_specs=o_spec,
        out_shape=jax.ShapeDtypeStruct((q_all.shape[0], 512), BF16),
        compiler_params=_cparams(sem), name=f"attn_{mode}_{'lat' if lat else 'ctx'}",
    )(*args)


def _mlstm_kernel(has_init, want_state, n_alias, *refs):
    tok = (refs[0:5], refs[5:10])
    pos = 10
    if has_init:
        cn0_ref, m0_ref = refs[pos:pos + 2]
        pos += 2
    pos += n_alias
    h_refs = refs[pos:pos + 2]
    pos += 2
    if want_state:
        c_o, n_o, m_o = refs[pos:pos + 3]
        pos += 3
    cn_s, m_s = refs[pos:pos + 2]

    c = pl.program_id(1)
    last = pl.num_programs(1) - 1
    L = MLSTM_CHUNK

    @pl.when(c == 0)
    def _():
        if has_init:
            cn_s[...] = cn0_ref[...]
            m_s[...] = m0_ref[...]
        else:
            cn_s[...] = jnp.zeros_like(cn_s)
            m_s[...] = jnp.zeros_like(m_s)

    row = lax.broadcasted_iota(I32, (L, L), 0)
    col = lax.broadcasted_iota(I32, (L, L), 1)
    lane = lax.broadcasted_iota(I32, (1, LANES), 1)
    ones_col = jnp.where(lane == 0, 1.0, 0.0).astype(BF16)

    chains = []
    for d in range(2):
        q_ref, k_ref, v_ref, g_ref, gt_ref = tok[d]
        keep = (col <= row) if d == 0 else (col >= row)
        m_c = keep.astype(BF16)
        m_r = ((row <= col) if d == 0 else (row >= col)).astype(BF16)
        g = g_ref[...]
        gt = gt_ref[...]
        bc = _dot_exact_lhs01(m_c, g)
        br = _dot_exact_rhs01(gt, m_r)
        for h in range(B_HEADS):
            sl = slice(h * L, (h + 1) * L)
            gi, gf = 4 * d + h, 8 + 4 * d + h
            ch = dict(d=d, h=h, sl=sl, keep=keep, q=q_ref[:, sl], k=k_ref[:, sl], v=v_ref[:, sl],
                      i_col=g[:, gi:gi + 1], f_col=g[:, gf:gf + 1], b_col=bc[:, gf:gf + 1],
                      r_row=gt[gi:gi + 1, :] - br[gf:gf + 1, :],
                      m_prev=m_s[d, h][:, 0:1], cn_prev=cn_s[d, h])
            chains.append(ch)
    for ch in chains:
        ch["s"] = _dot(ch["q"], ch["k"], ((1,), (1,)))
        ch["qc"] = _dot(ch["q"], ch["cn_prev"].astype(BF16))
    for ch in chains:
        ch["total"] = jnp.sum(ch["f_col"], axis=0, keepdims=True)
        ch["inter"] = ch["b_col"] + ch["m_prev"]
        ch["v_aug"] = jnp.concatenate([ch["v"], jnp.broadcast_to(ones_col, (L, LANES))], axis=1)
    for ch in chains:
        ch["dmat"] = jnp.where(ch["keep"], ch["b_col"] + ch["r_row"], -jnp.inf)
    for ch in chains:
        ch["dmax"] = jnp.max(ch["dmat"], axis=-1, keepdims=True)
    for ch in chains:
        ch["m_row"] = jnp.maximum(ch["inter"], ch["dmax"])
    for ch in chains:
        ch["sc"] = (ch["s"] * jnp.exp(ch["dmat"] - ch["m_row"])).astype(BF16)
    for ch in chains:
        ch["a"] = _dot(ch["sc"], ch["v_aug"])
    for ch in chains:
        ch["nd"] = jnp.exp(ch["inter"] - ch["m_row"]) * ch["qc"] + ch["a"]
    for ch in chains:
        ch["scale"] = 1.0 / jnp.maximum(jnp.abs(ch["nd"][:, L:L + 1]), jnp.exp(-ch["m_row"]))
    for ch in chains:
        h_refs[ch["d"]][:, ch["sl"]] = ch["nd"][:, :L] * ch["scale"]
    for ch in chains:
        ch["wel"] = ch["total"] - ch["b_col"] + ch["i_col"]
        ch["m_new"] = jnp.maximum(ch["total"] + ch["m_prev"], jnp.max(ch["wel"], axis=0, keepdims=True))
    for ch in chains:
        ch["kw"] = (jnp.exp(ch["wel"] - ch["m_new"]) * ch["k"].astype(F32)).astype(BF16)
    for ch in chains:
        ch["u"] = _dot(ch["kw"], ch["v_aug"], ((0,), (0,)))
    for ch in chains:
        cn_s[ch["d"], ch["h"]] = jnp.exp(ch["total"] + ch["m_prev"] - ch["m_new"]) * ch["cn_prev"] + ch["u"]
        m_s[ch["d"], ch["h"]] = jnp.broadcast_to(ch["m_new"], (1, LANES))

    if want_state:
        @pl.when(c == last)
        def _():
            pick = ((lax.broadcasted_iota(I32, (8, 2 * L), 1) == L)
                    & (lax.broadcasted_iota(I32, (8, 2 * L), 0) == 0)).astype(BF16)
            for d in range(2):
                for h in range(B_HEADS):
                    cn = cn_s[d, h]
                    c_o[d, h] = cn[:, :L]
                    n_o[d, h:h + 1, :] = _dot_exact_lhs01(pick, cn, ((1,), (1,)))[0:1]
            m_o[...] = m_s[...]


def _mlstm_call(l, lat, q_all, k_all, v_all, g_all, gt_all, cn0, m0, state_bufs):
    L = MLSTM_CHUNK
    n = q_all.shape[0]
    nb, nc = (DEC_BATCH, DEC_SEQ // L) if lat else (BATCH, SEQ // L)
    chunk = (lambda b, c: b * nc + c, lambda b, c: b * nc + (nc - 1 - c))

    in_specs, args = [], []
    for d in range(2):
        tok = lambda w, d=d: pl.BlockSpec((L, w), lambda b, c: (chunk[d](b, c), 0))
        in_specs += [tok(512), tok(512), tok(512), tok(LANES),
                     pl.BlockSpec((16, L), lambda b, c, d=d: (0, chunk[d](b, c)))]
        args += [q_all, k_all, v_all, g_all, gt_all]
    if lat:
        in_specs += [pl.BlockSpec((None, 2, B_HEADS, L, 2 * L), lambda b, c: (b, 0, 0, 0, 0)),
                     pl.BlockSpec((None, 2, B_HEADS, 1, LANES), lambda b, c: (b, 0, 0, 0, 0))]
        args += [cn0, m0]
    aliases = {}
    alias_in = list(state_bufs) if state_bufs is not None else []
    if alias_in:
        aliases = {len(args): 2, len(args) + 1: 3}
    in_specs += [pl.BlockSpec(memory_space=pl.ANY)] * len(alias_in)
    args += alias_in
    out_specs = [pl.BlockSpec((L, 512), lambda b, c, d=d: (chunk[d](b, c), 0)) for d in range(2)]
    out_shape = [jax.ShapeDtypeStruct((n, 512), F32)] * 2
    if not lat:
        out_specs += [pl.BlockSpec((None, None, 2, B_HEADS, L, L), lambda b, c: (b, l, 0, 0, 0, 0)),
                      pl.BlockSpec((None, None, 2, B_HEADS, L), lambda b, c: (b, l, 0, 0, 0)),
                      pl.BlockSpec((None, 2, B_HEADS, 1, LANES), lambda b, c: (b, 0, 0, 0, 0))]
        out_shape += [jax.ShapeDtypeStruct((BATCH, DEPTH, 2, B_HEADS, L, L), F32),
                      jax.ShapeDtypeStruct((BATCH, DEPTH, 2, B_HEADS, L), F32),
                      jax.ShapeDtypeStruct((nb, 2, B_HEADS, 1, LANES), F32)]
    return pl.pallas_call(
        functools.partial(_mlstm_kernel, lat, not lat, len(alias_in)),
        grid=(nb, nc), in_specs=in_specs, out_specs=out_specs, out_shape=out_shape,
        input_output_aliases=aliases,
        scratch_shapes=[pltpu.VMEM((2, B_HEADS, L, 2 * L), F32), pltpu.VMEM((2, B_HEADS, 1, LANES), F32)],
        compiler_params=_cparams(("arbitrary", "arbitrary")), name="mlstm_lat" if lat else "mlstm_ctx",
    )(*args)


def _merge_kernel(x_ref, mod_ref, oa_ref, hf_ref, hb_ref, ob_ref, oc_ref, g_ref, hn_ref,
                  wpa_ref, wpb_ref, wpc_ref, wo_ref, bo_ref, ln_g_ref, ln_b_ref, wr_ref, br_ref,
                  x1_o, u2_o, ti_o, tw_o):
    gate1 = mod_ref[0, 2:3, :]
    shift2 = mod_ref[0, 3:4, :]
    scale2 = mod_ref[0, 4:5, :]
    wr = _split3(wr_ref[...])
    eio = lax.broadcasted_iota(I32, (N_EXPERTS, 1), 0)
    nt = ((1,), (1,))
    cs = [dict(rs=slice(r, r + MERGE_ROWS)) for r in range(0, TM, MERGE_ROWS)]
    for c in cs:
        c["pa"] = _dot(oa_ref[c["rs"], :], wpa_ref[...])
        c["pc"] = _dot(oc_ref[c["rs"], :], wpc_ref[...])
    for c in cs:
        hm = hf_ref[c["rs"], :] + hb_ref[c["rs"], :]
        parts = []
        for h in range(B_HEADS):
            hh = hm[:, h * LANES:(h + 1) * LANES]
            parts.append(hh * lax.rsqrt(jnp.mean(hh * hh, axis=-1, keepdims=True) + EPS))
        c["out_b"] = (jnp.concatenate(parts, axis=1) * hn_ref[...]
                      * jax.nn.sigmoid(ob_ref[c["rs"], :].astype(F32))).astype(BF16)
    for c in cs:
        c["pb"] = _dot(c["out_b"], wpb_ref[...])
    for c in cs:
        rs = c["rs"]
        c["merged"] = (g_ref[rs, 0:1024].astype(F32) * c["pa"] + g_ref[rs, 1024:2048].astype(F32) * c["pb"]
                       + g_ref[rs, 2048:3072].astype(F32) * c["pc"]).astype(BF16)
    for c in cs:
        c["y"] = _dot(c["merged"], wo_ref[...]) + bo_ref[...]
    for c in cs:
        c["x1"] = _ln(DEEPNORM_ALPHA * x_ref[c["rs"], :] + gate1 * c["y"]) * ln_g_ref[...] + ln_b_ref[...]
        x1_o[c["rs"], :] = c["x1"]
    for c in cs:
        c["u2"] = _ln(c["x1"]) * (1.0 + scale2) + shift2
        u2_o[c["rs"], :] = _pack_rows(c["u2"])
    for c in cs:
        up = _split3(c["u2"])
        c["cur"] = (_dot(wr[0], up[0], nt) + _dot(wr[0], up[1], nt) + _dot(wr[1], up[0], nt)
                    + _dot(wr[0], up[2], nt) + _dot(wr[2], up[0], nt) + _dot(wr[1], up[1], nt)) + br_ref[...]
        c["vals"], c["idxs"] = [], []
    for _ in range(TOP_K):
        for c in cs:
            mx = jnp.max(c["cur"], axis=0, keepdims=True)
            ix = jnp.min(jnp.where(c["cur"] == mx, eio, N_EXPERTS), axis=0, keepdims=True)
            c["vals"].append(mx)
            c["idxs"].append(ix)
            c["cur"] = jnp.where(eio == ix, -jnp.inf, c["cur"])
    for c in cs:
        ex = [jnp.exp(v - c["vals"][0]) for v in c["vals"]]
        inv = 1.0 / sum(ex)
        ti_o[:, c["rs"]] = jnp.concatenate(c["idxs"], axis=0)
        tw_o[:, c["rs"]] = jnp.concatenate([e * inv for e in ex], axis=0)


def _merge_call(lat, x, mod, out_a, hf, hb, ob, out_c, g, hn, wpa, wpb, wpc, wo, bo, ln_g, ln_b, wr_t, br):
    n = x.shape[0]
    row = lambda w: pl.BlockSpec((TM, w), lambda i: (i, 0))
    const = lambda shape: pl.BlockSpec(shape, lambda i: (0,) * len(shape), pipeline_mode=pl.Buffered(1))
    return pl.pallas_call(
        _merge_kernel,
        grid=(n // TM,),
        in_specs=[row(D_MODEL), _mod_spec(lat),
                  row(512), row(512), row(512), row(512), row(512), row(3072),
                  const((1, 512)), const((512, D_MODEL)), const((512, D_MODEL)), const((512, D_MODEL)),
                  const((D_MODEL, D_MODEL)), const((1, D_MODEL)), const((1, D_MODEL)), const((1, D_MODEL)),
                  const((N_EXPERTS, D_MODEL)), const((N_EXPERTS, 1))],
        out_specs=[row(D_MODEL), row(D_PACK),
                   pl.BlockSpec((TOP_K, TM), lambda i: (0, i)), pl.BlockSpec((TOP_K, TM), lambda i: (0, i))],
        out_shape=[jax.ShapeDtypeStruct((n, D_MODEL), F32), jax.ShapeDtypeStruct((n, D_PACK), I32),
                   jax.ShapeDtypeStruct((TOP_K, n), I32), jax.ShapeDtypeStruct((TOP_K, n), F32)],
        compiler_params=_cparams(("arbitrary",)), name="merge_lat" if lat else "merge_ctx",
    )(x, mod, out_a, hf, hb, ob, out_c, g, hn, wpa, wpb, wpc, wo, bo, ln_g, ln_b, wr_t, br)


def _route_kernel(n_first, ti_ref, pos_o, te_o, tv_o, ts_o, nt_o, rank_s):
    ch = ROUTE_CHUNK
    n_tok = ti_ref.shape[1]
    eio = lax.broadcasted_iota(I32, (N_EXPERTS, 1), 0)
    tri = (lax.broadcasted_iota(I32, (ch, ch), 0) <= lax.broadcasted_iota(I32, (ch, ch), 1)).astype(BF16)

    def onehots(off):
        ti = ti_ref[:, pl.ds(off, ch)]
        return [(eio == ti[k:k + 1, :]).astype(F32) for k in range(TOP_K)]

    def count_body(c, carry):
        off = pl.multiple_of(c * ch, ch)
        ohs = onehots(off)
        oh = ohs[0] + ohs[1] + ohs[2] + ohs[3]
        incl = _dot(oh.astype(BF16), tri)
        excl = incl - oh + carry
        for k in range(TOP_K):
            rank_s[k:k + 1, pl.ds(off, ch)] = jnp.sum(ohs[k] * excl, axis=0, keepdims=True)
        return carry + incl[:, ch - 1:ch]

    counts_first = lax.fori_loop(0, n_first // ch, count_body, jnp.zeros((N_EXPERTS, 1), F32))
    counts = lax.fori_loop(n_first // ch, n_tok // ch, count_body, counts_first)
    ntile_e = jnp.floor((counts + (TME - 1)) * (1.0 / TME))
    lower = (lax.broadcasted_iota(I32, (N_EXPERTS, N_EXPERTS), 1)
             < lax.broadcasted_iota(I32, (N_EXPERTS, N_EXPERTS), 0)).astype(BF16)
    off_t = _dot(lower, jnp.broadcast_to(ntile_e, (N_EXPERTS, LANES)).astype(BF16))[:, 0:1]
    end_t = off_t + ntile_e
    total = jnp.sum(ntile_e, axis=0, keepdims=True)
    jt = lax.broadcasted_iota(I32, (1, TILE_LANES), 1).astype(F32)
    te = jnp.minimum(jnp.sum((end_t <= jt).astype(F32), axis=0, keepdims=True), N_EXPERTS - 1.0)
    sel = (eio.astype(F32) == te).astype(F32)
    cnt_j = jnp.sum(sel * counts, axis=0, keepdims=True)
    off_j = jnp.sum(sel * off_t, axis=0, keepdims=True)
    valid = jnp.clip(cnt_j - (jt - off_j) * TME, 0.0, float(TME))
    first_j = jnp.sum(sel * counts_first, axis=0, keepdims=True)
    te_o[...] = te.astype(I32)
    tv_o[...] = jnp.where(jt < total, valid, 0.0).astype(I32)
    ts_o[...] = jnp.clip(first_j - (jt - off_j) * TME, 0.0, float(TME)).astype(I32)
    nt_o[...] = jnp.broadcast_to(total, (1, LANES)).astype(I32)
    off_rows = off_t * TME

    def pos_body(c, carry):
        off = pl.multiple_of(c * ch, ch)
        ohs = onehots(off)
        for k in range(TOP_K):
            base = jnp.sum(ohs[k] * off_rows, axis=0, keepdims=True)
            pos_o[k:k + 1, pl.ds(off, ch)] = (rank_s[k:k + 1, pl.ds(off, ch)] + base).astype(I32)
        return carry

    lax.fori_loop(0, n_tok // ch, pos_body, 0)


def _route_call(top_i, n_first):
    n = top_i.shape[1]
    assert _max_tiles(n) <= TILE_LANES and n_first % ROUTE_CHUNK == 0
    tiles = jax.ShapeDtypeStruct((1, TILE_LANES), I32)
    return pl.pallas_call(
        functools.partial(_route_kernel, n_first),
        out_shape=[jax.ShapeDtypeStruct((TOP_K, n), I32), tiles, tiles, tiles,
                   jax.ShapeDtypeStruct((1, LANES), I32)],
        scratch_shapes=[pltpu.VMEM((TOP_K, n), F32)],
        compiler_params=pltpu.CompilerParams(vmem_limit_bytes=VMEM_LIMIT), name="route",
    )(top_i)


def _sc_mesh():
    return plsc.VectorSubcoreMesh(core_axis_name="c", subcore_axis_name="s")


def _sc_worker(n_rows):
    info = plsc.get_sparse_core_info()
    n_workers = info.num_cores * info.num_subcores
    per_w = n_rows // n_workers
    assert per_w * n_workers == n_rows and per_w % SC_WINDOW == 0
    wid = lax.axis_index("s") * info.num_cores + lax.axis_index("c")
    return wid * per_w, per_w


def _dispatch_call(u2, pos_flat, p_rows):
    w = SC_WINDOW
    n = u2.shape[0]

    @functools.partial(pl.kernel, out_type=jax.ShapeDtypeStruct((p_rows, u2.shape[1]), u2.dtype), mesh=_sc_mesh(),
                       scratch_types=[pltpu.VMEM((w,), I32)] * (2 * TOP_K)
                       + [pltpu.VMEM((w, u2.shape[1]), u2.dtype)] * 2 + [pltpu.SemaphoreType.DMA] * 4)
    def k(x_hbm, p_hbm, o_hbm, *scratch):
        idx = (scratch[0:TOP_K], scratch[TOP_K:2 * TOP_K])
        rows = scratch[2 * TOP_K:2 * TOP_K + 2]
        sem_load = scratch[2 * TOP_K + 2:2 * TOP_K + 4]
        sem_scat = scratch[2 * TOP_K + 4:2 * TOP_K + 6]
        base, per_w = _sc_worker(n)
        n_chunks = per_w // w

        def start_loads(c, s):
            off = pl.multiple_of(base + c * w, 8)
            cps = [pltpu.async_copy(x_hbm.at[pl.ds(off, w)], rows[s], sem_load[s])]
            cps += [pltpu.async_copy(p_hbm.at[pl.ds(pl.multiple_of(kk * n + off, 8), w)], idx[s][kk], sem_load[s])
                    for kk in range(TOP_K)]
            return cps

        loads = start_loads(0, 0)
        scatters = [None, None]
        for c in range(n_chunks):
            s = c % 2
            for cp in loads:
                cp.wait()
            if c + 1 < n_chunks:
                if scatters[1 - s] is not None:
                    for cp in scatters[1 - s]:
                        cp.wait()
                    scatters[1 - s] = None
                loads = start_loads(c + 1, 1 - s)
            scatters[s] = [pltpu.async_copy(rows[s], o_hbm.at[idx[s][kk]], sem_scat[s]) for kk in range(TOP_K)]
        for group in scatters:
            if group is not None:
                for cp in group:
                    cp.wait()

    return k(u2, pos_flat)


def _combine_call(ys, pos_flat):
    w = SC_WINDOW
    n = pos_flat.shape[0]

    @functools.partial(pl.kernel, out_type=jax.ShapeDtypeStruct((n, ys.shape[1]), ys.dtype), mesh=_sc_mesh(),
                       scratch_types=[pltpu.VMEM((w,), I32)] * 2 + [pltpu.VMEM((w, ys.shape[1]), ys.dtype)] * 2
                       + [pltpu.SemaphoreType.DMA] * 4)
    def k(y_hbm, p_hbm, o_hbm, idx0, idx1, rows0, rows1, sg0, sg1, ss0, ss1):
        idx, rows, sem_gather, sem_store = (idx0, idx1), (rows0, rows1), (sg0, sg1), (ss0, ss1)
        base, per_w = _sc_worker(n)
        n_chunks = per_w // w

        def offset(c):
            return pl.multiple_of(base + c * w, 8)

        pltpu.sync_copy(p_hbm.at[pl.ds(offset(0), w)], idx[0])
        gather = pltpu.async_copy(y_hbm.at[idx[0]], rows[0], sem_gather[0])
        stores = [None, None]
        for c in range(n_chunks):
            s = c % 2
            if c + 1 < n_chunks:
                pltpu.sync_copy(p_hbm.at[pl.ds(offset(c + 1), w)], idx[1 - s])
            gather.wait()
            if c + 1 < n_chunks:
                if stores[1 - s] is not None:
                    stores[1 - s].wait()
                    stores[1 - s] = None
                gather = pltpu.async_copy(y_hbm.at[idx[1 - s]], rows[1 - s], sem_gather[1 - s])
            stores[s] = pltpu.async_copy(rows[s], o_hbm.at[pl.ds(offset(c), w)], sem_store[s])
        for st in stores:
            if st is not None:
                st.wait()

    return k(ys, pos_flat)


def _moe_kernel(l, n_tiles_max, te_ref, tv_ref, ts_ref, nt_ref, xc_ref, xl_ref, wu_hbm, bu_ref, wd_hbm, bd_ref,
                y_ref, wu_f, wd_f, wu_s, wd_s, sem, group_s):
    i = pl.program_id(0)
    nt = nt_ref[0]

    def fetch(e, slot):
        return (pltpu.make_async_copy(wu_hbm.at[l, e], wu_f.at[slot], sem.at[0, slot]),
                pltpu.make_async_copy(wd_hbm.at[l, e], wd_f.at[slot], sem.at[1, slot]))

    @pl.when(i == 0)
    def _():
        group_s[0] = 0
        for cp in fetch(te_ref[0], 0):
            cp.start()

    @pl.when(i < nt)
    def _():
        e = te_ref[i]
        prev = te_ref[jnp.maximum(i - 1, 0)]

        @pl.when((i == 0) | (e != prev))
        def _():
            group = group_s[0]
            slot = group % 2
            j = lax.while_loop(lambda j: (j < nt) & (te_ref[jnp.minimum(j, n_tiles_max - 1)] == e),
                               lambda j: j + 1, i + 1)

            @pl.when(j < nt)
            def _():
                for cp in fetch(te_ref[jnp.minimum(j, n_tiles_max - 1)], 1 - slot):
                    cp.start()

            for cp in fetch(e, slot):
                cp.wait()
            wu_s[...] = wu_f[slot].astype(BF16)
            wd_s[...] = wd_f[slot].astype(BF16)
            group_s[0] = group + 1

        valid = tv_ref[i]
        split = ts_ref[i]
        half = TME // 2
        for j in range(2):
            @pl.when(valid > j * half)
            def _():
                sl = slice(j * half, (j + 1) * half)
                rows = lax.broadcasted_iota(I32, (half, 1), 0) + j * half
                words = jnp.where(rows < split, xc_ref[sl, :], xl_ref[sl, :])
                lo, hi = _unpack_rows(jnp.where(rows < valid, words, 0))
                x = jnp.concatenate([lo.astype(BF16), hi.astype(BF16)], axis=1)
                h = _dot(x, wu_s[...]) + bu_ref[...]
                h_glu = jnp.minimum(h[:, :D_EXPERT], SWIGLU_LIMIT)
                h_lin = jnp.clip(h[:, D_EXPERT:], -SWIGLU_LIMIT, SWIGLU_LIMIT)
                act = h_glu * jax.nn.sigmoid(SWIGLU_ALPHA * h_glu) * (h_lin + 1.0)
                y_ref[sl, :] = _pack_rows(_dot(act.astype(BF16), wd_s[...]) + bd_ref[...])


def _moe_call(l, tile_e, tile_v, tile_s, n_tiles, xs_ctx, xs_lat, w_up, b_up, w_down, b_down):
    n_tiles_max = xs_ctx.shape[0] // TME

    def tile(i, te, tv, ts, nt):
        return jnp.minimum(i, nt[0] - 1)

    def wmap(i, te, tv, ts, nt):
        return (l, te[tile(i, te, tv, ts, nt)], 0, 0)

    rows = pl.BlockSpec((TME, D_PACK), lambda i, te, tv, ts, nt: (tile(i, te, tv, ts, nt), 0))
    grid_spec = pltpu.PrefetchScalarGridSpec(
        num_scalar_prefetch=4,
        grid=(n_tiles_max,),
        in_specs=[rows, rows,
                  pl.BlockSpec(memory_space=pl.ANY),
                  pl.BlockSpec((None, None, 1, 2 * D_EXPERT), wmap),
                  pl.BlockSpec(memory_space=pl.ANY),
                  pl.BlockSpec((None, None, 1, D_MODEL), wmap)],
        out_specs=rows,
        scratch_shapes=[pltpu.VMEM((2, D_MODEL, 2 * D_EXPERT), F32), pltpu.VMEM((2, D_EXPERT, D_MODEL), F32),
                        pltpu.VMEM((D_MODEL, 2 * D_EXPERT), BF16), pltpu.VMEM((D_EXPERT, D_MODEL), BF16),
                        pltpu.SemaphoreType.DMA((2, 2)), pltpu.SMEM((1,), I32)],
    )
    return pl.pallas_call(
        functools.partial(_moe_kernel, l, n_tiles_max), grid_spec=grid_spec,
        out_shape=jax.ShapeDtypeStruct((xs_ctx.shape[0], D_PACK), I32),
        compiler_params=_cparams(("arbitrary",)), name="moe",
    )(tile_e, tile_v, tile_s, n_tiles, xs_ctx, xs_lat, w_up, b_up.reshape(DEPTH, N_EXPERTS, 1, -1),
      w_down, b_down.reshape(DEPTH, N_EXPERTS, 1, -1))


def _final_kernel(yg_ref, tw_ref, x1_ref, mod_ref, ln_g_ref, ln_b_ref, *refs):
    o_ref = refs[-1]
    tw = tw_ref[...]
    halves = [_unpack_rows(yg_ref[k]) for k in range(TOP_K)]
    moe = jnp.concatenate([sum(tw[:, k:k + 1] * halves[k][j] for k in range(TOP_K)) for j in range(2)], axis=1)
    gate2 = mod_ref[0, 5:6, :]
    o_ref[...] = _ln(DEEPNORM_ALPHA * x1_ref[...] + gate2 * moe) * ln_g_ref[...] + ln_b_ref[...]


def _final_call(lat, yg, tw, x1, mod, ln_g, ln_b, after=None):
    n = x1.shape[0]
    row = lambda w: pl.BlockSpec((TM, w), lambda i: (i, 0))
    const = lambda shape: pl.BlockSpec(shape, lambda i: (0,) * len(shape))
    extra = [] if after is None else [after]
    return pl.pallas_call(
        _final_kernel,
        grid=(n // TM,),
        in_specs=[pl.BlockSpec((TOP_K, TM, D_PACK), lambda i: (0, i, 0)), row(TOP_K), row(D_MODEL),
                  _mod_spec(lat), const((1, D_MODEL)), const((1, D_MODEL))]
        + [pl.BlockSpec(memory_space=pl.ANY)] * len(extra),
        out_specs=row(D_MODEL), out_shape=jax.ShapeDtypeStruct((n, D_MODEL), F32),
        compiler_params=_cparams(("arbitrary",)), name="final_lat" if lat else "final_ctx",
    )(yg, tw, x1, mod, ln_g, ln_b, *extra)


def _rope_tables():
    half = A_HEAD_DIM // 2
    freqs = ROPE_BASE ** (-jnp.arange(0, half, 2, dtype=F32) / half)
    t = jnp.arange(DEC_SEQ, dtype=jnp.int32)
    ang_r = (t // GRID_W).astype(F32)[:, None] * freqs[None, :]
    ang_c = (t % GRID_W).astype(F32)[:, None] * freqs[None, :]
    cos64 = jnp.concatenate([jnp.cos(ang_r)] * 2 + [jnp.cos(ang_c)] * 2, axis=1)
    sin64 = jnp.concatenate([-jnp.sin(ang_r), jnp.sin(ang_r), -jnp.sin(ang_c), jnp.sin(ang_c)], axis=1)
    lat = (jnp.tile(cos64, (1, 2)), jnp.tile(sin64, (1, 2)))
    ctx = (jnp.ones((TM, LANES), F32), jnp.zeros((TM, LANES), F32))
    return ctx, lat


def _permute_in_cols(t, axis=-1):
    take = lambda lo, hi: lax.slice_in_dim(t, lo, hi, axis=axis)
    pad_shape = list(t.shape)
    pad_shape[axis] = D_IN_PAD - D_IN
    parts = [take(h * 64, (h + 1) * 64) for h in QA_HEAD_ORDER]
    parts += [take(512, 2816), take(2832, 7440), take(2816, 2832), jnp.zeros(pad_shape, t.dtype)]
    return jnp.concatenate(parts, axis=axis)


def _layer_weights(p):
    w_in, b_in = p["w_in"], p["b_in"]
    wpa = p["w_pa"].reshape(A_HEADS, A_HEAD_DIM, D_MODEL)[np.asarray(QA_HEAD_ORDER)].reshape(512, D_MODEL)
    row = lambda t: t.reshape(1, -1)
    return dict(
        inproj=(_permute_in_cols(w_in.T, axis=0).astype(BF16), _permute_in_cols(b_in).reshape(1, D_IN_PAD),
                w_in[:, 2816:2832].T.astype(BF16), b_in[2816:2832].reshape(16, 1),
                row(jnp.tile(p["qn_a"], A_HEADS)), row(jnp.tile(p["kn_a"], A_KV_HEADS)),
                jnp.asarray(np.kron(np.eye(8), np.ones((64, 64))), BF16)),
        merge=(row(p["hn_b"]), wpa.astype(BF16), p["w_pb"].astype(BF16), p["w_pc"].astype(BF16),
               p["w_o"].astype(BF16), row(p["b_o"]), row(p["ln1_g"]), row(p["ln1_b"]),
               p["w_router"].T, p["b_router"].reshape(N_EXPERTS, 1)),
        lam_c=p["lam_c"], sn_c=row(p["sn_c"]), ln2=(row(p["ln2_g"]), row(p["ln2_b"])))


def _chain_mixer(l, lat, x, mod, tables, caches, w, outs):
    cos_t, sin_t = tables
    lam_init = 0.8 - 0.6 * math.exp(-0.3 * l)
    res = _inproj_call(l, lat, x, mod, cos_t, sin_t, *w["inproj"], None if outs is None else outs[:4])
    qa, kab, vab, qb, kb, vb, ob, qc, kcb, vcb, g, ifg, ifgt = res[:13]
    if lat:
        ck_a, cv_a, ck_c, cv_c, cn0, m0 = caches
    else:
        ck_a = cv_a = ck_c = cv_c = cn0 = m0 = None
    oa = _attn_call("a", lat, lam_init, qa, kab, vab, ck_a, cv_a, None, None)
    oc = _attn_call("c", lat, lam_init, qc, kcb, vcb, ck_c, cv_c, w["lam_c"], w["sn_c"])
    ml = _mlstm_call(l, lat, qb, kb, vb, ifg, ifgt, cn0, m0, None if outs is None else outs[4:6])
    hf, hb = ml[:2]
    merged = _merge_call(lat, x, mod, oa, hf, hb, ob, oc, g, *w["merge"])
    if lat:
        return merged, None
    m_l = ml[4][:, :, :, 0, 0]
    m_all = m_l[:, None] if outs is None else jnp.concatenate([outs[6], m_l[:, None]], axis=1)
    return merged, tuple(res[13:]) + (ml[2], ml[3], m_all)


def _moe_layer(l, mix_ctx, mix_lat, mod, w, experts):
    top_i = jnp.concatenate([mix_ctx[2], mix_lat[2]], axis=1)
    pos, tile_e, tile_v, tile_s, n_tiles = _route_call(top_i, N_CTX)
    p_rows = _max_tiles(N_CTX + N_LAT) * TME
    pos_flat = (pos[:, :N_CTX].reshape(TOP_K * N_CTX), pos[:, N_CTX:].reshape(TOP_K * N_LAT))
    xs = [_dispatch_call(mix[1], pf, p_rows) for mix, pf in zip((mix_ctx, mix_lat), pos_flat)]
    ys = _moe_call(l, tile_e[0], tile_v[0], tile_s[0], n_tiles[0, :1], xs[0], xs[1],
                   experts["w_up"], experts["b_up"], experts["w_down"], experts["b_down"])
    yg_ctx = _combine_call(ys, pos_flat[0]).reshape(TOP_K, N_CTX, D_PACK)
    yg_lat = _combine_call(ys, pos_flat[1]).reshape(TOP_K, N_LAT, D_PACK)
    x_ctx = _final_call(False, yg_ctx, mix_ctx[3].T, mix_ctx[0], mod, *w["ln2"])
    finish_lat = functools.partial(_final_call, True, yg_lat, mix_lat[3].T, mix_lat[0], mod, *w["ln2"])
    return x_ctx, finish_lat


def kernel(x_prompt, x_sample, cache_k_gqa, cache_v_gqa, cache_k_diff, cache_v_diff, state_mlstm_c, state_mlstm_n, state_mlstm_m, c, c_ctx, w_in, b_in, qn_a, kn_a, hn_b, lam_c, sn_c, w_pa, w_pb, w_pc, w_o, b_o, w_mod, b_mod, ln1_g, ln1_b, ln2_g, ln2_b, w_router, b_router, w_up, b_up, w_down, b_down):
    x_ctx = x_prompt.reshape(N_CTX, D_MODEL)
    x_lat = x_sample.reshape(N_LAT, D_MODEL)
    cvecs = jnp.zeros((8, D_MODEL), F32).at[0].set(c_ctx).at[1:1 + DEC_BATCH].set(c)
    mod_all = _mod_call(cvecs, w_mod, b_mod).reshape(DEPTH, 8, N_MOD, D_MODEL)[:, :N_GROUPS]
    tables_ctx, tables_lat = _rope_tables()
    params = dict(w_in=w_in, b_in=b_in, qn_a=qn_a, kn_a=kn_a, hn_b=hn_b, lam_c=lam_c, sn_c=sn_c, w_pa=w_pa,
                  w_pb=w_pb, w_pc=w_pc, w_o=w_o, b_o=b_o, ln1_g=ln1_g, ln1_b=ln1_b, ln2_g=ln2_g, ln2_b=ln2_b,
                  w_router=w_router, b_router=b_router)
    experts = dict(w_up=w_up, b_up=b_up, w_down=w_down, b_down=b_down)
    outs = None
    finish_lat = None
    for l in range(DEPTH):
        w = _layer_weights({k: v[l] for k, v in params.items()})
        ck_a = cache_k_gqa[:, l].transpose(0, 2, 1, 3).reshape(DEC_BATCH, PAST_LEN, 128).astype(BF16)
        cv_a = cache_v_gqa[:, l].transpose(0, 2, 1, 3).reshape(DEC_BATCH, PAST_LEN, 128).astype(BF16)
        ck_c = cache_k_diff[:, l].astype(BF16)
        cv_c = cache_v_diff[:, l].astype(BF16)
        cn0 = jnp.concatenate([state_mlstm_c[:, l], state_mlstm_n[:, l][..., None],
                               jnp.zeros((DEC_BATCH, 2, B_HEADS, B_HEAD_DIM, B_HEAD_DIM - 1), F32)], axis=-1)
        m0 = jnp.broadcast_to(state_mlstm_m[:, l][..., None, None], (DEC_BATCH, 2, B_HEADS, 1, LANES))
        mix_ctx, outs = _chain_mixer(l, False, x_ctx, mod_all[l], tables_ctx, None, w, outs)
        if finish_lat is not None:
            x_lat = finish_lat(after=mix_ctx[3])
        mix_lat, _ = _chain_mixer(l, True, x_lat, mod_all[l], tables_lat, (ck_a, cv_a, ck_c, cv_c, cn0, m0), w, None)
        x_ctx, finish_lat = _moe_layer(l, mix_ctx, mix_lat, mod_all[l], w, experts)
    x_lat = finish_lat()
    return (x_ctx.reshape(BATCH, SEQ, D_MODEL), x_lat.reshape(DEC_BATCH, DEC_SEQ, D_MODEL)) + tuple(outs)
```

```python
import functools
import math

import numpy as np
import jax
import jax.numpy as jnp
from jax import lax
from jax.experimental import pallas as pl
from jax.experimental.pallas import tpu as pltpu
from jax.experimental.pallas import tpu_sc as plsc

F32 = jnp.float32
BF16 = jnp.bfloat16
I32 = jnp.int32

D_MODEL = 1024
BATCH = 16
SEQ = 256
DEPTH = 2
DEC_BATCH = 2
DEC_SEQ = 4096
PAST_LEN = 512
GRID_W = 64
ROPE_BASE = 10000.0
EPS = 1e-6
A_HEADS = 8
A_KV_HEADS = 2
A_HEAD_DIM = 64
B_HEADS = 4
B_HEAD_DIM = 128
MLSTM_CHUNK = 128
C_HEADS = 4
C_HEAD_DIM = 64
C_V_DIM = 128
N_EXPERTS = 32
TOP_K = 4
D_EXPERT = 1024
SWIGLU_LIMIT = 7.0
SWIGLU_ALPHA = 1.702
N_MOD = 6
DEEPNORM_ALPHA = (2 * DEPTH) ** 0.25

N_CTX = BATCH * SEQ
N_LAT = DEC_BATCH * DEC_SEQ
N_GROUPS = 1 + DEC_BATCH

LANES = 128
TM = 256
TQ = 512
KEY_BLOCK = 1024
MERGE_ROWS = 128
TME = 512
TILE_LANES = 128
ROUTE_CHUNK = 512
D_PACK = D_MODEL // 2
SC_WINDOW = 64
VMEM_LIMIT = 56 * 1024 * 1024

QA_HEAD_ORDER = (0, 4, 1, 5, 2, 6, 3, 7)
D_IN = 7440
D_IN_PAD = 7552
P_QA, P_KA, P_VA, P_QB, P_KB, P_VB, P_OB, P_QC, P_KC, P_VC, P_G, P_IF = (
    0, 512, 640, 768, 1280, 1792, 2304, 2816, 3328, 3840, 4352, 7424)


def _max_tiles(n_tok):
    return (n_tok * TOP_K) // TME + N_EXPERTS


def _cparams(sem):
    return pltpu.CompilerParams(dimension_semantics=sem, vmem_limit_bytes=VMEM_LIMIT)


def _split3(a):
    a1 = a.astype(BF16)
    r = a - a1.astype(F32)
    a2 = r.astype(BF16)
    a3 = (r - a2.astype(F32)).astype(BF16)
    return a1, a2, a3


def _dot(a, b, dims=None):
    if dims is None:
        return jnp.dot(a, b, preferred_element_type=F32)
    return lax.dot_general(a, b, (dims, ((), ())), preferred_element_type=F32)


def _dot_exact_rhs01(a, m, dims=None):
    return sum(_dot(p, m, dims) for p in _split3(a))


def _dot_exact_lhs01(m, a, dims=None):
    return sum(_dot(m, p, dims) for p in _split3(a))


def _pack_rows(x):
    xb = x.astype(BF16).astype(F32)
    lo = pltpu.bitcast(xb[:, :D_PACK], I32)
    lo = lax.shift_right_logical(lo, jnp.full_like(lo, 16))
    hi = pltpu.bitcast(xb[:, D_PACK:], I32) & jnp.int32(-65536)
    return lo | hi


def _unpack_rows(w):
    return (pltpu.bitcast(lax.shift_left(w, jnp.full_like(w, 16)), F32),
            pltpu.bitcast(w & jnp.int32(-65536), F32))


def _ln(x):
    mu = jnp.mean(x, axis=-1, keepdims=True)
    xc = x - mu
    var = jnp.mean(xc * xc, axis=-1, keepdims=True)
    return xc * lax.rsqrt(var + EPS)


def _log_sigmoid(x):
    return jnp.minimum(x, 0.0) - jnp.log(1.0 + jnp.exp(-jnp.abs(x)))


def _rope(x, cos, sin):
    lane = lax.broadcasted_iota(I32, (1, LANES), 1)
    first = (lane % 32) < 16
    outs = []
    for j in range(x.shape[1] // LANES):
        xj = x[:, j * LANES:(j + 1) * LANES]
        partner = jnp.where(first, pltpu.roll(xj, LANES - 16, 1), pltpu.roll(xj, 16, 1))
        outs.append(xj * cos + partner * sin)
    return outs[0] if len(outs) == 1 else jnp.concatenate(outs, axis=1)


def _mod_spec(lat):
    if lat:
        return pl.BlockSpec((1, N_MOD, D_MODEL), lambda i: (1 + i * TM // DEC_SEQ, 0, 0))
    return pl.BlockSpec((1, N_MOD, D_MODEL), lambda i: (0, 0, 0))


def _mod_kernel(c_ref, w_ref, b_ref, o_ref):
    c = c_ref[...]
    s = c * jax.nn.sigmoid(c)
    o_ref[...] = _dot(s.astype(BF16), w_ref[...].astype(BF16)) + b_ref[...]


def _mod_call(cvecs, w_mod, b_mod):
    tn = 1536
    n = N_MOD * D_MODEL
    return pl.pallas_call(
        _mod_kernel,
        grid=(DEPTH, n // tn),
        in_specs=[pl.BlockSpec((8, D_MODEL), lambda l, j: (0, 0)),
                  pl.BlockSpec((None, D_MODEL, tn), lambda l, j: (l, 0, j)),
                  pl.BlockSpec((None, 1, tn), lambda l, j: (l, 0, j))],
        out_specs=pl.BlockSpec((None, 8, tn), lambda l, j: (l, 0, j)),
        out_shape=jax.ShapeDtypeStruct((DEPTH, 8, n), F32),
        compiler_params=_cparams(("arbitrary", "arbitrary")),
        name="mod",
    )(cvecs, w_mod, b_mod.reshape(DEPTH, 1, n))


def _moe_residual_norm(yg_ref, tw_ref, x1_ref, mod_ref, ln_g_ref, ln_b_ref):
    tw = tw_ref[...]
    halves = [_unpack_rows(yg_ref[k]) for k in range(TOP_K)]
    moe = jnp.concatenate([sum(tw[:, k:k + 1] * halves[k][j] for k in range(TOP_K)) for j in range(2)], axis=1)
    gate2 = mod_ref[0, 5:6, :]
    return _ln(DEEPNORM_ALPHA * x1_ref[...] + gate2 * moe) * ln_g_ref[...] + ln_b_ref[...]


def _inproj_kernel(with_cache, fused, n_unread, *refs):
    if fused:
        x = _moe_residual_norm(*refs[:6])
        refs = refs[6:]
    else:
        x = refs[0][...]
        refs = refs[1:]
    mod_ref, cos_ref, sin_ref, w_ref, b_ref, wgt_ref, bgt_ref, gq_ref, gk_ref, gmat_ref = refs[:10]
    outs = refs[10 + n_unread:]
    qa_o, kab_o, vab_o, qb_o, kb_o, vb_o, ob_o, qc_o, kcb_o, vcb_o, g_o, ifg_o, ifgt_o = outs[:13]
    if fused:
        outs[-1][...] = x
    shift = mod_ref[0, 0:1, :]
    scale = mod_ref[0, 1:2, :]
    ub = (_ln(x) * (1.0 + scale) + shift).astype(BF16)
    cos = cos_ref[...]
    sin = sin_ref[...]

    def proj(lo, n):
        return _dot(ub, w_ref[lo:lo + n, :], ((1,), (1,))) + b_ref[:, lo:lo + n]

    def headnorm(z, gmat, gain):
        ss = _dot((z * z).astype(BF16), gmat)
        return z * lax.rsqrt(ss * (1.0 / A_HEAD_DIM) + EPS) * gain

    segs = [(P_QA, 512), (P_KA, 128), (P_VA, 128), (P_QB, 512), (P_KB, 512), (P_VB, 512), (P_OB, 512),
            (P_QC, 512), (P_KC, 512), (P_VC, 512), (P_G, 1024), (P_G + 1024, 1024), (P_G + 2048, 1024),
            (P_IF, LANES)]
    keep = {}

    def epilogue(k, z):
        if k == 0:
            z = headnorm(z, gmat_ref[...], gq_ref[...])
            qa_o[...] = (_rope(z, cos, sin) * 0.125).astype(BF16)
        elif k == 1:
            keep["k"] = _rope(headnorm(z, gmat_ref[0:128, 0:128], gk_ref[...]), cos, sin)
            kab_o[...] = keep["k"].astype(BF16)
        elif k == 2:
            keep["v"] = z
            vab_o[...] = z.astype(BF16)
        elif k == 3:
            qb_o[...] = z.astype(BF16)
        elif k == 4:
            kb_o[...] = (z * (B_HEAD_DIM ** -0.5)).astype(BF16)
        elif k == 5:
            vb_o[...] = z.astype(BF16)
        elif k == 6:
            ob_o[...] = z.astype(BF16)
        elif k == 7:
            qc_o[...] = (_rope(z, cos, sin) * 0.125).astype(BF16)
        elif k == 8:
            keep["kc"] = _rope(z, cos, sin)
            kcb_o[...] = keep["kc"].astype(BF16)
        elif k == 9:
            keep["vc"] = z
            vcb_o[...] = z.astype(BF16)
        elif k in (10, 11, 12):
            g_o[:, (k - 10) * 1024:(k - 9) * 1024] = jax.nn.sigmoid(z).astype(BF16)
        else:
            lane = lax.broadcasted_iota(I32, (1, LANES), 1)
            ifg_o[...] = jnp.where(lane >= 8, _log_sigmoid(z), z)

    z_prev = proj(*segs[0])
    for k in range(1, len(segs)):
        z_next = proj(*segs[k])
        epilogue(k - 1, z_prev)
        z_prev = z_next
    zt = _dot(wgt_ref[...], ub, ((1,), (1,))) + bgt_ref[...]
    epilogue(len(segs) - 1, z_prev)
    row = lax.broadcasted_iota(I32, (16, 1), 0)
    ifgt_o[...] = jnp.where(row >= 8, _log_sigmoid(zt), zt)

    if with_cache:
        nka_o, nva_o, nkc_o, nvc_o = outs[13:17]
        for h in range(A_KV_HEADS):
            nka_o[h] = keep["k"][:, h * A_HEAD_DIM:(h + 1) * A_HEAD_DIM]
            nva_o[h] = keep["v"][:, h * A_HEAD_DIM:(h + 1) * A_HEAD_DIM]
        for h in range(C_HEADS):
            nkc_o[h] = keep["kc"][:, h * LANES:(h + 1) * LANES]
            nvc_o[h] = keep["vc"][:, h * LANES:(h + 1) * LANES]


def _inproj_call(l, lat, x, mod, cos_t, sin_t, w_p, b_p, w_gt, b_gt, gq, gk, gmat, cache_outs, after=None):
    fused = isinstance(x, tuple)
    n = x[2].shape[0] if fused else x.shape[0]
    row = lambda w: pl.BlockSpec((TM, w), lambda i: (i, 0))
    const = lambda shape: pl.BlockSpec(shape, lambda i: (0,) * len(shape), pipeline_mode=pl.Buffered(1))
    sds = lambda w, dt: jax.ShapeDtypeStruct((n, w), dt)
    out_defs = [(512, BF16), (128, BF16), (128, BF16), (512, BF16), (512, BF16), (512, BF16), (512, BF16),
                (512, BF16), (512, BF16), (512, BF16), (3072, BF16), (LANES, F32)]
    out_specs = [row(w) for w, _ in out_defs] + [pl.BlockSpec((16, TM), lambda i: (0, i))]
    out_shape = [sds(w, dt) for w, dt in out_defs] + [jax.ShapeDtypeStruct((16, n), F32)]
    alias_in, aliases = [], {}
    if lat:
        tbl = pl.BlockSpec((TM, LANES), lambda i: (i % (DEC_SEQ // TM), 0))
    else:
        assert TM == SEQ
        tbl = pl.BlockSpec((TM, LANES), lambda i: (0, 0))
        cache_defs = [(A_KV_HEADS, A_HEAD_DIM), (A_KV_HEADS, A_HEAD_DIM), (C_HEADS, LANES), (C_HEADS, LANES)]
        out_specs += [pl.BlockSpec((None, None, nh, SEQ, d), lambda i: (i, l, 0, 0, 0)) for nh, d in cache_defs]
        out_shape += [jax.ShapeDtypeStruct((BATCH, DEPTH, nh, SEQ, d), F32) for nh, d in cache_defs]
        if cache_outs is not None:
            alias_in = list(cache_outs)
    if fused:
        plain = lambda shape: pl.BlockSpec(shape, lambda i: (0,) * len(shape))
        lead_specs = [pl.BlockSpec((TOP_K, TM, D_PACK), lambda i: (0, i, 0)), row(TOP_K), row(D_MODEL),
                      _mod_spec(lat), plain((1, D_MODEL)), plain((1, D_MODEL))]
        lead = list(x)
        out_specs.append(row(D_MODEL))
        out_shape.append(sds(D_MODEL, F32))
    else:
        lead_specs, lead = [row(D_MODEL)], [x]
    n_in = len(lead) + 10
    aliases = {n_in + j: len(out_defs) + 1 + j for j in range(len(alias_in))}
    unread = alias_in + ([] if after is None else [after])
    return pl.pallas_call(
        functools.partial(_inproj_kernel, not lat, fused, len(unread)),
        grid=(n // TM,),
        in_specs=lead_specs + [_mod_spec(lat), tbl, tbl,
                               const((D_IN_PAD, D_MODEL)), const((1, D_IN_PAD)),
                               const((16, D_MODEL)), const((16, 1)),
                               const((1, 512)), const((1, 128)), const((512, 512))]
        + [pl.BlockSpec(memory_space=pl.ANY)] * len(unread),
        out_specs=out_specs, out_shape=out_shape, input_output_aliases=aliases,
        compiler_params=_cparams(("arbitrary",)),
        name="inproj_lat" if lat else "inproj_ctx",
    )(*lead, mod, cos_t, sin_t, w_p, b_p, w_gt, b_gt, gq, gk, gmat, *unread)


def _attn_kernel(mode, nseg, lam_init, *refs):
    q_ref = refs[0]
    kv = refs[1:1 + 2 * nseg]
    rest = refs[1 + 2 * nseg:]
    o_ref = rest[-1]
    tq = q_ref.shape[0]
    lane = lax.broadcasted_iota(I32, (1, LANES), 1)
    if mode == "c":
        lam_ref, sn_ref = rest[0], rest[1]
        lp = lam_ref[...]
        lam = (jnp.exp(jnp.sum(lp[0:1] * lp[1:2], axis=-1, keepdims=True))
               - jnp.exp(jnp.sum(lp[2:3] * lp[3:4], axis=-1, keepdims=True)) + lam_init)
    for blk in range(q_ref.shape[1] // LANES):
        qs = slice(blk * LANES, (blk + 1) * LANES)
        ks = qs if kv[0].shape[1] > LANES else slice(0, LANES)
        q = q_ref[:, qs]
        zero = jnp.zeros_like(q)
        qq = jnp.concatenate([jnp.where(lane < 64, q, zero), jnp.where(lane >= 64, q, zero)], axis=0)
        m = l = acc = None
        for j in range(nseg):
            k_ref, v_ref = kv[2 * j], kv[2 * j + 1]
            for off in range(0, k_ref.shape[0], KEY_BLOCK):
                n = min(KEY_BLOCK, k_ref.shape[0] - off)
                s = _dot(qq, k_ref[off:off + n, ks], ((1,), (1,)))
                bm = jnp.max(s, axis=-1, keepdims=True)
                if m is None:
                    m = bm
                    e = jnp.exp(s - m)
                    l = jnp.sum(e, axis=-1, keepdims=True)
                    acc = _dot(e.astype(BF16), v_ref[off:off + n, ks])
                else:
                    m_new = jnp.maximum(m, bm)
                    alpha = jnp.exp(m - m_new)
                    e = jnp.exp(s - m_new)
                    l = alpha * l + jnp.sum(e, axis=-1, keepdims=True)
                    acc = alpha * acc + _dot(e.astype(BF16), v_ref[off:off + n, ks])
                    m = m_new
        o = acc * (1.0 / l)
        if mode == "a":
            out = jnp.where(lane < 64, o[:tq], o[tq:])
        else:
            o = o[:tq] - lam * o[tq:]
            ms = jnp.mean(o * o, axis=-1, keepdims=True)
            out = o * lax.rsqrt(ms + EPS) * sn_ref[...] * (1.0 - lam_init)
        o_ref[:, qs] = out.astype(o_ref.dtype)


def _attn_call(mode, lat, lam_init, q_all, k_all, v_all, k_cache, v_cache, lam_c, sn_c):
    wide = k_all.shape[1] == 512
    kcol = (lambda h: h) if wide else (lambda h: 0)
    if lat:
        nq = DEC_SEQ // TQ
        grid = (DEC_BATCH, nq, 4)
        q_spec = pl.BlockSpec((TQ, LANES), lambda b, i, h: (b * nq + i, h))
        kn_spec = pl.BlockSpec((DEC_SEQ, LANES), lambda b, i, h: (b, kcol(h)))
        if wide:
            kc_spec = pl.BlockSpec((None, None, PAST_LEN, LANES), lambda b, i, h: (b, h, 0, 0))
        else:
            kc_spec = pl.BlockSpec((None, PAST_LEN, LANES), lambda b, i, h: (b, 0, 0))
        in_specs = [q_spec, kc_spec, kc_spec, kn_spec, kn_spec]
        args = [q_all, k_cache, v_cache, k_all, v_all]
        nseg = 2
        o_spec = q_spec
        sem = ("arbitrary", "arbitrary", "arbitrary")
        zero_map = lambda b, i, h: (0, 0)
    else:
        grid = (BATCH,)
        q_spec = pl.BlockSpec((SEQ, 512), lambda b: (b, 0))
        k_spec = pl.BlockSpec((SEQ, k_all.shape[1]), lambda b: (b, 0))
        in_specs = [q_spec, k_spec, k_spec]
        args = [q_all, k_all, v_all]
        nseg = 1
        o_spec = q_spec
        sem = ("arbitrary",)
        zero_map = lambda b: (0, 0)
    if mode == "c":
        in_specs += [pl.BlockSpec((4, C_HEAD_DIM), zero_map), pl.BlockSpec((1, C_V_DIM), zero_map)]
        args += [lam_c, sn_c]
    return pl.pallas_call(
        functools.partial(_attn_kernel, mode, nseg, lam_init),
        grid=grid, in_specs=in_specs, out_specs=o_spec,
        out_shape=jax.ShapeDtypeStruct((q_all.shape[0], 512), BF16),
        compiler_params=_cparams(sem), name=f"attn_{mode}_{'lat' if lat else 'ctx'}",
    )(*args)


def _mlstm_kernel(has_init, want_state, n_alias, *refs):
    tok = (refs[0:5], refs[5:10])
    pos = 10
    if has_init:
        cn0_ref, m0_ref = refs[pos:pos + 2]
        pos += 2
    pos += n_alias
    h_refs = refs[pos:pos + 2]
    pos += 2
    if want_state:
        c_o, n_o, m_o = refs[pos:pos + 3]
        pos += 3
    cn_s, m_s = refs[pos:pos + 2]

    c = pl.program_id(1)
    last = pl.num_programs(1) - 1
    L = MLSTM_CHUNK

    @pl.when(c == 0)
    def _():
        if has_init:
            cn_s[...] = cn0_ref[...]
            m_s[...] = m0_ref[...]
        else:
            cn_s[...] = jnp.zeros_like(cn_s)
            m_s[...] = jnp.zeros_like(m_s)

    row = lax.broadcasted_iota(I32, (L, L), 0)
    col = lax.broadcasted_iota(I32, (L, L), 1)
    lane = lax.broadcasted_iota(I32, (1, LANES), 1)
    ones_col = jnp.where(lane == 0, 1.0, 0.0).astype(BF16)

    chains = []
    for d in range(2):
        q_ref, k_ref, v_ref, g_ref, gt_ref = tok[d]
        keep = (col <= row) if d == 0 else (col >= row)
        m_c = keep.astype(BF16)
        m_r = ((row <= col) if d == 0 else (row >= col)).astype(BF16)
        g = g_ref[...]
        gt = gt_ref[...]
        bc = _dot_exact_lhs01(m_c, g)
        br = _dot_exact_rhs01(gt, m_r)
        for h in range(B_HEADS):
            sl = slice(h * L, (h + 1) * L)
            gi, gf = 4 * d + h, 8 + 4 * d + h
            ch = dict(d=d, h=h, sl=sl, keep=keep, q=q_ref[:, sl], k=k_ref[:, sl], v=v_ref[:, sl],
                      i_col=g[:, gi:gi + 1], f_col=g[:, gf:gf + 1], b_col=bc[:, gf:gf + 1],
                      r_row=gt[gi:gi + 1, :] - br[gf:gf + 1, :],
                      m_prev=m_s[d, h][:, 0:1], cn_prev=cn_s[d, h])
            chains.append(ch)
    for ch in chains:
        ch["s"] = _dot(ch["q"], ch["k"], ((1,), (1,)))
        ch["qc"] = _dot(ch["q"], ch["cn_prev"].astype(BF16))
    for ch in chains:
        ch["total"] = jnp.sum(ch["f_col"], axis=0, keepdims=True)
        ch["inter"] = ch["b_col"] + ch["m_prev"]
        ch["v_aug"] = jnp.concatenate([ch["v"], jnp.broadcast_to(ones_col, (L, LANES))], axis=1)
    for ch in chains:
        ch["dmat"] = jnp.where(ch["keep"], ch["b_col"] + ch["r_row"], -jnp.inf)
    for ch in chains:
        ch["dmax"] = jnp.max(ch["dmat"], axis=-1, keepdims=True)
    for ch in chains:
        ch["m_row"] = jnp.maximum(ch["inter"], ch["dmax"])
    for ch in chains:
        ch["sc"] = (ch["s"] * jnp.exp(ch["dmat"] - ch["m_row"])).astype(BF16)
    for ch in chains:
        ch["a"] = _dot(ch["sc"], ch["v_aug"])
    for ch in chains:
        ch["nd"] = jnp.exp(ch["inter"] - ch["m_row"]) * ch["qc"] + ch["a"]
    for ch in chains:
        ch["scale"] = 1.0 / jnp.maximum(jnp.abs(ch["nd"][:, L:L + 1]), jnp.exp(-ch["m_row"]))
    for ch in chains:
        h_refs[ch["d"]][:, ch["sl"]] = ch["nd"][:, :L] * ch["scale"]
    for ch in chains:
        ch["wel"] = ch["total"] - ch["b_col"] + ch["i_col"]
        ch["m_new"] = jnp.maximum(ch["total"] + ch["m_prev"], jnp.max(ch["wel"], axis=0, keepdims=True))
    for ch in chains:
        ch["kw"] = (jnp.exp(ch["wel"] - ch["m_new"]) * ch["k"].astype(F32)).astype(BF16)
    for ch in chains:
        ch["u"] = _dot(ch["kw"], ch["v_aug"], ((0,), (0,)))
    for ch in chains:
        cn_s[ch["d"], ch["h"]] = jnp.exp(ch["total"] + ch["m_prev"] - ch["m_new"]) * ch["cn_prev"] + ch["u"]
        m_s[ch["d"], ch["h"]] = jnp.broadcast_to(ch["m_new"], (1, LANES))

    if want_state:
        @pl.when(c == last)
        def _():
            pick = ((lax.broadcasted_iota(I32, (8, 2 * L), 1) == L)
                    & (lax.broadcasted_iota(I32, (8, 2 * L), 0) == 0)).astype(BF16)
            for d in range(2):
                for h in range(B_HEADS):
                    cn = cn_s[d, h]
                    c_o[d, h] = cn[:, :L]
                    n_o[d, h:h + 1, :] = _dot_exact_lhs01(pick, cn, ((1,), (1,)))[0:1]
            m_o[...] = m_s[...]


def _mlstm_call(l, lat, q_all, k_all, v_all, g_all, gt_all, cn0, m0, state_bufs):
    L = MLSTM_CHUNK
    n = q_all.shape[0]
    nb, nc = (DEC_BATCH, DEC_SEQ // L) if lat else (BATCH, SEQ // L)
    chunk = (lambda b, c: b * nc + c, lambda b, c: b * nc + (nc - 1 - c))

    in_specs, args = [], []
    for d in range(2):
        tok = lambda w, d=d: pl.BlockSpec((L, w), lambda b, c: (chunk[d](b, c), 0))
        in_specs += [tok(512), tok(512), tok(512), tok(LANES),
                     pl.BlockSpec((16, L), lambda b, c, d=d: (0, chunk[d](b, c)))]
        args += [q_all, k_all, v_all, g_all, gt_all]
    if lat:
        in_specs += [pl.BlockSpec((None, 2, B_HEADS, L, 2 * L), lambda b, c: (b, 0, 0, 0, 0)),
                     pl.BlockSpec((None, 2, B_HEADS, 1, LANES), lambda b, c: (b, 0, 0, 0, 0))]
        args += [cn0, m0]
    aliases = {}
    alias_in = list(state_bufs) if state_bufs is not None else []
    if alias_in:
        aliases = {len(args): 2, len(args) + 1: 3}
    in_specs += [pl.BlockSpec(memory_space=pl.ANY)] * len(alias_in)
    args += alias_in
    out_specs = [pl.BlockSpec((L, 512), lambda b, c, d=d: (chunk[d](b, c), 0)) for d in range(2)]
    out_shape = [jax.ShapeDtypeStruct((n, 512), F32)] * 2
    if not lat:
        out_specs += [pl.BlockSpec((None, None, 2, B_HEADS, L, L), lambda b, c: (b, l, 0, 0, 0, 0)),
                      pl.BlockSpec((None, None, 2, B_HEADS, L), lambda b, c: (b, l, 0, 0, 0)),
                      pl.BlockSpec((None, 2, B_HEADS, 1, LANES), lambda b, c: (b, 0, 0, 0, 0))]
        out_shape += [jax.ShapeDtypeStruct((BATCH, DEPTH, 2, B_HEADS, L, L), F32),
                      jax.ShapeDtypeStruct((BATCH, DEPTH, 2, B_HEADS, L), F32),
                      jax.ShapeDtypeStruct((nb, 2, B_HEADS, 1, LANES), F32)]
    return pl.pallas_call(
        functools.partial(_mlstm_kernel, lat, not lat, len(alias_in)),
        grid=(nb, nc), in_specs=in_specs, out_specs=out_specs, out_shape=out_shape,
        input_output_aliases=aliases,
        scratch_shapes=[pltpu.VMEM((2, B_HEADS, L, 2 * L), F32), pltpu.VMEM((2, B_HEADS, 1, LANES), F32)],
        compiler_params=_cparams(("arbitrary", "arbitrary")), name="mlstm_lat" if lat else "mlstm_ctx",
    )(*args)


def _merge_kernel(x_ref, mod_ref, oa_ref, hf_ref, hb_ref, ob_ref, oc_ref, g_ref, hn_ref,
                  wpa_ref, wpb_ref, wpc_ref, wo_ref, bo_ref, ln_g_ref, ln_b_ref, wr_ref, br_ref,
                  x1_o, u2_o, ti_o, tw_o):
    gate1 = mod_ref[0, 2:3, :]
    shift2 = mod_ref[0, 3:4, :]
    scale2 = mod_ref[0, 4:5, :]
    wr = _split3(wr_ref[...])
    eio = lax.broadcasted_iota(I32, (N_EXPERTS, 1), 0)
    nt = ((1,), (1,))
    cs = [dict(rs=slice(r, r + MERGE_ROWS)) for r in range(0, TM, MERGE_ROWS)]
    for c in cs:
        c["pa"] = _dot(oa_ref[c["rs"], :], wpa_ref[...])
        c["pc"] = _dot(oc_ref[c["rs"], :], wpc_ref[...])
    for c in cs:
        hm = hf_ref[c["rs"], :] + hb_ref[c["rs"], :]
        parts = []
        for h in range(B_HEADS):
            hh = hm[:, h * LANES:(h + 1) * LANES]
            parts.append(hh * lax.rsqrt(jnp.mean(hh * hh, axis=-1, keepdims=True) + EPS))
        c["out_b"] = (jnp.concatenate(parts, axis=1) * hn_ref[...]
                      * jax.nn.sigmoid(ob_ref[c["rs"], :].astype(F32))).astype(BF16)
    for c in cs:
        c["pb"] = _dot(c["out_b"], wpb_ref[...])
    for c in cs:
        rs = c["rs"]
        c["merged"] = (g_ref[rs, 0:1024].astype(F32) * c["pa"] + g_ref[rs, 1024:2048].astype(F32) * c["pb"]
                       + g_ref[rs, 2048:3072].astype(F32) * c["pc"]).astype(BF16)
    for c in cs:
        c["y"] = _dot(c["merged"], wo_ref[...]) + bo_ref[...]
    for c in cs:
        c["x1"] = _ln(DEEPNORM_ALPHA * x_ref[c["rs"], :] + gate1 * c["y"]) * ln_g_ref[...] + ln_b_ref[...]
        x1_o[c["rs"], :] = c["x1"]
    for c in cs:
        c["u2"] = _ln(c["x1"]) * (1.0 + scale2) + shift2
        u2_o[c["rs"], :] = _pack_rows(c["u2"])
    for c in cs:
        up = _split3(c["u2"])
        c["cur"] = (_dot(wr[0], up[0], nt) + _dot(wr[0], up[1], nt) + _dot(wr[1], up[0], nt)
                    + _dot(wr[0], up[2], nt) + _dot(wr[2], up[0], nt) + _dot(wr[1], up[1], nt)) + br_ref[...]
        c["vals"], c["idxs"] = [], []
    for _ in range(TOP_K):
        for c in cs:
            mx = jnp.max(c["cur"], axis=0, keepdims=True)
            ix = jnp.min(jnp.where(c["cur"] == mx, eio, N_EXPERTS), axis=0, keepdims=True)
            c["vals"].append(mx)
            c["idxs"].append(ix)
            c["cur"] = jnp.where(eio == ix, -jnp.inf, c["cur"])
    for c in cs:
        ex = [jnp.exp(v - c["vals"][0]) for v in c["vals"]]
        inv = 1.0 / sum(ex)
        ti_o[:, c["rs"]] = jnp.concatenate(c["idxs"], axis=0)
        tw_o[:, c["rs"]] = jnp.concatenate([e * inv for e in ex], axis=0)


def _merge_call(lat, x, mod, out_a, hf, hb, ob, out_c, g, hn, wpa, wpb, wpc, wo, bo, ln_g, ln_b, wr_t, br):
    n = x.shape[0]
    row = lambda w: pl.BlockSpec((TM, w), lambda i: (i, 0))
    const = lambda shape: pl.BlockSpec(shape, lambda i: (0,) * len(shape), pipeline_mode=pl.Buffered(1))
    return pl.pallas_call(
        _merge_kernel,
        grid=(n // TM,),
        in_specs=[row(D_MODEL), _mod_spec(lat),
                  row(512), row(512), row(512), row(512), row(512), row(3072),
                  const((1, 512)), const((512, D_MODEL)), const((512, D_MODEL)), const((512, D_MODEL)),
                  const((D_MODEL, D_MODEL)), const((1, D_MODEL)), const((1, D_MODEL)), const((1, D_MODEL)),
                  const((N_EXPERTS, D_MODEL)), const((N_EXPERTS, 1))],
        out_specs=[row(D_MODEL), row(D_PACK),
                   pl.BlockSpec((TOP_K, TM), lambda i: (0, i)), pl.BlockSpec((TOP_K, TM), lambda i: (0, i))],
        out_shape=[jax.ShapeDtypeStruct((n, D_MODEL), F32), jax.ShapeDtypeStruct((n, D_PACK), I32),
                   jax.ShapeDtypeStruct((TOP_K, n), I32), jax.ShapeDtypeStruct((TOP_K, n), F32)],
        compiler_params=_cparams(("arbitrary",)), name="merge_lat" if lat else "merge_ctx",
    )(x, mod, out_a, hf, hb, ob, out_c, g, hn, wpa, wpb, wpc, wo, bo, ln_g, ln_b, wr_t, br)


def _route_kernel(n_first, ti_ref, pos_o, te_o, tv_o, ts_o, nt_o, rank_s):
    ch = ROUTE_CHUNK
    n_tok = ti_ref.shape[1]
    eio = lax.broadcasted_iota(I32, (N_EXPERTS, 1), 0)
    tri = (lax.broadcasted_iota(I32, (ch, ch), 0) <= lax.broadcasted_iota(I32, (ch, ch), 1)).astype(BF16)

    def onehots(off):
        ti = ti_ref[:, pl.ds(off, ch)]
        return [(eio == ti[k:k + 1, :]).astype(F32) for k in range(TOP_K)]

    def count_body(c, carry):
        off = pl.multiple_of(c * ch, ch)
        ohs = onehots(off)
        oh = ohs[0] + ohs[1] + ohs[2] + ohs[3]
        incl = _dot(oh.astype(BF16), tri)
        excl = incl - oh + carry
        for k in range(TOP_K):
            rank_s[k:k + 1, pl.ds(off, ch)] = jnp.sum(ohs[k] * excl, axis=0, keepdims=True)
        return carry + incl[:, ch - 1:ch]

    counts_first = lax.fori_loop(0, n_first // ch, count_body, jnp.zeros((N_EXPERTS, 1), F32))
    counts = lax.fori_loop(n_first // ch, n_tok // ch, count_body, counts_first)
    ntile_e = jnp.floor((counts + (TME - 1)) * (1.0 / TME))
    lower = (lax.broadcasted_iota(I32, (N_EXPERTS, N_EXPERTS), 1)
             < lax.broadcasted_iota(I32, (N_EXPERTS, N_EXPERTS), 0)).astype(BF16)
    off_t = _dot(lower, jnp.broadcast_to(ntile_e, (N_EXPERTS, LANES)).astype(BF16))[:, 0:1]
    end_t = off_t + ntile_e
    total = jnp.sum(ntile_e, axis=0, keepdims=True)
    jt = lax.broadcasted_iota(I32, (1, TILE_LANES), 1).astype(F32)
    te = jnp.minimum(jnp.sum((end_t <= jt).astype(F32), axis=0, keepdims=True), N_EXPERTS - 1.0)
    sel = (eio.astype(F32) == te).astype(F32)
    cnt_j = jnp.sum(sel * counts, axis=0, keepdims=True)
    off_j = jnp.sum(sel * off_t, axis=0, keepdims=True)
    valid = jnp.clip(cnt_j - (jt - off_j) * TME, 0.0, float(TME))
    first_j = jnp.sum(sel * counts_first, axis=0, keepdims=True)
    te_o[...] = te.astype(I32)
    tv_o[...] = jnp.where(jt < total, valid, 0.0).astype(I32)
    ts_o[...] = jnp.clip(first_j - (jt - off_j) * TME, 0.0, float(TME)).astype(I32)
    nt_o[...] = jnp.broadcast_to(total, (1, LANES)).astype(I32)
    off_rows = off_t * TME

    def pos_body(c, carry):
        off = pl.multiple_of(c * ch, ch)
        ohs = onehots(off)
        for k in range(TOP_K):
            base = jnp.sum(ohs[k] * off_rows, axis=0, keepdims=True)
            pos_o[k:k + 1, pl.ds(off, ch)] = (rank_s[k:k + 1, pl.ds(off, ch)] + base).astype(I32)
        return carry

    lax.fori_loop(0, n_tok // ch, pos_body, 0)


def _route_call(top_i, n_first):
    n = top_i.shape[1]
    assert _max_tiles(n) <= TILE_LANES and n_first % ROUTE_CHUNK == 0
    tiles = jax.ShapeDtypeStruct((1, TILE_LANES), I32)
    return pl.pallas_call(
        functools.partial(_route_kernel, n_first),
        out_shape=[jax.ShapeDtypeStruct((TOP_K, n), I32), tiles, tiles, tiles,
                   jax.ShapeDtypeStruct((1, LANES), I32)],
        scratch_shapes=[pltpu.VMEM((TOP_K, n), F32)],
        compiler_params=pltpu.CompilerParams(vmem_limit_bytes=VMEM_LIMIT), name="route",
    )(top_i)


def _sc_mesh():
    return plsc.VectorSubcoreMesh(core_axis_name="c", subcore_axis_name="s")


def _sc_worker(n_rows):
    info = plsc.get_sparse_core_info()
    n_workers = info.num_cores * info.num_subcores
    per_w = n_rows // n_workers
    assert per_w * n_workers == n_rows and per_w % SC_WINDOW == 0
    wid = lax.axis_index("s") * info.num_cores + lax.axis_index("c")
    return wid * per_w, per_w


def _dispatch_call(u2, pos_flat, p_rows):
    w = SC_WINDOW
    n = u2.shape[0]

    @functools.partial(pl.kernel, out_type=jax.ShapeDtypeStruct((p_rows, u2.shape[1]), u2.dtype), mesh=_sc_mesh(),
                       scratch_types=[pltpu.VMEM((w,), I32)] * (2 * TOP_K)
                       + [pltpu.VMEM((w, u2.shape[1]), u2.dtype)] * 2 + [pltpu.SemaphoreType.DMA] * 4)
    def k(x_hbm, p_hbm, o_hbm, *scratch):
        idx = (scratch[0:TOP_K], scratch[TOP_K:2 * TOP_K])
        rows = scratch[2 * TOP_K:2 * TOP_K + 2]
        sem_load = scratch[2 * TOP_K + 2:2 * TOP_K + 4]
        sem_scat = scratch[2 * TOP_K + 4:2 * TOP_K + 6]
        base, per_w = _sc_worker(n)
        n_chunks = per_w // w

        def start_loads(c, s):
            off = pl.multiple_of(base + c * w, 8)
            cps = [pltpu.async_copy(x_hbm.at[pl.ds(off, w)], rows[s], sem_load[s])]
            cps += [pltpu.async_copy(p_hbm.at[pl.ds(pl.multiple_of(kk * n + off, 8), w)], idx[s][kk], sem_load[s])
                    for kk in range(TOP_K)]
            return cps

        loads = start_loads(0, 0)
        scatters = [None, None]
        for c in range(n_chunks):
            s = c % 2
            for cp in loads:
                cp.wait()
            if c + 1 < n_chunks:
                if scatters[1 - s] is not None:
                    for cp in scatters[1 - s]:
                        cp.wait()
                    scatters[1 - s] = None
                loads = start_loads(c + 1, 1 - s)
            scatters[s] = [pltpu.async_copy(rows[s], o_hbm.at[idx[s][kk]], sem_scat[s]) for kk in range(TOP_K)]
        for group in scatters:
            if group is not None:
                for cp in group:
                    cp.wait()

    return k(u2, pos_flat)


def _combine_call(ys, pos_flat):
    w = SC_WINDOW
    n = pos_flat.shape[0]

    @functools.partial(pl.kernel, out_type=jax.ShapeDtypeStruct((n, ys.shape[1]), ys.dtype), mesh=_sc_mesh(),
                       scratch_types=[pltpu.VMEM((w,), I32)] * 2 + [pltpu.VMEM((w, ys.shape[1]), ys.dtype)] * 2
                       + [pltpu.SemaphoreType.DMA] * 4)
    def k(y_hbm, p_hbm, o_hbm, idx0, idx1, rows0, rows1, sg0, sg1, ss0, ss1):
        idx, rows, sem_gather, sem_store = (idx0, idx1), (rows0, rows1), (sg0, sg1), (ss0, ss1)
        base, per_w = _sc_worker(n)
        n_chunks = per_w // w

        def offset(c):
            return pl.multiple_of(base + c * w, 8)

        pltpu.sync_copy(p_hbm.at[pl.ds(offset(0), w)], idx[0])
        gather = pltpu.async_copy(y_hbm.at[idx[0]], rows[0], sem_gather[0])
        stores = [None, None]
        for c in range(n_chunks):
            s = c % 2
            if c + 1 < n_chunks:
                pltpu.sync_copy(p_hbm.at[pl.ds(offset(c + 1), w)], idx[1 - s])
            gather.wait()
            if c + 1 < n_chunks:
                if stores[1 - s] is not None:
                    stores[1 - s].wait()
                    stores[1 - s] = None
                gather = pltpu.async_copy(y_hbm.at[idx[1 - s]], rows[1 - s], sem_gather[1 - s])
            stores[s] = pltpu.async_copy(rows[s], o_hbm.at[pl.ds(offset(c), w)], sem_store[s])
        for st in stores:
            if st is not None:
                st.wait()

    return k(ys, pos_flat)


def _moe_kernel(l, n_tiles_max, te_ref, tv_ref, ts_ref, nt_ref, xc_ref, xl_ref, wu_hbm, bu_ref, wd_hbm, bd_ref,
                y_ref, wu_f, wd_f, wu_s, wd_s, sem, group_s):
    i = pl.program_id(0)
    nt = nt_ref[0]

    def fetch(e, slot):
        return (pltpu.make_async_copy(wu_hbm.at[l, e], wu_f.at[slot], sem.at[0, slot]),
                pltpu.make_async_copy(wd_hbm.at[l, e], wd_f.at[slot], sem.at[1, slot]))

    @pl.when(i == 0)
    def _():
        group_s[0] = 0
        for cp in fetch(te_ref[0], 0):
            cp.start()

    @pl.when(i < nt)
    def _():
        e = te_ref[i]
        prev = te_ref[jnp.maximum(i - 1, 0)]

        @pl.when((i == 0) | (e != prev))
        def _():
            group = group_s[0]
            slot = group % 2
            j = lax.while_loop(lambda j: (j < nt) & (te_ref[jnp.minimum(j, n_tiles_max - 1)] == e),
                               lambda j: j + 1, i + 1)

            @pl.when(j < nt)
            def _():
                for cp in fetch(te_ref[jnp.minimum(j, n_tiles_max - 1)], 1 - slot):
                    cp.start()

            for cp in fetch(e, slot):
                cp.wait()
            wu_s[...] = wu_f[slot].astype(BF16)
            wd_s[...] = wd_f[slot].astype(BF16)
            group_s[0] = group + 1

        valid = tv_ref[i]
        split = ts_ref[i]
        half = TME // 2
        for j in range(2):
            @pl.when(valid > j * half)
            def _():
                sl = slice(j * half, (j + 1) * half)
                rows = lax.broadcasted_iota(I32, (half, 1), 0) + j * half
                words = jnp.where(rows < split, xc_ref[sl, :], xl_ref[sl, :])
                lo, hi = _unpack_rows(jnp.where(rows < valid, words, 0))
                x = jnp.concatenate([lo.astype(BF16), hi.astype(BF16)], axis=1)
                h = _dot(x, wu_s[...]) + bu_ref[...]
                h_glu = jnp.minimum(h[:, :D_EXPERT], SWIGLU_LIMIT)
                h_lin = jnp.clip(h[:, D_EXPERT:], -SWIGLU_LIMIT, SWIGLU_LIMIT)
                act = h_glu * jax.nn.sigmoid(SWIGLU_ALPHA * h_glu) * (h_lin + 1.0)
                y_ref[sl, :] = _pack_rows(_dot(act.astype(BF16), wd_s[...]) + bd_ref[...])


def _moe_call(l, tile_e, tile_v, tile_s, n_tiles, xs_ctx, xs_lat, w_up, b_up, w_down, b_down):
    n_tiles_max = xs_ctx.shape[0] // TME

    def tile(i, te, tv, ts, nt):
        return jnp.minimum(i, nt[0] - 1)

    def wmap(i, te, tv, ts, nt):
        return (l, te[tile(i, te, tv, ts, nt)], 0, 0)

    rows = pl.BlockSpec((TME, D_PACK), lambda i, te, tv, ts, nt: (tile(i, te, tv, ts, nt), 0))
    grid_spec = pltpu.PrefetchScalarGridSpec(
        num_scalar_prefetch=4,
        grid=(n_tiles_max,),
        in_specs=[rows, rows,
                  pl.BlockSpec(memory_space=pl.ANY),
                  pl.BlockSpec((None, None, 1, 2 * D_EXPERT), wmap),
                  pl.BlockSpec(memory_space=pl.ANY),
                  pl.BlockSpec((None, None, 1, D_MODEL), wmap)],
        out_specs=rows,
        scratch_shapes=[pltpu.VMEM((2, D_MODEL, 2 * D_EXPERT), F32), pltpu.VMEM((2, D_EXPERT, D_MODEL), F32),
                        pltpu.VMEM((D_MODEL, 2 * D_EXPERT), BF16), pltpu.VMEM((D_EXPERT, D_MODEL), BF16),
                        pltpu.SemaphoreType.DMA((2, 2)), pltpu.SMEM((1,), I32)],
    )
    return pl.pallas_call(
        functools.partial(_moe_kernel, l, n_tiles_max), grid_spec=grid_spec,
        out_shape=jax.ShapeDtypeStruct((xs_ctx.shape[0], D_PACK), I32),
        compiler_params=_cparams(("arbitrary",)), name="moe",
    )(tile_e, tile_v, tile_s, n_tiles, xs_ctx, xs_lat, w_up, b_up.reshape(DEPTH, N_EXPERTS, 1, -1),
      w_down, b_down.reshape(DEPTH, N_EXPERTS, 1, -1))


def _final_kernel(yg_ref, tw_ref, x1_ref, mod_ref, ln_g_ref, ln_b_ref, o_ref):
    o_ref[...] = _moe_residual_norm(yg_ref, tw_ref, x1_ref, mod_ref, ln_g_ref, ln_b_ref)


def _final_call(lat, yg, tw, x1, mod, ln_g, ln_b):
    n = x1.shape[0]
    row = lambda w: pl.BlockSpec((TM, w), lambda i: (i, 0))
    const = lambda shape: pl.BlockSpec(shape, lambda i: (0,) * len(shape))
    return pl.pallas_call(
        _final_kernel,
        grid=(n // TM,),
        in_specs=[pl.BlockSpec((TOP_K, TM, D_PACK), lambda i: (0, i, 0)), row(TOP_K), row(D_MODEL),
                  _mod_spec(lat), const((1, D_MODEL)), const((1, D_MODEL))],
        out_specs=row(D_MODEL), out_shape=jax.ShapeDtypeStruct((n, D_MODEL), F32),
        compiler_params=_cparams(("arbitrary",)), name="final_lat" if lat else "final_ctx",
    )(yg, tw, x1, mod, ln_g, ln_b)


def _rope_tables():
    half = A_HEAD_DIM // 2
    freqs = ROPE_BASE ** (-jnp.arange(0, half, 2, dtype=F32) / half)
    t = jnp.arange(DEC_SEQ, dtype=jnp.int32)
    ang_r = (t // GRID_W).astype(F32)[:, None] * freqs[None, :]
    ang_c = (t % GRID_W).astype(F32)[:, None] * freqs[None, :]
    cos64 = jnp.concatenate([jnp.cos(ang_r)] * 2 + [jnp.cos(ang_c)] * 2, axis=1)
    sin64 = jnp.concatenate([-jnp.sin(ang_r), jnp.sin(ang_r), -jnp.sin(ang_c), jnp.sin(ang_c)], axis=1)
    lat = (jnp.tile(cos64, (1, 2)), jnp.tile(sin64, (1, 2)))
    ctx = (jnp.ones((TM, LANES), F32), jnp.zeros((TM, LANES), F32))
    return ctx, lat


def _permute_in_cols(t, axis=-1):
    take = lambda lo, hi: lax.slice_in_dim(t, lo, hi, axis=axis)
    pad_shape = list(t.shape)
    pad_shape[axis] = D_IN_PAD - D_IN
    parts = [take(h * 64, (h + 1) * 64) for h in QA_HEAD_ORDER]
    parts += [take(512, 2816), take(2832, 7440), take(2816, 2832), jnp.zeros(pad_shape, t.dtype)]
    return jnp.concatenate(parts, axis=axis)


def _layer_weights(p):
    w_in, b_in = p["w_in"], p["b_in"]
    wpa = p["w_pa"].reshape(A_HEADS, A_HEAD_DIM, D_MODEL)[np.asarray(QA_HEAD_ORDER)].reshape(512, D_MODEL)
    row = lambda t: t.reshape(1, -1)
    return dict(
        inproj=(_permute_in_cols(w_in.T, axis=0).astype(BF16), _permute_in_cols(b_in).reshape(1, D_IN_PAD),
                w_in[:, 2816:2832].T.astype(BF16), b_in[2816:2832].reshape(16, 1),
                row(jnp.tile(p["qn_a"], A_HEADS)), row(jnp.tile(p["kn_a"], A_KV_HEADS)),
                jnp.asarray(np.kron(np.eye(8), np.ones((64, 64))), BF16)),
        merge=(row(p["hn_b"]), wpa.astype(BF16), p["w_pb"].astype(BF16), p["w_pc"].astype(BF16),
               p["w_o"].astype(BF16), row(p["b_o"]), row(p["ln1_g"]), row(p["ln1_b"]),
               p["w_router"].T, p["b_router"].reshape(N_EXPERTS, 1)),
        lam_c=p["lam_c"], sn_c=row(p["sn_c"]), ln2=(row(p["ln2_g"]), row(p["ln2_b"])))


def _chain_mixer(l, lat, x, mod, tables, caches, w, outs, after=None):
    cos_t, sin_t = tables
    lam_init = 0.8 - 0.6 * math.exp(-0.3 * l)
    res = _inproj_call(l, lat, x, mod, cos_t, sin_t, *w["inproj"], None if outs is None else outs[:4], after)
    qa, kab, vab, qb, kb, vb, ob, qc, kcb, vcb, g, ifg, ifgt = res[:13]
    if isinstance(x, tuple):
        x = res[-1]
    if lat:
        ck_a, cv_a, ck_c, cv_c, cn0, m0 = caches
    else:
        ck_a = cv_a = ck_c = cv_c = cn0 = m0 = None
    oa = _attn_call("a", lat, lam_init, qa, kab, vab, ck_a, cv_a, None, None)
    oc = _attn_call("c", lat, lam_init, qc, kcb, vcb, ck_c, cv_c, w["lam_c"], w["sn_c"])
    ml = _mlstm_call(l, lat, qb, kb, vb, ifg, ifgt, cn0, m0, None if outs is None else outs[4:6])
    hf, hb = ml[:2]
    merged = _merge_call(lat, x, mod, oa, hf, hb, ob, oc, g, *w["merge"])
    if lat:
        return merged, None
    m_l = ml[4][:, :, :, 0, 0]
    m_all = m_l[:, None] if outs is None else jnp.concatenate([outs[6], m_l[:, None]], axis=1)
    return merged, tuple(res[13:17]) + (ml[2], ml[3], m_all)


def _moe_layer(l, mix_ctx, mix_lat, mod, w, experts):
    top_i = jnp.concatenate([mix_ctx[2], mix_lat[2]], axis=1)
    pos, tile_e, tile_v, tile_s, n_tiles = _route_call(top_i, N_CTX)
    p_rows = _max_tiles(N_CTX + N_LAT) * TME
    pos_flat = (pos[:, :N_CTX].reshape(TOP_K * N_CTX), pos[:, N_CTX:].reshape(TOP_K * N_LAT))
    xs = [_dispatch_call(mix[1], pf, p_rows) for mix, pf in zip((mix_ctx, mix_lat), pos_flat)]
    ys = _moe_call(l, tile_e[0], tile_v[0], tile_s[0], n_tiles[0, :1], xs[0], xs[1],
                   experts["w_up"], experts["b_up"], experts["w_down"], experts["b_down"])
    yg_ctx = _combine_call(ys, pos_flat[0]).reshape(TOP_K, N_CTX, D_PACK)
    yg_lat = _combine_call(ys, pos_flat[1]).reshape(TOP_K, N_LAT, D_PACK)
    return ((yg_ctx, mix_ctx[3].T, mix_ctx[0], mod) + w["ln2"], (yg_lat, mix_lat[3].T, mix_lat[0], mod) + w["ln2"])


def kernel(x_prompt, x_sample, cache_k_gqa, cache_v_gqa, cache_k_diff, cache_v_diff, state_mlstm_c, state_mlstm_n, state_mlstm_m, c, c_ctx, w_in, b_in, qn_a, kn_a, hn_b, lam_c, sn_c, w_pa, w_pb, w_pc, w_o, b_o, w_mod, b_mod, ln1_g, ln1_b, ln2_g, ln2_b, w_router, b_router, w_up, b_up, w_down, b_down):
    x_ctx = x_prompt.reshape(N_CTX, D_MODEL)
    x_lat = x_sample.reshape(N_LAT, D_MODEL)
    cvecs = jnp.zeros((8, D_MODEL), F32).at[0].set(c_ctx).at[1:1 + DEC_BATCH].set(c)
    mod_all = _mod_call(cvecs, w_mod, b_mod).reshape(DEPTH, 8, N_MOD, D_MODEL)[:, :N_GROUPS]
    tables_ctx, tables_lat = _rope_tables()
    params = dict(w_in=w_in, b_in=b_in, qn_a=qn_a, kn_a=kn_a, hn_b=hn_b, lam_c=lam_c, sn_c=sn_c, w_pa=w_pa,
                  w_pb=w_pb, w_pc=w_pc, w_o=w_o, b_o=b_o, ln1_g=ln1_g, ln1_b=ln1_b, ln2_g=ln2_g, ln2_b=ln2_b,
                  w_router=w_router, b_router=b_router)
    experts = dict(w_up=w_up, b_up=b_up, w_down=w_down, b_down=b_down)
    outs = None
    for l in range(DEPTH):
        w = _layer_weights({k: v[l] for k, v in params.items()})
        ck_a = cache_k_gqa[:, l].transpose(0, 2, 1, 3).reshape(DEC_BATCH, PAST_LEN, 128).astype(BF16)
        cv_a = cache_v_gqa[:, l].transpose(0, 2, 1, 3).reshape(DEC_BATCH, PAST_LEN, 128).astype(BF16)
        ck_c = cache_k_diff[:, l].astype(BF16)
        cv_c = cache_v_diff[:, l].astype(BF16)
        cn0 = jnp.concatenate([state_mlstm_c[:, l], state_mlstm_n[:, l][..., None],
                               jnp.zeros((DEC_BATCH, 2, B_HEADS, B_HEAD_DIM, B_HEAD_DIM - 1), F32)], axis=-1)
        m0 = jnp.broadcast_to(state_mlstm_m[:, l][..., None, None], (DEC_BATCH, 2, B_HEADS, 1, LANES))
        mix_ctx, outs = _chain_mixer(l, False, x_ctx, mod_all[l], tables_ctx, None, w, outs)
        mix_lat, _ = _chain_mixer(l, True, x_lat, mod_all[l], tables_lat, (ck_a, cv_a, ck_c, cv_c, cn0, m0), w, None,
                                  after=mix_ctx[3] if l > 0 else None)
        x_ctx, x_lat = _moe_layer(l, mix_ctx, mix_lat, mod_all[l], w, experts)
    x_ctx = _final_call(False, *x_ctx)
    x_lat = _final_call(True, *x_lat)
    return (x_ctx.reshape(BATCH, SEQ, D_MODEL), x_lat.reshape(DEC_BATCH, DEC_SEQ, D_MODEL)) + tuple(outs)
```

```python
import functools
import math

import numpy as np
import jax
import jax.numpy as jnp
from jax import lax
from jax.experimental import pallas as pl
from jax.experimental.pallas import tpu as pltpu
from jax.experimental.pallas import tpu_sc as plsc

F32 = jnp.float32
BF16 = jnp.bfloat16
I32 = jnp.int32

D_MODEL = 1024
BATCH = 16
SEQ = 256
DEPTH = 2
DEC_BATCH = 2
DEC_SEQ = 4096
PAST_LEN = 512
GRID_W = 64
ROPE_BASE = 10000.0
EPS = 1e-6
A_HEADS = 8
A_KV_HEADS = 2
A_HEAD_DIM = 64
B_HEADS = 4
B_HEAD_DIM = 128
MLSTM_CHUNK = 128
C_HEADS = 4
C_HEAD_DIM = 64
C_V_DIM = 128
N_EXPERTS = 32
TOP_K = 4
D_EXPERT = 1024
SWIGLU_LIMIT = 7.0
SWIGLU_ALPHA = 1.702
N_MOD = 6
DEEPNORM_ALPHA = (2 * DEPTH) ** 0.25

N_CTX = BATCH * SEQ
N_LAT = DEC_BATCH * DEC_SEQ
N_GROUPS = 1 + DEC_BATCH

LANES = 128
TM = 256
TQ = 512
KEY_BLOCK = 1024
TM_MERGE = 512
MERGE_ROWS = 128
TME = 1024
MOE_ROWS = 256
TILE_LANES = 128
ROUTE_CHUNK = 512
D_PACK = D_MODEL // 2
SC_WINDOW = 64
VMEM_LIMIT = 56 * 1024 * 1024

QA_HEAD_ORDER = (0, 4, 1, 5, 2, 6, 3, 7)
D_IN = 7440
D_IN_PAD = 7552
P_QA, P_KA, P_VA, P_QB, P_KB, P_VB, P_OB, P_QC, P_KC, P_VC, P_G, P_IF = (
    0, 512, 640, 768, 1280, 1792, 2304, 2816, 3328, 3840, 4352, 7424)


def _max_tiles(n_tok):
    return (n_tok * TOP_K) // TME + N_EXPERTS


def _cparams(sem):
    return pltpu.CompilerParams(dimension_semantics=sem, vmem_limit_bytes=VMEM_LIMIT)


def _split3(a):
    a1 = a.astype(BF16)
    r = a - a1.astype(F32)
    a2 = r.astype(BF16)
    a3 = (r - a2.astype(F32)).astype(BF16)
    return a1, a2, a3


def _dot(a, b, dims=None):
    if dims is None:
        return jnp.dot(a, b, preferred_element_type=F32)
    return lax.dot_general(a, b, (dims, ((), ())), preferred_element_type=F32)


def _dot_exact_rhs01(a, m, dims=None):
    return sum(_dot(p, m, dims) for p in _split3(a))


def _dot_exact_lhs01(m, a, dims=None):
    return sum(_dot(m, p, dims) for p in _split3(a))


def _pack_rows(x):
    xb = x.astype(BF16).astype(F32)
    lo = pltpu.bitcast(xb[:, :D_PACK], I32)
    lo = lax.shift_right_logical(lo, jnp.full_like(lo, 16))
    hi = pltpu.bitcast(xb[:, D_PACK:], I32) & jnp.int32(-65536)
    return lo | hi


def _unpack_rows(w):
    return (pltpu.bitcast(lax.shift_left(w, jnp.full_like(w, 16)), F32),
            pltpu.bitcast(w & jnp.int32(-65536), F32))


def _ln(x):
    mu = jnp.mean(x, axis=-1, keepdims=True)
    xc = x - mu
    var = jnp.mean(xc * xc, axis=-1, keepdims=True)
    return xc * lax.rsqrt(var + EPS)


def _log_sigmoid(x):
    return jnp.minimum(x, 0.0) - jnp.log(1.0 + jnp.exp(-jnp.abs(x)))


def _rope(x, cos, sin):
    lane = lax.broadcasted_iota(I32, (1, LANES), 1)
    first = (lane % 32) < 16
    outs = []
    for j in range(x.shape[1] // LANES):
        xj = x[:, j * LANES:(j + 1) * LANES]
        partner = jnp.where(first, pltpu.roll(xj, LANES - 16, 1), pltpu.roll(xj, 16, 1))
        outs.append(xj * cos + partner * sin)
    return outs[0] if len(outs) == 1 else jnp.concatenate(outs, axis=1)


def _mod_spec(lat, tile=TM):
    if lat:
        return pl.BlockSpec((1, N_MOD, D_MODEL), lambda i: (1 + i * tile // DEC_SEQ, 0, 0))
    return pl.BlockSpec((1, N_MOD, D_MODEL), lambda i: (0, 0, 0))


def _mod_kernel(c_ref, w_ref, b_ref, o_ref):
    c = c_ref[...]
    s = c * jax.nn.sigmoid(c)
    o_ref[...] = _dot(s.astype(BF16), w_ref[...].astype(BF16)) + b_ref[...]


def _mod_call(cvecs, w_mod, b_mod):
    tn = 1536
    n = N_MOD * D_MODEL
    return pl.pallas_call(
        _mod_kernel,
        grid=(DEPTH, n // tn),
        in_specs=[pl.BlockSpec((8, D_MODEL), lambda l, j: (0, 0)),
                  pl.BlockSpec((None, D_MODEL, tn), lambda l, j: (l, 0, j)),
                  pl.BlockSpec((None, 1, tn), lambda l, j: (l, 0, j))],
        out_specs=pl.BlockSpec((None, 8, tn), lambda l, j: (l, 0, j)),
        out_shape=jax.ShapeDtypeStruct((DEPTH, 8, n), F32),
        compiler_params=_cparams(("arbitrary", "arbitrary")),
        name="mod",
    )(cvecs, w_mod, b_mod.reshape(DEPTH, 1, n))


def _moe_residual_norm(yg_ref, tw_ref, x1_ref, mod_ref, ln_g_ref, ln_b_ref):
    tw = tw_ref[...]
    halves = [_unpack_rows(yg_ref[k]) for k in range(TOP_K)]
    moe = jnp.concatenate([sum(tw[:, k:k + 1] * halves[k][j] for k in range(TOP_K)) for j in range(2)], axis=1)
    gate2 = mod_ref[0, 5:6, :]
    return _ln(DEEPNORM_ALPHA * x1_ref[...] + gate2 * moe) * ln_g_ref[...] + ln_b_ref[...]


def _inproj_kernel(with_cache, fused, n_unread, *refs):
    if fused:
        x = _moe_residual_norm(*refs[:6])
        refs = refs[6:]
    else:
        x = refs[0][...]
        refs = refs[1:]
    mod_ref, cos_ref, sin_ref, w_ref, b_ref, wgt_ref, bgt_ref, gq_ref, gk_ref, gmat_ref = refs[:10]
    outs = refs[10 + n_unread:]
    qa_o, kab_o, vab_o, qb_o, kb_o, vb_o, ob_o, qc_o, kcb_o, vcb_o, g_o, ifg_o, ifgt_o = outs[:13]
    if fused:
        outs[-1][...] = x
    shift = mod_ref[0, 0:1, :]
    scale = mod_ref[0, 1:2, :]
    ub = (_ln(x) * (1.0 + scale) + shift).astype(BF16)
    cos = cos_ref[...]
    sin = sin_ref[...]

    def proj(lo, n):
        return _dot(ub, w_ref[lo:lo + n, :], ((1,), (1,))) + b_ref[:, lo:lo + n]

    def headnorm(z, gmat, gain):
        ss = _dot((z * z).astype(BF16), gmat)
        return z * lax.rsqrt(ss * (1.0 / A_HEAD_DIM) + EPS) * gain

    segs = [(P_QA, 512), (P_KA, 128), (P_VA, 128), (P_QB, 512), (P_KB, 512), (P_VB, 512), (P_OB, 512),
            (P_QC, 512), (P_KC, 512), (P_VC, 512), (P_G, 1024), (P_G + 1024, 1024), (P_G + 2048, 1024),
            (P_IF, LANES)]
    keep = {}

    def epilogue(k, z):
        if k == 0:
            z = headnorm(z, gmat_ref[...], gq_ref[...])
            qa_o[...] = (_rope(z, cos, sin) * 0.125).astype(BF16)
        elif k == 1:
            keep["k"] = _rope(headnorm(z, gmat_ref[0:128, 0:128], gk_ref[...]), cos, sin)
            kab_o[...] = keep["k"].astype(BF16)
        elif k == 2:
            keep["v"] = z
            vab_o[...] = z.astype(BF16)
        elif k == 3:
            qb_o[...] = z.astype(BF16)
        elif k == 4:
            kb_o[...] = (z * (B_HEAD_DIM ** -0.5)).astype(BF16)
        elif k == 5:
            vb_o[...] = z.astype(BF16)
        elif k == 6:
            ob_o[...] = z.astype(BF16)
        elif k == 7:
            qc_o[...] = (_rope(z, cos, sin) * 0.125).astype(BF16)
        elif k == 8:
            keep["kc"] = _rope(z, cos, sin)
            kcb_o[...] = keep["kc"].astype(BF16)
        elif k == 9:
            keep["vc"] = z
            vcb_o[...] = z.astype(BF16)
        elif k in (10, 11, 12):
            g_o[:, (k - 10) * 1024:(k - 9) * 1024] = jax.nn.sigmoid(z).astype(BF16)
        else:
            lane = lax.broadcasted_iota(I32, (1, LANES), 1)
            ifg_o[...] = jnp.where(lane >= 8, _log_sigmoid(z), z)

    z_prev = proj(*segs[0])
    for k in range(1, len(segs)):
        z_next = proj(*segs[k])
        epilogue(k - 1, z_prev)
        z_prev = z_next
    zt = _dot(wgt_ref[...], ub, ((1,), (1,))) + bgt_ref[...]
    epilogue(len(segs) - 1, z_prev)
    row = lax.broadcasted_iota(I32, (16, 1), 0)
    ifgt_o[...] = jnp.where(row >= 8, _log_sigmoid(zt), zt)

    if with_cache:
        nka_o, nva_o, nkc_o, nvc_o = outs[13:17]
        for h in range(A_KV_HEADS):
            nka_o[h] = keep["k"][:, h * A_HEAD_DIM:(h + 1) * A_HEAD_DIM]
            nva_o[h] = keep["v"][:, h * A_HEAD_DIM:(h + 1) * A_HEAD_DIM]
        for h in range(C_HEADS):
            nkc_o[h] = keep["kc"][:, h * LANES:(h + 1) * LANES]
            nvc_o[h] = keep["vc"][:, h * LANES:(h + 1) * LANES]


def _inproj_call(l, lat, x, mod, cos_t, sin_t, w_p, b_p, w_gt, b_gt, gq, gk, gmat, cache_outs, after=None):
    fused = isinstance(x, tuple)
    n = x[2].shape[0] if fused else x.shape[0]
    row = lambda w: pl.BlockSpec((TM, w), lambda i: (i, 0))
    const = lambda shape: pl.BlockSpec(shape, lambda i: (0,) * len(shape), pipeline_mode=pl.Buffered(1))
    sds = lambda w, dt: jax.ShapeDtypeStruct((n, w), dt)
    out_defs = [(512, BF16), (128, BF16), (128, BF16), (512, BF16), (512, BF16), (512, BF16), (512, BF16),
                (512, BF16), (512, BF16), (512, BF16), (3072, BF16), (LANES, F32)]
    out_specs = [row(w) for w, _ in out_defs] + [pl.BlockSpec((16, TM), lambda i: (0, i))]
    out_shape = [sds(w, dt) for w, dt in out_defs] + [jax.ShapeDtypeStruct((16, n), F32)]
    alias_in, aliases = [], {}
    if lat:
        tbl = pl.BlockSpec((TM, LANES), lambda i: (i % (DEC_SEQ // TM), 0))
    else:
        assert TM == SEQ
        tbl = pl.BlockSpec((TM, LANES), lambda i: (0, 0))
        cache_defs = [(A_KV_HEADS, A_HEAD_DIM), (A_KV_HEADS, A_HEAD_DIM), (C_HEADS, LANES), (C_HEADS, LANES)]
        out_specs += [pl.BlockSpec((None, None, nh, SEQ, d), lambda i: (i, l, 0, 0, 0)) for nh, d in cache_defs]
        out_shape += [jax.ShapeDtypeStruct((BATCH, DEPTH, nh, SEQ, d), F32) for nh, d in cache_defs]
        if cache_outs is not None:
            alias_in = list(cache_outs)
    if fused:
        plain = lambda shape: pl.BlockSpec(shape, lambda i: (0,) * len(shape))
        lead_specs = [pl.BlockSpec((TOP_K, TM, D_PACK), lambda i: (0, i, 0)), row(TOP_K), row(D_MODEL),
                      _mod_spec(lat), plain((1, D_MODEL)), plain((1, D_MODEL))]
        lead = list(x)
        out_specs.append(row(D_MODEL))
        out_shape.append(sds(D_MODEL, F32))
    else:
        lead_specs, lead = [row(D_MODEL)], [x]
    n_in = len(lead) + 10
    aliases = {n_in + j: len(out_defs) + 1 + j for j in range(len(alias_in))}
    unread = alias_in + ([] if after is None else [after])
    return pl.pallas_call(
        functools.partial(_inproj_kernel, not lat, fused, len(unread)),
        grid=(n // TM,),
        in_specs=lead_specs + [_mod_spec(lat), tbl, tbl,
                               const((D_IN_PAD, D_MODEL)), const((1, D_IN_PAD)),
                               const((16, D_MODEL)), const((16, 1)),
                               const((1, 512)), const((1, 128)), const((512, 512))]
        + [pl.BlockSpec(memory_space=pl.ANY)] * len(unread),
        out_specs=out_specs, out_shape=out_shape, input_output_aliases=aliases,
        compiler_params=_cparams(("arbitrary",)),
        name="inproj_lat" if lat else "inproj_ctx",
    )(*lead, mod, cos_t, sin_t, w_p, b_p, w_gt, b_gt, gq, gk, gmat, *unread)


def _attn_kernel(mode, nseg, lam_init, *refs):
    q_ref = refs[0]
    kv = refs[1:1 + 2 * nseg]
    rest = refs[1 + 2 * nseg:]
    o_ref = rest[-1]
    tq = q_ref.shape[0]
    lane = lax.broadcasted_iota(I32, (1, LANES), 1)
    if mode == "c":
        lam_ref, sn_ref = rest[0], rest[1]
        lp = lam_ref[...]
        lam = (jnp.exp(jnp.sum(lp[0:1] * lp[1:2], axis=-1, keepdims=True))
               - jnp.exp(jnp.sum(lp[2:3] * lp[3:4], axis=-1, keepdims=True)) + lam_init)
    for blk in range(q_ref.shape[1] // LANES):
        qs = slice(blk * LANES, (blk + 1) * LANES)
        ks = qs if kv[0].shape[1] > LANES else slice(0, LANES)
        q = q_ref[:, qs]
        zero = jnp.zeros_like(q)
        qq = jnp.concatenate([jnp.where(lane < 64, q, zero), jnp.where(lane >= 64, q, zero)], axis=0)
        m = l = acc = None
        for j in range(nseg):
            k_ref, v_ref = kv[2 * j], kv[2 * j + 1]
            for off in range(0, k_ref.shape[0], KEY_BLOCK):
                n = min(KEY_BLOCK, k_ref.shape[0] - off)
                s = _dot(qq, k_ref[off:off + n, ks], ((1,), (1,)))
                bm = jnp.max(s, axis=-1, keepdims=True)
                if m is None:
                    m = bm
                    e = jnp.exp(s - m)
                    l = jnp.sum(e, axis=-1, keepdims=True)
                    acc = _dot(e.astype(BF16), v_ref[off:off + n, ks])
                else:
                    m_new = jnp.maximum(m, bm)
                    alpha = jnp.exp(m - m_new)
                    e = jnp.exp(s - m_new)
                    l = alpha * l + jnp.sum(e, axis=-1, keepdims=True)
                    acc = alpha * acc + _dot(e.astype(BF16), v_ref[off:off + n, ks])
                    m = m_new
        o = acc * (1.0 / l)
        if mode == "a":
            out = jnp.where(lane < 64, o[:tq], o[tq:])
        else:
            o = o[:tq] - lam * o[tq:]
            ms = jnp.mean(o * o, axis=-1, keepdims=True)
            out = o * lax.rsqrt(ms + EPS) * sn_ref[...] * (1.0 - lam_init)
        o_ref[:, qs] = out.astype(o_ref.dtype)


def _attn_call(mode, lat, lam_init, q_all, k_all, v_all, k_cache, v_cache, lam_c, sn_c):
    wide = k_all.shape[1] == 512
    kcol = (lambda h: h) if wide else (lambda h: 0)
    if lat:
        nq = DEC_SEQ // TQ
        grid = (DEC_BATCH, nq, 4)
        q_spec = pl.BlockSpec((TQ, LANES), lambda b, i, h: (b * nq + i, h))
        kn_spec = pl.BlockSpec((DEC_SEQ, LANES), lambda b, i, h: (b, kcol(h)))
        if wide:
            kc_spec = pl.BlockSpec((None, None, PAST_LEN, LANES), lambda b, i, h: (b, h, 0, 0))
        else:
            kc_spec = pl.BlockSpec((None, PAST_LEN, LANES), lambda b, i, h: (b, 0, 0))
        in_specs = [q_spec, kc_spec, kc_spec, kn_spec, kn_spec]
        args = [q_all, k_cache, v_cache, k_all, v_all]
        nseg = 2
        o_spec = q_spec
        sem = ("arbitrary", "arbitrary", "arbitrary")
        zero_map = lambda b, i, h: (0, 0)
    else:
        grid = (BATCH,)
        q_spec = pl.BlockSpec((SEQ, 512), lambda b: (b, 0))
        k_spec = pl.BlockSpec((SEQ, k_all.shape[1]), lambda b: (b, 0))
        in_specs = [q_spec, k_spec, k_spec]
        args = [q_all, k_all, v_all]
        nseg = 1
        o_spec = q_spec
        sem = ("arbitrary",)
        zero_map = lambda b: (0, 0)
    if mode == "c":
        in_specs += [pl.BlockSpec((4, C_HEAD_DIM), zero_map), pl.BlockSpec((1, C_V_DIM), zero_map)]
        args += [lam_c, sn_c]
    return pl.pallas_call(
        functools.partial(_attn_kernel, mode, nseg, lam_init),
        grid=grid, in_specs=in_specs, out_specs=o_spec,
        out_shape=jax.ShapeDtypeStruct((q_all.shape[0], 512), BF16),
        compiler_params=_cparams(sem), name=f"attn_{mode}_{'lat' if lat else 'ctx'}",
    )(*args)


def _mlstm_kernel(has_init, want_state, n_alias, *refs):
    tok = (refs[0:5], refs[5:10])
    pos = 10
    if has_init:
        cn0_ref, m0_ref = refs[pos:pos + 2]
        pos += 2
    pos += n_alias
    h_refs = refs[pos:pos + 2]
    pos += 2
    if want_state:
        c_o, n_o, m_o = refs[pos:pos + 3]
        pos += 3
    cn_s, m_s = refs[pos:pos + 2]

    c = pl.program_id(1)
    last = pl.num_programs(1) - 1
    L = MLSTM_CHUNK

    @pl.when(c == 0)
    def _():
        if has_init:
            cn_s[...] = cn0_ref[...]
            m_s[...] = m0_ref[...]
        else:
            cn_s[...] = jnp.zeros_like(cn_s)
            m_s[...] = jnp.zeros_like(m_s)

    row = lax.broadcasted_iota(I32, (L, L), 0)
    col = lax.broadcasted_iota(I32, (L, L), 1)
    lane = lax.broadcasted_iota(I32, (1, LANES), 1)
    ones_col = jnp.where(lane == 0, 1.0, 0.0).astype(BF16)

    chains = []
    for d in range(2):
        q_ref, k_ref, v_ref, g_ref, gt_ref = tok[d]
        keep = (col <= row) if d == 0 else (col >= row)
        m_c = keep.astype(BF16)
        m_r = ((row <= col) if d == 0 else (row >= col)).astype(BF16)
        g = g_ref[...]
        gt = gt_ref[...]
        bc = _dot_exact_lhs01(m_c, g)
        br = _dot_exact_rhs01(gt, m_r)
        for h in range(B_HEADS):
            sl = slice(h * L, (h + 1) * L)
            gi, gf = 4 * d + h, 8 + 4 * d + h
            ch = dict(d=d, h=h, sl=sl, keep=keep, q=q_ref[:, sl], k=k_ref[:, sl], v=v_ref[:, sl],
                      i_col=g[:, gi:gi + 1], f_col=g[:, gf:gf + 1], b_col=bc[:, gf:gf + 1],
                      r_row=gt[gi:gi + 1, :] - br[gf:gf + 1, :],
                      m_prev=m_s[d, h][:, 0:1], cn_prev=cn_s[d, h])
            chains.append(ch)
    for ch in chains:
        ch["s"] = _dot(ch["q"], ch["k"], ((1,), (1,)))
        ch["qc"] = _dot(ch["q"], ch["cn_prev"].astype(BF16))
    for ch in chains:
        ch["total"] = jnp.sum(ch["f_col"], axis=0, keepdims=True)
        ch["inter"] = ch["b_col"] + ch["m_prev"]
        ch["v_aug"] = jnp.concatenate([ch["v"], jnp.broadcast_to(ones_col, (L, LANES))], axis=1)
    for ch in chains:
        ch["rmat"] = jnp.where(ch["keep"], ch["r_row"], -jnp.inf)
    for ch in chains:
        ch["dmax"] = ch["b_col"] + jnp.max(ch["rmat"], axis=-1, keepdims=True)
    for ch in chains:
        ch["m_row"] = jnp.maximum(ch["inter"], ch["dmax"])
    for ch in chains:
        ch["sc"] = (ch["s"] * jnp.exp(ch["rmat"] + (ch["b_col"] - ch["m_row"]))).astype(BF16)
    for ch in chains:
        ch["a"] = _dot(ch["sc"], ch["v_aug"])
    for ch in chains:
        ch["nd"] = jnp.exp(ch["inter"] - ch["m_row"]) * ch["qc"] + ch["a"]
    for ch in chains:
        ch["scale"] = 1.0 / jnp.maximum(jnp.abs(ch["nd"][:, L:L + 1]), jnp.exp(-ch["m_row"]))
    for ch in chains:
        h_refs[ch["d"]][:, ch["sl"]] = ch["nd"][:, :L] * ch["scale"]
    for ch in chains:
        ch["wel"] = ch["total"] - ch["b_col"] + ch["i_col"]
        ch["m_new"] = jnp.maximum(ch["total"] + ch["m_prev"], jnp.max(ch["wel"], axis=0, keepdims=True))
    for ch in chains:
        ch["kw"] = (jnp.exp(ch["wel"] - ch["m_new"]) * ch["k"].astype(F32)).astype(BF16)
    for ch in chains:
        ch["u"] = _dot(ch["kw"], ch["v_aug"], ((0,), (0,)))
    for ch in chains:
        cn_s[ch["d"], ch["h"]] = jnp.exp(ch["total"] + ch["m_prev"] - ch["m_new"]) * ch["cn_prev"] + ch["u"]
        m_s[ch["d"], ch["h"]] = jnp.broadcast_to(ch["m_new"], (1, LANES))

    if want_state:
        @pl.when(c == last)
        def _():
            pick = ((lax.broadcasted_iota(I32, (8, 2 * L), 1) == L)
                    & (lax.broadcasted_iota(I32, (8, 2 * L), 0) == 0)).astype(BF16)
            for d in range(2):
                for h in range(B_HEADS):
                    cn = cn_s[d, h]
                    c_o[d, h] = cn[:, :L]
                    n_o[d, h:h + 1, :] = _dot_exact_lhs01(pick, cn, ((1,), (1,)))[0:1]
            m_o[...] = m_s[...]


def _mlstm_call(l, lat, q_all, k_all, v_all, g_all, gt_all, cn0, m0, state_bufs):
    L = MLSTM_CHUNK
    n = q_all.shape[0]
    nb, nc = (DEC_BATCH, DEC_SEQ // L) if lat else (BATCH, SEQ // L)
    chunk = (lambda b, c: b * nc + c, lambda b, c: b * nc + (nc - 1 - c))

    in_specs, args = [], []
    for d in range(2):
        tok = lambda w, d=d: pl.BlockSpec((L, w), lambda b, c: (chunk[d](b, c), 0))
        in_specs += [tok(512), tok(512), tok(512), tok(LANES),
                     pl.BlockSpec((16, L), lambda b, c, d=d: (0, chunk[d](b, c)))]
        args += [q_all, k_all, v_all, g_all, gt_all]
    if lat:
        in_specs += [pl.BlockSpec((None, 2, B_HEADS, L, 2 * L), lambda b, c: (b, 0, 0, 0, 0)),
                     pl.BlockSpec((None, 2, B_HEADS, 1, LANES), lambda b, c: (b, 0, 0, 0, 0))]
        args += [cn0, m0]
    aliases = {}
    alias_in = list(state_bufs) if state_bufs is not None else []
    if alias_in:
        aliases = {len(args): 2, len(args) + 1: 3}
    in_specs += [pl.BlockSpec(memory_space=pl.ANY)] * len(alias_in)
    args += alias_in
    out_specs = [pl.BlockSpec((L, 512), lambda b, c, d=d: (chunk[d](b, c), 0)) for d in range(2)]
    out_shape = [jax.ShapeDtypeStruct((n, 512), F32)] * 2
    if not lat:
        out_specs += [pl.BlockSpec((None, None, 2, B_HEADS, L, L), lambda b, c: (b, l, 0, 0, 0, 0)),
                      pl.BlockSpec((None, None, 2, B_HEADS, L), lambda b, c: (b, l, 0, 0, 0)),
                      pl.BlockSpec((None, 2, B_HEADS, 1, LANES), lambda b, c: (b, 0, 0, 0, 0))]
        out_shape += [jax.ShapeDtypeStruct((BATCH, DEPTH, 2, B_HEADS, L, L), F32),
                      jax.ShapeDtypeStruct((BATCH, DEPTH, 2, B_HEADS, L), F32),
                      jax.ShapeDtypeStruct((nb, 2, B_HEADS, 1, LANES), F32)]
    return pl.pallas_call(
        functools.partial(_mlstm_kernel, lat, not lat, len(alias_in)),
        grid=(nb, nc), in_specs=in_specs, out_specs=out_specs, out_shape=out_shape,
        input_output_aliases=aliases,
        scratch_shapes=[pltpu.VMEM((2, B_HEADS, L, 2 * L), F32), pltpu.VMEM((2, B_HEADS, 1, LANES), F32)],
        compiler_params=_cparams(("arbitrary", "arbitrary")), name="mlstm_lat" if lat else "mlstm_ctx",
    )(*args)


def _merge_kernel(x_ref, mod_ref, oa_ref, hf_ref, hb_ref, ob_ref, oc_ref, g_ref, hn_ref,
                  wpa_ref, wpb_ref, wpc_ref, wo_ref, bo_ref, ln_g_ref, ln_b_ref, wr_ref, br_ref,
                  x1_o, u2_o, ti_o, tw_o):
    gate1 = mod_ref[0, 2:3, :]
    shift2 = mod_ref[0, 3:4, :]
    scale2 = mod_ref[0, 4:5, :]
    wr = _split3(wr_ref[...])
    eio = lax.broadcasted_iota(I32, (N_EXPERTS, 1), 0)
    nt = ((1,), (1,))
    cs = [dict(rs=slice(r, r + MERGE_ROWS)) for r in range(0, TM_MERGE, MERGE_ROWS)]
    for c in cs:
        c["pa"] = _dot(oa_ref[c["rs"], :], wpa_ref[...])
        c["pc"] = _dot(oc_ref[c["rs"], :], wpc_ref[...])
    for c in cs:
        hm = hf_ref[c["rs"], :] + hb_ref[c["rs"], :]
        parts = []
        for h in range(B_HEADS):
            hh = hm[:, h * LANES:(h + 1) * LANES]
            parts.append(hh * lax.rsqrt(jnp.mean(hh * hh, axis=-1, keepdims=True) + EPS))
        c["out_b"] = (jnp.concatenate(parts, axis=1) * hn_ref[...]
                      * jax.nn.sigmoid(ob_ref[c["rs"], :].astype(F32))).astype(BF16)
    for c in cs:
        c["pb"] = _dot(c["out_b"], wpb_ref[...])
    for c in cs:
        rs = c["rs"]
        c["merged"] = (g_ref[rs, 0:1024].astype(F32) * c["pa"] + g_ref[rs, 1024:2048].astype(F32) * c["pb"]
                       + g_ref[rs, 2048:3072].astype(F32) * c["pc"]).astype(BF16)
    for c in cs:
        c["y"] = _dot(c["merged"], wo_ref[...]) + bo_ref[...]
    for c in cs:
        c["x1"] = _ln(DEEPNORM_ALPHA * x_ref[c["rs"], :] + gate1 * c["y"]) * ln_g_ref[...] + ln_b_ref[...]
        x1_o[c["rs"], :] = c["x1"]
    for c in cs:
        c["u2"] = _ln(c["x1"]) * (1.0 + scale2) + shift2
        u2_o[c["rs"], :] = _pack_rows(c["u2"])
    for c in cs:
        up = _split3(c["u2"])
        c["cur"] = (_dot(wr[0], up[0], nt) + _dot(wr[0], up[1], nt) + _dot(wr[1], up[0], nt)
                    + _dot(wr[0], up[2], nt) + _dot(wr[2], up[0], nt) + _dot(wr[1], up[1], nt)) + br_ref[...]
        c["vals"], c["idxs"] = [], []
    for _ in range(TOP_K):
        for c in cs:
            mx = jnp.max(c["cur"], axis=0, keepdims=True)
            ix = jnp.min(jnp.where(c["cur"] == mx, eio, N_EXPERTS), axis=0, keepdims=True)
            c["vals"].append(mx)
            c["idxs"].append(ix)
            c["cur"] = jnp.where(eio == ix, -jnp.inf, c["cur"])
    for c in cs:
        ex = [jnp.exp(v - c["vals"][0]) for v in c["vals"]]
        inv = 1.0 / sum(ex)
        ti_o[:, c["rs"]] = jnp.concatenate(c["idxs"], axis=0)
        tw_o[:, c["rs"]] = jnp.concatenate([e * inv for e in ex], axis=0)


def _merge_call(lat, x, mod, out_a, hf, hb, ob, out_c, g, hn, wpa, wpb, wpc, wo, bo, ln_g, ln_b, wr_t, br):
    n = x.shape[0]
    tm = TM_MERGE
    row = lambda w: pl.BlockSpec((tm, w), lambda i: (i, 0))
    const = lambda shape: pl.BlockSpec(shape, lambda i: (0,) * len(shape), pipeline_mode=pl.Buffered(1))
    return pl.pallas_call(
        _merge_kernel,
        grid=(n // tm,),
        in_specs=[row(D_MODEL), _mod_spec(lat, tm),
                  row(512), row(512), row(512), row(512), row(512), row(3072),
                  const((1, 512)), const((512, D_MODEL)), const((512, D_MODEL)), const((512, D_MODEL)),
                  const((D_MODEL, D_MODEL)), const((1, D_MODEL)), const((1, D_MODEL)), const((1, D_MODEL)),
                  const((N_EXPERTS, D_MODEL)), const((N_EXPERTS, 1))],
        out_specs=[row(D_MODEL), row(D_PACK),
                   pl.BlockSpec((TOP_K, tm), lambda i: (0, i)), pl.BlockSpec((TOP_K, tm), lambda i: (0, i))],
        out_shape=[jax.ShapeDtypeStruct((n, D_MODEL), F32), jax.ShapeDtypeStruct((n, D_PACK), I32),
                   jax.ShapeDtypeStruct((TOP_K, n), I32), jax.ShapeDtypeStruct((TOP_K, n), F32)],
        compiler_params=_cparams(("arbitrary",)), name="merge_lat" if lat else "merge_ctx",
    )(x, mod, out_a, hf, hb, ob, out_c, g, hn, wpa, wpb, wpc, wo, bo, ln_g, ln_b, wr_t, br)


def _route_kernel(n_first, ti_ref, pos_o, te_o, tv_o, ts_o, nt_o, rank_s):
    ch = ROUTE_CHUNK
    n_tok = ti_ref.shape[1]
    eio = lax.broadcasted_iota(I32, (N_EXPERTS, 1), 0)
    tri = (lax.broadcasted_iota(I32, (ch, ch), 0) <= lax.broadcasted_iota(I32, (ch, ch), 1)).astype(BF16)

    def onehots(off):
        ti = ti_ref[:, pl.ds(off, ch)]
        return [(eio == ti[k:k + 1, :]).astype(F32) for k in range(TOP_K)]

    def count_body(c, carry):
        off = pl.multiple_of(c * ch, ch)
        ohs = onehots(off)
        oh = ohs[0] + ohs[1] + ohs[2] + ohs[3]
        incl = _dot(oh.astype(BF16), tri)
        excl = incl - oh + carry
        for k in range(TOP_K):
            rank_s[k:k + 1, pl.ds(off, ch)] = jnp.sum(ohs[k] * excl, axis=0, keepdims=True)
        return carry + incl[:, ch - 1:ch]

    counts_first = lax.fori_loop(0, n_first // ch, count_body, jnp.zeros((N_EXPERTS, 1), F32))
    counts = lax.fori_loop(n_first // ch, n_tok // ch, count_body, counts_first)
    ntile_e = jnp.floor((counts + (TME - 1)) * (1.0 / TME))
    lower = (lax.broadcasted_iota(I32, (N_EXPERTS, N_EXPERTS), 1)
             < lax.broadcasted_iota(I32, (N_EXPERTS, N_EXPERTS), 0)).astype(BF16)
    off_t = _dot(lower, jnp.broadcast_to(ntile_e, (N_EXPERTS, LANES)).astype(BF16))[:, 0:1]
    end_t = off_t + ntile_e
    total = jnp.sum(ntile_e, axis=0, keepdims=True)
    jt = lax.broadcasted_iota(I32, (1, TILE_LANES), 1).astype(F32)
    te = jnp.minimum(jnp.sum((end_t <= jt).astype(F32), axis=0, keepdims=True), N_EXPERTS - 1.0)
    sel = (eio.astype(F32) == te).astype(F32)
    cnt_j = jnp.sum(sel * counts, axis=0, keepdims=True)
    off_j = jnp.sum(sel * off_t, axis=0, keepdims=True)
    valid = jnp.clip(cnt_j - (jt - off_j) * TME, 0.0, float(TME))
    first_j = jnp.sum(sel * counts_first, axis=0, keepdims=True)
    te_o[...] = te.astype(I32)
    tv_o[...] = jnp.where(jt < total, valid, 0.0).astype(I32)
    ts_o[...] = jnp.clip(first_j - (jt - off_j) * TME, 0.0, float(TME)).astype(I32)
    nt_o[...] = jnp.broadcast_to(total, (1, LANES)).astype(I32)
    off_rows = off_t * TME

    def pos_body(c, carry):
        off = pl.multiple_of(c * ch, ch)
        ohs = onehots(off)
        for k in range(TOP_K):
            base = jnp.sum(ohs[k] * off_rows, axis=0, keepdims=True)
            pos_o[k:k + 1, pl.ds(off, ch)] = (rank_s[k:k + 1, pl.ds(off, ch)] + base).astype(I32)
        return carry

    lax.fori_loop(0, n_tok // ch, pos_body, 0)


def _route_call(top_i, n_first):
    n = top_i.shape[1]
    assert _max_tiles(n) <= TILE_LANES and n_first % ROUTE_CHUNK == 0
    tiles = jax.ShapeDtypeStruct((1, TILE_LANES), I32)
    return pl.pallas_call(
        functools.partial(_route_kernel, n_first),
        out_shape=[jax.ShapeDtypeStruct((TOP_K, n), I32), tiles, tiles, tiles,
                   jax.ShapeDtypeStruct((1, LANES), I32)],
        scratch_shapes=[pltpu.VMEM((TOP_K, n), F32)],
        compiler_params=pltpu.CompilerParams(vmem_limit_bytes=VMEM_LIMIT), name="route",
    )(top_i)


def _sc_mesh():
    return plsc.VectorSubcoreMesh(core_axis_name="c", subcore_axis_name="s")


def _sc_worker(n_rows):
    info = plsc.get_sparse_core_info()
    n_workers = info.num_cores * info.num_subcores
    per_w = n_rows // n_workers
    assert per_w * n_workers == n_rows and per_w % SC_WINDOW == 0
    wid = lax.axis_index("s") * info.num_cores + lax.axis_index("c")
    return wid * per_w, per_w


def _dispatch_call(u2, pos_flat, p_rows):
    w = SC_WINDOW
    n = u2.shape[0]

    @functools.partial(pl.kernel, out_type=jax.ShapeDtypeStruct((p_rows, u2.shape[1]), u2.dtype), mesh=_sc_mesh(),
                       scratch_types=[pltpu.VMEM((w,), I32)] * (2 * TOP_K)
                       + [pltpu.VMEM((w, u2.shape[1]), u2.dtype)] * 2 + [pltpu.SemaphoreType.DMA] * 4)
    def k(x_hbm, p_hbm, o_hbm, *scratch):
        idx = (scratch[0:TOP_K], scratch[TOP_K:2 * TOP_K])
        rows = scratch[2 * TOP_K:2 * TOP_K + 2]
        sem_load = scratch[2 * TOP_K + 2:2 * TOP_K + 4]
        sem_scat = scratch[2 * TOP_K + 4:2 * TOP_K + 6]
        base, per_w = _sc_worker(n)
        n_chunks = per_w // w

        def start_loads(c, s):
            off = pl.multiple_of(base + c * w, 8)
            cps = [pltpu.async_copy(x_hbm.at[pl.ds(off, w)], rows[s], sem_load[s])]
            cps += [pltpu.async_copy(p_hbm.at[pl.ds(pl.multiple_of(kk * n + off, 8), w)], idx[s][kk], sem_load[s])
                    for kk in range(TOP_K)]
            return cps

        loads = start_loads(0, 0)
        scatters = [None, None]
        for c in range(n_chunks):
            s = c % 2
            for cp in loads:
                cp.wait()
            if c + 1 < n_chunks:
                if scatters[1 - s] is not None:
                    for cp in scatters[1 - s]:
                        cp.wait()
                    scatters[1 - s] = None
                loads = start_loads(c + 1, 1 - s)
            scatters[s] = [pltpu.async_copy(rows[s], o_hbm.at[idx[s][kk]], sem_scat[s]) for kk in range(TOP_K)]
        for group in scatters:
            if group is not None:
                for cp in group:
                    cp.wait()

    return k(u2, pos_flat)


def _combine_call(ys, pos_flat):
    w = SC_WINDOW
    n = pos_flat.shape[0]

    @functools.partial(pl.kernel, out_type=jax.ShapeDtypeStruct((n, ys.shape[1]), ys.dtype), mesh=_sc_mesh(),
                       scratch_types=[pltpu.VMEM((w,), I32)] * 2 + [pltpu.VMEM((w, ys.shape[1]), ys.dtype)] * 2
                       + [pltpu.SemaphoreType.DMA] * 4)
    def k(y_hbm, p_hbm, o_hbm, idx0, idx1, rows0, rows1, sg0, sg1, ss0, ss1):
        idx, rows, sem_gather, sem_store = (idx0, idx1), (rows0, rows1), (sg0, sg1), (ss0, ss1)
        base, per_w = _sc_worker(n)
        n_chunks = per_w // w

        def offset(c):
            return pl.multiple_of(base + c * w, 8)

        pltpu.sync_copy(p_hbm.at[pl.ds(offset(0), w)], idx[0])
        gather = pltpu.async_copy(y_hbm.at[idx[0]], rows[0], sem_gather[0])
        stores = [None, None]
        for c in range(n_chunks):
            s = c % 2
            if c + 1 < n_chunks:
                pltpu.sync_copy(p_hbm.at[pl.ds(offset(c + 1), w)], idx[1 - s])
            gather.wait()
            if c + 1 < n_chunks:
                if stores[1 - s] is not None:
                    stores[1 - s].wait()
                    stores[1 - s] = None
                gather = pltpu.async_copy(y_hbm.at[idx[1 - s]], rows[1 - s], sem_gather[1 - s])
            stores[s] = pltpu.async_copy(rows[s], o_hbm.at[pl.ds(offset(c), w)], sem_store[s])
        for st in stores:
            if st is not None:
                st.wait()

    return k(ys, pos_flat)


def _moe_kernel(l, n_tiles_max, te_ref, tv_ref, ts_ref, nt_ref, xc_ref, xl_ref, wu_hbm, bu_ref, wd_hbm, bd_ref,
                y_ref, wu_f, wd_f, wu_s, wd_s, sem, group_s):
    i = pl.program_id(0)
    nt = nt_ref[0]

    def fetch(e, slot):
        return (pltpu.make_async_copy(wu_hbm.at[l, e], wu_f.at[slot], sem.at[0, slot]),
                pltpu.make_async_copy(wd_hbm.at[l, e], wd_f.at[slot], sem.at[1, slot]))

    @pl.when(i == 0)
    def _():
        group_s[0] = 0
        for cp in fetch(te_ref[0], 0):
            cp.start()

    @pl.when(i < nt)
    def _():
        e = te_ref[i]
        prev = te_ref[jnp.maximum(i - 1, 0)]

        @pl.when((i == 0) | (e != prev))
        def _():
            group = group_s[0]
            slot = group % 2
            j = lax.while_loop(lambda j: (j < nt) & (te_ref[jnp.minimum(j, n_tiles_max - 1)] == e),
                               lambda j: j + 1, i + 1)

            @pl.when(j < nt)
            def _():
                for cp in fetch(te_ref[jnp.minimum(j, n_tiles_max - 1)], 1 - slot):
                    cp.start()

            for cp in fetch(e, slot):
                cp.wait()
            wu_s[...] = wu_f[slot].astype(BF16)
            wd_s[...] = wd_f[slot].astype(BF16)
            group_s[0] = group + 1

        valid = tv_ref[i]
        split = ts_ref[i]
        half = MOE_ROWS
        for j in range(TME // half):
            @pl.when(valid > j * half)
            def _():
                sl = slice(j * half, (j + 1) * half)
                rows = lax.broadcasted_iota(I32, (half, 1), 0) + j * half
                words = jnp.where(rows < split, xc_ref[sl, :], xl_ref[sl, :])
                lo, hi = _unpack_rows(jnp.where(rows < valid, words, 0))
                x = jnp.concatenate([lo.astype(BF16), hi.astype(BF16)], axis=1)
                h = _dot(x, wu_s[...]) + bu_ref[...]
                h_glu = jnp.minimum(h[:, :D_EXPERT], SWIGLU_LIMIT)
                h_lin = jnp.clip(h[:, D_EXPERT:], -SWIGLU_LIMIT, SWIGLU_LIMIT)
                act = h_glu * jax.nn.sigmoid(SWIGLU_ALPHA * h_glu) * (h_lin + 1.0)
                y_ref[sl, :] = _pack_rows(_dot(act.astype(BF16), wd_s[...]) + bd_ref[...])


def _moe_call(l, tile_e, tile_v, tile_s, n_tiles, xs_ctx, xs_lat, w_up, b_up, w_down, b_down):
    n_tiles_max = xs_ctx.shape[0] // TME

    def tile(i, te, tv, ts, nt):
        return jnp.minimum(i, nt[0] - 1)

    def wmap(i, te, tv, ts, nt):
        return (l, te[tile(i, te, tv, ts, nt)], 0, 0)

    rows = pl.BlockSpec((TME, D_PACK), lambda i, te, tv, ts, nt: (tile(i, te, tv, ts, nt), 0))
    grid_spec = pltpu.PrefetchScalarGridSpec(
        num_scalar_prefetch=4,
        grid=(n_tiles_max,),
        in_specs=[rows, rows,
                  pl.BlockSpec(memory_space=pl.ANY),
                  pl.BlockSpec((None, None, 1, 2 * D_EXPERT), wmap),
                  pl.BlockSpec(memory_space=pl.ANY),
                  pl.BlockSpec((None, None, 1, D_MODEL), wmap)],
        out_specs=rows,
        scratch_shapes=[pltpu.VMEM((2, D_MODEL, 2 * D_EXPERT), F32), pltpu.VMEM((2, D_EXPERT, D_MODEL), F32),
                        pltpu.VMEM((D_MODEL, 2 * D_EXPERT), BF16), pltpu.VMEM((D_EXPERT, D_MODEL), BF16),
                        pltpu.SemaphoreType.DMA((2, 2)), pltpu.SMEM((1,), I32)],
    )
    return pl.pallas_call(
        functools.partial(_moe_kernel, l, n_tiles_max), grid_spec=grid_spec,
        out_shape=jax.ShapeDtypeStruct((xs_ctx.shape[0], D_PACK), I32),
        compiler_params=_cparams(("arbitrary",)), name="moe",
    )(tile_e, tile_v, tile_s, n_tiles, xs_ctx, xs_lat, w_up, b_up.reshape(DEPTH, N_EXPERTS, 1, -1),
      w_down, b_down.reshape(DEPTH, N_EXPERTS, 1, -1))


def _final_kernel(yg_ref, tw_ref, x1_ref, mod_ref, ln_g_ref, ln_b_ref, o_ref):
    o_ref[...] = _moe_residual_norm(yg_ref, tw_ref, x1_ref, mod_ref, ln_g_ref, ln_b_ref)


def _final_call(lat, yg, tw, x1, mod, ln_g, ln_b):
    n = x1.shape[0]
    tm = TM_MERGE
    row = lambda w: pl.BlockSpec((tm, w), lambda i: (i, 0))
    const = lambda shape: pl.BlockSpec(shape, lambda i: (0,) * len(shape))
    return pl.pallas_call(
        _final_kernel,
        grid=(n // tm,),
        in_specs=[pl.BlockSpec((TOP_K, tm, D_PACK), lambda i: (0, i, 0)), row(TOP_K), row(D_MODEL),
                  _mod_spec(lat, tm), const((1, D_MODEL)), const((1, D_MODEL))],
        out_specs=row(D_MODEL), out_shape=jax.ShapeDtypeStruct((n, D_MODEL), F32),
        compiler_params=_cparams(("arbitrary",)), name="final_lat" if lat else "final_ctx",
    )(yg, tw, x1, mod, ln_g, ln_b)


def _rope_tables():
    half = A_HEAD_DIM // 2
    freqs = ROPE_BASE ** (-jnp.arange(0, half, 2, dtype=F32) / half)
    t = jnp.arange(DEC_SEQ, dtype=jnp.int32)
    ang_r = (t // GRID_W).astype(F32)[:, None] * freqs[None, :]
    ang_c = (t % GRID_W).astype(F32)[:, None] * freqs[None, :]
    cos64 = jnp.concatenate([jnp.cos(ang_r)] * 2 + [jnp.cos(ang_c)] * 2, axis=1)
    sin64 = jnp.concatenate([-jnp.sin(ang_r), jnp.sin(ang_r), -jnp.sin(ang_c), jnp.sin(ang_c)], axis=1)
    lat = (jnp.tile(cos64, (1, 2)), jnp.tile(sin64, (1, 2)))
    ctx = (jnp.ones((TM, LANES), F32), jnp.zeros((TM, LANES), F32))
    return ctx, lat


def _permute_in_cols(t, axis=-1):
    take = lambda lo, hi: lax.slice_in_dim(t, lo, hi, axis=axis)
    pad_shape = list(t.shape)
    pad_shape[axis] = D_IN_PAD - D_IN
    parts = [take(h * 64, (h + 1) * 64) for h in QA_HEAD_ORDER]
    parts += [take(512, 2816), take(2832, 7440), take(2816, 2832), jnp.zeros(pad_shape, t.dtype)]
    return jnp.concatenate(parts, axis=axis)


def _layer_weights(p):
    w_in, b_in = p["w_in"], p["b_in"]
    wpa = p["w_pa"].reshape(A_HEADS, A_HEAD_DIM, D_MODEL)[np.asarray(QA_HEAD_ORDER)].reshape(512, D_MODEL)
    row = lambda t: t.reshape(1, -1)
    return dict(
        inproj=(_permute_in_cols(w_in.T, axis=0).astype(BF16), _permute_in_cols(b_in).reshape(1, D_IN_PAD),
                w_in[:, 2816:2832].T.astype(BF16), b_in[2816:2832].reshape(16, 1),
                row(jnp.tile(p["qn_a"], A_HEADS)), row(jnp.tile(p["kn_a"], A_KV_HEADS)),
                jnp.asarray(np.kron(np.eye(8), np.ones((64, 64))), BF16)),
        merge=(row(p["hn_b"]), wpa.astype(BF16), p["w_pb"].astype(BF16), p["w_pc"].astype(BF16),
               p["w_o"].astype(BF16), row(p["b_o"]), row(p["ln1_g"]), row(p["ln1_b"]),
               p["w_router"].T, p["b_router"].reshape(N_EXPERTS, 1)),
        lam_c=p["lam_c"], sn_c=row(p["sn_c"]), ln2=(row(p["ln2_g"]), row(p["ln2_b"])))


def _chain_mixer(l, lat, x, mod, tables, caches, w, outs, after=None):
    cos_t, sin_t = tables
    lam_init = 0.8 - 0.6 * math.exp(-0.3 * l)
    res = _inproj_call(l, lat, x, mod, cos_t, sin_t, *w["inproj"], None if outs is None else outs[:4], after)
    qa, kab, vab, qb, kb, vb, ob, qc, kcb, vcb, g, ifg, ifgt = res[:13]
    if isinstance(x, tuple):
        x = res[-1]
    if lat:
        ck_a, cv_a, ck_c, cv_c, cn0, m0 = caches
    else:
        ck_a = cv_a = ck_c = cv_c = cn0 = m0 = None
    oa = _attn_call("a", lat, lam_init, qa, kab, vab, ck_a, cv_a, None, None)
    oc = _attn_call("c", lat, lam_init, qc, kcb, vcb, ck_c, cv_c, w["lam_c"], w["sn_c"])
    ml = _mlstm_call(l, lat, qb, kb, vb, ifg, ifgt, cn0, m0, None if outs is None else outs[4:6])
    hf, hb = ml[:2]
    merged = _merge_call(lat, x, mod, oa, hf, hb, ob, oc, g, *w["merge"])
    if lat:
        return merged, None
    m_l = ml[4][:, :, :, 0, 0]
    m_all = m_l[:, None] if outs is None else jnp.concatenate([outs[6], m_l[:, None]], axis=1)
    return merged, tuple(res[13:17]) + (ml[2], ml[3], m_all)


def _moe_layer(l, mix_ctx, mix_lat, mod, w, experts):
    top_i = jnp.concatenate([mix_ctx[2], mix_lat[2]], axis=1)
    pos, tile_e, tile_v, tile_s, n_tiles = _route_call(top_i, N_CTX)
    p_rows = _max_tiles(N_CTX + N_LAT) * TME
    pos_flat = (pos[:, :N_CTX].reshape(TOP_K * N_CTX), pos[:, N_CTX:].reshape(TOP_K * N_LAT))
    xs = [_dispatch_call(mix[1], pf, p_rows) for mix, pf in zip((mix_ctx, mix_lat), pos_flat)]
    ys = _moe_call(l, tile_e[0], tile_v[0], tile_s[0], n_tiles[0, :1], xs[0], xs[1],
                   experts["w_up"], experts["b_up"], experts["w_down"], experts["b_down"])
    yg_ctx = _combine_call(ys, pos_flat[0]).reshape(TOP_K, N_CTX, D_PACK)
    yg_lat = _combine_call(ys, pos_flat[1]).reshape(TOP_K, N_LAT, D_PACK)
    return ((yg_ctx, mix_ctx[3].T, mix_ctx[0], mod) + w["ln2"], (yg_lat, mix_lat[3].T, mix_lat[0], mod) + w["ln2"])


def kernel(x_prompt, x_sample, cache_k_gqa, cache_v_gqa, cache_k_diff, cache_v_diff, state_mlstm_c, state_mlstm_n, state_mlstm_m, c, c_ctx, w_in, b_in, qn_a, kn_a, hn_b, lam_c, sn_c, w_pa, w_pb, w_pc, w_o, b_o, w_mod, b_mod, ln1_g, ln1_b, ln2_g, ln2_b, w_router, b_router, w_up, b_up, w_down, b_down):
    x_ctx = x_prompt.reshape(N_CTX, D_MODEL)
    x_lat = x_sample.reshape(N_LAT, D_MODEL)
    cvecs = jnp.zeros((8, D_MODEL), F32).at[0].set(c_ctx).at[1:1 + DEC_BATCH].set(c)
    mod_all = _mod_call(cvecs, w_mod, b_mod).reshape(DEPTH, 8, N_MOD, D_MODEL)[:, :N_GROUPS]
    tables_ctx, tables_lat = _rope_tables()
    params = dict(w_in=w_in, b_in=b_in, qn_a=qn_a, kn_a=kn_a, hn_b=hn_b, lam_c=lam_c, sn_c=sn_c, w_pa=w_pa,
                  w_pb=w_pb, w_pc=w_pc, w_o=w_o, b_o=b_o, ln1_g=ln1_g, ln1_b=ln1_b, ln2_g=ln2_g, ln2_b=ln2_b,
                  w_router=w_router, b_router=b_router)
    experts = dict(w_up=w_up, b_up=b_up, w_down=w_down, b_down=b_down)
    outs = None
    for l in range(DEPTH):
        w = _layer_weights({k: v[l] for k, v in params.items()})
        ck_a = cache_k_gqa[:, l].transpose(0, 2, 1, 3).reshape(DEC_BATCH, PAST_LEN, 128).astype(BF16)
        cv_a = cache_v_gqa[:, l].transpose(0, 2, 1, 3).reshape(DEC_BATCH, PAST_LEN, 128).astype(BF16)
        ck_c = cache_k_diff[:, l].astype(BF16)
        cv_c = cache_v_diff[:, l].astype(BF16)
        cn0 = jnp.concatenate([state_mlstm_c[:, l], state_mlstm_n[:, l][..., None],
                               jnp.zeros((DEC_BATCH, 2, B_HEADS, B_HEAD_DIM, B_HEAD_DIM - 1), F32)], axis=-1)
        m0 = jnp.broadcast_to(state_mlstm_m[:, l][..., None, None], (DEC_BATCH, 2, B_HEADS, 1, LANES))
        mix_ctx, outs = _chain_mixer(l, False, x_ctx, mod_all[l], tables_ctx, None, w, outs)
        mix_lat, _ = _chain_mixer(l, True, x_lat, mod_all[l], tables_lat, (ck_a, cv_a, ck_c, cv_c, cn0, m0), w, None,
                                  after=mix_ctx[3] if l > 0 else None)
        x_ctx, x_lat = _moe_layer(l, mix_ctx, mix_lat, mod_all[l], w, experts)
    x_ctx = _final_call(False, *x_ctx)
    x_lat = _final_call(True, *x_lat)
    return (x_ctx.reshape(BATCH, SEQ, D_MODEL), x_lat.reshape(DEC_BATCH, DEC_SEQ, D_MODEL)) + tuple(outs)
```

```python
import functools
import math

import numpy as np
import jax
import jax.numpy as jnp
from jax import lax
from jax.experimental import pallas as pl
from jax.experimental.pallas import tpu as pltpu
from jax.experimental.pallas import tpu_sc as plsc

F32 = jnp.float32
BF16 = jnp.bfloat16
I32 = jnp.int32

D_MODEL = 1024
BATCH = 16
SEQ = 256
DEPTH = 2
DEC_BATCH = 2
DEC_SEQ = 4096
PAST_LEN = 512
GRID_W = 64
ROPE_BASE = 10000.0
EPS = 1e-6
A_HEADS = 8
A_KV_HEADS = 2
A_HEAD_DIM = 64
B_HEADS = 4
B_HEAD_DIM = 128
MLSTM_CHUNK = 128
C_HEADS = 4
C_HEAD_DIM = 64
C_V_DIM = 128
N_EXPERTS = 32
TOP_K = 4
D_EXPERT = 1024
SWIGLU_LIMIT = 7.0
SWIGLU_ALPHA = 1.702
N_MOD = 6
DEEPNORM_ALPHA = (2 * DEPTH) ** 0.25

N_CTX = BATCH * SEQ
N_LAT = DEC_BATCH * DEC_SEQ
N_GROUPS = 1 + DEC_BATCH

LANES = 128
TM = 256
TQ = 512
KEY_BLOCK = 1024
TM_MERGE = 512
MERGE_ROWS = 128
TME = 1024
MOE_ROWS = 256
TILE_LANES = 128
ROUTE_CHUNK = 512
D_PACK = D_MODEL // 2
SC_WINDOW = 64
VMEM_LIMIT = 56 * 1024 * 1024

QA_HEAD_ORDER = (0, 4, 1, 5, 2, 6, 3, 7)
D_IN = 7440
D_IN_PAD = 7552
P_QA, P_KA, P_VA, P_QB, P_KB, P_VB, P_OB, P_QC, P_KC, P_VC, P_G, P_IF = (
    0, 512, 640, 768, 1280, 1792, 2304, 2816, 3328, 3840, 4352, 7424)


def _max_tiles(n_tok):
    return (n_tok * TOP_K) // TME + N_EXPERTS


def _cparams(sem):
    return pltpu.CompilerParams(dimension_semantics=sem, vmem_limit_bytes=VMEM_LIMIT)


def _split3(a):
    a1 = a.astype(BF16)
    r = a - a1.astype(F32)
    a2 = r.astype(BF16)
    a3 = (r - a2.astype(F32)).astype(BF16)
    return a1, a2, a3


def _dot(a, b, dims=None):
    if dims is None:
        return jnp.dot(a, b, preferred_element_type=F32)
    return lax.dot_general(a, b, (dims, ((), ())), preferred_element_type=F32)


def _dot_exact_rhs01(a, m, dims=None):
    return sum(_dot(p, m, dims) for p in _split3(a))


def _dot_exact_lhs01(m, a, dims=None):
    return sum(_dot(m, p, dims) for p in _split3(a))


def _pack_rows(x):
    xb = x.astype(BF16).astype(F32)
    lo = pltpu.bitcast(xb[:, :D_PACK], I32)
    lo = lax.shift_right_logical(lo, jnp.full_like(lo, 16))
    hi = pltpu.bitcast(xb[:, D_PACK:], I32) & jnp.int32(-65536)
    return lo | hi


def _unpack_rows(w):
    return (pltpu.bitcast(lax.shift_left(w, jnp.full_like(w, 16)), F32),
            pltpu.bitcast(w & jnp.int32(-65536), F32))


def _ln(x):
    mu = jnp.mean(x, axis=-1, keepdims=True)
    xc = x - mu
    var = jnp.mean(xc * xc, axis=-1, keepdims=True)
    return xc * lax.rsqrt(var + EPS)


def _log_sigmoid(x):
    return jnp.minimum(x, 0.0) - jnp.log(1.0 + jnp.exp(-jnp.abs(x)))


def _rope(x, cos, sin):
    lane = lax.broadcasted_iota(I32, (1, LANES), 1)
    first = (lane % 32) < 16
    outs = []
    for j in range(x.shape[1] // LANES):
        xj = x[:, j * LANES:(j + 1) * LANES]
        partner = jnp.where(first, pltpu.roll(xj, LANES - 16, 1), pltpu.roll(xj, 16, 1))
        outs.append(xj * cos + partner * sin)
    return outs[0] if len(outs) == 1 else jnp.concatenate(outs, axis=1)


def _mod_spec(lat, tile=TM):
    if lat:
        return pl.BlockSpec((1, N_MOD, D_MODEL), lambda i: (1 + i * tile // DEC_SEQ, 0, 0))
    return pl.BlockSpec((1, N_MOD, D_MODEL), lambda i: (0, 0, 0))


def _mod_kernel(c_ref, w_ref, b_ref, o_ref):
    c = c_ref[...]
    s = c * jax.nn.sigmoid(c)
    o_ref[...] = _dot(s.astype(BF16), w_ref[...].astype(BF16)) + b_ref[...]


def _mod_call(l, cvecs, w_mod, b_mod):
    tn = 1536
    n = N_MOD * D_MODEL
    return pl.pallas_call(
        _mod_kernel,
        grid=(n // tn,),
        in_specs=[pl.BlockSpec((8, D_MODEL), lambda j: (0, 0)),
                  pl.BlockSpec((None, D_MODEL, tn), lambda j: (l, 0, j)),
                  pl.BlockSpec((None, 1, tn), lambda j: (l, 0, j))],
        out_specs=pl.BlockSpec((8, tn), lambda j: (0, j)),
        out_shape=jax.ShapeDtypeStruct((8, n), F32),
        compiler_params=_cparams(("arbitrary",)),
        name="mod",
    )(cvecs, w_mod, b_mod.reshape(DEPTH, 1, n))


def _moe_residual_norm(yg_ref, tw_ref, x1_ref, mod_ref, ln_g_ref, ln_b_ref):
    tw = tw_ref[...]
    halves = [_unpack_rows(yg_ref[k]) for k in range(TOP_K)]
    moe = jnp.concatenate([sum(tw[:, k:k + 1] * halves[k][j] for k in range(TOP_K)) for j in range(2)], axis=1)
    gate2 = mod_ref[0, 5:6, :]
    return _ln(DEEPNORM_ALPHA * x1_ref[...] + gate2 * moe) * ln_g_ref[...] + ln_b_ref[...]


def _inproj_kernel(with_cache, fused, n_unread, *refs):
    if fused:
        x = _moe_residual_norm(*refs[:6])
        refs = refs[6:]
    else:
        x = refs[0][...]
        refs = refs[1:]
    mod_ref, cos_ref, sin_ref, w_ref, b_ref, wgt_ref, bgt_ref, gq_ref, gk_ref, gmat_ref = refs[:10]
    outs = refs[10 + n_unread:]
    qa_o, kab_o, vab_o, qb_o, kb_o, vb_o, ob_o, qc_o, kcb_o, vcb_o, g_o, ifg_o, ifgt_o = outs[:13]
    if fused:
        outs[-1][...] = x
    shift = mod_ref[0, 0:1, :]
    scale = mod_ref[0, 1:2, :]
    ub = (_ln(x) * (1.0 + scale) + shift).astype(BF16)
    cos = cos_ref[...]
    sin = sin_ref[...]

    def proj(lo, n):
        return _dot(ub, w_ref[lo:lo + n, :], ((1,), (1,))) + b_ref[:, lo:lo + n]

    def headnorm(z, gmat, gain):
        ss = _dot((z * z).astype(BF16), gmat)
        return z * lax.rsqrt(ss * (1.0 / A_HEAD_DIM) + EPS) * gain

    segs = [(P_QA, 512), (P_KA, 128), (P_VA, 128), (P_QB, 512), (P_KB, 512), (P_VB, 512), (P_OB, 512),
            (P_QC, 512), (P_KC, 512), (P_VC, 512), (P_G, 1024), (P_G + 1024, 1024), (P_G + 2048, 1024),
            (P_IF, LANES)]
    keep = {}

    def epilogue(k, z):
        if k == 0:
            z = headnorm(z, gmat_ref[...], gq_ref[...])
            qa_o[...] = (_rope(z, cos, sin) * 0.125).astype(BF16)
        elif k == 1:
            keep["k"] = _rope(headnorm(z, gmat_ref[0:128, 0:128], gk_ref[...]), cos, sin)
            kab_o[...] = keep["k"].astype(BF16)
        elif k == 2:
            keep["v"] = z
            vab_o[...] = z.astype(BF16)
        elif k == 3:
            qb_o[...] = z.astype(BF16)
        elif k == 4:
            kb_o[...] = (z * (B_HEAD_DIM ** -0.5)).astype(BF16)
        elif k == 5:
            vb_o[...] = z.astype(BF16)
        elif k == 6:
            ob_o[...] = z.astype(BF16)
        elif k == 7:
            qc_o[...] = (_rope(z, cos, sin) * 0.125).astype(BF16)
        elif k == 8:
            keep["kc"] = _rope(z, cos, sin)
            kcb_o[...] = keep["kc"].astype(BF16)
        elif k == 9:
            keep["vc"] = z
            vcb_o[...] = z.astype(BF16)
        elif k in (10, 11, 12):
            g_o[:, (k - 10) * 1024:(k - 9) * 1024] = jax.nn.sigmoid(z).astype(BF16)
        else:
            lane = lax.broadcasted_iota(I32, (1, LANES), 1)
            ifg_o[...] = jnp.where(lane >= 8, _log_sigmoid(z), z)

    z_prev = proj(*segs[0])
    for k in range(1, len(segs)):
        z_next = proj(*segs[k])
        epilogue(k - 1, z_prev)
        z_prev = z_next
    zt = _dot(wgt_ref[...], ub, ((1,), (1,))) + bgt_ref[...]
    epilogue(len(segs) - 1, z_prev)
    row = lax.broadcasted_iota(I32, (16, 1), 0)
    ifgt_o[...] = jnp.where(row >= 8, _log_sigmoid(zt), zt)

    if with_cache:
        nka_o, nva_o, nkc_o, nvc_o = outs[13:17]
        for h in range(A_KV_HEADS):
            nka_o[h] = keep["k"][:, h * A_HEAD_DIM:(h + 1) * A_HEAD_DIM]
            nva_o[h] = keep["v"][:, h * A_HEAD_DIM:(h + 1) * A_HEAD_DIM]
        for h in range(C_HEADS):
            nkc_o[h] = keep["kc"][:, h * LANES:(h + 1) * LANES]
            nvc_o[h] = keep["vc"][:, h * LANES:(h + 1) * LANES]


def _inproj_call(l, lat, x, mod, cos_t, sin_t, w_p, b_p, w_gt, b_gt, gq, gk, gmat, cache_outs, after=None):
    fused = isinstance(x, tuple)
    n = x[2].shape[0] if fused else x.shape[0]
    row = lambda w: pl.BlockSpec((TM, w), lambda i: (i, 0))
    const = lambda shape: pl.BlockSpec(shape, lambda i: (0,) * len(shape), pipeline_mode=pl.Buffered(1))
    sds = lambda w, dt: jax.ShapeDtypeStruct((n, w), dt)
    out_defs = [(512, BF16), (128, BF16), (128, BF16), (512, BF16), (512, BF16), (512, BF16), (512, BF16),
                (512, BF16), (512, BF16), (512, BF16), (3072, BF16), (LANES, F32)]
    out_specs = [row(w) for w, _ in out_defs] + [pl.BlockSpec((16, TM), lambda i: (0, i))]
    out_shape = [sds(w, dt) for w, dt in out_defs] + [jax.ShapeDtypeStruct((16, n), F32)]
    alias_in, aliases = [], {}
    if lat:
        tbl = pl.BlockSpec((TM, LANES), lambda i: (i % (DEC_SEQ // TM), 0))
    else:
        assert TM == SEQ
        tbl = pl.BlockSpec((TM, LANES), lambda i: (0, 0))
        cache_defs = [(A_KV_HEADS, A_HEAD_DIM), (A_KV_HEADS, A_HEAD_DIM), (C_HEADS, LANES), (C_HEADS, LANES)]
        out_specs += [pl.BlockSpec((None, None, nh, SEQ, d), lambda i: (i, l, 0, 0, 0)) for nh, d in cache_defs]
        out_shape += [jax.ShapeDtypeStruct((BATCH, DEPTH, nh, SEQ, d), F32) for nh, d in cache_defs]
        if cache_outs is not None:
            alias_in = list(cache_outs)
    if fused:
        plain = lambda shape: pl.BlockSpec(shape, lambda i: (0,) * len(shape))
        lead_specs = [pl.BlockSpec((TOP_K, TM, D_PACK), lambda i: (0, i, 0)), row(TOP_K), row(D_MODEL),
                      _mod_spec(lat), plain((1, D_MODEL)), plain((1, D_MODEL))]
        lead = list(x)
        out_specs.append(row(D_MODEL))
        out_shape.append(sds(D_MODEL, F32))
    else:
        lead_specs, lead = [row(D_MODEL)], [x]
    n_in = len(lead) + 10
    aliases = {n_in + j: len(out_defs) + 1 + j for j in range(len(alias_in))}
    unread = alias_in + ([] if after is None else [after])
    return pl.pallas_call(
        functools.partial(_inproj_kernel, not lat, fused, len(unread)),
        grid=(n // TM,),
        in_specs=lead_specs + [_mod_spec(lat), tbl, tbl,
                               const((D_IN_PAD, D_MODEL)), const((1, D_IN_PAD)),
                               const((16, D_MODEL)), const((16, 1)),
                               const((1, 512)), const((1, 128)), const((512, 512))]
        + [pl.BlockSpec(memory_space=pl.ANY)] * len(unread),
        out_specs=out_specs, out_shape=out_shape, input_output_aliases=aliases,
        compiler_params=_cparams(("arbitrary",)),
        name="inproj_lat" if lat else "inproj_ctx",
    )(*lead, mod, cos_t, sin_t, w_p, b_p, w_gt, b_gt, gq, gk, gmat, *unread)


def _attn_kernel(mode, nseg, lam_init, *refs):
    q_ref = refs[0]
    kv = refs[1:1 + 2 * nseg]
    rest = refs[1 + 2 * nseg:]
    o_ref = rest[-1]
    tq = q_ref.shape[0]
    lane = lax.broadcasted_iota(I32, (1, LANES), 1)
    if mode == "c":
        lam_ref, sn_ref = rest[0], rest[1]
        lp = lam_ref[...]
        lam = (jnp.exp(jnp.sum(lp[0:1] * lp[1:2], axis=-1, keepdims=True))
               - jnp.exp(jnp.sum(lp[2:3] * lp[3:4], axis=-1, keepdims=True)) + lam_init)
    for blk in range(q_ref.shape[1] // LANES):
        qs = slice(blk * LANES, (blk + 1) * LANES)
        ks = qs if kv[0].shape[1] > LANES else slice(0, LANES)
        q = q_ref[:, qs]
        zero = jnp.zeros_like(q)
        qq = jnp.concatenate([jnp.where(lane < 64, q, zero), jnp.where(lane >= 64, q, zero)], axis=0)
        m = l = acc = None
        for j in range(nseg):
            k_ref, v_ref = kv[2 * j], kv[2 * j + 1]
            for off in range(0, k_ref.shape[0], KEY_BLOCK):
                n = min(KEY_BLOCK, k_ref.shape[0] - off)
                s = _dot(qq, k_ref[off:off + n, ks], ((1,), (1,)))
                bm = jnp.max(s, axis=-1, keepdims=True)
                if m is None:
                    m = bm
                    e = jnp.exp(s - m)
                    l = jnp.sum(e, axis=-1, keepdims=True)
                    acc = _dot(e.astype(BF16), v_ref[off:off + n, ks])
                else:
                    m_new = jnp.maximum(m, bm)
                    alpha = jnp.exp(m - m_new)
                    e = jnp.exp(s - m_new)
                    l = alpha * l + jnp.sum(e, axis=-1, keepdims=True)
                    acc = alpha * acc + _dot(e.astype(BF16), v_ref[off:off + n, ks])
                    m = m_new
        o = acc * (1.0 / l)
        if mode == "a":
            out = jnp.where(lane < 64, o[:tq], o[tq:])
        else:
            o = o[:tq] - lam * o[tq:]
            ms = jnp.mean(o * o, axis=-1, keepdims=True)
            out = o * lax.rsqrt(ms + EPS) * sn_ref[...] * (1.0 - lam_init)
        o_ref[:, qs] = out.astype(o_ref.dtype)


def _attn_call(mode, lat, lam_init, q_all, k_all, v_all, k_cache, v_cache, lam_c, sn_c):
    wide = k_all.shape[1] == 512
    kcol = (lambda h: h) if wide else (lambda h: 0)
    if lat:
        nq = DEC_SEQ // TQ
        grid = (DEC_BATCH, nq, 4)
        q_spec = pl.BlockSpec((TQ, LANES), lambda b, i, h: (b * nq + i, h))
        kn_spec = pl.BlockSpec((DEC_SEQ, LANES), lambda b, i, h: (b, kcol(h)))
        if wide:
            kc_spec = pl.BlockSpec((None, None, PAST_LEN, LANES), lambda b, i, h: (b, h, 0, 0))
        else:
            kc_spec = pl.BlockSpec((None, PAST_LEN, LANES), lambda b, i, h: (b, 0, 0))
        in_specs = [q_spec, kc_spec, kc_spec, kn_spec, kn_spec]
        args = [q_all, k_cache, v_cache, k_all, v_all]
        nseg = 2
        o_spec = q_spec
        sem = ("arbitrary", "arbitrary", "arbitrary")
        zero_map = lambda b, i, h: (0, 0)
    else:
        grid = (BATCH,)
        q_spec = pl.BlockSpec((SEQ, 512), lambda b: (b, 0))
        k_spec = pl.BlockSpec((SEQ, k_all.shape[1]), lambda b: (b, 0))
        in_specs = [q_spec, k_spec, k_spec]
        args = [q_all, k_all, v_all]
        nseg = 1
        o_spec = q_spec
        sem = ("arbitrary",)
        zero_map = lambda b: (0, 0)
    if mode == "c":
        in_specs += [pl.BlockSpec((4, C_HEAD_DIM), zero_map), pl.BlockSpec((1, C_V_DIM), zero_map)]
        args += [lam_c, sn_c]
    return pl.pallas_call(
        functools.partial(_attn_kernel, mode, nseg, lam_init),
        grid=grid, in_specs=in_specs, out_specs=o_spec,
        out_shape=jax.ShapeDtypeStruct((q_all.shape[0], 512), BF16),
        compiler_params=_cparams(sem), name=f"attn_{mode}_{'lat' if lat else 'ctx'}",
    )(*args)


def _mlstm_kernel(has_init, want_state, n_alias, *refs):
    tok = (refs[0:5], refs[5:10])
    pos = 10
    if has_init:
        cn0_ref, m0_ref = refs[pos:pos + 2]
        pos += 2
    pos += n_alias
    h_refs = refs[pos:pos + 2]
    pos += 2
    if want_state:
        c_o, n_o, m_o = refs[pos:pos + 3]
        pos += 3
    cn_s, m_s = refs[pos:pos + 2]

    c = pl.program_id(1)
    last = pl.num_programs(1) - 1
    L = MLSTM_CHUNK

    @pl.when(c == 0)
    def _():
        if has_init:
            cn_s[...] = cn0_ref[...]
            m_s[...] = m0_ref[...]
        else:
            cn_s[...] = jnp.zeros_like(cn_s)
            m_s[...] = jnp.zeros_like(m_s)

    row = lax.broadcasted_iota(I32, (L, L), 0)
    col = lax.broadcasted_iota(I32, (L, L), 1)
    lane = lax.broadcasted_iota(I32, (1, LANES), 1)
    ones_col = jnp.where(lane == 0, 1.0, 0.0).astype(BF16)

    chains = []
    for d in range(2):
        q_ref, k_ref, v_ref, g_ref, gt_ref = tok[d]
        keep = (col <= row) if d == 0 else (col >= row)
        m_c = keep.astype(BF16)
        m_r = ((row <= col) if d == 0 else (row >= col)).astype(BF16)
        g = g_ref[...]
        gt = gt_ref[...]
        bc = _dot_exact_lhs01(m_c, g)
        br = _dot_exact_rhs01(gt, m_r)
        for h in range(B_HEADS):
            sl = slice(h * L, (h + 1) * L)
            gi, gf = 4 * d + h, 8 + 4 * d + h
            ch = dict(d=d, h=h, sl=sl, keep=keep, q=q_ref[:, sl], k=k_ref[:, sl], v=v_ref[:, sl],
                      i_col=g[:, gi:gi + 1], f_col=g[:, gf:gf + 1], b_col=bc[:, gf:gf + 1],
                      r_row=gt[gi:gi + 1, :] - br[gf:gf + 1, :],
                      m_prev=m_s[d, h][:, 0:1], cn_prev=cn_s[d, h])
            chains.append(ch)
    for ch in chains:
        ch["s"] = _dot(ch["q"], ch["k"], ((1,), (1,)))
        ch["qc"] = _dot(ch["q"], ch["cn_prev"].astype(BF16))
    for ch in chains:
        ch["total"] = jnp.sum(ch["f_col"], axis=0, keepdims=True)
        ch["inter"] = ch["b_col"] + ch["m_prev"]
        ch["v_aug"] = jnp.concatenate([ch["v"], jnp.broadcast_to(ones_col, (L, LANES))], axis=1)
    for ch in chains:
        ch["rmat"] = jnp.where(ch["keep"], ch["r_row"], -jnp.inf)
    for ch in chains:
        ch["dmax"] = ch["b_col"] + jnp.max(ch["rmat"], axis=-1, keepdims=True)
    for ch in chains:
        ch["m_row"] = jnp.maximum(ch["inter"], ch["dmax"])
    for ch in chains:
        ch["sc"] = (ch["s"] * jnp.exp(ch["rmat"] + (ch["b_col"] - ch["m_row"]))).astype(BF16)
    for ch in chains:
        ch["a"] = _dot(ch["sc"], ch["v_aug"])
    for ch in chains:
        ch["nd"] = jnp.exp(ch["inter"] - ch["m_row"]) * ch["qc"] + ch["a"]
    for ch in chains:
        ch["scale"] = 1.0 / jnp.maximum(jnp.abs(ch["nd"][:, L:L + 1]), jnp.exp(-ch["m_row"]))
    for ch in chains:
        h_refs[ch["d"]][:, ch["sl"]] = ch["nd"][:, :L] * ch["scale"]
    for ch in chains:
        ch["wel"] = ch["total"] - ch["b_col"] + ch["i_col"]
        ch["m_new"] = jnp.maximum(ch["total"] + ch["m_prev"], jnp.max(ch["wel"], axis=0, keepdims=True))
    for ch in chains:
        ch["kw"] = (jnp.exp(ch["wel"] - ch["m_new"]) * ch["k"].astype(F32)).astype(BF16)
    for ch in chains:
        ch["u"] = _dot(ch["kw"], ch["v_aug"], ((0,), (0,)))
    for ch in chains:
        cn_s[ch["d"], ch["h"]] = jnp.exp(ch["total"] + ch["m_prev"] - ch["m_new"]) * ch["cn_prev"] + ch["u"]
        m_s[ch["d"], ch["h"]] = jnp.broadcast_to(ch["m_new"], (1, LANES))

    if want_state:
        @pl.when(c == last)
        def _():
            pick = ((lax.broadcasted_iota(I32, (8, 2 * L), 1) == L)
                    & (lax.broadcasted_iota(I32, (8, 2 * L), 0) == 0)).astype(BF16)
            for d in range(2):
                for h in range(B_HEADS):
                    cn = cn_s[d, h]
                    c_o[d, h] = cn[:, :L]
                    n_o[d, h:h + 1, :] = _dot_exact_lhs01(pick, cn, ((1,), (1,)))[0:1]
            m_o[...] = m_s[...]


def _mlstm_call(l, lat, q_all, k_all, v_all, g_all, gt_all, cn0, m0, state_bufs):
    L = MLSTM_CHUNK
    n = q_all.shape[0]
    nb, nc = (DEC_BATCH, DEC_SEQ // L) if lat else (BATCH, SEQ // L)
    chunk = (lambda b, c: b * nc + c, lambda b, c: b * nc + (nc - 1 - c))

    in_specs, args = [], []
    for d in range(2):
        tok = lambda w, d=d: pl.BlockSpec((L, w), lambda b, c: (chunk[d](b, c), 0))
        in_specs += [tok(512), tok(512), tok(512), tok(LANES),
                     pl.BlockSpec((16, L), lambda b, c, d=d: (0, chunk[d](b, c)))]
        args += [q_all, k_all, v_all, g_all, gt_all]
    if lat:
        in_specs += [pl.BlockSpec((None, 2, B_HEADS, L, 2 * L), lambda b, c: (b, 0, 0, 0, 0)),
                     pl.BlockSpec((None, 2, B_HEADS, 1, LANES), lambda b, c: (b, 0, 0, 0, 0))]
        args += [cn0, m0]
    aliases = {}
    alias_in = list(state_bufs) if state_bufs is not None else []
    if alias_in:
        aliases = {len(args): 2, len(args) + 1: 3}
    in_specs += [pl.BlockSpec(memory_space=pl.ANY)] * len(alias_in)
    args += alias_in
    out_specs = [pl.BlockSpec((L, 512), lambda b, c, d=d: (chunk[d](b, c), 0)) for d in range(2)]
    out_shape = [jax.ShapeDtypeStruct((n, 512), F32)] * 2
    if not lat:
        out_specs += [pl.BlockSpec((None, None, 2, B_HEADS, L, L), lambda b, c: (b, l, 0, 0, 0, 0)),
                      pl.BlockSpec((None, None, 2, B_HEADS, L), lambda b, c: (b, l, 0, 0, 0)),
                      pl.BlockSpec((None, 2, B_HEADS, 1, LANES), lambda b, c: (b, 0, 0, 0, 0))]
        out_shape += [jax.ShapeDtypeStruct((BATCH, DEPTH, 2, B_HEADS, L, L), F32),
                      jax.ShapeDtypeStruct((BATCH, DEPTH, 2, B_HEADS, L), F32),
                      jax.ShapeDtypeStruct((nb, 2, B_HEADS, 1, LANES), F32)]
    return pl.pallas_call(
        functools.partial(_mlstm_kernel, lat, not lat, len(alias_in)),
        grid=(nb, nc), in_specs=in_specs, out_specs=out_specs, out_shape=out_shape,
        input_output_aliases=aliases,
        scratch_shapes=[pltpu.VMEM((2, B_HEADS, L, 2 * L), F32), pltpu.VMEM((2, B_HEADS, 1, LANES), F32)],
        compiler_params=_cparams(("arbitrary", "arbitrary")), name="mlstm_lat" if lat else "mlstm_ctx",
    )(*args)


def _merge_kernel(x_ref, mod_ref, oa_ref, hf_ref, hb_ref, ob_ref, oc_ref, g_ref, hn_ref,
                  wpa_ref, wpb_ref, wpc_ref, wo_ref, bo_ref, ln_g_ref, ln_b_ref, wr_ref, br_ref,
                  x1_o, u2_o, ti_o, tw_o):
    gate1 = mod_ref[0, 2:3, :]
    shift2 = mod_ref[0, 3:4, :]
    scale2 = mod_ref[0, 4:5, :]
    wr = _split3(wr_ref[...])
    eio = lax.broadcasted_iota(I32, (N_EXPERTS, 1), 0)
    nt = ((1,), (1,))
    cs = [dict(rs=slice(r, r + MERGE_ROWS)) for r in range(0, TM_MERGE, MERGE_ROWS)]
    for c in cs:
        c["pa"] = _dot(oa_ref[c["rs"], :], wpa_ref[...])
        c["pc"] = _dot(oc_ref[c["rs"], :], wpc_ref[...])
    for c in cs:
        hm = hf_ref[c["rs"], :] + hb_ref[c["rs"], :]
        parts = []
        for h in range(B_HEADS):
            hh = hm[:, h * LANES:(h + 1) * LANES]
            parts.append(hh * lax.rsqrt(jnp.mean(hh * hh, axis=-1, keepdims=True) + EPS))
        c["out_b"] = (jnp.concatenate(parts, axis=1) * hn_ref[...]
                      * jax.nn.sigmoid(ob_ref[c["rs"], :].astype(F32))).astype(BF16)
    for c in cs:
        c["pb"] = _dot(c["out_b"], wpb_ref[...])
    for c in cs:
        rs = c["rs"]
        c["merged"] = (g_ref[rs, 0:1024].astype(F32) * c["pa"] + g_ref[rs, 1024:2048].astype(F32) * c["pb"]
                       + g_ref[rs, 2048:3072].astype(F32) * c["pc"]).astype(BF16)
    for c in cs:
        c["y"] = _dot(c["merged"], wo_ref[...]) + bo_ref[...]
    for c in cs:
        c["x1"] = _ln(DEEPNORM_ALPHA * x_ref[c["rs"], :] + gate1 * c["y"]) * ln_g_ref[...] + ln_b_ref[...]
        x1_o[c["rs"], :] = c["x1"]
    for c in cs:
        c["u2"] = _ln(c["x1"]) * (1.0 + scale2) + shift2
        u2_o[c["rs"], :] = _pack_rows(c["u2"])
    for c in cs:
        up = _split3(c["u2"])
        c["cur"] = (_dot(wr[0], up[0], nt) + _dot(wr[0], up[1], nt) + _dot(wr[1], up[0], nt)
                    + _dot(wr[0], up[2], nt) + _dot(wr[2], up[0], nt) + _dot(wr[1], up[1], nt)) + br_ref[...]
        c["vals"], c["idxs"] = [], []
    for _ in range(TOP_K):
        for c in cs:
            mx = jnp.max(c["cur"], axis=0, keepdims=True)
            ix = jnp.min(jnp.where(c["cur"] == mx, eio, N_EXPERTS), axis=0, keepdims=True)
            c["vals"].append(mx)
            c["idxs"].append(ix)
            c["cur"] = jnp.where(eio == ix, -jnp.inf, c["cur"])
    for c in cs:
        ex = [jnp.exp(v - c["vals"][0]) for v in c["vals"]]
        inv = 1.0 / sum(ex)
        ti_o[:, c["rs"]] = jnp.concatenate(c["idxs"], axis=0)
        tw_o[:, c["rs"]] = jnp.concatenate([e * inv for e in ex], axis=0)


def _merge_call(lat, x, mod, out_a, hf, hb, ob, out_c, g, hn, wpa, wpb, wpc, wo, bo, ln_g, ln_b, wr_t, br):
    n = x.shape[0]
    tm = TM_MERGE
    row = lambda w: pl.BlockSpec((tm, w), lambda i: (i, 0))
    const = lambda shape: pl.BlockSpec(shape, lambda i: (0,) * len(shape), pipeline_mode=pl.Buffered(1))
    return pl.pallas_call(
        _merge_kernel,
        grid=(n // tm,),
        in_specs=[row(D_MODEL), _mod_spec(lat, tm),
                  row(512), row(512), row(512), row(512), row(512), row(3072),
                  const((1, 512)), const((512, D_MODEL)), const((512, D_MODEL)), const((512, D_MODEL)),
                  const((D_MODEL, D_MODEL)), const((1, D_MODEL)), const((1, D_MODEL)), const((1, D_MODEL)),
                  const((N_EXPERTS, D_MODEL)), const((N_EXPERTS, 1))],
        out_specs=[row(D_MODEL), row(D_PACK),
                   pl.BlockSpec((TOP_K, tm), lambda i: (0, i)), pl.BlockSpec((TOP_K, tm), lambda i: (0, i))],
        out_shape=[jax.ShapeDtypeStruct((n, D_MODEL), F32), jax.ShapeDtypeStruct((n, D_PACK), I32),
                   jax.ShapeDtypeStruct((TOP_K, n), I32), jax.ShapeDtypeStruct((TOP_K, n), F32)],
        compiler_params=_cparams(("arbitrary",)), name="merge_lat" if lat else "merge_ctx",
    )(x, mod, out_a, hf, hb, ob, out_c, g, hn, wpa, wpb, wpc, wo, bo, ln_g, ln_b, wr_t, br)


def _route_kernel(n_first, ti_ref, pos_o, te_o, tv_o, ts_o, nt_o, rank_s):
    ch = ROUTE_CHUNK
    n_tok = ti_ref.shape[1]
    eio = lax.broadcasted_iota(I32, (N_EXPERTS, 1), 0)
    tri = (lax.broadcasted_iota(I32, (ch, ch), 0) <= lax.broadcasted_iota(I32, (ch, ch), 1)).astype(BF16)

    def onehots(off):
        ti = ti_ref[:, pl.ds(off, ch)]
        return [(eio == ti[k:k + 1, :]).astype(F32) for k in range(TOP_K)]

    def count_body(c, carry):
        off = pl.multiple_of(c * ch, ch)
        ohs = onehots(off)
        oh = ohs[0] + ohs[1] + ohs[2] + ohs[3]
        incl = _dot(oh.astype(BF16), tri)
        excl = incl - oh + carry
        for k in range(TOP_K):
            rank_s[k:k + 1, pl.ds(off, ch)] = jnp.sum(ohs[k] * excl, axis=0, keepdims=True)
        return carry + incl[:, ch - 1:ch]

    counts_first = lax.fori_loop(0, n_first // ch, count_body, jnp.zeros((N_EXPERTS, 1), F32))
    counts = lax.fori_loop(n_first // ch, n_tok // ch, count_body, counts_first)
    ntile_e = jnp.floor((counts + (TME - 1)) * (1.0 / TME))
    lower = (lax.broadcasted_iota(I32, (N_EXPERTS, N_EXPERTS), 1)
             < lax.broadcasted_iota(I32, (N_EXPERTS, N_EXPERTS), 0)).astype(BF16)
    off_t = _dot(lower, jnp.broadcast_to(ntile_e, (N_EXPERTS, LANES)).astype(BF16))[:, 0:1]
    end_t = off_t + ntile_e
    total = jnp.sum(ntile_e, axis=0, keepdims=True)
    jt = lax.broadcasted_iota(I32, (1, TILE_LANES), 1).astype(F32)
    te = jnp.minimum(jnp.sum((end_t <= jt).astype(F32), axis=0, keepdims=True), N_EXPERTS - 1.0)
    sel = (eio.astype(F32) == te).astype(F32)
    cnt_j = jnp.sum(sel * counts, axis=0, keepdims=True)
    off_j = jnp.sum(sel * off_t, axis=0, keepdims=True)
    valid = jnp.clip(cnt_j - (jt - off_j) * TME, 0.0, float(TME))
    first_j = jnp.sum(sel * counts_first, axis=0, keepdims=True)
    te_o[...] = te.astype(I32)
    tv_o[...] = jnp.where(jt < total, valid, 0.0).astype(I32)
    ts_o[...] = jnp.clip(first_j - (jt - off_j) * TME, 0.0, float(TME)).astype(I32)
    nt_o[...] = jnp.broadcast_to(total, (1, LANES)).astype(I32)
    off_rows = off_t * TME

    def pos_body(c, carry):
        off = pl.multiple_of(c * ch, ch)
        ohs = onehots(off)
        for k in range(TOP_K):
            base = jnp.sum(ohs[k] * off_rows, axis=0, keepdims=True)
            pos_o[k:k + 1, pl.ds(off, ch)] = (rank_s[k:k + 1, pl.ds(off, ch)] + base).astype(I32)
        return carry

    lax.fori_loop(0, n_tok // ch, pos_body, 0)


def _route_call(top_i, n_first):
    n = top_i.shape[1]
    assert _max_tiles(n) <= TILE_LANES and n_first % ROUTE_CHUNK == 0
    tiles = jax.ShapeDtypeStruct((1, TILE_LANES), I32)
    return pl.pallas_call(
        functools.partial(_route_kernel, n_first),
        out_shape=[jax.ShapeDtypeStruct((TOP_K, n), I32), tiles, tiles, tiles,
                   jax.ShapeDtypeStruct((1, LANES), I32)],
        scratch_shapes=[pltpu.VMEM((TOP_K, n), F32)],
        compiler_params=pltpu.CompilerParams(vmem_limit_bytes=VMEM_LIMIT), name="route",
    )(top_i)


def _sc_mesh():
    return plsc.VectorSubcoreMesh(core_axis_name="c", subcore_axis_name="s")


def _sc_worker(n_rows):
    info = plsc.get_sparse_core_info()
    n_workers = info.num_cores * info.num_subcores
    per_w = n_rows // n_workers
    assert per_w * n_workers == n_rows and per_w % SC_WINDOW == 0
    wid = lax.axis_index("s") * info.num_cores + lax.axis_index("c")
    return wid * per_w, per_w


def _dispatch_call(u2, pos_flat, p_rows):
    w = SC_WINDOW
    n = u2.shape[0]

    @functools.partial(pl.kernel, out_type=jax.ShapeDtypeStruct((p_rows, u2.shape[1]), u2.dtype), mesh=_sc_mesh(),
                       scratch_types=[pltpu.VMEM((w,), I32)] * (2 * TOP_K)
                       + [pltpu.VMEM((w, u2.shape[1]), u2.dtype)] * 2 + [pltpu.SemaphoreType.DMA] * 4)
    def k(x_hbm, p_hbm, o_hbm, *scratch):
        idx = (scratch[0:TOP_K], scratch[TOP_K:2 * TOP_K])
        rows = scratch[2 * TOP_K:2 * TOP_K + 2]
        sem_load = scratch[2 * TOP_K + 2:2 * TOP_K + 4]
        sem_scat = scratch[2 * TOP_K + 4:2 * TOP_K + 6]
        base, per_w = _sc_worker(n)
        n_chunks = per_w // w

        def start_loads(c, s):
            off = pl.multiple_of(base + c * w, 8)
            cps = [pltpu.async_copy(x_hbm.at[pl.ds(off, w)], rows[s], sem_load[s])]
            cps += [pltpu.async_copy(p_hbm.at[pl.ds(pl.multiple_of(kk * n + off, 8), w)], idx[s][kk], sem_load[s])
                    for kk in range(TOP_K)]
            return cps

        loads = start_loads(0, 0)
        scatters = [None, None]
        for c in range(n_chunks):
            s = c % 2
            for cp in loads:
                cp.wait()
            if c + 1 < n_chunks:
                if scatters[1 - s] is not None:
                    for cp in scatters[1 - s]:
                        cp.wait()
                    scatters[1 - s] = None
                loads = start_loads(c + 1, 1 - s)
            scatters[s] = [pltpu.async_copy(rows[s], o_hbm.at[idx[s][kk]], sem_scat[s]) for kk in range(TOP_K)]
        for group in scatters:
            if group is not None:
                for cp in group:
                    cp.wait()

    return k(u2, pos_flat)


def _combine_call(ys, pos_flat):
    w = SC_WINDOW
    n = pos_flat.shape[0]

    @functools.partial(pl.kernel, out_type=jax.ShapeDtypeStruct((n, ys.shape[1]), ys.dtype), mesh=_sc_mesh(),
                       scratch_types=[pltpu.VMEM((w,), I32)] * 2 + [pltpu.VMEM((w, ys.shape[1]), ys.dtype)] * 2
                       + [pltpu.SemaphoreType.DMA] * 4)
    def k(y_hbm, p_hbm, o_hbm, idx0, idx1, rows0, rows1, sg0, sg1, ss0, ss1):
        idx, rows, sem_gather, sem_store = (idx0, idx1), (rows0, rows1), (sg0, sg1), (ss0, ss1)
        base, per_w = _sc_worker(n)
        n_chunks = per_w // w

        def offset(c):
            return pl.multiple_of(base + c * w, 8)

        pltpu.sync_copy(p_hbm.at[pl.ds(offset(0), w)], idx[0])
        gather = pltpu.async_copy(y_hbm.at[idx[0]], rows[0], sem_gather[0])
        stores = [None, None]
        for c in range(n_chunks):
            s = c % 2
            if c + 1 < n_chunks:
                pltpu.sync_copy(p_hbm.at[pl.ds(offset(c + 1), w)], idx[1 - s])
            gather.wait()
            if c + 1 < n_chunks:
                if stores[1 - s] is not None:
                    stores[1 - s].wait()
                    stores[1 - s] = None
                gather = pltpu.async_copy(y_hbm.at[idx[1 - s]], rows[1 - s], sem_gather[1 - s])
            stores[s] = pltpu.async_copy(rows[s], o_hbm.at[pl.ds(offset(c), w)], sem_store[s])
        for st in stores:
            if st is not None:
                st.wait()

    return k(ys, pos_flat)


def _moe_kernel(l, n_tiles_max, te_ref, tv_ref, ts_ref, nt_ref, xc_ref, xl_ref, wu_hbm, bu_ref, wd_hbm, bd_ref,
                y_ref, wu_f, wd_f, wu_s, wd_s, sem, group_s):
    i = pl.program_id(0)
    nt = nt_ref[0]

    def fetch(e, slot):
        return (pltpu.make_async_copy(wu_hbm.at[l, e], wu_f.at[slot], sem.at[0, slot]),
                pltpu.make_async_copy(wd_hbm.at[l, e], wd_f.at[slot], sem.at[1, slot]))

    @pl.when(i == 0)
    def _():
        group_s[0] = 0
        for cp in fetch(te_ref[0], 0):
            cp.start()

    @pl.when(i < nt)
    def _():
        e = te_ref[i]
        prev = te_ref[jnp.maximum(i - 1, 0)]

        @pl.when((i == 0) | (e != prev))
        def _():
            group = group_s[0]
            slot = group % 2
            j = lax.while_loop(lambda j: (j < nt) & (te_ref[jnp.minimum(j, n_tiles_max - 1)] == e),
                               lambda j: j + 1, i + 1)

            @pl.when(j < nt)
            def _():
                for cp in fetch(te_ref[jnp.minimum(j, n_tiles_max - 1)], 1 - slot):
                    cp.start()

            for cp in fetch(e, slot):
                cp.wait()
            wu_s[...] = wu_f[slot].astype(BF16)
            wd_s[...] = wd_f[slot].astype(BF16)
            group_s[0] = group + 1

        valid = tv_ref[i]
        split = ts_ref[i]
        half = MOE_ROWS
        for j in range(TME // half):
            @pl.when(valid > j * half)
            def _():
                sl = slice(j * half, (j + 1) * half)
                rows = lax.broadcasted_iota(I32, (half, 1), 0) + j * half
                words = jnp.where(rows < split, xc_ref[sl, :], xl_ref[sl, :])
                lo, hi = _unpack_rows(jnp.where(rows < valid, words, 0))
                x = jnp.concatenate([lo.astype(BF16), hi.astype(BF16)], axis=1)
                h = _dot(x, wu_s[...]) + bu_ref[...]
                h_glu = jnp.minimum(h[:, :D_EXPERT], SWIGLU_LIMIT)
                h_lin = jnp.clip(h[:, D_EXPERT:], -SWIGLU_LIMIT, SWIGLU_LIMIT)
                act = h_glu * jax.nn.sigmoid(SWIGLU_ALPHA * h_glu) * (h_lin + 1.0)
                y_ref[sl, :] = _pack_rows(_dot(act.astype(BF16), wd_s[...]) + bd_ref[...])


def _moe_call(l, tile_e, tile_v, tile_s, n_tiles, xs_ctx, xs_lat, w_up, b_up, w_down, b_down):
    n_tiles_max = xs_ctx.shape[0] // TME

    def tile(i, te, tv, ts, nt):
        return jnp.minimum(i, nt[0] - 1)

    def wmap(i, te, tv, ts, nt):
        return (l, te[tile(i, te, tv, ts, nt)], 0, 0)

    rows = pl.BlockSpec((TME, D_PACK), lambda i, te, tv, ts, nt: (tile(i, te, tv, ts, nt), 0))
    grid_spec = pltpu.PrefetchScalarGridSpec(
        num_scalar_prefetch=4,
        grid=(n_tiles_max,),
        in_specs=[rows, rows,
                  pl.BlockSpec(memory_space=pl.ANY),
                  pl.BlockSpec((None, None, 1, 2 * D_EXPERT), wmap),
                  pl.BlockSpec(memory_space=pl.ANY),
                  pl.BlockSpec((None, None, 1, D_MODEL), wmap)],
        out_specs=rows,
        scratch_shapes=[pltpu.VMEM((2, D_MODEL, 2 * D_EXPERT), F32), pltpu.VMEM((2, D_EXPERT, D_MODEL), F32),
                        pltpu.VMEM((D_MODEL, 2 * D_EXPERT), BF16), pltpu.VMEM((D_EXPERT, D_MODEL), BF16),
                        pltpu.SemaphoreType.DMA((2, 2)), pltpu.SMEM((1,), I32)],
    )
    return pl.pallas_call(
        functools.partial(_moe_kernel, l, n_tiles_max), grid_spec=grid_spec,
        out_shape=jax.ShapeDtypeStruct((xs_ctx.shape[0], D_PACK), I32),
        compiler_params=_cparams(("arbitrary",)), name="moe",
    )(tile_e, tile_v, tile_s, n_tiles, xs_ctx, xs_lat, w_up, b_up.reshape(DEPTH, N_EXPERTS, 1, -1),
      w_down, b_down.reshape(DEPTH, N_EXPERTS, 1, -1))


def _final_kernel(yg_ref, tw_ref, x1_ref, mod_ref, ln_g_ref, ln_b_ref, o_ref):
    o_ref[...] = _moe_residual_norm(yg_ref, tw_ref, x1_ref, mod_ref, ln_g_ref, ln_b_ref)


def _final_call(lat, yg, tw, x1, mod, ln_g, ln_b):
    n = x1.shape[0]
    tm = TM_MERGE
    row = lambda w: pl.BlockSpec((tm, w), lambda i: (i, 0))
    const = lambda shape: pl.BlockSpec(shape, lambda i: (0,) * len(shape))
    return pl.pallas_call(
        _final_kernel,
        grid=(n // tm,),
        in_specs=[pl.BlockSpec((TOP_K, tm, D_PACK), lambda i: (0, i, 0)), row(TOP_K), row(D_MODEL),
                  _mod_spec(lat, tm), const((1, D_MODEL)), const((1, D_MODEL))],
        out_specs=row(D_MODEL), out_shape=jax.ShapeDtypeStruct((n, D_MODEL), F32),
        compiler_params=_cparams(("arbitrary",)), name="final_lat" if lat else "final_ctx",
    )(yg, tw, x1, mod, ln_g, ln_b)


def _rope_tables():
    half = A_HEAD_DIM // 2
    freqs = ROPE_BASE ** (-jnp.arange(0, half, 2, dtype=F32) / half)
    t = jnp.arange(DEC_SEQ, dtype=jnp.int32)
    ang_r = (t // GRID_W).astype(F32)[:, None] * freqs[None, :]
    ang_c = (t % GRID_W).astype(F32)[:, None] * freqs[None, :]
    cos64 = jnp.concatenate([jnp.cos(ang_r)] * 2 + [jnp.cos(ang_c)] * 2, axis=1)
    sin64 = jnp.concatenate([-jnp.sin(ang_r), jnp.sin(ang_r), -jnp.sin(ang_c), jnp.sin(ang_c)], axis=1)
    lat = (jnp.tile(cos64, (1, 2)), jnp.tile(sin64, (1, 2)))
    ctx = (jnp.ones((TM, LANES), F32), jnp.zeros((TM, LANES), F32))
    return ctx, lat


def _permute_in_cols(t, axis=-1):
    take = lambda lo, hi: lax.slice_in_dim(t, lo, hi, axis=axis)
    pad_shape = list(t.shape)
    pad_shape[axis] = D_IN_PAD - D_IN
    parts = [take(h * 64, (h + 1) * 64) for h in QA_HEAD_ORDER]
    parts += [take(512, 2816), take(2832, 7440), take(2816, 2832), jnp.zeros(pad_shape, t.dtype)]
    return jnp.concatenate(parts, axis=axis)


def _layer_weights(p):
    w_in, b_in = p["w_in"], p["b_in"]
    wpa = p["w_pa"].reshape(A_HEADS, A_HEAD_DIM, D_MODEL)[np.asarray(QA_HEAD_ORDER)].reshape(512, D_MODEL)
    row = lambda t: t.reshape(1, -1)
    return dict(
        inproj=(_permute_in_cols(w_in.T, axis=0).astype(BF16), _permute_in_cols(b_in).reshape(1, D_IN_PAD),
                w_in[:, 2816:2832].T.astype(BF16), b_in[2816:2832].reshape(16, 1),
                row(jnp.tile(p["qn_a"], A_HEADS)), row(jnp.tile(p["kn_a"], A_KV_HEADS)),
                jnp.asarray(np.kron(np.eye(8), np.ones((64, 64))), BF16)),
        merge=(row(p["hn_b"]), wpa.astype(BF16), p["w_pb"].astype(BF16), p["w_pc"].astype(BF16),
               p["w_o"].astype(BF16), row(p["b_o"]), row(p["ln1_g"]), row(p["ln1_b"]),
               p["w_router"].T, p["b_router"].reshape(N_EXPERTS, 1)),
        lam_c=p["lam_c"], sn_c=row(p["sn_c"]), ln2=(row(p["ln2_g"]), row(p["ln2_b"])))


def _chain_mixer(l, lat, x, mod, tables, caches, w, outs, after=None):
    cos_t, sin_t = tables
    lam_init = 0.8 - 0.6 * math.exp(-0.3 * l)
    res = _inproj_call(l, lat, x, mod, cos_t, sin_t, *w["inproj"], None if outs is None else outs[:4], after)
    qa, kab, vab, qb, kb, vb, ob, qc, kcb, vcb, g, ifg, ifgt = res[:13]
    if isinstance(x, tuple):
        x = res[-1]
    if lat:
        ck_a, cv_a, ck_c, cv_c, cn0, m0 = caches
    else:
        ck_a = cv_a = ck_c = cv_c = cn0 = m0 = None
    oa = _attn_call("a", lat, lam_init, qa, kab, vab, ck_a, cv_a, None, None)
    oc = _attn_call("c", lat, lam_init, qc, kcb, vcb, ck_c, cv_c, w["lam_c"], w["sn_c"])
    ml = _mlstm_call(l, lat, qb, kb, vb, ifg, ifgt, cn0, m0, None if outs is None else outs[4:6])
    hf, hb = ml[:2]
    merged = _merge_call(lat, x, mod, oa, hf, hb, ob, oc, g, *w["merge"])
    if lat:
        return merged, None
    m_l = ml[4][:, :, :, 0, 0]
    m_all = m_l[:, None] if outs is None else jnp.concatenate([outs[6], m_l[:, None]], axis=1)
    return merged, tuple(res[13:17]) + (ml[2], ml[3], m_all)


def _moe_layer(l, mix_ctx, mix_lat, mod, w, experts):
    top_i = jnp.concatenate([mix_ctx[2], mix_lat[2]], axis=1)
    pos, tile_e, tile_v, tile_s, n_tiles = _route_call(top_i, N_CTX)
    p_rows = _max_tiles(N_CTX + N_LAT) * TME
    pos_flat = (pos[:, :N_CTX].reshape(TOP_K * N_CTX), pos[:, N_CTX:].reshape(TOP_K * N_LAT))
    xs = [_dispatch_call(mix[1], pf, p_rows) for mix, pf in zip((mix_ctx, mix_lat), pos_flat)]
    ys = _moe_call(l, tile_e[0], tile_v[0], tile_s[0], n_tiles[0, :1], xs[0], xs[1],
                   experts["w_up"], experts["b_up"], experts["w_down"], experts["b_down"])
    yg_ctx = _combine_call(ys, pos_flat[0]).reshape(TOP_K, N_CTX, D_PACK)
    yg_lat = _combine_call(ys, pos_flat[1]).reshape(TOP_K, N_LAT, D_PACK)
    return ((yg_ctx, mix_ctx[3].T, mix_ctx[0], mod) + w["ln2"], (yg_lat, mix_lat[3].T, mix_lat[0], mod) + w["ln2"])


def kernel(x_prompt, x_sample, cache_k_gqa, cache_v_gqa, cache_k_diff, cache_v_diff, state_mlstm_c, state_mlstm_n, state_mlstm_m, c, c_ctx, w_in, b_in, qn_a, kn_a, hn_b, lam_c, sn_c, w_pa, w_pb, w_pc, w_o, b_o, w_mod, b_mod, ln1_g, ln1_b, ln2_g, ln2_b, w_router, b_router, w_up, b_up, w_down, b_down):
    x_ctx = x_prompt.reshape(N_CTX, D_MODEL)
    x_lat = x_sample.reshape(N_LAT, D_MODEL)
    cvecs = jnp.zeros((8, D_MODEL), F32).at[0].set(c_ctx).at[1:1 + DEC_BATCH].set(c)
    tables_ctx, tables_lat = _rope_tables()
    params = dict(w_in=w_in, b_in=b_in, qn_a=qn_a, kn_a=kn_a, hn_b=hn_b, lam_c=lam_c, sn_c=sn_c, w_pa=w_pa,
                  w_pb=w_pb, w_pc=w_pc, w_o=w_o, b_o=b_o, ln1_g=ln1_g, ln1_b=ln1_b, ln2_g=ln2_g, ln2_b=ln2_b,
                  w_router=w_router, b_router=b_router, w_mod=w_mod, b_mod=b_mod)
    experts = dict(w_up=w_up, b_up=b_up, w_down=w_down, b_down=b_down)
    outs = None
    mix_lat = None
    for l in range(DEPTH):
        if mix_lat is not None:
            params, _ = lax.optimization_barrier((params, mix_lat[3]))
        mod = _mod_call(l, cvecs, params["w_mod"], params["b_mod"]).reshape(8, N_MOD, D_MODEL)[:N_GROUPS]
        w = _layer_weights({k: v[l] for k, v in params.items() if k not in ("w_mod", "b_mod")})
        ck_a = cache_k_gqa[:, l].transpose(0, 2, 1, 3).reshape(DEC_BATCH, PAST_LEN, 128).astype(BF16)
        cv_a = cache_v_gqa[:, l].transpose(0, 2, 1, 3).reshape(DEC_BATCH, PAST_LEN, 128).astype(BF16)
        ck_c = cache_k_diff[:, l].astype(BF16)
        cv_c = cache_v_diff[:, l].astype(BF16)
        cn0 = jnp.concatenate([state_mlstm_c[:, l], state_mlstm_n[:, l][..., None],
                               jnp.zeros((DEC_BATCH, 2, B_HEADS, B_HEAD_DIM, B_HEAD_DIM - 1), F32)], axis=-1)
        m0 = jnp.broadcast_to(state_mlstm_m[:, l][..., None, None], (DEC_BATCH, 2, B_HEADS, 1, LANES))
        mix_ctx, outs = _chain_mixer(l, False, x_ctx, mod, tables_ctx, None, w, outs)
        mix_lat, _ = _chain_mixer(l, True, x_lat, mod, tables_lat, (ck_a, cv_a, ck_c, cv_c, cn0, m0), w, None,
                                  after=mix_ctx[3] if l > 0 else None)
        x_ctx, x_lat = _moe_layer(l, mix_ctx, mix_lat, mod, w, experts)
    x_ctx = _final_call(False, *x_ctx)
    x_lat = _final_call(True, *x_lat)
    return (x_ctx.reshape(BATCH, SEQ, D_MODEL), x_lat.reshape(DEC_BATCH, DEC_SEQ, D_MODEL)) + tuple(outs)
```

```python
import functools
import math

import numpy as np
import jax
import jax.numpy as jnp
from jax import lax
from jax.experimental import pallas as pl
from jax.experimental.pallas import tpu as pltpu
from jax.experimental.pallas import tpu_sc as plsc

F32 = jnp.float32
BF16 = jnp.bfloat16
I32 = jnp.int32

D_MODEL = 1024
BATCH = 16
SEQ = 256
DEPTH = 2
DEC_BATCH = 2
DEC_SEQ = 4096
PAST_LEN = 512
GRID_W = 64
ROPE_BASE = 10000.0
EPS = 1e-6
A_HEADS = 8
A_KV_HEADS = 2
A_HEAD_DIM = 64
B_HEADS = 4
B_HEAD_DIM = 128
MLSTM_CHUNK = 128
C_HEADS = 4
C_HEAD_DIM = 64
C_V_DIM = 128
N_EXPERTS = 32
TOP_K = 4
D_EXPERT = 1024
SWIGLU_LIMIT = 7.0
SWIGLU_ALPHA = 1.702
N_MOD = 6
DEEPNORM_ALPHA = (2 * DEPTH) ** 0.25

N_CTX = BATCH * SEQ
N_LAT = DEC_BATCH * DEC_SEQ
N_GROUPS = 1 + DEC_BATCH

LANES = 128
TM = 256
TQ = 512
KEY_BLOCK = 2048
MLSTM_SEQS_LAT = 2
MLSTM_SEQS_CTX = 4
TM_MERGE = 512
MERGE_ROWS = 128
TME = 1024
MOE_ROWS = 256
TILE_LANES = 128
ROUTE_CHUNK = 512
D_PACK = D_MODEL // 2
SC_WINDOW = 64
VMEM_LIMIT = 56 * 1024 * 1024

QA_HEAD_ORDER = (0, 4, 1, 5, 2, 6, 3, 7)
D_IN = 7440
D_IN_PAD = 7552
P_QA, P_KA, P_VA, P_QB, P_KB, P_VB, P_OB, P_QC, P_KC, P_VC, P_G, P_IF = (
    0, 512, 640, 768, 1280, 1792, 2304, 2816, 3328, 3840, 4352, 7424)


def _max_tiles(n_tok):
    return (n_tok * TOP_K) // TME + N_EXPERTS


def _cparams(sem):
    return pltpu.CompilerParams(dimension_semantics=sem, vmem_limit_bytes=VMEM_LIMIT)


def _split3(a):
    a1 = a.astype(BF16)
    r = a - a1.astype(F32)
    a2 = r.astype(BF16)
    a3 = (r - a2.astype(F32)).astype(BF16)
    return a1, a2, a3


def _dot(a, b, dims=None):
    if dims is None:
        return jnp.dot(a, b, preferred_element_type=F32)
    return lax.dot_general(a, b, (dims, ((), ())), preferred_element_type=F32)


def _dot_exact_rhs01(a, m, dims=None):
    return sum(_dot(p, m, dims) for p in _split3(a))


def _dot_exact_lhs01(m, a, dims=None):
    return sum(_dot(m, p, dims) for p in _split3(a))


def _pack_rows(x):
    xb = x.astype(BF16).astype(F32)
    lo = pltpu.bitcast(xb[:, :D_PACK], I32)
    lo = lax.shift_right_logical(lo, jnp.full_like(lo, 16))
    hi = pltpu.bitcast(xb[:, D_PACK:], I32) & jnp.int32(-65536)
    return lo | hi


def _unpack_rows(w):
    return (pltpu.bitcast(lax.shift_left(w, jnp.full_like(w, 16)), F32),
            pltpu.bitcast(w & jnp.int32(-65536), F32))


def _ln(x):
    mu = jnp.mean(x, axis=-1, keepdims=True)
    xc = x - mu
    var = jnp.mean(xc * xc, axis=-1, keepdims=True)
    return xc * lax.rsqrt(var + EPS)


def _log_sigmoid(x):
    return jnp.minimum(x, 0.0) - jnp.log(1.0 + jnp.exp(-jnp.abs(x)))


def _rope(x, cos, sin):
    lane = lax.broadcasted_iota(I32, (1, LANES), 1)
    first = (lane % 32) < 16
    outs = []
    for j in range(x.shape[1] // LANES):
        xj = x[:, j * LANES:(j + 1) * LANES]
        partner = jnp.where(first, pltpu.roll(xj, LANES - 16, 1), pltpu.roll(xj, 16, 1))
        outs.append(xj * cos + partner * sin)
    return outs[0] if len(outs) == 1 else jnp.concatenate(outs, axis=1)


def _mod_spec(lat, tile=TM):
    if lat:
        return pl.BlockSpec((1, N_MOD, D_MODEL), lambda i: (1 + i * tile // DEC_SEQ, 0, 0))
    return pl.BlockSpec((1, N_MOD, D_MODEL), lambda i: (0, 0, 0))


def _mod_kernel(c_ref, w_ref, b_ref, o_ref):
    c = c_ref[...]
    s = c * jax.nn.sigmoid(c)
    o_ref[...] = _dot(s.astype(BF16), w_ref[...].astype(BF16)) + b_ref[...]


def _mod_call(cvecs, w_mod, b_mod):
    tn = 1536
    n = N_MOD * D_MODEL
    return pl.pallas_call(
        _mod_kernel,
        grid=(DEPTH, n // tn),
        in_specs=[pl.BlockSpec((8, D_MODEL), lambda l, j: (0, 0)),
                  pl.BlockSpec((None, D_MODEL, tn), lambda l, j: (l, 0, j)),
                  pl.BlockSpec((None, 1, tn), lambda l, j: (l, 0, j))],
        out_specs=pl.BlockSpec((None, 8, tn), lambda l, j: (l, 0, j)),
        out_shape=jax.ShapeDtypeStruct((DEPTH, 8, n), F32),
        compiler_params=_cparams(("arbitrary", "arbitrary")),
        name="mod",
    )(cvecs, w_mod, b_mod.reshape(DEPTH, 1, n))


def _moe_residual_norm(yg_ref, tw_ref, x1_ref, mod_ref, ln_g_ref, ln_b_ref):
    tw = tw_ref[...]
    halves = [_unpack_rows(yg_ref[k]) for k in range(TOP_K)]
    moe = jnp.concatenate([sum(tw[:, k:k + 1] * halves[k][j] for k in range(TOP_K)) for j in range(2)], axis=1)
    gate2 = mod_ref[0, 5:6, :]
    return _ln(DEEPNORM_ALPHA * x1_ref[...] + gate2 * moe) * ln_g_ref[...] + ln_b_ref[...]


def _inproj_kernel(with_cache, fused, n_unread, *refs):
    if fused:
        x = _moe_residual_norm(*refs[:6])
        refs = refs[6:]
    else:
        x = refs[0][...]
        refs = refs[1:]
    mod_ref, cos_ref, sin_ref, w_ref, b_ref, wgt_ref, bgt_ref, gq_ref, gk_ref, gmat_ref = refs[:10]
    outs = refs[10 + n_unread:]
    qa_o, kab_o, vab_o, qb_o, kb_o, vb_o, ob_o, qc_o, kcb_o, vcb_o, g_o, ifg_o, ifgt_o = outs[:13]
    if fused:
        outs[-1][...] = x
    shift = mod_ref[0, 0:1, :]
    scale = mod_ref[0, 1:2, :]
    ub = (_ln(x) * (1.0 + scale) + shift).astype(BF16)
    cos = cos_ref[...]
    sin = sin_ref[...]

    def proj(lo, n):
        return _dot(ub, w_ref[lo:lo + n, :], ((1,), (1,))) + b_ref[:, lo:lo + n]

    def headnorm(z, gmat, gain):
        ss = _dot((z * z).astype(BF16), gmat)
        return z * lax.rsqrt(ss * (1.0 / A_HEAD_DIM) + EPS) * gain

    segs = [(P_QA, 512), (P_KA, 128), (P_VA, 128), (P_QB, 512), (P_KB, 512), (P_VB, 512), (P_OB, 512),
            (P_QC, 512), (P_KC, 512), (P_VC, 512), (P_G, 1024), (P_G + 1024, 1024), (P_G + 2048, 1024),
            (P_IF, LANES)]
    keep = {}

    def epilogue(k, z):
        if k == 0:
            z = headnorm(z, gmat_ref[...], gq_ref[...])
            qa_o[...] = (_rope(z, cos, sin) * 0.125).astype(BF16)
        elif k == 1:
            keep["k"] = _rope(headnorm(z, gmat_ref[0:128, 0:128], gk_ref[...]), cos, sin)
            kab_o[...] = keep["k"].astype(BF16)
        elif k == 2:
            keep["v"] = z
            vab_o[...] = z.astype(BF16)
        elif k == 3:
            qb_o[...] = z.astype(BF16)
        elif k == 4:
            kb_o[...] = (z * (B_HEAD_DIM ** -0.5)).astype(BF16)
        elif k == 5:
            vb_o[...] = z.astype(BF16)
        elif k == 6:
            ob_o[...] = z.astype(BF16)
        elif k == 7:
            qc_o[...] = (_rope(z, cos, sin) * 0.125).astype(BF16)
        elif k == 8:
            keep["kc"] = _rope(z, cos, sin)
            kcb_o[...] = keep["kc"].astype(BF16)
        elif k == 9:
            keep["vc"] = z
            vcb_o[...] = z.astype(BF16)
        elif k in (10, 11, 12):
            g_o[:, (k - 10) * 1024:(k - 9) * 1024] = jax.nn.sigmoid(z).astype(BF16)
        else:
            lane = lax.broadcasted_iota(I32, (1, LANES), 1)
            ifg_o[...] = jnp.where(lane >= 8, _log_sigmoid(z), z)

    z_prev = proj(*segs[0])
    for k in range(1, len(segs)):
        z_next = proj(*segs[k])
        epilogue(k - 1, z_prev)
        z_prev = z_next
    zt = _dot(wgt_ref[...], ub, ((1,), (1,))) + bgt_ref[...]
    epilogue(len(segs) - 1, z_prev)
    row = lax.broadcasted_iota(I32, (16, 1), 0)
    ifgt_o[...] = jnp.where(row >= 8, _log_sigmoid(zt), zt)

    if with_cache:
        nka_o, nva_o, nkc_o, nvc_o = outs[13:17]
        for h in range(A_KV_HEADS):
            nka_o[h] = keep["k"][:, h * A_HEAD_DIM:(h + 1) * A_HEAD_DIM]
            nva_o[h] = keep["v"][:, h * A_HEAD_DIM:(h + 1) * A_HEAD_DIM]
        for h in range(C_HEADS):
            nkc_o[h] = keep["kc"][:, h * LANES:(h + 1) * LANES]
            nvc_o[h] = keep["vc"][:, h * LANES:(h + 1) * LANES]


def _inproj_call(l, lat, x, mod, cos_t, sin_t, w_p, b_p, w_gt, b_gt, gq, gk, gmat, cache_outs, after=None):
    fused = isinstance(x, tuple)
    n = x[2].shape[0] if fused else x.shape[0]
    row = lambda w: pl.BlockSpec((TM, w), lambda i: (i, 0))
    const = lambda shape: pl.BlockSpec(shape, lambda i: (0,) * len(shape), pipeline_mode=pl.Buffered(1))
    sds = lambda w, dt: jax.ShapeDtypeStruct((n, w), dt)
    out_defs = [(512, BF16), (128, BF16), (128, BF16), (512, BF16), (512, BF16), (512, BF16), (512, BF16),
                (512, BF16), (512, BF16), (512, BF16), (3072, BF16), (LANES, F32)]
    seq = DEC_SEQ if lat else SEQ
    tiles_per_seq = seq // TM
    out_specs = [row(w) for w, _ in out_defs] + [
        pl.BlockSpec((None, 16, TM), lambda i: (i // tiles_per_seq, 0, i % tiles_per_seq))]
    out_shape = [sds(w, dt) for w, dt in out_defs] + [jax.ShapeDtypeStruct((n // seq, 16, seq), F32)]
    alias_in, aliases = [], {}
    if lat:
        tbl = pl.BlockSpec((TM, LANES), lambda i: (i % (DEC_SEQ // TM), 0))
    else:
        assert TM == SEQ
        tbl = pl.BlockSpec((TM, LANES), lambda i: (0, 0))
        cache_defs = [(A_KV_HEADS, A_HEAD_DIM), (A_KV_HEADS, A_HEAD_DIM), (C_HEADS, LANES), (C_HEADS, LANES)]
        out_specs += [pl.BlockSpec((None, None, nh, SEQ, d), lambda i: (i, l, 0, 0, 0)) for nh, d in cache_defs]
        out_shape += [jax.ShapeDtypeStruct((BATCH, DEPTH, nh, SEQ, d), F32) for nh, d in cache_defs]
        if cache_outs is not None:
            alias_in = list(cache_outs)
    if fused:
        plain = lambda shape: pl.BlockSpec(shape, lambda i: (0,) * len(shape))
        lead_specs = [pl.BlockSpec((TOP_K, TM, D_PACK), lambda i: (0, i, 0)), row(TOP_K), row(D_MODEL),
                      _mod_spec(lat), plain((1, D_MODEL)), plain((1, D_MODEL))]
        lead = list(x)
        out_specs.append(row(D_MODEL))
        out_shape.append(sds(D_MODEL, F32))
    else:
        lead_specs, lead = [row(D_MODEL)], [x]
    n_in = len(lead) + 10
    aliases = {n_in + j: len(out_defs) + 1 + j for j in range(len(alias_in))}
    unread = alias_in + ([] if after is None else [after])
    return pl.pallas_call(
        functools.partial(_inproj_kernel, not lat, fused, len(unread)),
        grid=(n // TM,),
        in_specs=lead_specs + [_mod_spec(lat), tbl, tbl,
                               const((D_IN_PAD, D_MODEL)), const((1, D_IN_PAD)),
                               const((16, D_MODEL)), const((16, 1)),
                               const((1, 512)), const((1, 128)), const((512, 512))]
        + [pl.BlockSpec(memory_space=pl.ANY)] * len(unread),
        out_specs=out_specs, out_shape=out_shape, input_output_aliases=aliases,
        compiler_params=_cparams(("arbitrary",)),
        name="inproj_lat" if lat else "inproj_ctx",
    )(*lead, mod, cos_t, sin_t, w_p, b_p, w_gt, b_gt, gq, gk, gmat, *unread)


def _attn_kernel(mode, nseg, lam_init, *refs):
    q_ref = refs[0]
    kv = refs[1:1 + 2 * nseg]
    rest = refs[1 + 2 * nseg:]
    o_ref = rest[-1]
    tq = q_ref.shape[0]
    lane = lax.broadcasted_iota(I32, (1, LANES), 1)
    if mode == "c":
        lam_ref, sn_ref = rest[0], rest[1]
        lp = lam_ref[...]
        lam = (jnp.exp(jnp.sum(lp[0:1] * lp[1:2], axis=-1, keepdims=True))
               - jnp.exp(jnp.sum(lp[2:3] * lp[3:4], axis=-1, keepdims=True)) + lam_init)
    for blk in range(q_ref.shape[1] // LANES):
        qs = slice(blk * LANES, (blk + 1) * LANES)
        ks = qs if kv[0].shape[1] > LANES else slice(0, LANES)
        q = q_ref[:, qs]
        zero = jnp.zeros_like(q)
        qq = jnp.concatenate([jnp.where(lane < 64, q, zero), jnp.where(lane >= 64, q, zero)], axis=0)
        m = l = acc = None
        for j in range(nseg):
            k_ref, v_ref = kv[2 * j], kv[2 * j + 1]
            for off in range(0, k_ref.shape[0], KEY_BLOCK):
                n = min(KEY_BLOCK, k_ref.shape[0] - off)
                s = _dot(qq, k_ref[off:off + n, ks], ((1,), (1,)))
                bm = jnp.max(s, axis=-1, keepdims=True)
                if m is None:
                    m = bm
                    e = jnp.exp(s - m)
                    l = jnp.sum(e, axis=-1, keepdims=True)
                    acc = _dot(e.astype(BF16), v_ref[off:off + n, ks])
                else:
                    m_new = jnp.maximum(m, bm)
                    alpha = jnp.exp(m - m_new)
                    e = jnp.exp(s - m_new)
                    l = alpha * l + jnp.sum(e, axis=-1, keepdims=True)
                    acc = alpha * acc + _dot(e.astype(BF16), v_ref[off:off + n, ks])
                    m = m_new
        o = acc * (1.0 / l)
        if mode == "a":
            out = jnp.where(lane < 64, o[:tq], o[tq:])
        else:
            o = o[:tq] - lam * o[tq:]
            ms = jnp.mean(o * o, axis=-1, keepdims=True)
            out = o * lax.rsqrt(ms + EPS) * sn_ref[...] * (1.0 - lam_init)
        o_ref[:, qs] = out.astype(o_ref.dtype)


def _attn_call(mode, lat, lam_init, q_all, k_all, v_all, k_cache, v_cache, lam_c, sn_c):
    wide = k_all.shape[1] == 512
    kcol = (lambda h: h) if wide else (lambda h: 0)
    if lat:
        nq = DEC_SEQ // TQ
        grid = (DEC_BATCH, nq, 4)
        q_spec = pl.BlockSpec((TQ, LANES), lambda b, i, h: (b * nq + i, h))
        kn_spec = pl.BlockSpec((DEC_SEQ, LANES), lambda b, i, h: (b, kcol(h)))
        if wide:
            kc_spec = pl.BlockSpec((None, None, PAST_LEN, LANES), lambda b, i, h: (b, h, 0, 0))
        else:
            kc_spec = pl.BlockSpec((None, PAST_LEN, LANES), lambda b, i, h: (b, 0, 0))
        in_specs = [q_spec, kc_spec, kc_spec, kn_spec, kn_spec]
        args = [q_all, k_cache, v_cache, k_all, v_all]
        nseg = 2
        o_spec = q_spec
        sem = ("arbitrary", "arbitrary", "arbitrary")
        zero_map = lambda b, i, h: (0, 0)
    else:
        grid = (BATCH,)
        q_spec = pl.BlockSpec((SEQ, 512), lambda b: (b, 0))
        k_spec = pl.BlockSpec((SEQ, k_all.shape[1]), lambda b: (b, 0))
        in_specs = [q_spec, k_spec, k_spec]
        args = [q_all, k_all, v_all]
        nseg = 1
        o_spec = q_spec
        sem = ("arbitrary",)
        zero_map = lambda b: (0, 0)
    if mode == "c":
        in_specs += [pl.BlockSpec((4, C_HEAD_DIM), zero_map), pl.BlockSpec((1, C_V_DIM), zero_map)]
        args += [lam_c, sn_c]
    return pl.pallas_call(
        functools.partial(_attn_kernel, mode, nseg, lam_init),
        grid=grid, in_specs=in_specs, out_specs=o_spec,
        out_shape=jax.ShapeDtypeStruct((q_all.shape[0], 512), BF16),
        compiler_params=_cparams(sem), name=f"attn_{mode}_{'lat' if lat else 'ctx'}",
    )(*args)


def _mlstm_kernel(has_init, want_state, n_alias, *refs):
    tok = (refs[0:5], refs[5:10])
    pos = 10
    if has_init:
        cn0_ref, m0_ref = refs[pos:pos + 2]
        pos += 2
    pos += n_alias
    h_refs = refs[pos:pos + 2]
    pos += 2
    if want_state:
        c_o, n_o, m_o = refs[pos:pos + 3]
        pos += 3
    cn_s, m_s = refs[pos:pos + 2]

    c = pl.program_id(1)
    last = pl.num_programs(1) - 1
    L = MLSTM_CHUNK

    @pl.when(c == 0)
    def _():
        if has_init:
            cn_s[...] = cn0_ref[...]
            m_s[...] = m0_ref[...]
        else:
            cn_s[...] = jnp.zeros_like(cn_s)
            m_s[...] = jnp.zeros_like(m_s)

    row = lax.broadcasted_iota(I32, (L, L), 0)
    col = lax.broadcasted_iota(I32, (L, L), 1)
    lane = lax.broadcasted_iota(I32, (1, LANES), 1)
    ones_col = jnp.where(lane == 0, 1.0, 0.0).astype(BF16)

    chains = []
    for d in range(2):
        q_ref, k_ref, v_ref, g_ref, gt_ref = tok[d]
        keep = (col <= row) if d == 0 else (col >= row)
        m_c = keep.astype(BF16)
        m_r = ((row <= col) if d == 0 else (row >= col)).astype(BF16)
        for s in range(q_ref.shape[0]):
            g = g_ref[s]
            gt = gt_ref[s]
            bc = _dot_exact_lhs01(m_c, g)
            br = _dot_exact_rhs01(gt, m_r)
            for h in range(B_HEADS):
                sl = slice(h * L, (h + 1) * L)
                gi, gf = 4 * d + h, 8 + 4 * d + h
                ch = dict(seq=s, d=d, h=h, sl=sl, keep=keep, q=q_ref[s, :, sl], k=k_ref[s, :, sl], v=v_ref[s, :, sl],
                          i_col=g[:, gi:gi + 1], f_col=g[:, gf:gf + 1], b_col=bc[:, gf:gf + 1],
                          r_row=gt[gi:gi + 1, :] - br[gf:gf + 1, :],
                          m_prev=m_s[s, d, h][:, 0:1], cn_prev=cn_s[s, d, h])
                chains.append(ch)
    for ch in chains:
        ch["s"] = _dot(ch["q"], ch["k"], ((1,), (1,)))
        ch["qc"] = _dot(ch["q"], ch["cn_prev"].astype(BF16))
    for ch in chains:
        ch["total"] = jnp.sum(ch["f_col"], axis=0, keepdims=True)
        ch["inter"] = ch["b_col"] + ch["m_prev"]
        ch["v_aug"] = jnp.concatenate([ch["v"], jnp.broadcast_to(ones_col, (L, LANES))], axis=1)
    for ch in chains:
        ch["rmat"] = jnp.where(ch["keep"], ch["r_row"], -jnp.inf)
    for ch in chains:
        ch["dmax"] = ch["b_col"] + jnp.max(ch["rmat"], axis=-1, keepdims=True)
    for ch in chains:
        ch["m_row"] = jnp.maximum(ch["inter"], ch["dmax"])
    for ch in chains:
        ch["sc"] = (ch["s"] * jnp.exp(ch["rmat"] + (ch["b_col"] - ch["m_row"]))).astype(BF16)
    for ch in chains:
        ch["a"] = _dot(ch["sc"], ch["v_aug"])
    for ch in chains:
        ch["nd"] = jnp.exp(ch["inter"] - ch["m_row"]) * ch["qc"] + ch["a"]
    for ch in chains:
        ch["scale"] = 1.0 / jnp.maximum(jnp.abs(ch["nd"][:, L:L + 1]), jnp.exp(-ch["m_row"]))
    for ch in chains:
        h_refs[ch["d"]][ch["seq"], :, ch["sl"]] = ch["nd"][:, :L] * ch["scale"]
    for ch in chains:
        ch["wel"] = ch["total"] - ch["b_col"] + ch["i_col"]
        ch["m_new"] = jnp.maximum(ch["total"] + ch["m_prev"], jnp.max(ch["wel"], axis=0, keepdims=True))
    for ch in chains:
        ch["kw"] = (jnp.exp(ch["wel"] - ch["m_new"]) * ch["k"].astype(F32)).astype(BF16)
    for ch in chains:
        ch["u"] = _dot(ch["kw"], ch["v_aug"], ((0,), (0,)))
    for ch in chains:
        key = (ch["seq"], ch["d"], ch["h"])
        cn_s[key] = jnp.exp(ch["total"] + ch["m_prev"] - ch["m_new"]) * ch["cn_prev"] + ch["u"]
        m_s[key] = jnp.broadcast_to(ch["m_new"], (1, LANES))

    if want_state:
        @pl.when(c == last)
        def _():
            pick = ((lax.broadcasted_iota(I32, (8, 2 * L), 1) == L)
                    & (lax.broadcasted_iota(I32, (8, 2 * L), 0) == 0)).astype(BF16)
            for s in range(cn_s.shape[0]):
                for d in range(2):
                    for h in range(B_HEADS):
                        cn = cn_s[s, d, h]
                        c_o[s, d, h] = cn[:, :L]
                        n_o[s, d, h:h + 1, :] = _dot_exact_lhs01(pick, cn, ((1,), (1,)))[0:1]
            m_o[...] = m_s[...]


def _mlstm_call(l, lat, q_all, k_all, v_all, g_all, gt_all, cn0, m0, state_bufs):
    L = MLSTM_CHUNK
    ns = MLSTM_SEQS_LAT if lat else MLSTM_SEQS_CTX
    n = q_all.shape[0]
    nb, seq = (DEC_BATCH, DEC_SEQ) if lat else (BATCH, SEQ)
    nc = seq // L
    chunk = (lambda c: c, lambda c: nc - 1 - c)
    by_seq = lambda t: t.reshape(nb, seq, t.shape[-1])

    in_specs, args = [], []
    for d in range(2):
        tok = lambda w, d=d: pl.BlockSpec((ns, L, w), lambda b, c: (b, chunk[d](c), 0))
        in_specs += [tok(512), tok(512), tok(512), tok(LANES),
                     pl.BlockSpec((ns, 16, L), lambda b, c, d=d: (b, 0, chunk[d](c)))]
        args += [by_seq(q_all), by_seq(k_all), by_seq(v_all), by_seq(g_all), gt_all]
    if lat:
        in_specs += [pl.BlockSpec((ns, 2, B_HEADS, L, 2 * L), lambda b, c: (b, 0, 0, 0, 0)),
                     pl.BlockSpec((ns, 2, B_HEADS, 1, LANES), lambda b, c: (b, 0, 0, 0, 0))]
        args += [cn0, m0]
    aliases = {}
    alias_in = list(state_bufs) if state_bufs is not None else []
    if alias_in:
        aliases = {len(args): 2, len(args) + 1: 3}
    in_specs += [pl.BlockSpec(memory_space=pl.ANY)] * len(alias_in)
    args += alias_in
    out_specs = [pl.BlockSpec((ns, L, 512), lambda b, c, d=d: (b, chunk[d](c), 0)) for d in range(2)]
    out_shape = [jax.ShapeDtypeStruct((nb, seq, 512), F32)] * 2
    if not lat:
        out_specs += [pl.BlockSpec((ns, None, 2, B_HEADS, L, L), lambda b, c: (b, l, 0, 0, 0, 0)),
                      pl.BlockSpec((ns, None, 2, B_HEADS, L), lambda b, c: (b, l, 0, 0, 0)),
                      pl.BlockSpec((ns, 2, B_HEADS, 1, LANES), lambda b, c: (b, 0, 0, 0, 0))]
        out_shape += [jax.ShapeDtypeStruct((BATCH, DEPTH, 2, B_HEADS, L, L), F32),
                      jax.ShapeDtypeStruct((BATCH, DEPTH, 2, B_HEADS, L), F32),
                      jax.ShapeDtypeStruct((nb, 2, B_HEADS, 1, LANES), F32)]
    res = pl.pallas_call(
        functools.partial(_mlstm_kernel, lat, not lat, len(alias_in)),
        grid=(nb // ns, nc), in_specs=in_specs, out_specs=out_specs, out_shape=out_shape,
        input_output_aliases=aliases,
        scratch_shapes=[pltpu.VMEM((ns, 2, B_HEADS, L, 2 * L), F32), pltpu.VMEM((ns, 2, B_HEADS, 1, LANES), F32)],
        compiler_params=_cparams(("arbitrary", "arbitrary")), name="mlstm_lat" if lat else "mlstm_ctx",
    )(*args)
    return [res[0].reshape(n, 512), res[1].reshape(n, 512)] + list(res[2:])


def _merge_kernel(x_ref, mod_ref, oa_ref, hf_ref, hb_ref, ob_ref, oc_ref, g_ref, hn_ref,
                  wpa_ref, wpb_ref, wpc_ref, wo_ref, bo_ref, ln_g_ref, ln_b_ref, wr_ref, br_ref,
                  x1_o, u2_o, ti_o, tw_o):
    gate1 = mod_ref[0, 2:3, :]
    shift2 = mod_ref[0, 3:4, :]
    scale2 = mod_ref[0, 4:5, :]
    wr = _split3(wr_ref[...])
    eio = lax.broadcasted_iota(I32, (N_EXPERTS, 1), 0)
    nt = ((1,), (1,))
    cs = [dict(rs=slice(r, r + MERGE_ROWS)) for r in range(0, TM_MERGE, MERGE_ROWS)]
    for c in cs:
        c["pa"] = _dot(oa_ref[c["rs"], :], wpa_ref[...])
        c["pc"] = _dot(oc_ref[c["rs"], :], wpc_ref[...])
    for c in cs:
        hm = hf_ref[c["rs"], :] + hb_ref[c["rs"], :]
        parts = []
        for h in range(B_HEADS):
            hh = hm[:, h * LANES:(h + 1) * LANES]
            parts.append(hh * lax.rsqrt(jnp.mean(hh * hh, axis=-1, keepdims=True) + EPS))
        c["out_b"] = (jnp.concatenate(parts, axis=1) * hn_ref[...]
                      * jax.nn.sigmoid(ob_ref[c["rs"], :].astype(F32))).astype(BF16)
    for c in cs:
        c["pb"] = _dot(c["out_b"], wpb_ref[...])
    for c in cs:
        rs = c["rs"]
        c["merged"] = (g_ref[rs, 0:1024].astype(F32) * c["pa"] + g_ref[rs, 1024:2048].astype(F32) * c["pb"]
                       + g_ref[rs, 2048:3072].astype(F32) * c["pc"]).astype(BF16)
    for c in cs:
        c["y"] = _dot(c["merged"], wo_ref[...]) + bo_ref[...]
    for c in cs:
        c["x1"] = _ln(DEEPNORM_ALPHA * x_ref[c["rs"], :] + gate1 * c["y"]) * ln_g_ref[...] + ln_b_ref[...]
        x1_o[c["rs"], :] = c["x1"]
    for c in cs:
        c["u2"] = _ln(c["x1"]) * (1.0 + scale2) + shift2
        u2_o[c["rs"], :] = _pack_rows(c["u2"])
    for c in cs:
        up = _split3(c["u2"])
        c["cur"] = (_dot(wr[0], up[0], nt) + _dot(wr[0], up[1], nt) + _dot(wr[1], up[0], nt)
                    + _dot(wr[0], up[2], nt) + _dot(wr[2], up[0], nt) + _dot(wr[1], up[1], nt)) + br_ref[...]
        c["vals"], c["idxs"] = [], []
    for _ in range(TOP_K):
        for c in cs:
            mx = jnp.max(c["cur"], axis=0, keepdims=True)
            ix = jnp.min(jnp.where(c["cur"] == mx, eio, N_EXPERTS), axis=0, keepdims=True)
            c["vals"].append(mx)
            c["idxs"].append(ix)
            c["cur"] = jnp.where(eio == ix, -jnp.inf, c["cur"])
    for c in cs:
        ex = [jnp.exp(v - c["vals"][0]) for v in c["vals"]]
        inv = 1.0 / sum(ex)
        ti_o[:, c["rs"]] = jnp.concatenate(c["idxs"], axis=0)
        tw_o[:, c["rs"]] = jnp.concatenate([e * inv for e in ex], axis=0)


def _merge_call(lat, x, mod, out_a, hf, hb, ob, out_c, g, hn, wpa, wpb, wpc, wo, bo, ln_g, ln_b, wr_t, br):
    n = x.shape[0]
    tm = TM_MERGE
    row = lambda w: pl.BlockSpec((tm, w), lambda i: (i, 0))
    const = lambda shape: pl.BlockSpec(shape, lambda i: (0,) * len(shape), pipeline_mode=pl.Buffered(1))
    return pl.pallas_call(
        _merge_kernel,
        grid=(n // tm,),
        in_specs=[row(D_MODEL), _mod_spec(lat, tm),
                  row(512), row(512), row(512), row(512), row(512), row(3072),
                  const((1, 512)), const((512, D_MODEL)), const((512, D_MODEL)), const((512, D_MODEL)),
                  const((D_MODEL, D_MODEL)), const((1, D_MODEL)), const((1, D_MODEL)), const((1, D_MODEL)),
                  const((N_EXPERTS, D_MODEL)), const((N_EXPERTS, 1))],
        out_specs=[row(D_MODEL), row(D_PACK),
                   pl.BlockSpec((TOP_K, tm), lambda i: (0, i)), pl.BlockSpec((TOP_K, tm), lambda i: (0, i))],
        out_shape=[jax.ShapeDtypeStruct((n, D_MODEL), F32), jax.ShapeDtypeStruct((n, D_PACK), I32),
                   jax.ShapeDtypeStruct((TOP_K, n), I32), jax.ShapeDtypeStruct((TOP_K, n), F32)],
        compiler_params=_cparams(("arbitrary",)), name="merge_lat" if lat else "merge_ctx",
    )(x, mod, out_a, hf, hb, ob, out_c, g, hn, wpa, wpb, wpc, wo, bo, ln_g, ln_b, wr_t, br)


def _route_kernel(n_first, ti_ref, pos_o, te_o, tv_o, ts_o, nt_o, rank_s):
    ch = ROUTE_CHUNK
    n_tok = ti_ref.shape[1]
    eio = lax.broadcasted_iota(I32, (N_EXPERTS, 1), 0)
    tri = (lax.broadcasted_iota(I32, (ch, ch), 0) <= lax.broadcasted_iota(I32, (ch, ch), 1)).astype(BF16)

    def onehots(off):
        ti = ti_ref[:, pl.ds(off, ch)]
        return [(eio == ti[k:k + 1, :]).astype(F32) for k in range(TOP_K)]

    def count_body(c, carry):
        off = pl.multiple_of(c * ch, ch)
        ohs = onehots(off)
        oh = ohs[0] + ohs[1] + ohs[2] + ohs[3]
        incl = _dot(oh.astype(BF16), tri)
        excl = incl - oh + carry
        for k in range(TOP_K):
            rank_s[k:k + 1, pl.ds(off, ch)] = jnp.sum(ohs[k] * excl, axis=0, keepdims=True)
        return carry + incl[:, ch - 1:ch]

    counts_first = lax.fori_loop(0, n_first // ch, count_body, jnp.zeros((N_EXPERTS, 1), F32))
    counts = lax.fori_loop(n_first // ch, n_tok // ch, count_body, counts_first)
    ntile_e = jnp.floor((counts + (TME - 1)) * (1.0 / TME))
    lower = (lax.broadcasted_iota(I32, (N_EXPERTS, N_EXPERTS), 1)
             < lax.broadcasted_iota(I32, (N_EXPERTS, N_EXPERTS), 0)).astype(BF16)
    off_t = _dot(lower, jnp.broadcast_to(ntile_e, (N_EXPERTS, LANES)).astype(BF16))[:, 0:1]
    end_t = off_t + ntile_e
    total = jnp.sum(ntile_e, axis=0, keepdims=True)
    jt = lax.broadcasted_iota(I32, (1, TILE_LANES), 1).astype(F32)
    te = jnp.minimum(jnp.sum((end_t <= jt).astype(F32), axis=0, keepdims=True), N_EXPERTS - 1.0)
    sel = (eio.astype(F32) == te).astype(F32)
    cnt_j = jnp.sum(sel * counts, axis=0, keepdims=True)
    off_j = jnp.sum(sel * off_t, axis=0, keepdims=True)
    valid = jnp.clip(cnt_j - (jt - off_j) * TME, 0.0, float(TME))
    first_j = jnp.sum(sel * counts_first, axis=0, keepdims=True)
    te_o[...] = te.astype(I32)
    tv_o[...] = jnp.where(jt < total, valid, 0.0).astype(I32)
    ts_o[...] = jnp.clip(first_j - (jt - off_j) * TME, 0.0, float(TME)).astype(I32)
    nt_o[...] = jnp.broadcast_to(total, (1, LANES)).astype(I32)
    off_rows = off_t * TME

    def pos_body(c, carry):
        off = pl.multiple_of(c * ch, ch)
        ohs = onehots(off)
        for k in range(TOP_K):
            base = jnp.sum(ohs[k] * off_rows, axis=0, keepdims=True)
            pos_o[k:k + 1, pl.ds(off, ch)] = (rank_s[k:k + 1, pl.ds(off, ch)] + base).astype(I32)
        return carry

    lax.fori_loop(0, n_tok // ch, pos_body, 0)


def _route_call(top_i, n_first):
    n = top_i.shape[1]
    assert _max_tiles(n) <= TILE_LANES and n_first % ROUTE_CHUNK == 0
    tiles = jax.ShapeDtypeStruct((1, TILE_LANES), I32)
    return pl.pallas_call(
        functools.partial(_route_kernel, n_first),
        out_shape=[jax.ShapeDtypeStruct((TOP_K, n), I32), tiles, tiles, tiles,
                   jax.ShapeDtypeStruct((1, LANES), I32)],
        scratch_shapes=[pltpu.VMEM((TOP_K, n), F32)],
        compiler_params=pltpu.CompilerParams(vmem_limit_bytes=VMEM_LIMIT), name="route",
    )(top_i)


def _sc_mesh():
    return plsc.VectorSubcoreMesh(core_axis_name="c", subcore_axis_name="s")


def _sc_worker(n_rows):
    info = plsc.get_sparse_core_info()
    n_workers = info.num_cores * info.num_subcores
    per_w = n_rows // n_workers
    assert per_w * n_workers == n_rows and per_w % SC_WINDOW == 0
    wid = lax.axis_index("s") * info.num_cores + lax.axis_index("c")
    return wid * per_w, per_w


def _dispatch_call(u2, pos_flat, p_rows):
    w = SC_WINDOW
    n = u2.shape[0]

    @functools.partial(pl.kernel, out_type=jax.ShapeDtypeStruct((p_rows, u2.shape[1]), u2.dtype), mesh=_sc_mesh(),
                       scratch_types=[pltpu.VMEM((w,), I32)] * (2 * TOP_K)
                       + [pltpu.VMEM((w, u2.shape[1]), u2.dtype)] * 2 + [pltpu.SemaphoreType.DMA] * 4)
    def k(x_hbm, p_hbm, o_hbm, *scratch):
        idx = (scratch[0:TOP_K], scratch[TOP_K:2 * TOP_K])
        rows = scratch[2 * TOP_K:2 * TOP_K + 2]
        sem_load = scratch[2 * TOP_K + 2:2 * TOP_K + 4]
        sem_scat = scratch[2 * TOP_K + 4:2 * TOP_K + 6]
        base, per_w = _sc_worker(n)
        n_chunks = per_w // w

        def start_loads(c, s):
            off = pl.multiple_of(base + c * w, 8)
            cps = [pltpu.async_copy(x_hbm.at[pl.ds(off, w)], rows[s], sem_load[s])]
            cps += [pltpu.async_copy(p_hbm.at[pl.ds(pl.multiple_of(kk * n + off, 8), w)], idx[s][kk], sem_load[s])
                    for kk in range(TOP_K)]
            return cps

        loads = start_loads(0, 0)
        scatters = [None, None]
        for c in range(n_chunks):
            s = c % 2
            for cp in loads:
                cp.wait()
            if c + 1 < n_chunks:
                if scatters[1 - s] is not None:
                    for cp in scatters[1 - s]:
                        cp.wait()
                    scatters[1 - s] = None
                loads = start_loads(c + 1, 1 - s)
            scatters[s] = [pltpu.async_copy(rows[s], o_hbm.at[idx[s][kk]], sem_scat[s]) for kk in range(TOP_K)]
        for group in scatters:
            if group is not None:
                for cp in group:
                    cp.wait()

    return k(u2, pos_flat)


def _combine_call(ys, pos_flat):
    w = SC_WINDOW
    n = pos_flat.shape[0]

    @functools.partial(pl.kernel, out_type=jax.ShapeDtypeStruct((n, ys.shape[1]), ys.dtype), mesh=_sc_mesh(),
                       scratch_types=[pltpu.VMEM((w,), I32)] * 2 + [pltpu.VMEM((w, ys.shape[1]), ys.dtype)] * 2
                       + [pltpu.SemaphoreType.DMA] * 4)
    def k(y_hbm, p_hbm, o_hbm, idx0, idx1, rows0, rows1, sg0, sg1, ss0, ss1):
        idx, rows, sem_gather, sem_store = (idx0, idx1), (rows0, rows1), (sg0, sg1), (ss0, ss1)
        base, per_w = _sc_worker(n)
        n_chunks = per_w // w

        def offset(c):
            return pl.multiple_of(base + c * w, 8)

        pltpu.sync_copy(p_hbm.at[pl.ds(offset(0), w)], idx[0])
        gather = pltpu.async_copy(y_hbm.at[idx[0]], rows[0], sem_gather[0])
        stores = [None, None]
        for c in range(n_chunks):
            s = c % 2
            if c + 1 < n_chunks:
                pltpu.sync_copy(p_hbm.at[pl.ds(offset(c + 1), w)], idx[1 - s])
            gather.wait()
            if c + 1 < n_chunks:
                if stores[1 - s] is not None:
                    stores[1 - s].wait()
                    stores[1 - s] = None
                gather = pltpu.async_copy(y_hbm.at[idx[1 - s]], rows[1 - s], sem_gather[1 - s])
            stores[s] = pltpu.async_copy(rows[s], o_hbm.at[pl.ds(offset(c), w)], sem_store[s])
        for st in stores:
            if st is not None:
                st.wait()

    return k(ys, pos_flat)


def _moe_kernel(l, n_tiles_max, te_ref, tv_ref, ts_ref, nt_ref, xc_ref, xl_ref, wu_hbm, bu_ref, wd_hbm, bd_ref,
                y_ref, wu_f, wd_f, wu_s, wd_s, sem, group_s):
    i = pl.program_id(0)
    nt = nt_ref[0]

    def fetch(e, slot):
        return (pltpu.make_async_copy(wu_hbm.at[l, e], wu_f.at[slot], sem.at[0, slot]),
                pltpu.make_async_copy(wd_hbm.at[l, e], wd_f.at[slot], sem.at[1, slot]))

    @pl.when(i == 0)
    def _():
        group_s[0] = 0
        for cp in fetch(te_ref[0], 0):
            cp.start()

    @pl.when(i < nt)
    def _():
        e = te_ref[i]
        prev = te_ref[jnp.maximum(i - 1, 0)]

        @pl.when((i == 0) | (e != prev))
        def _():
            group = group_s[0]
            slot = group % 2
            j = lax.while_loop(lambda j: (j < nt) & (te_ref[jnp.minimum(j, n_tiles_max - 1)] == e),
                               lambda j: j + 1, i + 1)

            @pl.when(j < nt)
            def _():
                for cp in fetch(te_ref[jnp.minimum(j, n_tiles_max - 1)], 1 - slot):
                    cp.start()

            for cp in fetch(e, slot):
                cp.wait()
            wu_s[...] = wu_f[slot].astype(BF16)
            wd_s[...] = wd_f[slot].astype(BF16)
            group_s[0] = group + 1

        valid = tv_ref[i]
        split = ts_ref[i]
        half = MOE_ROWS
        for j in range(TME // half):
            @pl.when(valid > j * half)
            def _():
                sl = slice(j * half, (j + 1) * half)
                rows = lax.broadcasted_iota(I32, (half, 1), 0) + j * half
                words = jnp.where(rows < split, xc_ref[sl, :], xl_ref[sl, :])
                lo, hi = _unpack_rows(jnp.where(rows < valid, words, 0))
                x = jnp.concatenate([lo.astype(BF16), hi.astype(BF16)], axis=1)
                h = _dot(x, wu_s[...]) + bu_ref[...]
                h_glu = jnp.minimum(h[:, :D_EXPERT], SWIGLU_LIMIT)
                h_lin = jnp.clip(h[:, D_EXPERT:], -SWIGLU_LIMIT, SWIGLU_LIMIT)
                act = h_glu * jax.nn.sigmoid(SWIGLU_ALPHA * h_glu) * (h_lin + 1.0)
                y_ref[sl, :] = _pack_rows(_dot(act.astype(BF16), wd_s[...]) + bd_ref[...])


def _moe_call(l, tile_e, tile_v, tile_s, n_tiles, xs_ctx, xs_lat, w_up, b_up, w_down, b_down):
    n_tiles_max = xs_ctx.shape[0] // TME

    def tile(i, te, tv, ts, nt):
        return jnp.minimum(i, nt[0] - 1)

    def wmap(i, te, tv, ts, nt):
        return (l, te[tile(i, te, tv, ts, nt)], 0, 0)

    rows = pl.BlockSpec((TME, D_PACK), lambda i, te, tv, ts, nt: (tile(i, te, tv, ts, nt), 0))
    grid_spec = pltpu.PrefetchScalarGridSpec(
        num_scalar_prefetch=4,
        grid=(n_tiles_max,),
        in_specs=[rows, rows,
                  pl.BlockSpec(memory_space=pl.ANY),
                  pl.BlockSpec((None, None, 1, 2 * D_EXPERT), wmap),
                  pl.BlockSpec(memory_space=pl.ANY),
                  pl.BlockSpec((None, None, 1, D_MODEL), wmap)],
        out_specs=rows,
        scratch_shapes=[pltpu.VMEM((2, D_MODEL, 2 * D_EXPERT), F32), pltpu.VMEM((2, D_EXPERT, D_MODEL), F32),
                        pltpu.VMEM((D_MODEL, 2 * D_EXPERT), BF16), pltpu.VMEM((D_EXPERT, D_MODEL), BF16),
                        pltpu.SemaphoreType.DMA((2, 2)), pltpu.SMEM((1,), I32)],
    )
    return pl.pallas_call(
        functools.partial(_moe_kernel, l, n_tiles_max), grid_spec=grid_spec,
        out_shape=jax.ShapeDtypeStruct((xs_ctx.shape[0], D_PACK), I32),
        compiler_params=_cparams(("arbitrary",)), name="moe",
    )(tile_e, tile_v, tile_s, n_tiles, xs_ctx, xs_lat, w_up, b_up.reshape(DEPTH, N_EXPERTS, 1, -1),
      w_down, b_down.reshape(DEPTH, N_EXPERTS, 1, -1))


def _final_kernel(yg_ref, tw_ref, x1_ref, mod_ref, ln_g_ref, ln_b_ref, o_ref):
    o_ref[...] = _moe_residual_norm(yg_ref, tw_ref, x1_ref, mod_ref, ln_g_ref, ln_b_ref)


def _final_call(lat, yg, tw, x1, mod, ln_g, ln_b):
    n = x1.shape[0]
    tm = TM_MERGE
    row = lambda w: pl.BlockSpec((tm, w), lambda i: (i, 0))
    const = lambda shape: pl.BlockSpec(shape, lambda i: (0,) * len(shape))
    return pl.pallas_call(
        _final_kernel,
        grid=(n // tm,),
        in_specs=[pl.BlockSpec((TOP_K, tm, D_PACK), lambda i: (0, i, 0)), row(TOP_K), row(D_MODEL),
                  _mod_spec(lat, tm), const((1, D_MODEL)), const((1, D_MODEL))],
        out_specs=row(D_MODEL), out_shape=jax.ShapeDtypeStruct((n, D_MODEL), F32),
        compiler_params=_cparams(("arbitrary",)), name="final_lat" if lat else "final_ctx",
    )(yg, tw, x1, mod, ln_g, ln_b)


def _rope_tables():
    half = A_HEAD_DIM // 2
    freqs = ROPE_BASE ** (-jnp.arange(0, half, 2, dtype=F32) / half)
    t = jnp.arange(DEC_SEQ, dtype=jnp.int32)
    ang_r = (t // GRID_W).astype(F32)[:, None] * freqs[None, :]
    ang_c = (t % GRID_W).astype(F32)[:, None] * freqs[None, :]
    cos64 = jnp.concatenate([jnp.cos(ang_r)] * 2 + [jnp.cos(ang_c)] * 2, axis=1)
    sin64 = jnp.concatenate([-jnp.sin(ang_r), jnp.sin(ang_r), -jnp.sin(ang_c), jnp.sin(ang_c)], axis=1)
    lat = (jnp.tile(cos64, (1, 2)), jnp.tile(sin64, (1, 2)))
    ctx = (jnp.ones((TM, LANES), F32), jnp.zeros((TM, LANES), F32))
    return ctx, lat


def _permute_in_cols(t, axis=-1):
    take = lambda lo, hi: lax.slice_in_dim(t, lo, hi, axis=axis)
    pad_shape = list(t.shape)
    pad_shape[axis] = D_IN_PAD - D_IN
    parts = [take(h * 64, (h + 1) * 64) for h in QA_HEAD_ORDER]
    parts += [take(512, 2816), take(2832, 7440), take(2816, 2832), jnp.zeros(pad_shape, t.dtype)]
    return jnp.concatenate(parts, axis=axis)


def _layer_weights(p):
    w_in, b_in = p["w_in"], p["b_in"]
    wpa = p["w_pa"].reshape(A_HEADS, A_HEAD_DIM, D_MODEL)[np.asarray(QA_HEAD_ORDER)].reshape(512, D_MODEL)
    row = lambda t: t.reshape(1, -1)
    return dict(
        inproj=(_permute_in_cols(w_in.T, axis=0).astype(BF16), _permute_in_cols(b_in).reshape(1, D_IN_PAD),
                w_in[:, 2816:2832].T.astype(BF16), b_in[2816:2832].reshape(16, 1),
                row(jnp.tile(p["qn_a"], A_HEADS)), row(jnp.tile(p["kn_a"], A_KV_HEADS)),
                jnp.asarray(np.kron(np.eye(8), np.ones((64, 64))), BF16)),
        merge=(row(p["hn_b"]), wpa.astype(BF16), p["w_pb"].astype(BF16), p["w_pc"].astype(BF16),
               p["w_o"].astype(BF16), row(p["b_o"]), row(p["ln1_g"]), row(p["ln1_b"]),
               p["w_router"].T, p["b_router"].reshape(N_EXPERTS, 1)),
        lam_c=p["lam_c"], sn_c=row(p["sn_c"]), ln2=(row(p["ln2_g"]), row(p["ln2_b"])))


def _chain_mixer(l, lat, x, mod, tables, caches, w, outs, after=None):
    cos_t, sin_t = tables
    lam_init = 0.8 - 0.6 * math.exp(-0.3 * l)
    res = _inproj_call(l, lat, x, mod, cos_t, sin_t, *w["inproj"], None if outs is None else outs[:4], after)
    qa, kab, vab, qb, kb, vb, ob, qc, kcb, vcb, g, ifg, ifgt = res[:13]
    if isinstance(x, tuple):
        x = res[-1]
    if lat:
        ck_a, cv_a, ck_c, cv_c, cn0, m0 = caches
    else:
        ck_a = cv_a = ck_c = cv_c = cn0 = m0 = None
    oa = _attn_call("a", lat, lam_init, qa, kab, vab, ck_a, cv_a, None, None)
    oc = _attn_call("c", lat, lam_init, qc, kcb, vcb, ck_c, cv_c, w["lam_c"], w["sn_c"])
    ml = _mlstm_call(l, lat, qb, kb, vb, ifg, ifgt, cn0, m0, None if outs is None else outs[4:6])
    hf, hb = ml[:2]
    merged = _merge_call(lat, x, mod, oa, hf, hb, ob, oc, g, *w["merge"])
    if lat:
        return merged, None
    m_l = ml[4][:, :, :, 0, 0]
    m_all = m_l[:, None] if outs is None else jnp.concatenate([outs[6], m_l[:, None]], axis=1)
    return merged, tuple(res[13:17]) + (ml[2], ml[3], m_all)


def _moe_layer(l, mix_ctx, mix_lat, mod, w, experts):
    top_i = jnp.concatenate([mix_ctx[2], mix_lat[2]], axis=1)
    pos, tile_e, tile_v, tile_s, n_tiles = _route_call(top_i, N_CTX)
    p_rows = _max_tiles(N_CTX + N_LAT) * TME
    pos_flat = (pos[:, :N_CTX].reshape(TOP_K * N_CTX), pos[:, N_CTX:].reshape(TOP_K * N_LAT))
    xs = [_dispatch_call(mix[1], pf, p_rows) for mix, pf in zip((mix_ctx, mix_lat), pos_flat)]
    ys = _moe_call(l, tile_e[0], tile_v[0], tile_s[0], n_tiles[0, :1], xs[0], xs[1],
                   experts["w_up"], experts["b_up"], experts["w_down"], experts["b_down"])
    yg_ctx = _combine_call(ys, pos_flat[0]).reshape(TOP_K, N_CTX, D_PACK)
    yg_lat = _combine_call(ys, pos_flat[1]).reshape(TOP_K, N_LAT, D_PACK)
    return ((yg_ctx, mix_ctx[3].T, mix_ctx[0], mod) + w["ln2"], (yg_lat, mix_lat[3].T, mix_lat[0], mod) + w["ln2"])


def kernel(x_prompt, x_sample, cache_k_gqa, cache_v_gqa, cache_k_diff, cache_v_diff, state_mlstm_c, state_mlstm_n, state_mlstm_m, c, c_ctx, w_in, b_in, qn_a, kn_a, hn_b, lam_c, sn_c, w_pa, w_pb, w_pc, w_o, b_o, w_mod, b_mod, ln1_g, ln1_b, ln2_g, ln2_b, w_router, b_router, w_up, b_up, w_down, b_down):
    x_ctx = x_prompt.reshape(N_CTX, D_MODEL)
    x_lat = x_sample.reshape(N_LAT, D_MODEL)
    cvecs = jnp.zeros((8, D_MODEL), F32).at[0].set(c_ctx).at[1:1 + DEC_BATCH].set(c)
    mod_all = _mod_call(cvecs, w_mod, b_mod).reshape(DEPTH, 8, N_MOD, D_MODEL)[:, :N_GROUPS]
    tables_ctx, tables_lat = _rope_tables()
    params = dict(w_in=w_in, b_in=b_in, qn_a=qn_a, kn_a=kn_a, hn_b=hn_b, lam_c=lam_c, sn_c=sn_c, w_pa=w_pa,
                  w_pb=w_pb, w_pc=w_pc, w_o=w_o, b_o=b_o, ln1_g=ln1_g, ln1_b=ln1_b, ln2_g=ln2_g, ln2_b=ln2_b,
                  w_router=w_router, b_router=b_router)
    experts = dict(w_up=w_up, b_up=b_up, w_down=w_down, b_down=b_down)
    outs = None
    for l in range(DEPTH):
        w = _layer_weights({k: v[l] for k, v in params.items()})
        ck_a = cache_k_gqa[:, l].transpose(0, 2, 1, 3).reshape(DEC_BATCH, PAST_LEN, 128).astype(BF16)
        cv_a = cache_v_gqa[:, l].transpose(0, 2, 1, 3).reshape(DEC_BATCH, PAST_LEN, 128).astype(BF16)
        ck_c = cache_k_diff[:, l].astype(BF16)
        cv_c = cache_v_diff[:, l].astype(BF16)
        cn0 = jnp.concatenate([state_mlstm_c[:, l], state_mlstm_n[:, l][..., None],
                               jnp.zeros((DEC_BATCH, 2, B_HEADS, B_HEAD_DIM, B_HEAD_DIM - 1), F32)], axis=-1)
        m0 = jnp.broadcast_to(state_mlstm_m[:, l][..., None, None], (DEC_BATCH, 2, B_HEADS, 1, LANES))
        mix_ctx, outs = _chain_mixer(l, False, x_ctx, mod_all[l], tables_ctx, None, w, outs)
        mix_lat, _ = _chain_mixer(l, True, x_lat, mod_all[l], tables_lat, (ck_a, cv_a, ck_c, cv_c, cn0, m0), w, None,
                                  after=mix_ctx[3] if l > 0 else None)
        x_ctx, x_lat = _moe_layer(l, mix_ctx, mix_lat, mod_all[l], w, experts)
    x_ctx = _final_call(False, *x_ctx)
    x_lat = _final_call(True, *x_lat)
    return (x_ctx.reshape(BATCH, SEQ, D_MODEL), x_lat.reshape(DEC_BATCH, DEC_SEQ, D_MODEL)) + tuple(outs)
```

```python
import functools
import math

import numpy as np
import jax
import jax.numpy as jnp
from jax import lax
from jax.experimental import pallas as pl
from jax.experimental.pallas import tpu as pltpu
from jax.experimental.pallas import tpu_sc as plsc

F32 = jnp.float32
BF16 = jnp.bfloat16
I32 = jnp.int32

D_MODEL = 1024
BATCH = 16
SEQ = 256
DEPTH = 2
DEC_BATCH = 2
DEC_SEQ = 4096
PAST_LEN = 512
GRID_W = 64
ROPE_BASE = 10000.0
EPS = 1e-6
A_HEADS = 8
A_KV_HEADS = 2
A_HEAD_DIM = 64
B_HEADS = 4
B_HEAD_DIM = 128
MLSTM_CHUNK = 128
C_HEADS = 4
C_HEAD_DIM = 64
C_V_DIM = 128
N_EXPERTS = 32
TOP_K = 4
D_EXPERT = 1024
SWIGLU_LIMIT = 7.0
SWIGLU_ALPHA = 1.702
N_MOD = 6
DEEPNORM_ALPHA = (2 * DEPTH) ** 0.25

N_CTX = BATCH * SEQ
N_LAT = DEC_BATCH * DEC_SEQ
N_GROUPS = 1 + DEC_BATCH

LANES = 128
TM = 256
TQ = 512
KEY_BLOCK = 2048
MLSTM_SEQS_LAT = 2
MLSTM_SEQS_CTX = 4
TM_MERGE = 512
MERGE_ROWS = 128
TME = 1024
MOE_ROWS = 256
TILE_LANES = 128
ROUTE_CHUNK = 512
D_PACK = D_MODEL // 2
SC_WINDOW = 64
VMEM_LIMIT = 56 * 1024 * 1024

QA_HEAD_ORDER = (0, 4, 1, 5, 2, 6, 3, 7)
D_IN = 7440
D_IN_PAD = 7552
P_QA, P_KA, P_VA, P_QB, P_KB, P_VB, P_OB, P_QC, P_KC, P_VC, P_G, P_IF = (
    0, 512, 640, 768, 1280, 1792, 2304, 2816, 3328, 3840, 4352, 7424)


def _max_tiles(n_tok):
    return (n_tok * TOP_K) // TME + N_EXPERTS


def _cparams(sem):
    return pltpu.CompilerParams(dimension_semantics=sem, vmem_limit_bytes=VMEM_LIMIT)


def _split3(a):
    a1 = a.astype(BF16)
    r = a - a1.astype(F32)
    a2 = r.astype(BF16)
    a3 = (r - a2.astype(F32)).astype(BF16)
    return a1, a2, a3


def _dot(a, b, dims=None):
    if dims is None:
        return jnp.dot(a, b, preferred_element_type=F32)
    return lax.dot_general(a, b, (dims, ((), ())), preferred_element_type=F32)


def _dot_exact_rhs01(a, m, dims=None):
    return sum(_dot(p, m, dims) for p in _split3(a))


def _dot_exact_lhs01(m, a, dims=None):
    return sum(_dot(m, p, dims) for p in _split3(a))


def _pack_rows(x):
    xb = x.astype(BF16).astype(F32)
    lo = pltpu.bitcast(xb[:, :D_PACK], I32)
    lo = lax.shift_right_logical(lo, jnp.full_like(lo, 16))
    hi = pltpu.bitcast(xb[:, D_PACK:], I32) & jnp.int32(-65536)
    return lo | hi


def _unpack_rows(w):
    return (pltpu.bitcast(lax.shift_left(w, jnp.full_like(w, 16)), F32),
            pltpu.bitcast(w & jnp.int32(-65536), F32))


def _ln(x):
    mu = jnp.mean(x, axis=-1, keepdims=True)
    xc = x - mu
    var = jnp.mean(xc * xc, axis=-1, keepdims=True)
    return xc * lax.rsqrt(var + EPS)


def _log_sigmoid(x):
    return jnp.minimum(x, 0.0) - jnp.log(1.0 + jnp.exp(-jnp.abs(x)))


def _rope(x, cos, sin):
    lane = lax.broadcasted_iota(I32, (1, LANES), 1)
    first = (lane % 32) < 16
    outs = []
    for j in range(x.shape[1] // LANES):
        xj = x[:, j * LANES:(j + 1) * LANES]
        partner = jnp.where(first, pltpu.roll(xj, LANES - 16, 1), pltpu.roll(xj, 16, 1))
        outs.append(xj * cos + partner * sin)
    return outs[0] if len(outs) == 1 else jnp.concatenate(outs, axis=1)


def _mod_spec(lat, tile=TM):
    if lat:
        return pl.BlockSpec((1, N_MOD, D_MODEL), lambda i: (1 + i * tile // DEC_SEQ, 0, 0))
    return pl.BlockSpec((1, N_MOD, D_MODEL), lambda i: (0, 0, 0))


def _mod_kernel(c_ref, w_ref, b_ref, o_ref):
    c = c_ref[...]
    s = c * jax.nn.sigmoid(c)
    o_ref[...] = _dot(s.astype(BF16), w_ref[...].astype(BF16)) + b_ref[...]


def _mod_call(cvecs, w_mod, b_mod):
    tn = 1536
    n = N_MOD * D_MODEL
    return pl.pallas_call(
        _mod_kernel,
        grid=(DEPTH, n // tn),
        in_specs=[pl.BlockSpec((8, D_MODEL), lambda l, j: (0, 0)),
                  pl.BlockSpec((None, D_MODEL, tn), lambda l, j: (l, 0, j)),
                  pl.BlockSpec((None, 1, tn), lambda l, j: (l, 0, j))],
        out_specs=pl.BlockSpec((None, 8, tn), lambda l, j: (l, 0, j)),
        out_shape=jax.ShapeDtypeStruct((DEPTH, 8, n), F32),
        compiler_params=_cparams(("arbitrary", "arbitrary")),
        name="mod",
    )(cvecs, w_mod, b_mod.reshape(DEPTH, 1, n))


def _moe_residual_norm(yg_ref, tw_ref, x1_ref, mod_ref, ln_g_ref, ln_b_ref):
    tw = tw_ref[...]
    halves = [_unpack_rows(yg_ref[k]) for k in range(TOP_K)]
    moe = jnp.concatenate([sum(tw[:, k:k + 1] * halves[k][j] for k in range(TOP_K)) for j in range(2)], axis=1)
    gate2 = mod_ref[0, 5:6, :]
    return _ln(DEEPNORM_ALPHA * x1_ref[...] + gate2 * moe) * ln_g_ref[...] + ln_b_ref[...]


def _inproj_kernel(with_cache, fused, n_unread, *refs):
    if fused:
        x = _moe_residual_norm(*refs[:6])
        refs = refs[6:]
    else:
        x = refs[0][...]
        refs = refs[1:]
    mod_ref, cos_ref, sin_ref, w_ref, b_ref, wgt_ref, bgt_ref, gq_ref, gk_ref, gmat_ref = refs[:10]
    outs = refs[10 + n_unread:]
    qa_o, kab_o, vab_o, qb_o, kb_o, vb_o, ob_o, qc_o, kcb_o, vcb_o, g_o, ifg_o, ifgt_o = outs[:13]
    if fused:
        outs[-1][...] = x
    shift = mod_ref[0, 0:1, :]
    scale = mod_ref[0, 1:2, :]
    ub = (_ln(x) * (1.0 + scale) + shift).astype(BF16)
    cos = cos_ref[...]
    sin = sin_ref[...]

    def proj(lo, n):
        return _dot(ub, w_ref[lo:lo + n, :], ((1,), (1,))) + b_ref[:, lo:lo + n]

    def headnorm(z, gmat, gain):
        ss = _dot((z * z).astype(BF16), gmat)
        return z * lax.rsqrt(ss * (1.0 / A_HEAD_DIM) + EPS) * gain

    segs = [(P_QA, 512), (P_KA, 128), (P_VA, 128), (P_QB, 512), (P_KB, 512), (P_VB, 512), (P_OB, 512),
            (P_QC, 512), (P_KC, 512), (P_VC, 512), (P_G, 1024), (P_G + 1024, 1024), (P_G + 2048, 1024),
            (P_IF, LANES)]
    keep = {}

    def epilogue(k, z):
        if k == 0:
            z = headnorm(z, gmat_ref[...], gq_ref[...])
            qa_o[...] = (_rope(z, cos, sin) * 0.125).astype(BF16)
        elif k == 1:
            keep["k"] = _rope(headnorm(z, gmat_ref[0:128, 0:128], gk_ref[...]), cos, sin)
            kab_o[...] = keep["k"].astype(BF16)
        elif k == 2:
            keep["v"] = z
            vab_o[...] = z.astype(BF16)
        elif k == 3:
            qb_o[...] = z.astype(BF16)
        elif k == 4:
            kb_o[...] = (z * (B_HEAD_DIM ** -0.5)).astype(BF16)
        elif k == 5:
            vb_o[...] = z.astype(BF16)
        elif k == 6:
            ob_o[...] = z.astype(BF16)
        elif k == 7:
            qc_o[...] = (_rope(z, cos, sin) * 0.125).astype(BF16)
        elif k == 8:
            keep["kc"] = _rope(z, cos, sin)
            kcb_o[...] = keep["kc"].astype(BF16)
        elif k == 9:
            keep["vc"] = z
            vcb_o[...] = z.astype(BF16)
        elif k in (10, 11, 12):
            g_o[:, (k - 10) * 1024:(k - 9) * 1024] = jax.nn.sigmoid(z).astype(BF16)
        else:
            lane = lax.broadcasted_iota(I32, (1, LANES), 1)
            ifg_o[...] = jnp.where(lane >= 8, _log_sigmoid(z), z)

    z_prev = proj(*segs[0])
    for k in range(1, len(segs)):
        z_next = proj(*segs[k])
        epilogue(k - 1, z_prev)
        z_prev = z_next
    zt = _dot(wgt_ref[...], ub, ((1,), (1,))) + bgt_ref[...]
    epilogue(len(segs) - 1, z_prev)
    row = lax.broadcasted_iota(I32, (16, 1), 0)
    ifgt_o[...] = jnp.where(row >= 8, _log_sigmoid(zt), zt)

    if with_cache:
        nka_o, nva_o, nkc_o, nvc_o = outs[13:17]
        for h in range(A_KV_HEADS):
            nka_o[h] = keep["k"][:, h * A_HEAD_DIM:(h + 1) * A_HEAD_DIM]
            nva_o[h] = keep["v"][:, h * A_HEAD_DIM:(h + 1) * A_HEAD_DIM]
        for h in range(C_HEADS):
            nkc_o[h] = keep["kc"][:, h * LANES:(h + 1) * LANES]
            nvc_o[h] = keep["vc"][:, h * LANES:(h + 1) * LANES]


def _inproj_call(l, lat, x, mod, cos_t, sin_t, w_p, b_p, w_gt, b_gt, gq, gk, gmat, cache_outs, after=None):
    fused = isinstance(x, tuple)
    n = x[2].shape[0] if fused else x.shape[0]
    row = lambda w: pl.BlockSpec((TM, w), lambda i: (i, 0))
    const = lambda shape: pl.BlockSpec(shape, lambda i: (0,) * len(shape), pipeline_mode=pl.Buffered(1))
    sds = lambda w, dt: jax.ShapeDtypeStruct((n, w), dt)
    out_defs = [(512, BF16), (128, BF16), (128, BF16), (512, BF16), (512, BF16), (512, BF16), (512, BF16),
                (512, BF16), (512, BF16), (512, BF16), (3072, BF16), (LANES, F32)]
    seq = DEC_SEQ if lat else SEQ
    tiles_per_seq = seq // TM
    out_specs = [row(w) for w, _ in out_defs] + [
        pl.BlockSpec((None, 16, TM), lambda i: (i // tiles_per_seq, 0, i % tiles_per_seq))]
    out_shape = [sds(w, dt) for w, dt in out_defs] + [jax.ShapeDtypeStruct((n // seq, 16, seq), F32)]
    alias_in, aliases = [], {}
    if lat:
        tbl = pl.BlockSpec((TM, LANES), lambda i: (i % (DEC_SEQ // TM), 0))
    else:
        assert TM == SEQ
        tbl = pl.BlockSpec((TM, LANES), lambda i: (0, 0))
        cache_defs = [(A_KV_HEADS, A_HEAD_DIM), (A_KV_HEADS, A_HEAD_DIM), (C_HEADS, LANES), (C_HEADS, LANES)]
        out_specs += [pl.BlockSpec((None, None, nh, SEQ, d), lambda i: (i, l, 0, 0, 0)) for nh, d in cache_defs]
        out_shape += [jax.ShapeDtypeStruct((BATCH, DEPTH, nh, SEQ, d), F32) for nh, d in cache_defs]
        if cache_outs is not None:
            alias_in = list(cache_outs)
    if fused:
        plain = lambda shape: pl.BlockSpec(shape, lambda i: (0,) * len(shape))
        lead_specs = [pl.BlockSpec((TOP_K, TM, D_PACK), lambda i: (0, i, 0)), row(TOP_K), row(D_MODEL),
                      _mod_spec(lat), plain((1, D_MODEL)), plain((1, D_MODEL))]
        lead = list(x)
        out_specs.append(row(D_MODEL))
        out_shape.append(sds(D_MODEL, F32))
    else:
        lead_specs, lead = [row(D_MODEL)], [x]
    n_in = len(lead) + 10
    aliases = {n_in + j: len(out_defs) + 1 + j for j in range(len(alias_in))}
    unread = alias_in + ([] if after is None else [after])
    return pl.pallas_call(
        functools.partial(_inproj_kernel, not lat, fused, len(unread)),
        grid=(n // TM,),
        in_specs=lead_specs + [_mod_spec(lat), tbl, tbl,
                               const((D_IN_PAD, D_MODEL)), const((1, D_IN_PAD)),
                               const((16, D_MODEL)), const((16, 1)),
                               const((1, 512)), const((1, 128)), const((512, 512))]
        + [pl.BlockSpec(memory_space=pl.ANY)] * len(unread),
        out_specs=out_specs, out_shape=out_shape, input_output_aliases=aliases,
        compiler_params=_cparams(("arbitrary",)),
        name="inproj_lat" if lat else "inproj_ctx",
    )(*lead, mod, cos_t, sin_t, w_p, b_p, w_gt, b_gt, gq, gk, gmat, *unread)


def _attn_kernel(mode, nseg, lam_init, *refs):
    q_ref = refs[0]
    kv = refs[1:1 + 2 * nseg]
    rest = refs[1 + 2 * nseg:]
    o_ref = rest[-1]
    tq = q_ref.shape[0]
    lane = lax.broadcasted_iota(I32, (1, LANES), 1)
    if mode == "c":
        lam_ref, sn_ref = rest[0], rest[1]
        lp = lam_ref[...]
        lam = (jnp.exp(jnp.sum(lp[0:1] * lp[1:2], axis=-1, keepdims=True))
               - jnp.exp(jnp.sum(lp[2:3] * lp[3:4], axis=-1, keepdims=True)) + lam_init)
    for blk in range(q_ref.shape[1] // LANES):
        qs = slice(blk * LANES, (blk + 1) * LANES)
        ks = qs if kv[0].shape[1] > LANES else slice(0, LANES)
        q = q_ref[:, qs]
        zero = jnp.zeros_like(q)
        qq = jnp.concatenate([jnp.where(lane < 64, q, zero), jnp.where(lane >= 64, q, zero)], axis=0)
        m = l = acc = None
        for j in range(nseg):
            k_ref, v_ref = kv[2 * j], kv[2 * j + 1]
            for off in range(0, k_ref.shape[0], KEY_BLOCK):
                n = min(KEY_BLOCK, k_ref.shape[0] - off)
                s = _dot(qq, k_ref[off:off + n, ks], ((1,), (1,)))
                bm = jnp.max(s, axis=-1, keepdims=True)
                if m is None:
                    m = bm
                    e = jnp.exp(s - m)
                    l = jnp.sum(e, axis=-1, keepdims=True)
                    acc = _dot(e.astype(BF16), v_ref[off:off + n, ks])
                else:
                    m_new = jnp.maximum(m, bm)
                    alpha = jnp.exp(m - m_new)
                    e = jnp.exp(s - m_new)
                    l = alpha * l + jnp.sum(e, axis=-1, keepdims=True)
                    acc = alpha * acc + _dot(e.astype(BF16), v_ref[off:off + n, ks])
                    m = m_new
        o = acc * (1.0 / l)
        if mode == "a":
            out = jnp.where(lane < 64, o[:tq], o[tq:])
        else:
            o = o[:tq] - lam * o[tq:]
            ms = jnp.mean(o * o, axis=-1, keepdims=True)
            out = o * lax.rsqrt(ms + EPS) * sn_ref[...] * (1.0 - lam_init)
        o_ref[:, qs] = out.astype(o_ref.dtype)


def _attn_call(mode, lat, lam_init, q_all, k_all, v_all, k_cache, v_cache, lam_c, sn_c):
    wide = k_all.shape[1] == 512
    kcol = (lambda h: h) if wide else (lambda h: 0)
    if lat:
        nq = DEC_SEQ // TQ
        grid = (DEC_BATCH, nq, 4)
        q_spec = pl.BlockSpec((TQ, LANES), lambda b, i, h: (b * nq + i, h))
        kn_spec = pl.BlockSpec((DEC_SEQ, LANES), lambda b, i, h: (b, kcol(h)))
        if wide:
            kc_spec = pl.BlockSpec((None, None, PAST_LEN, LANES), lambda b, i, h: (b, h, 0, 0))
        else:
            kc_spec = pl.BlockSpec((None, PAST_LEN, LANES), lambda b, i, h: (b, 0, 0))
        in_specs = [q_spec, kc_spec, kc_spec, kn_spec, kn_spec]
        args = [q_all, k_cache, v_cache, k_all, v_all]
        nseg = 2
        o_spec = q_spec
        sem = ("arbitrary", "arbitrary", "arbitrary")
        zero_map = lambda b, i, h: (0, 0)
    else:
        grid = (BATCH,)
        q_spec = pl.BlockSpec((SEQ, 512), lambda b: (b, 0))
        k_spec = pl.BlockSpec((SEQ, k_all.shape[1]), lambda b: (b, 0))
        in_specs = [q_spec, k_spec, k_spec]
        args = [q_all, k_all, v_all]
        nseg = 1
        o_spec = q_spec
        sem = ("arbitrary",)
        zero_map = lambda b: (0, 0)
    if mode == "c":
        in_specs += [pl.BlockSpec((4, C_HEAD_DIM), zero_map), pl.BlockSpec((1, C_V_DIM), zero_map)]
        args += [lam_c, sn_c]
    return pl.pallas_call(
        functools.partial(_attn_kernel, mode, nseg, lam_init),
        grid=grid, in_specs=in_specs, out_specs=o_spec,
        out_shape=jax.ShapeDtypeStruct((q_all.shape[0], 512), BF16),
        compiler_params=_cparams(sem), name=f"attn_{mode}_{'lat' if lat else 'ctx'}",
    )(*args)


def _mlstm_kernel(has_init, want_state, n_alias, *refs):
    tok = (refs[0:5], refs[5:10])
    pos = 10
    if has_init:
        cn0_ref, m0_ref = refs[pos:pos + 2]
        pos += 2
    pos += n_alias
    h_refs = refs[pos:pos + 2]
    pos += 2
    if want_state:
        c_o, n_o, m_o = refs[pos:pos + 3]
        pos += 3
    cn_s, m_s = refs[pos:pos + 2]

    c = pl.program_id(1)
    last = pl.num_programs(1) - 1
    L = MLSTM_CHUNK

    @pl.when(c == 0)
    def _():
        if has_init:
            cn_s[...] = cn0_ref[...]
            m_s[...] = m0_ref[...]
        else:
            cn_s[...] = jnp.zeros_like(cn_s)
            m_s[...] = jnp.zeros_like(m_s)

    row = lax.broadcasted_iota(I32, (L, L), 0)
    col = lax.broadcasted_iota(I32, (L, L), 1)
    lane = lax.broadcasted_iota(I32, (1, LANES), 1)
    ones_col = jnp.where(lane == 0, 1.0, 0.0).astype(BF16)

    chains = []
    for d in range(2):
        q_ref, k_ref, v_ref, g_ref, gt_ref = tok[d]
        keep = (col <= row) if d == 0 else (col >= row)
        m_c = keep.astype(BF16)
        m_r = ((row <= col) if d == 0 else (row >= col)).astype(BF16)
        for s in range(q_ref.shape[0]):
            g = g_ref[s]
            gt = gt_ref[s]
            bc = _dot_exact_lhs01(m_c, g)
            br = _dot_exact_rhs01(gt, m_r)
            for h in range(B_HEADS):
                sl = slice(h * L, (h + 1) * L)
                gi, gf = 4 * d + h, 8 + 4 * d + h
                ch = dict(seq=s, d=d, h=h, sl=sl, keep=keep, q=q_ref[s, :, sl], k=k_ref[s, :, sl], v=v_ref[s, :, sl],
                          i_col=g[:, gi:gi + 1], f_col=g[:, gf:gf + 1], b_col=bc[:, gf:gf + 1],
                          r_row=gt[gi:gi + 1, :] - br[gf:gf + 1, :],
                          m_prev=m_s[s, d, h][:, 0:1], cn_prev=cn_s[s, d, h])
                chains.append(ch)
    for ch in chains:
        ch["s"] = _dot(ch["q"], ch["k"], ((1,), (1,)))
        ch["qc"] = _dot(ch["q"], ch["cn_prev"].astype(BF16))
    for ch in chains:
        ch["total"] = jnp.sum(ch["f_col"], axis=0, keepdims=True)
        ch["inter"] = ch["b_col"] + ch["m_prev"]
        ch["v_aug"] = jnp.concatenate([ch["v"], jnp.broadcast_to(ones_col, (L, LANES))], axis=1)
    for ch in chains:
        ch["rmat"] = jnp.where(ch["keep"], ch["r_row"], -jnp.inf)
    for ch in chains:
        ch["dmax"] = ch["b_col"] + jnp.max(ch["rmat"], axis=-1, keepdims=True)
    for ch in chains:
        ch["m_row"] = jnp.maximum(ch["inter"], ch["dmax"])
    for ch in chains:
        ch["sc"] = (ch["s"] * jnp.exp(ch["rmat"] + (ch["b_col"] - ch["m_row"]))).astype(BF16)
    for ch in chains:
        ch["a"] = _dot(ch["sc"], ch["v_aug"])
    for ch in chains:
        ch["nd"] = jnp.exp(ch["inter"] - ch["m_row"]) * ch["qc"] + ch["a"]
    for ch in chains:
        ch["scale"] = 1.0 / jnp.maximum(jnp.abs(ch["nd"][:, L:L + 1]), jnp.exp(-ch["m_row"]))
    for ch in chains:
        h_refs[ch["d"]][ch["seq"], :, ch["sl"]] = ch["nd"][:, :L] * ch["scale"]
    for ch in chains:
        ch["wel"] = ch["total"] - ch["b_col"] + ch["i_col"]
        ch["m_new"] = jnp.maximum(ch["total"] + ch["m_prev"], jnp.max(ch["wel"], axis=0, keepdims=True))
    for ch in chains:
        ch["kw"] = (jnp.exp(ch["wel"] - ch["m_new"]) * ch["k"].astype(F32)).astype(BF16)
    for ch in chains:
        ch["u"] = _dot(ch["kw"], ch["v_aug"], ((0,), (0,)))
    for ch in chains:
        key = (ch["seq"], ch["d"], ch["h"])
        cn_s[key] = jnp.exp(ch["total"] + ch["m_prev"] - ch["m_new"]) * ch["cn_prev"] + ch["u"]
        m_s[key] = jnp.broadcast_to(ch["m_new"], (1, LANES))

    if want_state:
        @pl.when(c == last)
        def _():
            pick = ((lax.broadcasted_iota(I32, (8, 2 * L), 1) == L)
                    & (lax.broadcasted_iota(I32, (8, 2 * L), 0) == 0)).astype(BF16)
            for s in range(cn_s.shape[0]):
                for d in range(2):
                    for h in range(B_HEADS):
                        cn = cn_s[s, d, h]
                        c_o[s, d, h] = cn[:, :L]
                        n_o[s, d, h:h + 1, :] = _dot_exact_lhs01(pick, cn, ((1,), (1,)))[0:1]
            m_o[...] = m_s[...]


def _mlstm_call(l, lat, q_all, k_all, v_all, g_all, gt_all, cn0, m0, state_bufs):
    L = MLSTM_CHUNK
    ns = MLSTM_SEQS_LAT if lat else MLSTM_SEQS_CTX
    n = q_all.shape[0]
    nb, seq = (DEC_BATCH, DEC_SEQ) if lat else (BATCH, SEQ)
    nc = seq // L
    chunk = (lambda c: c, lambda c: nc - 1 - c)
    by_seq = lambda t: t.reshape(nb, seq, t.shape[-1])

    in_specs, args = [], []
    for d in range(2):
        tok = lambda w, d=d: pl.BlockSpec((ns, L, w), lambda b, c: (b, chunk[d](c), 0))
        in_specs += [tok(512), tok(512), tok(512), tok(LANES),
                     pl.BlockSpec((ns, 16, L), lambda b, c, d=d: (b, 0, chunk[d](c)))]
        args += [by_seq(q_all), by_seq(k_all), by_seq(v_all), by_seq(g_all), gt_all]
    if lat:
        in_specs += [pl.BlockSpec((ns, 2, B_HEADS, L, 2 * L), lambda b, c: (b, 0, 0, 0, 0)),
                     pl.BlockSpec((ns, 2, B_HEADS, 1, LANES), lambda b, c: (b, 0, 0, 0, 0))]
        args += [cn0, m0]
    aliases = {}
    alias_in = list(state_bufs) if state_bufs is not None else []
    if alias_in:
        aliases = {len(args): 2, len(args) + 1: 3}
    in_specs += [pl.BlockSpec(memory_space=pl.ANY)] * len(alias_in)
    args += alias_in
    out_specs = [pl.BlockSpec((ns, L, 512), lambda b, c, d=d: (b, chunk[d](c), 0)) for d in range(2)]
    out_shape = [jax.ShapeDtypeStruct((nb, seq, 512), F32)] * 2
    if not lat:
        out_specs += [pl.BlockSpec((ns, None, 2, B_HEADS, L, L), lambda b, c: (b, l, 0, 0, 0, 0)),
                      pl.BlockSpec((ns, None, 2, B_HEADS, L), lambda b, c: (b, l, 0, 0, 0)),
                      pl.BlockSpec((ns, 2, B_HEADS, 1, LANES), lambda b, c: (b, 0, 0, 0, 0))]
        out_shape += [jax.ShapeDtypeStruct((BATCH, DEPTH, 2, B_HEADS, L, L), F32),
                      jax.ShapeDtypeStruct((BATCH, DEPTH, 2, B_HEADS, L), F32),
                      jax.ShapeDtypeStruct((nb, 2, B_HEADS, 1, LANES), F32)]
    res = pl.pallas_call(
        functools.partial(_mlstm_kernel, lat, not lat, len(alias_in)),
        grid=(nb // ns, nc), in_specs=in_specs, out_specs=out_specs, out_shape=out_shape,
        input_output_aliases=aliases,
        scratch_shapes=[pltpu.VMEM((ns, 2, B_HEADS, L, 2 * L), F32), pltpu.VMEM((ns, 2, B_HEADS, 1, LANES), F32)],
        compiler_params=_cparams(("arbitrary", "arbitrary")), name="mlstm_lat" if lat else "mlstm_ctx",
    )(*args)
    return [res[0].reshape(n, 512), res[1].reshape(n, 512)] + list(res[2:])


def _merge_kernel(x_ref, mod_ref, oa_ref, hf_ref, hb_ref, ob_ref, oc_ref, g_ref, hn_ref,
                  wpa_ref, wpb_ref, wpc_ref, wo_ref, bo_ref, ln_g_ref, ln_b_ref, wr_ref, br_ref,
                  x1_o, u2_o, ti_o, tw_o):
    gate1 = mod_ref[0, 2:3, :]
    shift2 = mod_ref[0, 3:4, :]
    scale2 = mod_ref[0, 4:5, :]
    wr = _split3(wr_ref[...])
    eio = lax.broadcasted_iota(I32, (N_EXPERTS, 1), 0)
    nt = ((1,), (1,))
    cs = [dict(rs=slice(r, r + MERGE_ROWS)) for r in range(0, TM_MERGE, MERGE_ROWS)]
    for c in cs:
        c["pa"] = _dot(oa_ref[c["rs"], :], wpa_ref[...])
        c["pc"] = _dot(oc_ref[c["rs"], :], wpc_ref[...])
    for c in cs:
        hm = hf_ref[c["rs"], :] + hb_ref[c["rs"], :]
        parts = []
        for h in range(B_HEADS):
            hh = hm[:, h * LANES:(h + 1) * LANES]
            parts.append(hh * lax.rsqrt(jnp.mean(hh * hh, axis=-1, keepdims=True) + EPS))
        c["out_b"] = (jnp.concatenate(parts, axis=1) * hn_ref[...]
                      * jax.nn.sigmoid(ob_ref[c["rs"], :].astype(F32))).astype(BF16)
    for c in cs:
        c["pb"] = _dot(c["out_b"], wpb_ref[...])
    for c in cs:
        rs = c["rs"]
        c["merged"] = (g_ref[rs, 0:1024].astype(F32) * c["pa"] + g_ref[rs, 1024:2048].astype(F32) * c["pb"]
                       + g_ref[rs, 2048:3072].astype(F32) * c["pc"]).astype(BF16)
    for c in cs:
        c["y"] = _dot(c["merged"], wo_ref[...]) + bo_ref[...]
    for c in cs:
        c["x1"] = _ln(DEEPNORM_ALPHA * x_ref[c["rs"], :] + gate1 * c["y"]) * ln_g_ref[...] + ln_b_ref[...]
        x1_o[c["rs"], :] = c["x1"]
    for c in cs:
        c["u2"] = _ln(c["x1"]) * (1.0 + scale2) + shift2
        u2_o[c["rs"], :] = _pack_rows(c["u2"])
    for c in cs:
        up = _split3(c["u2"])
        c["cur"] = (_dot(wr[0], up[0], nt) + _dot(wr[0], up[1], nt) + _dot(wr[1], up[0], nt)
                    + _dot(wr[0], up[2], nt) + _dot(wr[2], up[0], nt) + _dot(wr[1], up[1], nt)) + br_ref[...]
        c["vals"], c["idxs"] = [], []
    for _ in range(TOP_K):
        for c in cs:
            mx = jnp.max(c["cur"], axis=0, keepdims=True)
            ix = jnp.min(jnp.where(c["cur"] == mx, eio, N_EXPERTS), axis=0, keepdims=True)
            c["vals"].append(mx)
            c["idxs"].append(ix)
            c["cur"] = jnp.where(eio == ix, -jnp.inf, c["cur"])
    for c in cs:
        ex = [jnp.exp(v - c["vals"][0]) for v in c["vals"]]
        inv = 1.0 / sum(ex)
        ti_o[:, c["rs"]] = jnp.concatenate(c["idxs"], axis=0)
        tw_o[:, c["rs"]] = jnp.concatenate([e * inv for e in ex], axis=0)


def _merge_call(lat, x, mod, out_a, hf, hb, ob, out_c, g, hn, wpa, wpb, wpc, wo, bo, ln_g, ln_b, wr_t, br):
    n = x.shape[0]
    tm = TM_MERGE
    row = lambda w: pl.BlockSpec((tm, w), lambda i: (i, 0))
    const = lambda shape: pl.BlockSpec(shape, lambda i: (0,) * len(shape), pipeline_mode=pl.Buffered(1))
    return pl.pallas_call(
        _merge_kernel,
        grid=(n // tm,),
        in_specs=[row(D_MODEL), _mod_spec(lat, tm),
                  row(512), row(512), row(512), row(512), row(512), row(3072),
                  const((1, 512)), const((512, D_MODEL)), const((512, D_MODEL)), const((512, D_MODEL)),
                  const((D_MODEL, D_MODEL)), const((1, D_MODEL)), const((1, D_MODEL)), const((1, D_MODEL)),
                  const((N_EXPERTS, D_MODEL)), const((N_EXPERTS, 1))],
        out_specs=[row(D_MODEL), row(D_PACK),
                   pl.BlockSpec((TOP_K, tm), lambda i: (0, i)), pl.BlockSpec((TOP_K, tm), lambda i: (0, i))],
        out_shape=[jax.ShapeDtypeStruct((n, D_MODEL), F32), jax.ShapeDtypeStruct((n, D_PACK), I32),
                   jax.ShapeDtypeStruct((TOP_K, n), I32), jax.ShapeDtypeStruct((TOP_K, n), F32)],
        compiler_params=_cparams(("arbitrary",)), name="merge_lat" if lat else "merge_ctx",
    )(x, mod, out_a, hf, hb, ob, out_c, g, hn, wpa, wpb, wpc, wo, bo, ln_g, ln_b, wr_t, br)


def _route_kernel(ti_ref, pos_o, te_o, tv_o, nt_o, rank_s):
    ch = ROUTE_CHUNK
    n_tok = ti_ref.shape[1]
    eio = lax.broadcasted_iota(I32, (N_EXPERTS, 1), 0)
    tri = (lax.broadcasted_iota(I32, (ch, ch), 0) <= lax.broadcasted_iota(I32, (ch, ch), 1)).astype(BF16)

    def onehots(off):
        ti = ti_ref[:, pl.ds(off, ch)]
        return [(eio == ti[k:k + 1, :]).astype(F32) for k in range(TOP_K)]

    def count_body(c, carry):
        off = pl.multiple_of(c * ch, ch)
        ohs = onehots(off)
        oh = ohs[0] + ohs[1] + ohs[2] + ohs[3]
        incl = _dot(oh.astype(BF16), tri)
        excl = incl - oh + carry
        for k in range(TOP_K):
            rank_s[k:k + 1, pl.ds(off, ch)] = jnp.sum(ohs[k] * excl, axis=0, keepdims=True)
        return carry + incl[:, ch - 1:ch]

    counts = lax.fori_loop(0, n_tok // ch, count_body, jnp.zeros((N_EXPERTS, 1), F32))
    ntile_e = jnp.floor((counts + (TME - 1)) * (1.0 / TME))
    lower = (lax.broadcasted_iota(I32, (N_EXPERTS, N_EXPERTS), 1)
             < lax.broadcasted_iota(I32, (N_EXPERTS, N_EXPERTS), 0)).astype(BF16)
    off_t = _dot(lower, jnp.broadcast_to(ntile_e, (N_EXPERTS, LANES)).astype(BF16))[:, 0:1]
    end_t = off_t + ntile_e
    total = jnp.sum(ntile_e, axis=0, keepdims=True)
    jt = lax.broadcasted_iota(I32, (1, TILE_LANES), 1).astype(F32)
    te = jnp.minimum(jnp.sum((end_t <= jt).astype(F32), axis=0, keepdims=True), N_EXPERTS - 1.0)
    sel = (eio.astype(F32) == te).astype(F32)
    cnt_j = jnp.sum(sel * counts, axis=0, keepdims=True)
    off_j = jnp.sum(sel * off_t, axis=0, keepdims=True)
    valid = jnp.clip(cnt_j - (jt - off_j) * TME, 0.0, float(TME))
    te_o[...] = te.astype(I32)
    tv_o[...] = jnp.where(jt < total, valid, 0.0).astype(I32)
    nt_o[...] = jnp.broadcast_to(total, (1, LANES)).astype(I32)
    off_rows = off_t * TME

    def pos_body(c, carry):
        off = pl.multiple_of(c * ch, ch)
        ohs = onehots(off)
        for k in range(TOP_K):
            base = jnp.sum(ohs[k] * off_rows, axis=0, keepdims=True)
            pos_o[k:k + 1, pl.ds(off, ch)] = (rank_s[k:k + 1, pl.ds(off, ch)] + base).astype(I32)
        return carry

    lax.fori_loop(0, n_tok // ch, pos_body, 0)


def _route_call(top_i):
    n = top_i.shape[1]
    assert _max_tiles(n) <= TILE_LANES
    tiles = jax.ShapeDtypeStruct((1, TILE_LANES), I32)
    return pl.pallas_call(
        _route_kernel,
        out_shape=[jax.ShapeDtypeStruct((TOP_K, n), I32), tiles, tiles,
                   jax.ShapeDtypeStruct((1, LANES), I32)],
        scratch_shapes=[pltpu.VMEM((TOP_K, n), F32)],
        compiler_params=pltpu.CompilerParams(vmem_limit_bytes=VMEM_LIMIT), name="route",
    )(top_i)


def _sc_mesh():
    return plsc.VectorSubcoreMesh(core_axis_name="c", subcore_axis_name="s")


def _sc_worker(n_rows):
    info = plsc.get_sparse_core_info()
    n_workers = info.num_cores * info.num_subcores
    per_w = n_rows // n_workers
    assert per_w * n_workers == n_rows and per_w % SC_WINDOW == 0
    wid = lax.axis_index("s") * info.num_cores + lax.axis_index("c")
    return wid * per_w, per_w


def _dispatch_call(u2s, pos_flats, p_rows):
    w = SC_WINDOW
    width, dtype = u2s[0].shape[1], u2s[0].dtype

    @functools.partial(pl.kernel, out_type=jax.ShapeDtypeStruct((p_rows, width), dtype), mesh=_sc_mesh(),
                       scratch_types=[pltpu.VMEM((w,), I32)] * (2 * TOP_K)
                       + [pltpu.VMEM((w, width), dtype)] * 2 + [pltpu.SemaphoreType.DMA] * 4)
    def k(*refs):
        n_chains = len(u2s)
        srcs = [(refs[2 * j], refs[2 * j + 1], u2s[j].shape[0]) for j in range(n_chains)]
        o_hbm = refs[2 * n_chains]
        scratch = refs[2 * n_chains + 1:]
        idx = (scratch[0:TOP_K], scratch[TOP_K:2 * TOP_K])
        rows = scratch[2 * TOP_K:2 * TOP_K + 2]
        sem_load = scratch[2 * TOP_K + 2:2 * TOP_K + 4]
        sem_scat = scratch[2 * TOP_K + 4:2 * TOP_K + 6]
        chunks = []
        for j, (_, _, n) in enumerate(srcs):
            base, per_w = _sc_worker(n)
            chunks += [(j, base + c * w) for c in range(per_w // w)]

        def start_loads(chunk, s):
            j, first = chunk
            x_hbm, p_hbm, n = srcs[j]
            off = pl.multiple_of(first, 8)
            cps = [pltpu.async_copy(x_hbm.at[pl.ds(off, w)], rows[s], sem_load[s])]
            cps += [pltpu.async_copy(p_hbm.at[pl.ds(pl.multiple_of(kk * n + off, 8), w)], idx[s][kk], sem_load[s])
                    for kk in range(TOP_K)]
            return cps

        loads = start_loads(chunks[0], 0)
        scatters = [None, None]
        for c in range(len(chunks)):
            s = c % 2
            for cp in loads:
                cp.wait()
            if c + 1 < len(chunks):
                if scatters[1 - s] is not None:
                    for cp in scatters[1 - s]:
                        cp.wait()
                    scatters[1 - s] = None
                loads = start_loads(chunks[c + 1], 1 - s)
            scatters[s] = [pltpu.async_copy(rows[s], o_hbm.at[idx[s][kk]], sem_scat[s]) for kk in range(TOP_K)]
        for group in scatters:
            if group is not None:
                for cp in group:
                    cp.wait()

    return k(*[a for pair in zip(u2s, pos_flats) for a in pair])


def _combine_call(ys, pos_flat):
    w = SC_WINDOW
    n = pos_flat.shape[0]

    @functools.partial(pl.kernel, out_type=jax.ShapeDtypeStruct((n, ys.shape[1]), ys.dtype), mesh=_sc_mesh(),
                       scratch_types=[pltpu.VMEM((w,), I32)] * 2 + [pltpu.VMEM((w, ys.shape[1]), ys.dtype)] * 2
                       + [pltpu.SemaphoreType.DMA] * 4)
    def k(y_hbm, p_hbm, o_hbm, idx0, idx1, rows0, rows1, sg0, sg1, ss0, ss1):
        idx, rows, sem_gather, sem_store = (idx0, idx1), (rows0, rows1), (sg0, sg1), (ss0, ss1)
        base, per_w = _sc_worker(n)
        n_chunks = per_w // w

        def offset(c):
            return pl.multiple_of(base + c * w, 8)

        pltpu.sync_copy(p_hbm.at[pl.ds(offset(0), w)], idx[0])
        gather = pltpu.async_copy(y_hbm.at[idx[0]], rows[0], sem_gather[0])
        stores = [None, None]
        for c in range(n_chunks):
            s = c % 2
            if c + 1 < n_chunks:
                pltpu.sync_copy(p_hbm.at[pl.ds(offset(c + 1), w)], idx[1 - s])
            gather.wait()
            if c + 1 < n_chunks:
                if stores[1 - s] is not None:
                    stores[1 - s].wait()
                    stores[1 - s] = None
                gather = pltpu.async_copy(y_hbm.at[idx[1 - s]], rows[1 - s], sem_gather[1 - s])
            stores[s] = pltpu.async_copy(rows[s], o_hbm.at[pl.ds(offset(c), w)], sem_store[s])
        for st in stores:
            if st is not None:
                st.wait()

    return k(ys, pos_flat)


def _moe_kernel(l, n_tiles_max, te_ref, tv_ref, nt_ref, x_ref, wu_hbm, bu_ref, wd_hbm, bd_ref,
                y_ref, wu_f, wd_f, wu_s, wd_s, sem, group_s):
    i = pl.program_id(0)
    nt = nt_ref[0]

    def fetch(e, slot):
        return (pltpu.make_async_copy(wu_hbm.at[l, e], wu_f.at[slot], sem.at[0, slot]),
                pltpu.make_async_copy(wd_hbm.at[l, e], wd_f.at[slot], sem.at[1, slot]))

    @pl.when(i == 0)
    def _():
        group_s[0] = 0
        for cp in fetch(te_ref[0], 0):
            cp.start()

    @pl.when(i < nt)
    def _():
        e = te_ref[i]
        prev = te_ref[jnp.maximum(i - 1, 0)]

        @pl.when((i == 0) | (e != prev))
        def _():
            group = group_s[0]
            slot = group % 2
            j = lax.while_loop(lambda j: (j < nt) & (te_ref[jnp.minimum(j, n_tiles_max - 1)] == e),
                               lambda j: j + 1, i + 1)

            @pl.when(j < nt)
            def _():
                for cp in fetch(te_ref[jnp.minimum(j, n_tiles_max - 1)], 1 - slot):
                    cp.start()

            for cp in fetch(e, slot):
                cp.wait()
            wu_s[...] = wu_f[slot].astype(BF16)
            wd_s[...] = wd_f[slot].astype(BF16)
            group_s[0] = group + 1

        valid = tv_ref[i]
        half = MOE_ROWS
        for j in range(TME // half):
            @pl.when(valid > j * half)
            def _():
                sl = slice(j * half, (j + 1) * half)
                rows = lax.broadcasted_iota(I32, (half, 1), 0) + j * half
                lo, hi = _unpack_rows(jnp.where(rows < valid, x_ref[sl, :], 0))
                x = jnp.concatenate([lo.astype(BF16), hi.astype(BF16)], axis=1)
                h = _dot(x, wu_s[...]) + bu_ref[...]
                h_glu = jnp.minimum(h[:, :D_EXPERT], SWIGLU_LIMIT)
                h_lin = jnp.clip(h[:, D_EXPERT:], -SWIGLU_LIMIT, SWIGLU_LIMIT)
                act = h_glu * jax.nn.sigmoid(SWIGLU_ALPHA * h_glu) * (h_lin + 1.0)
                y_ref[sl, :] = _pack_rows(_dot(act.astype(BF16), wd_s[...]) + bd_ref[...])


def _moe_call(l, tile_e, tile_v, n_tiles, xs, w_up, b_up, w_down, b_down):
    n_tiles_max = xs.shape[0] // TME

    def tile(i, te, tv, nt):
        return jnp.minimum(i, nt[0] - 1)

    def wmap(i, te, tv, nt):
        return (l, te[tile(i, te, tv, nt)], 0, 0)

    rows = pl.BlockSpec((TME, D_PACK), lambda i, te, tv, nt: (tile(i, te, tv, nt), 0))
    grid_spec = pltpu.PrefetchScalarGridSpec(
        num_scalar_prefetch=3,
        grid=(n_tiles_max,),
        in_specs=[rows,
                  pl.BlockSpec(memory_space=pl.ANY),
                  pl.BlockSpec((None, None, 1, 2 * D_EXPERT), wmap),
                  pl.BlockSpec(memory_space=pl.ANY),
                  pl.BlockSpec((None, None, 1, D_MODEL), wmap)],
        out_specs=rows,
        scratch_shapes=[pltpu.VMEM((2, D_MODEL, 2 * D_EXPERT), F32), pltpu.VMEM((2, D_EXPERT, D_MODEL), F32),
                        pltpu.VMEM((D_MODEL, 2 * D_EXPERT), BF16), pltpu.VMEM((D_EXPERT, D_MODEL), BF16),
                        pltpu.SemaphoreType.DMA((2, 2)), pltpu.SMEM((1,), I32)],
    )
    return pl.pallas_call(
        functools.partial(_moe_kernel, l, n_tiles_max), grid_spec=grid_spec,
        out_shape=jax.ShapeDtypeStruct((xs.shape[0], D_PACK), I32),
        compiler_params=_cparams(("arbitrary",)), name="moe",
    )(tile_e, tile_v, n_tiles, xs, w_up, b_up.reshape(DEPTH, N_EXPERTS, 1, -1),
      w_down, b_down.reshape(DEPTH, N_EXPERTS, 1, -1))


def _final_kernel(yg_ref, tw_ref, x1_ref, mod_ref, ln_g_ref, ln_b_ref, o_ref):
    o_ref[...] = _moe_residual_norm(yg_ref, tw_ref, x1_ref, mod_ref, ln_g_ref, ln_b_ref)


def _final_call(lat, yg, tw, x1, mod, ln_g, ln_b):
    n = x1.shape[0]
    tm = TM_MERGE
    row = lambda w: pl.BlockSpec((tm, w), lambda i: (i, 0))
    const = lambda shape: pl.BlockSpec(shape, lambda i: (0,) * len(shape))
    return pl.pallas_call(
        _final_kernel,
        grid=(n // tm,),
        in_specs=[pl.BlockSpec((TOP_K, tm, D_PACK), lambda i: (0, i, 0)), row(TOP_K), row(D_MODEL),
                  _mod_spec(lat, tm), const((1, D_MODEL)), const((1, D_MODEL))],
        out_specs=row(D_MODEL), out_shape=jax.ShapeDtypeStruct((n, D_MODEL), F32),
        compiler_params=_cparams(("arbitrary",)), name="final_lat" if lat else "final_ctx",
    )(yg, tw, x1, mod, ln_g, ln_b)


def _rope_tables():
    half = A_HEAD_DIM // 2
    freqs = ROPE_BASE ** (-jnp.arange(0, half, 2, dtype=F32) / half)
    t = jnp.arange(DEC_SEQ, dtype=jnp.int32)
    ang_r = (t // GRID_W).astype(F32)[:, None] * freqs[None, :]
    ang_c = (t % GRID_W).astype(F32)[:, None] * freqs[None, :]
    cos64 = jnp.concatenate([jnp.cos(ang_r)] * 2 + [jnp.cos(ang_c)] * 2, axis=1)
    sin64 = jnp.concatenate([-jnp.sin(ang_r), jnp.sin(ang_r), -jnp.sin(ang_c), jnp.sin(ang_c)], axis=1)
    lat = (jnp.tile(cos64, (1, 2)), jnp.tile(sin64, (1, 2)))
    ctx = (jnp.ones((TM, LANES), F32), jnp.zeros((TM, LANES), F32))
    return ctx, lat


def _permute_in_cols(t, axis=-1):
    take = lambda lo, hi: lax.slice_in_dim(t, lo, hi, axis=axis)
    pad_shape = list(t.shape)
    pad_shape[axis] = D_IN_PAD - D_IN
    parts = [take(h * 64, (h + 1) * 64) for h in QA_HEAD_ORDER]
    parts += [take(512, 2816), take(2832, 7440), take(2816, 2832), jnp.zeros(pad_shape, t.dtype)]
    return jnp.concatenate(parts, axis=axis)


def _layer_weights(p):
    w_in, b_in = p["w_in"], p["b_in"]
    wpa = p["w_pa"].reshape(A_HEADS, A_HEAD_DIM, D_MODEL)[np.asarray(QA_HEAD_ORDER)].reshape(512, D_MODEL)
    row = lambda t: t.reshape(1, -1)
    return dict(
        inproj=(_permute_in_cols(w_in.T, axis=0).astype(BF16), _permute_in_cols(b_in).reshape(1, D_IN_PAD),
                w_in[:, 2816:2832].T.astype(BF16), b_in[2816:2832].reshape(16, 1),
                row(jnp.tile(p["qn_a"], A_HEADS)), row(jnp.tile(p["kn_a"], A_KV_HEADS)),
                jnp.asarray(np.kron(np.eye(8), np.ones((64, 64))), BF16)),
        merge=(row(p["hn_b"]), wpa.astype(BF16), p["w_pb"].astype(BF16), p["w_pc"].astype(BF16),
               p["w_o"].astype(BF16), row(p["b_o"]), row(p["ln1_g"]), row(p["ln1_b"]),
               p["w_router"].T, p["b_router"].reshape(N_EXPERTS, 1)),
        lam_c=p["lam_c"], sn_c=row(p["sn_c"]), ln2=(row(p["ln2_g"]), row(p["ln2_b"])))


def _chain_mixer(l, lat, x, mod, tables, caches, w, outs, after=None):
    cos_t, sin_t = tables
    lam_init = 0.8 - 0.6 * math.exp(-0.3 * l)
    res = _inproj_call(l, lat, x, mod, cos_t, sin_t, *w["inproj"], None if outs is None else outs[:4], after)
    qa, kab, vab, qb, kb, vb, ob, qc, kcb, vcb, g, ifg, ifgt = res[:13]
    if isinstance(x, tuple):
        x = res[-1]
    if lat:
        ck_a, cv_a, ck_c, cv_c, cn0, m0 = caches
    else:
        ck_a = cv_a = ck_c = cv_c = cn0 = m0 = None
    oa = _attn_call("a", lat, lam_init, qa, kab, vab, ck_a, cv_a, None, None)
    oc = _attn_call("c", lat, lam_init, qc, kcb, vcb, ck_c, cv_c, w["lam_c"], w["sn_c"])
    ml = _mlstm_call(l, lat, qb, kb, vb, ifg, ifgt, cn0, m0, None if outs is None else outs[4:6])
    hf, hb = ml[:2]
    merged = _merge_call(lat, x, mod, oa, hf, hb, ob, oc, g, *w["merge"])
    if lat:
        return merged, None
    m_l = ml[4][:, :, :, 0, 0]
    m_all = m_l[:, None] if outs is None else jnp.concatenate([outs[6], m_l[:, None]], axis=1)
    return merged, tuple(res[13:17]) + (ml[2], ml[3], m_all)


def _moe_layer(l, mix_ctx, mix_lat, mod, w, experts):
    top_i = jnp.concatenate([mix_ctx[2], mix_lat[2]], axis=1)
    pos, tile_e, tile_v, n_tiles = _route_call(top_i)
    p_rows = _max_tiles(N_CTX + N_LAT) * TME
    pos_flat = (pos[:, :N_CTX].reshape(TOP_K * N_CTX), pos[:, N_CTX:].reshape(TOP_K * N_LAT))
    xs = _dispatch_call((mix_ctx[1], mix_lat[1]), pos_flat, p_rows)
    ys = _moe_call(l, tile_e[0], tile_v[0], n_tiles[0, :1], xs,
                   experts["w_up"], experts["b_up"], experts["w_down"], experts["b_down"])
    yg_ctx = _combine_call(ys, pos_flat[0]).reshape(TOP_K, N_CTX, D_PACK)
    yg_lat = _combine_call(ys, pos_flat[1]).reshape(TOP_K, N_LAT, D_PACK)
    return ((yg_ctx, mix_ctx[3].T, mix_ctx[0], mod) + w["ln2"], (yg_lat, mix_lat[3].T, mix_lat[0], mod) + w["ln2"])


def kernel(x_prompt, x_sample, cache_k_gqa, cache_v_gqa, cache_k_diff, cache_v_diff, state_mlstm_c, state_mlstm_n, state_mlstm_m, c, c_ctx, w_in, b_in, qn_a, kn_a, hn_b, lam_c, sn_c, w_pa, w_pb, w_pc, w_o, b_o, w_mod, b_mod, ln1_g, ln1_b, ln2_g, ln2_b, w_router, b_router, w_up, b_up, w_down, b_down):
    x_ctx = x_prompt.reshape(N_CTX, D_MODEL)
    x_lat = x_sample.reshape(N_LAT, D_MODEL)
    cvecs = jnp.zeros((8, D_MODEL), F32).at[0].set(c_ctx).at[1:1 + DEC_BATCH].set(c)
    mod_all = _mod_call(cvecs, w_mod, b_mod).reshape(DEPTH, 8, N_MOD, D_MODEL)[:, :N_GROUPS]
    tables_ctx, tables_lat = _rope_tables()
    params = dict(w_in=w_in, b_in=b_in, qn_a=qn_a, kn_a=kn_a, hn_b=hn_b, lam_c=lam_c, sn_c=sn_c, w_pa=w_pa,
                  w_pb=w_pb, w_pc=w_pc, w_o=w_o, b_o=b_o, ln1_g=ln1_g, ln1_b=ln1_b, ln2_g=ln2_g, ln2_b=ln2_b,
                  w_router=w_router, b_router=b_router)
    experts = dict(w_up=w_up, b_up=b_up, w_down=w_down, b_down=b_down)
    outs = None
    for l in range(DEPTH):
        w = _layer_weights({k: v[l] for k, v in params.items()})
        ck_a = cache_k_gqa[:, l].transpose(0, 2, 1, 3).reshape(DEC_BATCH, PAST_LEN, 128).astype(BF16)
        cv_a = cache_v_gqa[:, l].transpose(0, 2, 1, 3).reshape(DEC_BATCH, PAST_LEN, 128).astype(BF16)
        ck_c = cache_k_diff[:, l].astype(BF16)
        cv_c = cache_v_diff[:, l].astype(BF16)
        cn0 = jnp.concatenate([state_mlstm_c[:, l], state_mlstm_n[:, l][..., None],
                               jnp.zeros((DEC_BATCH, 2, B_HEADS, B_HEAD_DIM, B_HEAD_DIM - 1), F32)], axis=-1)
        m0 = jnp.broadcast_to(state_mlstm_m[:, l][..., None, None], (DEC_BATCH, 2, B_HEADS, 1, LANES))
        mix_ctx, outs = _chain_mixer(l, False, x_ctx, mod_all[l], tables_ctx, None, w, outs)
        mix_lat, _ = _chain_mixer(l, True, x_lat, mod_all[l], tables_lat, (ck_a, cv_a, ck_c, cv_c, cn0, m0), w, None,
                                  after=mix_ctx[3] if l > 0 else None)
        x_ctx, x_lat = _moe_layer(l, mix_ctx, mix_lat, mod_all[l], w, experts)
    x_ctx = _final_call(False, *x_ctx)
    x_lat = _final_call(True, *x_lat)
    return (x_ctx.reshape(BATCH, SEQ, D_MODEL), x_lat.reshape(DEC_BATCH, DEC_SEQ, D_MODEL)) + tuple(outs)
```

```python
import functools
import math

import numpy as np
import jax
import jax.numpy as jnp
from jax import lax
from jax.experimental import pallas as pl
from jax.experimental.pallas import tpu as pltpu
from jax.experimental.pallas import tpu_sc as plsc

F32 = jnp.float32
BF16 = jnp.bfloat16
I32 = jnp.int32

D_MODEL = 1024
BATCH = 16
SEQ = 256
DEPTH = 2
DEC_BATCH = 2
DEC_SEQ = 4096
PAST_LEN = 512
GRID_W = 64
ROPE_BASE = 10000.0
EPS = 1e-6
A_HEADS = 8
A_KV_HEADS = 2
A_HEAD_DIM = 64
B_HEADS = 4
B_HEAD_DIM = 128
MLSTM_CHUNK = 128
C_HEADS = 4
C_HEAD_DIM = 64
C_V_DIM = 128
N_EXPERTS = 32
TOP_K = 4
D_EXPERT = 1024
SWIGLU_LIMIT = 7.0
SWIGLU_ALPHA = 1.702
N_MOD = 6
DEEPNORM_ALPHA = (2 * DEPTH) ** 0.25

N_CTX = BATCH * SEQ
N_LAT = DEC_BATCH * DEC_SEQ
N_GROUPS = 1 + DEC_BATCH

LANES = 128
TM = 256
TQ = 512
KEY_BLOCK = 2048
MLSTM_SEQS_LAT = 2
MLSTM_SEQS_CTX = 4
TM_MERGE = 512
MERGE_ROWS = 128
TME = 1024
MOE_ROWS = 256
TILE_LANES = 128
ROUTE_CHUNK = 512
D_PACK = D_MODEL // 2
SC_WINDOW = 64
VMEM_LIMIT = 56 * 1024 * 1024

QA_HEAD_ORDER = (0, 4, 1, 5, 2, 6, 3, 7)
D_IN = 7440
D_IN_PAD = 7552
P_QA, P_KA, P_VA, P_QB, P_KB, P_VB, P_OB, P_QC, P_KC, P_VC, P_G, P_IF = (
    0, 512, 640, 768, 1280, 1792, 2304, 2816, 3328, 3840, 4352, 7424)


def _max_tiles(n_tok):
    return (n_tok * TOP_K) // TME + N_EXPERTS


def _cparams(sem):
    return pltpu.CompilerParams(dimension_semantics=sem, vmem_limit_bytes=VMEM_LIMIT)


def _split3(a):
    a1 = a.astype(BF16)
    r = a - a1.astype(F32)
    a2 = r.astype(BF16)
    a3 = (r - a2.astype(F32)).astype(BF16)
    return a1, a2, a3


def _dot(a, b, dims=None):
    if dims is None:
        return jnp.dot(a, b, preferred_element_type=F32)
    return lax.dot_general(a, b, (dims, ((), ())), preferred_element_type=F32)


def _dot_exact_rhs01(a, m, dims=None):
    return sum(_dot(p, m, dims) for p in _split3(a))


def _dot_exact_lhs01(m, a, dims=None):
    return sum(_dot(m, p, dims) for p in _split3(a))


def _pack_rows(x):
    xb = x.astype(BF16).astype(F32)
    lo = pltpu.bitcast(xb[:, :D_PACK], I32)
    lo = lax.shift_right_logical(lo, jnp.full_like(lo, 16))
    hi = pltpu.bitcast(xb[:, D_PACK:], I32) & jnp.int32(-65536)
    return lo | hi


def _unpack_rows(w):
    return (pltpu.bitcast(lax.shift_left(w, jnp.full_like(w, 16)), F32),
            pltpu.bitcast(w & jnp.int32(-65536), F32))


def _ln(x):
    mu = jnp.mean(x, axis=-1, keepdims=True)
    xc = x - mu
    var = jnp.mean(xc * xc, axis=-1, keepdims=True)
    return xc * lax.rsqrt(var + EPS)


def _log_sigmoid(x):
    return jnp.minimum(x, 0.0) - jnp.log(1.0 + jnp.exp(-jnp.abs(x)))


def _rope(x, cos, sin):
    lane = lax.broadcasted_iota(I32, (1, LANES), 1)
    first = (lane % 32) < 16
    outs = []
    for j in range(x.shape[1] // LANES):
        xj = x[:, j * LANES:(j + 1) * LANES]
        partner = jnp.where(first, pltpu.roll(xj, LANES - 16, 1), pltpu.roll(xj, 16, 1))
        outs.append(xj * cos + partner * sin)
    return outs[0] if len(outs) == 1 else jnp.concatenate(outs, axis=1)


def _mod_spec(lat, tile=TM):
    if lat:
        return pl.BlockSpec((1, N_MOD, D_MODEL), lambda i: (1 + i * tile // DEC_SEQ, 0, 0))
    return pl.BlockSpec((1, N_MOD, D_MODEL), lambda i: (0, 0, 0))


def _mod_kernel(c_ref, w_ref, b_ref, o_ref):
    c = c_ref[...]
    s = c * jax.nn.sigmoid(c)
    o_ref[...] = _dot(s.astype(BF16), w_ref[...].astype(BF16)) + b_ref[...]


def _mod_call(cvecs, w_mod, b_mod):
    tn = 1536
    n = N_MOD * D_MODEL
    return pl.pallas_call(
        _mod_kernel,
        grid=(DEPTH, n // tn),
        in_specs=[pl.BlockSpec((8, D_MODEL), lambda l, j: (0, 0)),
                  pl.BlockSpec((None, D_MODEL, tn), lambda l, j: (l, 0, j)),
                  pl.BlockSpec((None, 1, tn), lambda l, j: (l, 0, j))],
        out_specs=pl.BlockSpec((None, 8, tn), lambda l, j: (l, 0, j)),
        out_shape=jax.ShapeDtypeStruct((DEPTH, 8, n), F32),
        compiler_params=_cparams(("arbitrary", "arbitrary")),
        name="mod",
    )(cvecs, w_mod, b_mod.reshape(DEPTH, 1, n))


def _moe_residual_norm(yg_ref, tw_ref, x1_ref, mod_ref, ln_g_ref, ln_b_ref):
    tw = tw_ref[...]
    halves = [_unpack_rows(yg_ref[k]) for k in range(TOP_K)]
    moe = jnp.concatenate([sum(tw[:, k:k + 1] * halves[k][j] for k in range(TOP_K)) for j in range(2)], axis=1)
    gate2 = mod_ref[0, 5:6, :]
    return _ln(DEEPNORM_ALPHA * x1_ref[...] + gate2 * moe) * ln_g_ref[...] + ln_b_ref[...]


def _inproj_kernel(with_cache, fused, n_unread, *refs):
    if fused:
        x = _moe_residual_norm(*refs[:6])
        refs = refs[6:]
    else:
        x = refs[0][...]
        refs = refs[1:]
    mod_ref, cos_ref, sin_ref, w_ref, b_ref, wgt_ref, bgt_ref, gq_ref, gk_ref, gmat_ref = refs[:10]
    outs = refs[10 + n_unread:]
    qa_o, kab_o, vab_o, qb_o, kb_o, vb_o, ob_o, qc_o, kcb_o, vcb_o, g_o, ifg_o, ifgt_o = outs[:13]
    if fused:
        outs[-1][...] = x
    shift = mod_ref[0, 0:1, :]
    scale = mod_ref[0, 1:2, :]
    ub = (_ln(x) * (1.0 + scale) + shift).astype(BF16)
    cos = cos_ref[...]
    sin = sin_ref[...]

    def proj(lo, n):
        return _dot(ub, w_ref[lo:lo + n, :], ((1,), (1,))) + b_ref[:, lo:lo + n]

    def headnorm(z, gmat, gain):
        ss = _dot((z * z).astype(BF16), gmat)
        return z * lax.rsqrt(ss * (1.0 / A_HEAD_DIM) + EPS) * gain

    segs = [(P_QA, 512), (P_KA, 128), (P_VA, 128), (P_QB, 512), (P_KB, 512), (P_VB, 512), (P_OB, 512),
            (P_QC, 512), (P_KC, 512), (P_VC, 512), (P_G, 1024), (P_G + 1024, 1024), (P_G + 2048, 1024),
            (P_IF, LANES)]
    keep = {}

    def epilogue(k, z):
        if k == 0:
            z = headnorm(z, gmat_ref[...], gq_ref[...])
            qa_o[...] = (_rope(z, cos, sin) * 0.125).astype(BF16)
        elif k == 1:
            keep["k"] = _rope(headnorm(z, gmat_ref[0:128, 0:128], gk_ref[...]), cos, sin)
            kab_o[...] = keep["k"].astype(BF16)
        elif k == 2:
            keep["v"] = z
            vab_o[...] = z.astype(BF16)
        elif k == 3:
            qb_o[...] = z.astype(BF16)
        elif k == 4:
            kb_o[...] = (z * (B_HEAD_DIM ** -0.5)).astype(BF16)
        elif k == 5:
            vb_o[...] = z.astype(BF16)
        elif k == 6:
            ob_o[...] = z.astype(BF16)
        elif k == 7:
            qc_o[...] = (_rope(z, cos, sin) * 0.125).astype(BF16)
        elif k == 8:
            keep["kc"] = _rope(z, cos, sin)
            kcb_o[...] = keep["kc"].astype(BF16)
        elif k == 9:
            keep["vc"] = z
            vcb_o[...] = z.astype(BF16)
        elif k in (10, 11, 12):
            g_o[:, (k - 10) * 1024:(k - 9) * 1024] = jax.nn.sigmoid(z).astype(BF16)
        else:
            lane = lax.broadcasted_iota(I32, (1, LANES), 1)
            ifg_o[...] = jnp.where(lane >= 8, _log_sigmoid(z), z)

    z_prev = proj(*segs[0])
    for k in range(1, len(segs)):
        z_next = proj(*segs[k])
        epilogue(k - 1, z_prev)
        z_prev = z_next
    zt = _dot(wgt_ref[...], ub, ((1,), (1,))) + bgt_ref[...]
    epilogue(len(segs) - 1, z_prev)
    row = lax.broadcasted_iota(I32, (16, 1), 0)
    ifgt_o[...] = jnp.where(row >= 8, _log_sigmoid(zt), zt)

    if with_cache:
        nka_o, nva_o, nkc_o, nvc_o = outs[13:17]
        for h in range(A_KV_HEADS):
            nka_o[h] = keep["k"][:, h * A_HEAD_DIM:(h + 1) * A_HEAD_DIM]
            nva_o[h] = keep["v"][:, h * A_HEAD_DIM:(h + 1) * A_HEAD_DIM]
        for h in range(C_HEADS):
            nkc_o[h] = keep["kc"][:, h * LANES:(h + 1) * LANES]
            nvc_o[h] = keep["vc"][:, h * LANES:(h + 1) * LANES]


def _inproj_call(l, lat, x, mod, cos_t, sin_t, w_p, b_p, w_gt, b_gt, gq, gk, gmat, cache_outs, after=None):
    fused = isinstance(x, tuple)
    n = x[2].shape[0] if fused else x.shape[0]
    row = lambda w: pl.BlockSpec((TM, w), lambda i: (i, 0))
    const = lambda shape: pl.BlockSpec(shape, lambda i: (0,) * len(shape), pipeline_mode=pl.Buffered(1))
    sds = lambda w, dt: jax.ShapeDtypeStruct((n, w), dt)
    out_defs = [(512, BF16), (128, BF16), (128, BF16), (512, BF16), (512, BF16), (512, BF16), (512, BF16),
                (512, BF16), (512, BF16), (512, BF16), (3072, BF16), (LANES, F32)]
    seq = DEC_SEQ if lat else SEQ
    tiles_per_seq = seq // TM
    out_specs = [row(w) for w, _ in out_defs] + [
        pl.BlockSpec((None, 16, TM), lambda i: (i // tiles_per_seq, 0, i % tiles_per_seq))]
    out_shape = [sds(w, dt) for w, dt in out_defs] + [jax.ShapeDtypeStruct((n // seq, 16, seq), F32)]
    alias_in, aliases = [], {}
    if lat:
        tbl = pl.BlockSpec((TM, LANES), lambda i: (i % (DEC_SEQ // TM), 0))
    else:
        assert TM == SEQ
        tbl = pl.BlockSpec((TM, LANES), lambda i: (0, 0))
        cache_defs = [(A_KV_HEADS, A_HEAD_DIM), (A_KV_HEADS, A_HEAD_DIM), (C_HEADS, LANES), (C_HEADS, LANES)]
        out_specs += [pl.BlockSpec((None, None, nh, SEQ, d), lambda i: (i, l, 0, 0, 0)) for nh, d in cache_defs]
        out_shape += [jax.ShapeDtypeStruct((BATCH, DEPTH, nh, SEQ, d), F32) for nh, d in cache_defs]
        if cache_outs is not None:
            alias_in = list(cache_outs)
    if fused:
        plain = lambda shape: pl.BlockSpec(shape, lambda i: (0,) * len(shape))
        lead_specs = [pl.BlockSpec((TOP_K, TM, D_PACK), lambda i: (0, i, 0)), row(TOP_K), row(D_MODEL),
                      _mod_spec(lat), plain((1, D_MODEL)), plain((1, D_MODEL))]
        lead = list(x)
        out_specs.append(row(D_MODEL))
        out_shape.append(sds(D_MODEL, F32))
    else:
        lead_specs, lead = [row(D_MODEL)], [x]
    n_in = len(lead) + 10
    aliases = {n_in + j: len(out_defs) + 1 + j for j in range(len(alias_in))}
    unread = alias_in + ([] if after is None else [after])
    return pl.pallas_call(
        functools.partial(_inproj_kernel, not lat, fused, len(unread)),
        grid=(n // TM,),
        in_specs=lead_specs + [_mod_spec(lat), tbl, tbl,
                               const((D_IN_PAD, D_MODEL)), const((1, D_IN_PAD)),
                               const((16, D_MODEL)), const((16, 1)),
                               const((1, 512)), const((1, 128)), const((512, 512))]
        + [pl.BlockSpec(memory_space=pl.ANY)] * len(unread),
        out_specs=out_specs, out_shape=out_shape, input_output_aliases=aliases,
        compiler_params=_cparams(("arbitrary",)),
        name="inproj_lat" if lat else "inproj_ctx",
    )(*lead, mod, cos_t, sin_t, w_p, b_p, w_gt, b_gt, gq, gk, gmat, *unread)


def _attn_kernel(mode, nseg, lam_init, *refs):
    q_ref = refs[0]
    kv = refs[1:1 + 2 * nseg]
    rest = refs[1 + 2 * nseg:]
    o_ref = rest[-1]
    tq = q_ref.shape[0]
    lane = lax.broadcasted_iota(I32, (1, LANES), 1)
    if mode == "c":
        lam_ref, sn_ref = rest[0], rest[1]
        lp = lam_ref[...]
        lam = (jnp.exp(jnp.sum(lp[0:1] * lp[1:2], axis=-1, keepdims=True))
               - jnp.exp(jnp.sum(lp[2:3] * lp[3:4], axis=-1, keepdims=True)) + lam_init)
    for blk in range(q_ref.shape[1] // LANES):
        qs = slice(blk * LANES, (blk + 1) * LANES)
        ks = qs if kv[0].shape[1] > LANES else slice(0, LANES)
        q = q_ref[:, qs]
        zero = jnp.zeros_like(q)
        qq = jnp.concatenate([jnp.where(lane < 64, q, zero), jnp.where(lane >= 64, q, zero)], axis=0)
        m = l = acc = None
        for j in range(nseg):
            k_ref, v_ref = kv[2 * j], kv[2 * j + 1]
            for off in range(0, k_ref.shape[0], KEY_BLOCK):
                n = min(KEY_BLOCK, k_ref.shape[0] - off)
                s = _dot(qq, k_ref[off:off + n, ks], ((1,), (1,)))
                bm = jnp.max(s, axis=-1, keepdims=True)
                if m is None:
                    m = bm
                    e = jnp.exp(s - m)
                    l = jnp.sum(e, axis=-1, keepdims=True)
                    acc = _dot(e.astype(BF16), v_ref[off:off + n, ks])
                else:
                    m_new = jnp.maximum(m, bm)
                    alpha = jnp.exp(m - m_new)
                    e = jnp.exp(s - m_new)
                    l = alpha * l + jnp.sum(e, axis=-1, keepdims=True)
                    acc = alpha * acc + _dot(e.astype(BF16), v_ref[off:off + n, ks])
                    m = m_new
        o = acc * (1.0 / l)
        if mode == "a":
            out = jnp.where(lane < 64, o[:tq], o[tq:])
        else:
            o = o[:tq] - lam * o[tq:]
            ms = jnp.mean(o * o, axis=-1, keepdims=True)
            out = o * lax.rsqrt(ms + EPS) * sn_ref[...] * (1.0 - lam_init)
        o_ref[:, qs] = out.astype(o_ref.dtype)


def _attn_call(mode, lat, lam_init, q_all, k_all, v_all, k_cache, v_cache, lam_c, sn_c):
    wide = k_all.shape[1] == 512
    kcol = (lambda h: h) if wide else (lambda h: 0)
    if lat:
        nq = DEC_SEQ // TQ
        grid = (DEC_BATCH, nq, 4)
        q_spec = pl.BlockSpec((TQ, LANES), lambda b, i, h: (b * nq + i, h))
        kn_spec = pl.BlockSpec((DEC_SEQ, LANES), lambda b, i, h: (b, kcol(h)))
        if wide:
            kc_spec = pl.BlockSpec((None, None, PAST_LEN, LANES), lambda b, i, h: (b, h, 0, 0))
        else:
            kc_spec = pl.BlockSpec((None, PAST_LEN, LANES), lambda b, i, h: (b, 0, 0))
        in_specs = [q_spec, kc_spec, kc_spec, kn_spec, kn_spec]
        args = [q_all, k_cache, v_cache, k_all, v_all]
        nseg = 2
        o_spec = q_spec
        sem = ("arbitrary", "arbitrary", "arbitrary")
        zero_map = lambda b, i, h: (0, 0)
    else:
        grid = (BATCH,)
        q_spec = pl.BlockSpec((SEQ, 512), lambda b: (b, 0))
        k_spec = pl.BlockSpec((SEQ, k_all.shape[1]), lambda b: (b, 0))
        in_specs = [q_spec, k_spec, k_spec]
        args = [q_all, k_all, v_all]
        nseg = 1
        o_spec = q_spec
        sem = ("arbitrary",)
        zero_map = lambda b: (0, 0)
    if mode == "c":
        in_specs += [pl.BlockSpec((4, C_HEAD_DIM), zero_map), pl.BlockSpec((1, C_V_DIM), zero_map)]
        args += [lam_c, sn_c]
    return pl.pallas_call(
        functools.partial(_attn_kernel, mode, nseg, lam_init),
        grid=grid, in_specs=in_specs, out_specs=o_spec,
        out_shape=jax.ShapeDtypeStruct((q_all.shape[0], 512), BF16),
        compiler_params=_cparams(sem), name=f"attn_{mode}_{'lat' if lat else 'ctx'}",
    )(*args)


def _mlstm_kernel(has_init, want_state, n_alias, *refs):
    tok = (refs[0:5], refs[5:10])
    pos = 10
    if has_init:
        cn0_ref, m0_ref = refs[pos:pos + 2]
        pos += 2
    pos += n_alias
    h_refs = refs[pos:pos + 2]
    pos += 2
    if want_state:
        c_o, n_o, m_o = refs[pos:pos + 3]
        pos += 3
    cn_s, m_s = refs[pos:pos + 2]

    c = pl.program_id(1)
    last = pl.num_programs(1) - 1
    L = MLSTM_CHUNK

    @pl.when(c == 0)
    def _():
        if has_init:
            cn_s[...] = cn0_ref[...]
            m_s[...] = m0_ref[...]
        else:
            cn_s[...] = jnp.zeros_like(cn_s)
            m_s[...] = jnp.zeros_like(m_s)

    row = lax.broadcasted_iota(I32, (L, L), 0)
    col = lax.broadcasted_iota(I32, (L, L), 1)
    lane = lax.broadcasted_iota(I32, (1, LANES), 1)
    ones_col = jnp.where(lane == 0, 1.0, 0.0).astype(BF16)

    chains = []
    for d in range(2):
        q_ref, k_ref, v_ref, g_ref, gt_ref = tok[d]
        keep = (col <= row) if d == 0 else (col >= row)
        m_c = keep.astype(BF16)
        m_r = ((row <= col) if d == 0 else (row >= col)).astype(BF16)
        for s in range(q_ref.shape[0]):
            g = g_ref[s]
            gt = gt_ref[s]
            bc = _dot_exact_lhs01(m_c, g)
            br = _dot_exact_rhs01(gt, m_r)
            for h in range(B_HEADS):
                sl = slice(h * L, (h + 1) * L)
                gi, gf = 4 * d + h, 8 + 4 * d + h
                ch = dict(seq=s, d=d, h=h, sl=sl, keep=keep, q=q_ref[s, :, sl], k=k_ref[s, :, sl], v=v_ref[s, :, sl],
                          i_col=g[:, gi:gi + 1], f_col=g[:, gf:gf + 1], b_col=bc[:, gf:gf + 1],
                          r_row=gt[gi:gi + 1, :] - br[gf:gf + 1, :],
                          m_prev=m_s[s, d, h][:, 0:1], cn_prev=cn_s[s, d, h])
                chains.append(ch)
    for ch in chains:
        ch["s"] = _dot(ch["q"], ch["k"], ((1,), (1,)))
        ch["qc"] = _dot(ch["q"], ch["cn_prev"].astype(BF16))
    for ch in chains:
        ch["total"] = jnp.sum(ch["f_col"], axis=0, keepdims=True)
        ch["inter"] = ch["b_col"] + ch["m_prev"]
        ch["v_aug"] = jnp.concatenate([ch["v"], jnp.broadcast_to(ones_col, (L, LANES))], axis=1)
    for ch in chains:
        ch["rmat"] = jnp.where(ch["keep"], ch["r_row"], -jnp.inf)
    for ch in chains:
        ch["dmax"] = ch["b_col"] + jnp.max(ch["rmat"], axis=-1, keepdims=True)
    for ch in chains:
        ch["m_row"] = jnp.maximum(ch["inter"], ch["dmax"])
    for ch in chains:
        ch["sc"] = (ch["s"] * jnp.exp(ch["rmat"] + (ch["b_col"] - ch["m_row"]))).astype(BF16)
    for ch in chains:
        ch["a"] = _dot(ch["sc"], ch["v_aug"])
    for ch in chains:
        ch["nd"] = jnp.exp(ch["inter"] - ch["m_row"]) * ch["qc"] + ch["a"]
    for ch in chains:
        ch["scale"] = 1.0 / jnp.maximum(jnp.abs(ch["nd"][:, L:L + 1]), jnp.exp(-ch["m_row"]))
    for ch in chains:
        h_refs[ch["d"]][ch["seq"], :, ch["sl"]] = ch["nd"][:, :L] * ch["scale"]
    for ch in chains:
        ch["wel"] = ch["total"] - ch["b_col"] + ch["i_col"]
        ch["m_new"] = jnp.maximum(ch["total"] + ch["m_prev"], jnp.max(ch["wel"], axis=0, keepdims=True))
    for ch in chains:
        ch["kw"] = (jnp.exp(ch["wel"] - ch["m_new"]) * ch["k"].astype(F32)).astype(BF16)
    for ch in chains:
        ch["u"] = _dot(ch["kw"], ch["v_aug"], ((0,), (0,)))
    for ch in chains:
        key = (ch["seq"], ch["d"], ch["h"])
        cn_s[key] = jnp.exp(ch["total"] + ch["m_prev"] - ch["m_new"]) * ch["cn_prev"] + ch["u"]
        m_s[key] = jnp.broadcast_to(ch["m_new"], (1, LANES))

    if want_state:
        @pl.when(c == last)
        def _():
            pick = ((lax.broadcasted_iota(I32, (8, 2 * L), 1) == L)
                    & (lax.broadcasted_iota(I32, (8, 2 * L), 0) == 0)).astype(BF16)
            for s in range(cn_s.shape[0]):
                for d in range(2):
                    for h in range(B_HEADS):
                        cn = cn_s[s, d, h]
                        c_o[s, d, h] = cn[:, :L]
                        n_o[s, d, h:h + 1, :] = _dot_exact_lhs01(pick, cn, ((1,), (1,)))[0:1]
            m_o[...] = m_s[...]


def _mlstm_call(l, lat, q_all, k_all, v_all, g_all, gt_all, cn0, m0, state_bufs):
    L = MLSTM_CHUNK
    ns = MLSTM_SEQS_LAT if lat else MLSTM_SEQS_CTX
    n = q_all.shape[0]
    nb, seq = (DEC_BATCH, DEC_SEQ) if lat else (BATCH, SEQ)
    nc = seq // L
    chunk = (lambda c: c, lambda c: nc - 1 - c)
    by_seq = lambda t: t.reshape(nb, seq, t.shape[-1])

    in_specs, args = [], []
    for d in range(2):
        tok = lambda w, d=d: pl.BlockSpec((ns, L, w), lambda b, c: (b, chunk[d](c), 0))
        in_specs += [tok(512), tok(512), tok(512), tok(LANES),
                     pl.BlockSpec((ns, 16, L), lambda b, c, d=d: (b, 0, chunk[d](c)))]
        args += [by_seq(q_all), by_seq(k_all), by_seq(v_all), by_seq(g_all), gt_all]
    if lat:
        in_specs += [pl.BlockSpec((ns, 2, B_HEADS, L, 2 * L), lambda b, c: (b, 0, 0, 0, 0)),
                     pl.BlockSpec((ns, 2, B_HEADS, 1, LANES), lambda b, c: (b, 0, 0, 0, 0))]
        args += [cn0, m0]
    aliases = {}
    alias_in = list(state_bufs) if state_bufs is not None else []
    if alias_in:
        aliases = {len(args): 2, len(args) + 1: 3}
    in_specs += [pl.BlockSpec(memory_space=pl.ANY)] * len(alias_in)
    args += alias_in
    out_specs = [pl.BlockSpec((ns, L, 512), lambda b, c, d=d: (b, chunk[d](c), 0)) for d in range(2)]
    out_shape = [jax.ShapeDtypeStruct((nb, seq, 512), F32)] * 2
    if not lat:
        out_specs += [pl.BlockSpec((ns, None, 2, B_HEADS, L, L), lambda b, c: (b, l, 0, 0, 0, 0)),
                      pl.BlockSpec((ns, None, 2, B_HEADS, L), lambda b, c: (b, l, 0, 0, 0)),
                      pl.BlockSpec((ns, 2, B_HEADS, 1, LANES), lambda b, c: (b, 0, 0, 0, 0))]
        out_shape += [jax.ShapeDtypeStruct((BATCH, DEPTH, 2, B_HEADS, L, L), F32),
                      jax.ShapeDtypeStruct((BATCH, DEPTH, 2, B_HEADS, L), F32),
                      jax.ShapeDtypeStruct((nb, 2, B_HEADS, 1, LANES), F32)]
    res = pl.pallas_call(
        functools.partial(_mlstm_kernel, lat, not lat, len(alias_in)),
        grid=(nb // ns, nc), in_specs=in_specs, out_specs=out_specs, out_shape=out_shape,
        input_output_aliases=aliases,
        scratch_shapes=[pltpu.VMEM((ns, 2, B_HEADS, L, 2 * L), F32), pltpu.VMEM((ns, 2, B_HEADS, 1, LANES), F32)],
        compiler_params=_cparams(("arbitrary", "arbitrary")), name="mlstm_lat" if lat else "mlstm_ctx",
    )(*args)
    return [res[0].reshape(n, 512), res[1].reshape(n, 512)] + list(res[2:])


def _merge_kernel(x_ref, mod_ref, oa_ref, hf_ref, hb_ref, ob_ref, oc_ref, g_ref, hn_ref,
                  wpa_ref, wpb_ref, wpc_ref, wo_ref, bo_ref, ln_g_ref, ln_b_ref, wr_ref, br_ref,
                  x1_o, u2_o, ti_o, tw_o):
    gate1 = mod_ref[0, 2:3, :]
    shift2 = mod_ref[0, 3:4, :]
    scale2 = mod_ref[0, 4:5, :]
    wr = _split3(wr_ref[...])
    eio = lax.broadcasted_iota(I32, (N_EXPERTS, 1), 0)
    nt = ((1,), (1,))
    cs = [dict(rs=slice(r, r + MERGE_ROWS)) for r in range(0, TM_MERGE, MERGE_ROWS)]
    for c in cs:
        c["pa"] = _dot(oa_ref[c["rs"], :], wpa_ref[...])
        c["pc"] = _dot(oc_ref[c["rs"], :], wpc_ref[...])
    for c in cs:
        hm = hf_ref[c["rs"], :] + hb_ref[c["rs"], :]
        parts = []
        for h in range(B_HEADS):
            hh = hm[:, h * LANES:(h + 1) * LANES]
            parts.append(hh * lax.rsqrt(jnp.mean(hh * hh, axis=-1, keepdims=True) + EPS))
        c["out_b"] = (jnp.concatenate(parts, axis=1) * hn_ref[...]
                      * jax.nn.sigmoid(ob_ref[c["rs"], :].astype(F32))).astype(BF16)
    for c in cs:
        c["pb"] = _dot(c["out_b"], wpb_ref[...])
    for c in cs:
        rs = c["rs"]
        c["merged"] = (g_ref[rs, 0:1024].astype(F32) * c["pa"] + g_ref[rs, 1024:2048].astype(F32) * c["pb"]
                       + g_ref[rs, 2048:3072].astype(F32) * c["pc"]).astype(BF16)
    for c in cs:
        c["y"] = _dot(c["merged"], wo_ref[...]) + bo_ref[...]
    for c in cs:
        c["x1"] = _ln(DEEPNORM_ALPHA * x_ref[c["rs"], :] + gate1 * c["y"]) * ln_g_ref[...] + ln_b_ref[...]
        x1_o[c["rs"], :] = c["x1"]
    for c in cs:
        c["u2"] = _ln(c["x1"]) * (1.0 + scale2) + shift2
        u2_o[c["rs"], :] = _pack_rows(c["u2"])
    for c in cs:
        up = _split3(c["u2"])
        c["cur"] = (_dot(wr[0], up[0], nt) + _dot(wr[0], up[1], nt) + _dot(wr[1], up[0], nt)
                    + _dot(wr[0], up[2], nt) + _dot(wr[2], up[0], nt) + _dot(wr[1], up[1], nt)) + br_ref[...]
        c["vals"], c["idxs"] = [], []
    for _ in range(TOP_K):
        for c in cs:
            mx = jnp.max(c["cur"], axis=0, keepdims=True)
            ix = jnp.min(jnp.where(c["cur"] == mx, eio, N_EXPERTS), axis=0, keepdims=True)
            c["vals"].append(mx)
            c["idxs"].append(ix)
            c["cur"] = jnp.where(eio == ix, -jnp.inf, c["cur"])
    for c in cs:
        ex = [jnp.exp(v - c["vals"][0]) for v in c["vals"]]
        inv = 1.0 / sum(ex)
        ti_o[:, c["rs"]] = jnp.concatenate(c["idxs"], axis=0)
        tw_o[:, c["rs"]] = jnp.concatenate([e * inv for e in ex], axis=0)


def _merge_call(lat, x, mod, out_a, hf, hb, ob, out_c, g, hn, wpa, wpb, wpc, wo, bo, ln_g, ln_b, wr_t, br):
    n = x.shape[0]
    tm = TM_MERGE
    row = lambda w: pl.BlockSpec((tm, w), lambda i: (i, 0))
    const = lambda shape: pl.BlockSpec(shape, lambda i: (0,) * len(shape), pipeline_mode=pl.Buffered(1))
    return pl.pallas_call(
        _merge_kernel,
        grid=(n // tm,),
        in_specs=[row(D_MODEL), _mod_spec(lat, tm),
                  row(512), row(512), row(512), row(512), row(512), row(3072),
                  const((1, 512)), const((512, D_MODEL)), const((512, D_MODEL)), const((512, D_MODEL)),
                  const((D_MODEL, D_MODEL)), const((1, D_MODEL)), const((1, D_MODEL)), const((1, D_MODEL)),
                  const((N_EXPERTS, D_MODEL)), const((N_EXPERTS, 1))],
        out_specs=[row(D_MODEL), row(D_PACK),
                   pl.BlockSpec((TOP_K, tm), lambda i: (0, i)), pl.BlockSpec((TOP_K, tm), lambda i: (0, i))],
        out_shape=[jax.ShapeDtypeStruct((n, D_MODEL), F32), jax.ShapeDtypeStruct((n, D_PACK), I32),
                   jax.ShapeDtypeStruct((TOP_K, n), I32), jax.ShapeDtypeStruct((TOP_K, n), F32)],
        compiler_params=_cparams(("arbitrary",)), name="merge_lat" if lat else "merge_ctx",
    )(x, mod, out_a, hf, hb, ob, out_c, g, hn, wpa, wpb, wpc, wo, bo, ln_g, ln_b, wr_t, br)


def _route_kernel(ti_ref, pos_o, te_o, tv_o, nt_o, rank_s):
    ch = ROUTE_CHUNK
    n_tok = ti_ref.shape[1]
    eio = lax.broadcasted_iota(I32, (N_EXPERTS, 1), 0)
    tri = (lax.broadcasted_iota(I32, (ch, ch), 0) <= lax.broadcasted_iota(I32, (ch, ch), 1)).astype(BF16)

    def onehots(off):
        ti = ti_ref[:, pl.ds(off, ch)]
        return [(eio == ti[k:k + 1, :]).astype(F32) for k in range(TOP_K)]

    def count_body(c, carry):
        off = pl.multiple_of(c * ch, ch)
        ohs = onehots(off)
        oh = ohs[0] + ohs[1] + ohs[2] + ohs[3]
        incl = _dot(oh.astype(BF16), tri)
        excl = incl - oh + carry
        for k in range(TOP_K):
            rank_s[k:k + 1, pl.ds(off, ch)] = jnp.sum(ohs[k] * excl, axis=0, keepdims=True)
        return carry + incl[:, ch - 1:ch]

    counts = lax.fori_loop(0, n_tok // ch, count_body, jnp.zeros((N_EXPERTS, 1), F32))
    ntile_e = jnp.floor((counts + (TME - 1)) * (1.0 / TME))
    lower = (lax.broadcasted_iota(I32, (N_EXPERTS, N_EXPERTS), 1)
             < lax.broadcasted_iota(I32, (N_EXPERTS, N_EXPERTS), 0)).astype(BF16)
    off_t = _dot(lower, jnp.broadcast_to(ntile_e, (N_EXPERTS, LANES)).astype(BF16))[:, 0:1]
    end_t = off_t + ntile_e
    total = jnp.sum(ntile_e, axis=0, keepdims=True)
    jt = lax.broadcasted_iota(I32, (1, TILE_LANES), 1).astype(F32)
    te = jnp.minimum(jnp.sum((end_t <= jt).astype(F32), axis=0, keepdims=True), N_EXPERTS - 1.0)
    sel = (eio.astype(F32) == te).astype(F32)
    cnt_j = jnp.sum(sel * counts, axis=0, keepdims=True)
    off_j = jnp.sum(sel * off_t, axis=0, keepdims=True)
    valid = jnp.clip(cnt_j - (jt - off_j) * TME, 0.0, float(TME))
    te_o[...] = te.astype(I32)
    tv_o[...] = jnp.where(jt < total, valid, 0.0).astype(I32)
    nt_o[...] = jnp.broadcast_to(total, (1, LANES)).astype(I32)
    off_rows = off_t * TME

    def pos_body(c, carry):
        off = pl.multiple_of(c * ch, ch)
        ohs = onehots(off)
        for k in range(TOP_K):
            base = jnp.sum(ohs[k] * off_rows, axis=0, keepdims=True)
            pos_o[k:k + 1, pl.ds(off, ch)] = (rank_s[k:k + 1, pl.ds(off, ch)] + base).astype(I32)
        return carry

    lax.fori_loop(0, n_tok // ch, pos_body, 0)


def _route_call(top_i):
    n = top_i.shape[1]
    assert _max_tiles(n) <= TILE_LANES
    tiles = jax.ShapeDtypeStruct((1, TILE_LANES), I32)
    return pl.pallas_call(
        _route_kernel,
        out_shape=[jax.ShapeDtypeStruct((TOP_K, n), I32), tiles, tiles,
                   jax.ShapeDtypeStruct((1, LANES), I32)],
        scratch_shapes=[pltpu.VMEM((TOP_K, n), F32)],
        compiler_params=pltpu.CompilerParams(vmem_limit_bytes=VMEM_LIMIT), name="route",
    )(top_i)


def _sc_mesh():
    return plsc.VectorSubcoreMesh(core_axis_name="c", subcore_axis_name="s")


def _sc_worker(n_rows):
    info = plsc.get_sparse_core_info()
    n_workers = info.num_cores * info.num_subcores
    per_w = n_rows // n_workers
    assert per_w * n_workers == n_rows and per_w % SC_WINDOW == 0
    wid = lax.axis_index("s") * info.num_cores + lax.axis_index("c")
    return wid * per_w, per_w


def _dispatch_call(u2s, pos_flats, p_rows):
    w = SC_WINDOW
    width, dtype = u2s[0].shape[1], u2s[0].dtype

    @functools.partial(pl.kernel, out_type=jax.ShapeDtypeStruct((p_rows, width), dtype), mesh=_sc_mesh(),
                       scratch_types=[pltpu.VMEM((w,), I32)] * (2 * TOP_K)
                       + [pltpu.VMEM((w, width), dtype)] * 2 + [pltpu.SemaphoreType.DMA] * 4)
    def k(*refs):
        n_chains = len(u2s)
        srcs = [(refs[2 * j], refs[2 * j + 1], u2s[j].shape[0]) for j in range(n_chains)]
        o_hbm = refs[2 * n_chains]
        scratch = refs[2 * n_chains + 1:]
        idx = (scratch[0:TOP_K], scratch[TOP_K:2 * TOP_K])
        rows = scratch[2 * TOP_K:2 * TOP_K + 2]
        sem_load = scratch[2 * TOP_K + 2:2 * TOP_K + 4]
        sem_scat = scratch[2 * TOP_K + 4:2 * TOP_K + 6]
        chunks = []
        for j, (_, _, n) in enumerate(srcs):
            base, per_w = _sc_worker(n)
            chunks += [(j, base + c * w) for c in range(per_w // w)]

        def start_loads(chunk, s):
            j, first = chunk
            x_hbm, p_hbm, n = srcs[j]
            off = pl.multiple_of(first, 8)
            cps = [pltpu.async_copy(x_hbm.at[pl.ds(off, w)], rows[s], sem_load[s])]
            cps += [pltpu.async_copy(p_hbm.at[pl.ds(pl.multiple_of(kk * n + off, 8), w)], idx[s][kk], sem_load[s])
                    for kk in range(TOP_K)]
            return cps

        loads = start_loads(chunks[0], 0)
        scatters = [None, None]
        for c in range(len(chunks)):
            s = c % 2
            for cp in loads:
                cp.wait()
            if c + 1 < len(chunks):
                if scatters[1 - s] is not None:
                    for cp in scatters[1 - s]:
                        cp.wait()
                    scatters[1 - s] = None
                loads = start_loads(chunks[c + 1], 1 - s)
            scatters[s] = [pltpu.async_copy(rows[s], o_hbm.at[idx[s][kk]], sem_scat[s]) for kk in range(TOP_K)]
        for group in scatters:
            if group is not None:
                for cp in group:
                    cp.wait()

    return k(*[a for pair in zip(u2s, pos_flats) for a in pair])


def _combine_call(ys, pos_flat):
    w = SC_WINDOW
    n = pos_flat.shape[0]

    @functools.partial(pl.kernel, out_type=jax.ShapeDtypeStruct((n, ys.shape[1]), ys.dtype), mesh=_sc_mesh(),
                       scratch_types=[pltpu.VMEM((w,), I32)] * 2 + [pltpu.VMEM((w, ys.shape[1]), ys.dtype)] * 2
                       + [pltpu.SemaphoreType.DMA] * 4)
    def k(y_hbm, p_hbm, o_hbm, idx0, idx1, rows0, rows1, sg0, sg1, ss0, ss1):
        idx, rows, sem_gather, sem_store = (idx0, idx1), (rows0, rows1), (sg0, sg1), (ss0, ss1)
        base, per_w = _sc_worker(n)
        n_chunks = per_w // w

        def offset(c):
            return pl.multiple_of(base + c * w, 8)

        pltpu.sync_copy(p_hbm.at[pl.ds(offset(0), w)], idx[0])
        gather = pltpu.async_copy(y_hbm.at[idx[0]], rows[0], sem_gather[0])
        stores = [None, None]
        for c in range(n_chunks):
            s = c % 2
            if c + 1 < n_chunks:
                pltpu.sync_copy(p_hbm.at[pl.ds(offset(c + 1), w)], idx[1 - s])
            gather.wait()
            if c + 1 < n_chunks:
                if stores[1 - s] is not None:
                    stores[1 - s].wait()
                    stores[1 - s] = None
                gather = pltpu.async_copy(y_hbm.at[idx[1 - s]], rows[1 - s], sem_gather[1 - s])
            stores[s] = pltpu.async_copy(rows[s], o_hbm.at[pl.ds(offset(c), w)], sem_store[s])
        for st in stores:
            if st is not None:
                st.wait()

    return k(ys, pos_flat)


def _moe_kernel(l, n_tiles_max, te_ref, tv_ref, nt_ref, x_ref, wu_hbm, bu_ref, wd_hbm, bd_ref,
                y_ref, wu_f, wd_f, wu_s, wd_s, sem, group_s):
    i = pl.program_id(0)
    nt = nt_ref[0]

    def fetch(e, slot):
        return (pltpu.make_async_copy(wu_hbm.at[l, e], wu_f.at[slot], sem.at[0, slot]),
                pltpu.make_async_copy(wd_hbm.at[l, e], wd_f.at[slot], sem.at[1, slot]))

    @pl.when(i == 0)
    def _():
        group_s[0] = 0
        for cp in fetch(te_ref[0], 0):
            cp.start()

    @pl.when(i < nt)
    def _():
        e = te_ref[i]
        prev = te_ref[jnp.maximum(i - 1, 0)]

        @pl.when((i == 0) | (e != prev))
        def _():
            group = group_s[0]
            slot = group % 2
            j = lax.while_loop(lambda j: (j < nt) & (te_ref[jnp.minimum(j, n_tiles_max - 1)] == e),
                               lambda j: j + 1, i + 1)

            @pl.when(j < nt)
            def _():
                for cp in fetch(te_ref[jnp.minimum(j, n_tiles_max - 1)], 1 - slot):
                    cp.start(priority=1)

            for cp in fetch(e, slot):
                cp.wait()
            wu_s[...] = wu_f[slot].astype(BF16)
            wd_s[...] = wd_f[slot].astype(BF16)
            group_s[0] = group + 1

        valid = tv_ref[i]
        half = MOE_ROWS
        for j in range(TME // half):
            @pl.when(valid > j * half)
            def _():
                sl = slice(j * half, (j + 1) * half)
                rows = lax.broadcasted_iota(I32, (half, 1), 0) + j * half
                lo, hi = _unpack_rows(jnp.where(rows < valid, x_ref[sl, :], 0))
                x = jnp.concatenate([lo.astype(BF16), hi.astype(BF16)], axis=1)
                h = _dot(x, wu_s[...]) + bu_ref[...]
                h_glu = jnp.minimum(h[:, :D_EXPERT], SWIGLU_LIMIT)
                h_lin = jnp.clip(h[:, D_EXPERT:], -SWIGLU_LIMIT, SWIGLU_LIMIT)
                act = h_glu * jax.nn.sigmoid(SWIGLU_ALPHA * h_glu) * (h_lin + 1.0)
                y_ref[sl, :] = _pack_rows(_dot(act.astype(BF16), wd_s[...]) + bd_ref[...])


def _moe_call(l, tile_e, tile_v, n_tiles, xs, w_up, b_up, w_down, b_down):
    n_tiles_max = xs.shape[0] // TME

    def tile(i, te, tv, nt):
        return jnp.minimum(i, nt[0] - 1)

    def wmap(i, te, tv, nt):
        return (l, te[tile(i, te, tv, nt)], 0, 0)

    rows = pl.BlockSpec((TME, D_PACK), lambda i, te, tv, nt: (tile(i, te, tv, nt), 0))
    grid_spec = pltpu.PrefetchScalarGridSpec(
        num_scalar_prefetch=3,
        grid=(n_tiles_max,),
        in_specs=[rows,
                  pl.BlockSpec(memory_space=pl.ANY),
                  pl.BlockSpec((None, None, 1, 2 * D_EXPERT), wmap),
                  pl.BlockSpec(memory_space=pl.ANY),
                  pl.BlockSpec((None, None, 1, D_MODEL), wmap)],
        out_specs=rows,
        scratch_shapes=[pltpu.VMEM((2, D_MODEL, 2 * D_EXPERT), F32), pltpu.VMEM((2, D_EXPERT, D_MODEL), F32),
                        pltpu.VMEM((D_MODEL, 2 * D_EXPERT), BF16), pltpu.VMEM((D_EXPERT, D_MODEL), BF16),
                        pltpu.SemaphoreType.DMA((2, 2)), pltpu.SMEM((1,), I32)],
    )
    return pl.pallas_call(
        functools.partial(_moe_kernel, l, n_tiles_max), grid_spec=grid_spec,
        out_shape=jax.ShapeDtypeStruct((xs.shape[0], D_PACK), I32),
        compiler_params=_cparams(("arbitrary",)), name="moe",
    )(tile_e, tile_v, n_tiles, xs, w_up, b_up.reshape(DEPTH, N_EXPERTS, 1, -1),
      w_down, b_down.reshape(DEPTH, N_EXPERTS, 1, -1))


def _final_kernel(yg_ref, tw_ref, x1_ref, mod_ref, ln_g_ref, ln_b_ref, o_ref):
    o_ref[...] = _moe_residual_norm(yg_ref, tw_ref, x1_ref, mod_ref, ln_g_ref, ln_b_ref)


def _final_call(lat, yg, tw, x1, mod, ln_g, ln_b):
    n = x1.shape[0]
    tm = TM_MERGE
    row = lambda w: pl.BlockSpec((tm, w), lambda i: (i, 0))
    const = lambda shape: pl.BlockSpec(shape, lambda i: (0,) * len(shape))
    return pl.pallas_call(
        _final_kernel,
        grid=(n // tm,),
        in_specs=[pl.BlockSpec((TOP_K, tm, D_PACK), lambda i: (0, i, 0)), row(TOP_K), row(D_MODEL),
                  _mod_spec(lat, tm), const((1, D_MODEL)), const((1, D_MODEL))],
        out_specs=row(D_MODEL), out_shape=jax.ShapeDtypeStruct((n, D_MODEL), F32),
        compiler_params=_cparams(("arbitrary",)), name="final_lat" if lat else "final_ctx",
    )(yg, tw, x1, mod, ln_g, ln_b)


def _rope_tables():
    half = A_HEAD_DIM // 2
    freqs = ROPE_BASE ** (-jnp.arange(0, half, 2, dtype=F32) / half)
    t = jnp.arange(DEC_SEQ, dtype=jnp.int32)
    ang_r = (t // GRID_W).astype(F32)[:, None] * freqs[None, :]
    ang_c = (t % GRID_W).astype(F32)[:, None] * freqs[None, :]
    cos64 = jnp.concatenate([jnp.cos(ang_r)] * 2 + [jnp.cos(ang_c)] * 2, axis=1)
    sin64 = jnp.concatenate([-jnp.sin(ang_r), jnp.sin(ang_r), -jnp.sin(ang_c), jnp.sin(ang_c)], axis=1)
    lat = (jnp.tile(cos64, (1, 2)), jnp.tile(sin64, (1, 2)))
    ctx = (jnp.ones((TM, LANES), F32), jnp.zeros((TM, LANES), F32))
    return ctx, lat


def _permute_in_cols(t, axis=-1):
    take = lambda lo, hi: lax.slice_in_dim(t, lo, hi, axis=axis)
    pad_shape = list(t.shape)
    pad_shape[axis] = D_IN_PAD - D_IN
    parts = [take(h * 64, (h + 1) * 64) for h in QA_HEAD_ORDER]
    parts += [take(512, 2816), take(2832, 7440), take(2816, 2832), jnp.zeros(pad_shape, t.dtype)]
    return jnp.concatenate(parts, axis=axis)


def _layer_weights(p):
    w_in, b_in = p["w_in"], p["b_in"]
    wpa = p["w_pa"].reshape(A_HEADS, A_HEAD_DIM, D_MODEL)[np.asarray(QA_HEAD_ORDER)].reshape(512, D_MODEL)
    row = lambda t: t.reshape(1, -1)
    return dict(
        inproj=(_permute_in_cols(w_in.T, axis=0).astype(BF16), _permute_in_cols(b_in).reshape(1, D_IN_PAD),
                w_in[:, 2816:2832].T.astype(BF16), b_in[2816:2832].reshape(16, 1),
                row(jnp.tile(p["qn_a"], A_HEADS)), row(jnp.tile(p["kn_a"], A_KV_HEADS)),
                jnp.asarray(np.kron(np.eye(8), np.ones((64, 64))), BF16)),
        merge=(row(p["hn_b"]), wpa.astype(BF16), p["w_pb"].astype(BF16), p["w_pc"].astype(BF16),
               p["w_o"].astype(BF16), row(p["b_o"]), row(p["ln1_g"]), row(p["ln1_b"]),
               p["w_router"].T, p["b_router"].reshape(N_EXPERTS, 1)),
        lam_c=p["lam_c"], sn_c=row(p["sn_c"]), ln2=(row(p["ln2_g"]), row(p["ln2_b"])))


def _chain_mixer(l, lat, x, mod, tables, caches, w, outs, after=None):
    cos_t, sin_t = tables
    lam_init = 0.8 - 0.6 * math.exp(-0.3 * l)
    res = _inproj_call(l, lat, x, mod, cos_t, sin_t, *w["inproj"], None if outs is None else outs[:4], after)
    qa, kab, vab, qb, kb, vb, ob, qc, kcb, vcb, g, ifg, ifgt = res[:13]
    if isinstance(x, tuple):
        x = res[-1]
    if lat:
        ck_a, cv_a, ck_c, cv_c, cn0, m0 = caches
    else:
        ck_a = cv_a = ck_c = cv_c = cn0 = m0 = None
    oa = _attn_call("a", lat, lam_init, qa, kab, vab, ck_a, cv_a, None, None)
    oc = _attn_call("c", lat, lam_init, qc, kcb, vcb, ck_c, cv_c, w["lam_c"], w["sn_c"])
    ml = _mlstm_call(l, lat, qb, kb, vb, ifg, ifgt, cn0, m0, None if outs is None else outs[4:6])
    hf, hb = ml[:2]
    merged = _merge_call(lat, x, mod, oa, hf, hb, ob, oc, g, *w["merge"])
    if lat:
        return merged, None
    m_l = ml[4][:, :, :, 0, 0]
    m_all = m_l[:, None] if outs is None else jnp.concatenate([outs[6], m_l[:, None]], axis=1)
    return merged, tuple(res[13:17]) + (ml[2], ml[3], m_all)


def _moe_layer(l, mix_ctx, mix_lat, mod, w, experts):
    top_i = jnp.concatenate([mix_ctx[2], mix_lat[2]], axis=1)
    pos, tile_e, tile_v, n_tiles = _route_call(top_i)
    p_rows = _max_tiles(N_CTX + N_LAT) * TME
    pos_flat = (pos[:, :N_CTX].reshape(TOP_K * N_CTX), pos[:, N_CTX:].reshape(TOP_K * N_LAT))
    xs = _dispatch_call((mix_ctx[1], mix_lat[1]), pos_flat, p_rows)
    ys = _moe_call(l, tile_e[0], tile_v[0], n_tiles[0, :1], xs,
                   experts["w_up"], experts["b_up"], experts["w_down"], experts["b_down"])
    yg_ctx = _combine_call(ys, pos_flat[0]).reshape(TOP_K, N_CTX, D_PACK)
    yg_lat = _combine_call(ys, pos_flat[1]).reshape(TOP_K, N_LAT, D_PACK)
    return ((yg_ctx, mix_ctx[3].T, mix_ctx[0], mod) + w["ln2"], (yg_lat, mix_lat[3].T, mix_lat[0], mod) + w["ln2"])


def kernel(x_prompt, x_sample, cache_k_gqa, cache_v_gqa, cache_k_diff, cache_v_diff, state_mlstm_c, state_mlstm_n, state_mlstm_m, c, c_ctx, w_in, b_in, qn_a, kn_a, hn_b, lam_c, sn_c, w_pa, w_pb, w_pc, w_o, b_o, w_mod, b_mod, ln1_g, ln1_b, ln2_g, ln2_b, w_router, b_router, w_up, b_up, w_down, b_down):
    x_ctx = x_prompt.reshape(N_CTX, D_MODEL)
    x_lat = x_sample.reshape(N_LAT, D_MODEL)
    cvecs = jnp.zeros((8, D_MODEL), F32).at[0].set(c_ctx).at[1:1 + DEC_BATCH].set(c)
    mod_all = _mod_call(cvecs, w_mod, b_mod).reshape(DEPTH, 8, N_MOD, D_MODEL)[:, :N_GROUPS]
    tables_ctx, tables_lat = _rope_tables()
    params = dict(w_in=w_in, b_in=b_in, qn_a=qn_a, kn_a=kn_a, hn_b=hn_b, lam_c=lam_c, sn_c=sn_c, w_pa=w_pa,
                  w_pb=w_pb, w_pc=w_pc, w_o=w_o, b_o=b_o, ln1_g=ln1_g, ln1_b=ln1_b, ln2_g=ln2_g, ln2_b=ln2_b,
                  w_router=w_router, b_router=b_router)
    experts = dict(w_up=w_up, b_up=b_up, w_down=w_down, b_down=b_down)
    outs = None
    for l in range(DEPTH):
        w = _layer_weights({k: v[l] for k, v in params.items()})
        ck_a = cache_k_gqa[:, l].transpose(0, 2, 1, 3).reshape(DEC_BATCH, PAST_LEN, 128).astype(BF16)
        cv_a = cache_v_gqa[:, l].transpose(0, 2, 1, 3).reshape(DEC_BATCH, PAST_LEN, 128).astype(BF16)
        ck_c = cache_k_diff[:, l].astype(BF16)
        cv_c = cache_v_diff[:, l].astype(BF16)
        cn0 = jnp.concatenate([state_mlstm_c[:, l], state_mlstm_n[:, l][..., None],
                               jnp.zeros((DEC_BATCH, 2, B_HEADS, B_HEAD_DIM, B_HEAD_DIM - 1), F32)], axis=-1)
        m0 = jnp.broadcast_to(state_mlstm_m[:, l][..., None, None], (DEC_BATCH, 2, B_HEADS, 1, LANES))
        mix_ctx, outs = _chain_mixer(l, False, x_ctx, mod_all[l], tables_ctx, None, w, outs)
        mix_lat, _ = _chain_mixer(l, True, x_lat, mod_all[l], tables_lat, (ck_a, cv_a, ck_c, cv_c, cn0, m0), w, None,
                                  after=mix_ctx[3] if l > 0 else None)
        x_ctx, x_lat = _moe_layer(l, mix_ctx, mix_lat, mod_all[l], w, experts)
    x_ctx = _final_call(False, *x_ctx)
    x_lat = _final_call(True, *x_lat)
    return (x_ctx.reshape(BATCH, SEQ, D_MODEL), x_lat.reshape(DEC_BATCH, DEC_SEQ, D_MODEL)) + tuple(outs)
```
